```python
import math
import jax, jax.numpy as jnp
from jax import lax
import numpy as np

D_MODEL = 2048
BATCH = 4
SEQ = 2048
DEPTH = 2

HEAD_DIM = 64
BLOCK = 128
WINDOW = 128
A_HEADS = D_MODEL // HEAD_DIM
A_KV_HEADS = 4
A_GROUP = A_HEADS // A_KV_HEADS
A_Q_DIM = A_HEADS * HEAD_DIM
A_KV_DIM = A_KV_HEADS * HEAD_DIM
B_HEADS = D_MODEL // HEAD_DIM
B_DIM = B_HEADS * HEAD_DIM
REL_BUCKETS = 32
REL_MAX_DIST = 128
D_FF = 5632
CONV_WIDTH = 3
LN_EPS = 1e-5
ALPHA = (2 * DEPTH) ** 0.25
BETA = (8 * DEPTH) ** -0.25
SCALE = HEAD_DIM ** -0.5
N_A_LAYERS = (DEPTH + 1) // 2
N_B_LAYERS = DEPTH // 2

kernel_name = 'hybrid_swa_sink_fox_convffn_deepnorm'


def layer_norm(x, g, b):
    xf = x.astype(jnp.float32)
    mu = jnp.mean(xf, axis=-1, keepdims=True)
    xc = xf - mu
    var = jnp.mean(xc * xc, axis=-1, keepdims=True)
    y = xc * lax.rsqrt(var + LN_EPS)
    return (y * g.astype(jnp.float32) + b.astype(jnp.float32)).astype(x.dtype)


def t5_causal_bucket(dist):
    n = jnp.maximum(dist, 0)
    max_exact = REL_BUCKETS // 2
    nf = jnp.maximum(n, 1).astype(jnp.float32)
    large = max_exact + (jnp.log(nf / max_exact) / math.log(REL_MAX_DIST / max_exact)
                         * (REL_BUCKETS - max_exact)).astype(jnp.int32)
    large = jnp.minimum(large, REL_BUCKETS - 1)
    return jnp.where(n < max_exact, n, large)


def swa_sink_attention(x, w_qkv, w_o, sinks, rel_bias):
    B, S, _ = x.shape
    nb = S // BLOCK
    qkv = x @ w_qkv
    q = qkv[..., :A_Q_DIM].reshape(B, nb, BLOCK, A_KV_HEADS, A_GROUP, HEAD_DIM)
    k = qkv[..., A_Q_DIM:A_Q_DIM + A_KV_DIM].reshape(B, nb, BLOCK, A_KV_HEADS, HEAD_DIM)
    v = qkv[..., A_Q_DIM + A_KV_DIM:].reshape(B, nb, BLOCK, A_KV_HEADS, HEAD_DIM)

    def with_prev(t):
        prev = jnp.concatenate([jnp.zeros_like(t[:, :1]), t[:, :-1]], axis=1)
        return jnp.concatenate([prev, t], axis=2)

    kb, vb = with_prev(k), with_prev(v)
    s = jnp.einsum('bnqkgd,bnskd->bnkgqs', q, kb).astype(jnp.float32) * SCALE
    qi = jnp.arange(BLOCK)
    kj = jnp.arange(2 * BLOCK)
    dist = (BLOCK + qi)[:, None] - kj[None, :]
    bias = rel_bias.astype(jnp.float32)[t5_causal_bucket(dist)]
    bias = jnp.transpose(bias, (2, 0, 1)).reshape(A_KV_HEADS, A_GROUP, BLOCK, 2 * BLOCK)
    kpos = jnp.arange(nb)[:, None] * BLOCK + kj[None, :] - BLOCK
    mask = ((dist >= 0) & (dist < WINDOW))[None] & (kpos >= 0)[:, None, :]
    mask = mask[:, None, None]
    s = jnp.where(mask, s + bias, -jnp.inf)
    sink = sinks.astype(jnp.float32).reshape(A_KV_HEADS, A_GROUP, 1, 1)
    m = jnp.maximum(jnp.max(s, axis=-1, keepdims=True), sink)
    p = jnp.exp(s - m)
    p = p / (jnp.sum(p, axis=-1, keepdims=True) + jnp.exp(sink - m))
    o = jnp.einsum('bnkgqs,bnskd->bnqkgd', p.astype(v.dtype), vb).reshape(B, S, A_Q_DIM)
    return o @ w_o


def forgetting_attention(x, w_qkvf, f_bias, w_o):
    B, S, _ = x.shape
    nb = S // BLOCK
    proj = x @ w_qkvf
    q = proj[..., :B_DIM].reshape(B, S, B_HEADS, HEAD_DIM).transpose(0, 2, 1, 3)
    k = proj[..., B_DIM:2 * B_DIM].reshape(B, S, B_HEADS, HEAD_DIM).transpose(0, 2, 1, 3)
    v = proj[..., 2 * B_DIM:3 * B_DIM].reshape(B, S, B_HEADS, HEAD_DIM).transpose(0, 2, 1, 3)
    fz = proj[..., 3 * B_DIM:].astype(jnp.float32) + f_bias.astype(jnp.float32)
    cum = jnp.cumsum(jax.nn.log_sigmoid(fz), axis=1).transpose(0, 2, 1)
    kpos = jnp.arange(S)

    def block(i):
        start = i * BLOCK
        qb = lax.dynamic_slice_in_dim(q, start, BLOCK, axis=2)
        cb = lax.dynamic_slice_in_dim(cum, start, BLOCK, axis=2)
        s = (jnp.einsum('bhqd,bhkd->bhqk', qb, k).astype(jnp.float32) * SCALE
             + cb[..., :, None] - cum[:, :, None, :])
        qpos = start + jnp.arange(BLOCK)
        s = jnp.where(kpos[None, :] <= qpos[:, None], s, -jnp.inf)
        p = jax.nn.softmax(s, axis=-1)
        return jnp.einsum('bhqk,bhkd->bqhd', p.astype(v.dtype), v)

    o = lax.map(block, jnp.arange(nb))
    o = jnp.transpose(o, (1, 0, 2, 3, 4)).reshape(B, S, B_DIM)
    return o @ w_o


def conv_ffn(x, w_up, conv_w, conv_b, w_down):
    h = x @ w_up
    c = h.shape[-1]
    h = lax.conv_general_dilated(h, conv_w[:, None, :].astype(h.dtype), window_strides=(1,),
                                 padding=[(CONV_WIDTH - 1, 0)],
                                 dimension_numbers=('NWC', 'WIO', 'NWC'),
                                 feature_group_count=c) + conv_b
    g, u = h[..., :D_FF], h[..., D_FF:]
    return (jax.nn.silu(g) * u) @ w_down


def setup_inputs(seed: int = 0) -> dict:
    key = jax.random.key(seed)
    ks = jax.random.split(key, 20)
    nrm = jax.random.normal
    f32 = jnp.float32
    x = nrm(ks[0], (BATCH, SEQ, D_MODEL), f32)
    din = D_MODEL ** -0.5
    a_w_qkv = jnp.concatenate([
        nrm(ks[1], (N_A_LAYERS, D_MODEL, A_Q_DIM), f32) * din,
        nrm(ks[2], (N_A_LAYERS, D_MODEL, A_KV_DIM), f32) * din,
        nrm(ks[3], (N_A_LAYERS, D_MODEL, A_KV_DIM), f32) * (din * BETA)], axis=-1)
    a_w_o = nrm(ks[4], (N_A_LAYERS, A_Q_DIM, D_MODEL), f32) * (A_Q_DIM ** -0.5 * BETA)
    a_sinks = nrm(ks[5], (N_A_LAYERS, A_HEADS), f32) * 0.5
    rel_bias = nrm(ks[6], (REL_BUCKETS, A_HEADS), f32) * 0.5
    b_w_qkvf = jnp.concatenate([
        nrm(ks[7], (N_B_LAYERS, D_MODEL, B_DIM), f32) * din,
        nrm(ks[8], (N_B_LAYERS, D_MODEL, B_DIM), f32) * din,
        nrm(ks[9], (N_B_LAYERS, D_MODEL, B_DIM), f32) * (din * BETA),
        nrm(ks[10], (N_B_LAYERS, D_MODEL, B_HEADS), f32) * (din * 0.5)], axis=-1)
    b_f_bias = 3.0 + 0.5 * nrm(ks[11], (N_B_LAYERS, B_HEADS), f32)
    b_w_o = nrm(ks[12], (N_B_LAYERS, B_DIM, D_MODEL), f32) * (B_DIM ** -0.5 * BETA)
    ffn_w_up = nrm(ks[13], (DEPTH, D_MODEL, 2 * D_FF), f32) * (din * BETA)
    ffn_conv_w = nrm(ks[14], (DEPTH, CONV_WIDTH, 2 * D_FF), f32) * (CONV_WIDTH ** -0.5)
    ffn_conv_b = nrm(ks[15], (DEPTH, 2 * D_FF), f32) * 0.02
    ffn_w_down = nrm(ks[16], (DEPTH, D_FF, D_MODEL), f32) * (D_FF ** -0.5 * BETA)
    ln_g = 1.0 + 0.02 * nrm(ks[17], (DEPTH, 2, D_MODEL), f32)
    ln_b = 0.02 * nrm(ks[18], (DEPTH, 2, D_MODEL), f32)
    return {'x': x, 'a_w_qkv': a_w_qkv, 'a_w_o': a_w_o, 'a_sinks': a_sinks, 'rel_bias': rel_bias,
            'b_w_qkvf': b_w_qkvf, 'b_f_bias': b_f_bias, 'b_w_o': b_w_o,
            'ffn_w_up': ffn_w_up, 'ffn_conv_w': ffn_conv_w, 'ffn_conv_b': ffn_conv_b,
            'ffn_w_down': ffn_w_down, 'ln_g': ln_g, 'ln_b': ln_b}


def reference(x, a_w_qkv, a_w_o, a_sinks, rel_bias, b_w_qkvf, b_f_bias, b_w_o,
              ffn_w_up, ffn_conv_w, ffn_conv_b, ffn_w_down, ln_g, ln_b):
    for i in range(DEPTH):
        j = i // 2
        if i % 2 == 0:
            y = swa_sink_attention(x, a_w_qkv[j], a_w_o[j], a_sinks[j], rel_bias)
        else:
            y = forgetting_attention(x, b_w_qkvf[j], b_f_bias[j], b_w_o[j])
        x = layer_norm(ALPHA * x + y, ln_g[i, 0], ln_b[i, 0])
        y = conv_ffn(x, ffn_w_up[i], ffn_conv_w[i], ffn_conv_b[i], ffn_w_down[i])
        x = layer_norm(ALPHA * x + y, ln_g[i, 1], ln_b[i, 1])
    return x
```

```python
import functools
import math

import jax
import jax.numpy as jnp
import numpy as np
from jax import lax
from jax.experimental import pallas as pl
from jax.experimental.pallas import tpu as pltpu

D_MODEL = 2048
HEAD_DIM = 64
BLOCK = 128
N_HEADS = D_MODEL // HEAD_DIM
A_KV_HEADS = 4
A_GROUP = N_HEADS // A_KV_HEADS
A_KV_DIM = A_KV_HEADS * HEAD_DIM
REL_BUCKETS = 32
REL_MAX_DIST = 128
D_FF = 5632
LN_EPS = 1e-5
SCALE = HEAD_DIM ** -0.5

SUBLANES = 8
LANES = 128

BF16 = jnp.bfloat16
F32 = jnp.float32
NEG_INF = float("-inf")

_NT_DIMS = (((1,), (1,)), ((), ()))


def _params(*sem):
    return pltpu.CompilerParams(dimension_semantics=sem)


def _ws_matmul_kernel(x_ref, w_ref, o_ref, wb_ref):
    @pl.when(pl.program_id(1) == 0)
    def _():
        wb_ref[...] = w_ref[...].astype(BF16)

    x = x_ref[...].astype(BF16)
    o_ref[...] = jnp.dot(x, wb_ref[...], preferred_element_type=F32).astype(o_ref.dtype)


def _ws_matmul(x, w, layer, n_out, tn, tm, name):
    m, k = x.shape
    return pl.pallas_call(
        _ws_matmul_kernel,
        grid=(n_out // tn, m // tm),
        in_specs=[pl.BlockSpec((tm, k), lambda j, i: (i, 0)),
                  pl.BlockSpec((None, k, tn), lambda j, i: (layer, 0, j))],
        out_specs=pl.BlockSpec((tm, tn), lambda j, i: (i, j)),
        out_shape=jax.ShapeDtypeStruct((m, n_out), BF16),
        scratch_shapes=[pltpu.VMEM((k, tn), BF16)],
        compiler_params=_params("arbitrary", "arbitrary"),
        name=name,
    )(x, w)


def _t5_bucket_map():
    qi = np.arange(BLOCK)[:, None]
    kj = np.arange(2 * BLOCK)[None, :]
    dist = BLOCK + qi - kj
    n = np.maximum(dist, 0)
    max_exact = REL_BUCKETS // 2
    nf = np.maximum(n, 1).astype(np.float32)
    large = max_exact + (np.log(nf / max_exact) / math.log(REL_MAX_DIST / max_exact)
                         * (REL_BUCKETS - max_exact)).astype(np.int32)
    large = np.minimum(large, REL_BUCKETS - 1)
    bucket = np.where(n < max_exact, n, large)
    visible = (dist >= 0) & (dist < BLOCK)
    return np.where(visible, bucket, -1).astype(np.int32)


def _swa_bias_kernel(rb_ref, bk_ref, o_ref):
    bk = bk_ref[...]

    def head(h, _):
        acc = jnp.full(bk.shape, NEG_INF, F32)
        for b in range(REL_BUCKETS):
            acc = jnp.where(bk == b, rb_ref[b, h], acc)
        o_ref[h] = acc
        return 0

    lax.fori_loop(0, N_HEADS, head, 0)


def _swa_bias(rel_bias):
    bk = jnp.asarray(_t5_bucket_map())
    return pl.pallas_call(
        _swa_bias_kernel,
        in_specs=[pl.BlockSpec(memory_space=pltpu.SMEM),
                  pl.BlockSpec(memory_space=pltpu.VMEM)],
        out_specs=pl.BlockSpec(memory_space=pltpu.VMEM),
        out_shape=jax.ShapeDtypeStruct((N_HEADS, BLOCK, 2 * BLOCK), F32),
        name="swa_bias",
    )(rel_bias, bk)


def _swa_kernel(sink_ref, q_ref, kp_ref, kc_ref, vp_ref, vc_ref, bm_ref, o_ref):
    first = pl.program_id(1) == 0
    col = lax.broadcasted_iota(jnp.int32, (BLOCK, 2 * BLOCK), 1)
    kill = jnp.logical_and(first, col < BLOCK)
    for kh in range(A_KV_HEADS):
        ks = slice(kh * HEAD_DIM, (kh + 1) * HEAD_DIM)
        k2 = jnp.concatenate([kp_ref[:, ks], kc_ref[:, ks]], axis=0)
        v2 = jnp.concatenate([vp_ref[:, ks], vc_ref[:, ks]], axis=0)
        for g in range(A_GROUP):
            h = kh * A_GROUP + g
            hs = slice(h * HEAD_DIM, (h + 1) * HEAD_DIM)
            q = q_ref[:, hs] * SCALE
            s = lax.dot_general(q, k2, _NT_DIMS, preferred_element_type=F32)
            s = jnp.where(kill, NEG_INF, s + bm_ref[h])
            sink = sink_ref[h]
            m = jnp.maximum(jnp.max(s, axis=-1, keepdims=True), sink)
            p = jnp.exp(s - m)
            denom = jnp.sum(p, axis=-1, keepdims=True) + jnp.exp(sink - m)
            o = jnp.dot(p.astype(BF16), v2, preferred_element_type=F32)
            o_ref[:, hs] = (o / denom).astype(o_ref.dtype)


def _swa_attention(qkv, sinks, biasmask):
    b, s, _ = qkv.shape
    nb = s // BLOCK
    kcol = D_MODEL // A_KV_DIM
    prev = lambda n: jnp.maximum(n - 1, 0)
    return pl.pallas_call(
        _swa_kernel,
        grid=(b, nb),
        in_specs=[pl.BlockSpec(memory_space=pltpu.SMEM),
                  pl.BlockSpec((None, BLOCK, D_MODEL), lambda bi, n: (bi, n, 0)),
                  pl.BlockSpec((None, BLOCK, A_KV_DIM), lambda bi, n: (bi, prev(n), kcol)),
                  pl.BlockSpec((None, BLOCK, A_KV_DIM), lambda bi, n: (bi, n, kcol)),
                  pl.BlockSpec((None, BLOCK, A_KV_DIM), lambda bi, n: (bi, prev(n), kcol + 1)),
                  pl.BlockSpec((None, BLOCK, A_KV_DIM), lambda bi, n: (bi, n, kcol + 1)),
                  pl.BlockSpec((N_HEADS, BLOCK, 2 * BLOCK), lambda bi, n: (0, 0, 0))],
        out_specs=pl.BlockSpec((None, BLOCK, D_MODEL), lambda bi, n: (bi, n, 0)),
        out_shape=jax.ShapeDtypeStruct((b, s, D_MODEL), BF16),
        compiler_params=_params("arbitrary", "arbitrary"),
        name="swa_attn",
    )(sinks, qkv, qkv, qkv, qkv, qkv, biasmask)


FOX_CUM_BLOCK = 256


def _log_sigmoid(x):
    return jnp.minimum(x, 0.0) - jnp.log1p(jnp.exp(-jnp.abs(x)))


def _fox_gate_kernel(x_ref, wf_ref, fb_ref, cum_ref):
    fz = jnp.dot(x_ref[...], wf_ref[...].astype(BF16), preferred_element_type=F32) + fb_ref[...]
    lf = _log_sigmoid(fz)
    cb = FOX_CUM_BLOCK
    row = lax.broadcasted_iota(jnp.int32, (cb, cb), 0)
    col = lax.broadcasted_iota(jnp.int32, (cb, cb), 1)
    tri = jnp.where(row >= col, 1.0, 0.0).astype(BF16)
    carry = jnp.zeros((1, lf.shape[1]), F32)
    for blk in range(lf.shape[0] // cb):
        x = lf[blk * cb:(blk + 1) * cb]
        hi = x.astype(BF16)
        r1 = x - hi.astype(F32)
        mid = r1.astype(BF16)
        lo = (r1 - mid.astype(F32)).astype(BF16)
        c = (jnp.dot(tri, hi, preferred_element_type=F32)
             + jnp.dot(tri, mid, preferred_element_type=F32)
             + jnp.dot(tri, lo, preferred_element_type=F32)) + carry
        cum_ref[blk * cb:(blk + 1) * cb, :] = c
        carry = c[cb - 1:cb]


def _fox_gate(xb, wf, fb, batch, seq):
    return pl.pallas_call(
        _fox_gate_kernel,
        grid=(batch,),
        in_specs=[pl.BlockSpec((seq, D_MODEL), lambda bi: (bi, 0)),
                  pl.BlockSpec((D_MODEL, N_HEADS), lambda bi: (0, 0)),
                  pl.BlockSpec((1, N_HEADS), lambda bi: (0, 0))],
        out_specs=pl.BlockSpec((None, seq, N_HEADS), lambda bi: (bi, 0, 0)),
        out_shape=jax.ShapeDtypeStruct((batch, seq, N_HEADS), F32),
        compiler_params=_params("arbitrary"),
        name="fox_gate",
    )(xb, wf, fb)


FOX_TQ = 256
FOX_TK = 256
FOX_HEADS_PER_STEP = LANES // HEAD_DIM


def _fox_kernel(q_ref, k_ref, v_ref, cb_ref, cr_ref, o_ref):
    seq = q_ref.shape[0]
    tq, tk = FOX_TQ, FOX_TK
    row = lax.broadcasted_iota(jnp.int32, (tq, tk), 0)
    col = lax.broadcasted_iota(jnp.int32, (tq, tk), 1)
    causal = col <= row

    for hh in range(FOX_HEADS_PER_STEP):
        hs = slice(hh * HEAD_DIM, (hh + 1) * HEAD_DIM)

        def scores(q, cbq, j):
            off = pl.multiple_of(j * tk, tk)
            k = k_ref[pl.ds(off, tk), hs]
            s = lax.dot_general(q, k, _NT_DIMS, preferred_element_type=F32)
            return s + (cbq - cr_ref[hh, pl.ds(j, 1), :]), off

        def update(carry, s, off):
            m, l, acc = carry
            m_new = jnp.maximum(m, jnp.max(s, axis=-1, keepdims=True))
            alpha = jnp.exp(m - m_new)
            p = jnp.exp(s - m_new)
            l = alpha * l + jnp.sum(p, axis=-1, keepdims=True)
            pv = jnp.dot(p.astype(BF16), v_ref[pl.ds(off, tk), hs], preferred_element_type=F32)
            return m_new, l, alpha * acc + pv

        def qblock(i, _):
            qoff = pl.multiple_of(i * tq, tq)
            q = q_ref[pl.ds(qoff, tq), hs] * SCALE
            cbq = cb_ref[pl.ds(qoff, tq), hh:hh + 1]

            def chunk(j, carry):
                s, off = scores(q, cbq, j)
                return update(carry, s, off)

            init = (jnp.full((tq, 1), NEG_INF, F32), jnp.zeros((tq, 1), F32),
                    jnp.zeros((tq, HEAD_DIM), F32))
            carry = lax.fori_loop(0, i, chunk, init)
            s, off = scores(q, cbq, i)
            s = jnp.where(causal, s, NEG_INF)
            _, l, acc = update(carry, s, off)
            o_ref[pl.ds(qoff, tq), hs] = (acc / l).astype(o_ref.dtype)
            return 0

        lax.fori_loop(0, seq // tq, qblock, 0)


def _fox_attention(qkv, cum):
    b, s, _ = qkv.shape
    hps = FOX_HEADS_PER_STEP
    npair = N_HEADS // hps
    cb = cum.reshape(b, s, npair, hps).transpose(0, 2, 1, 3)
    cr = cum.transpose(0, 2, 1).reshape(b, npair, hps, s // FOX_TK, FOX_TK)
    ncol = D_MODEL // LANES
    return pl.pallas_call(
        _fox_kernel,
        grid=(b, npair),
        in_specs=[pl.BlockSpec((None, s, LANES), lambda bi, hp: (bi, 0, hp)),
                  pl.BlockSpec((None, s, LANES), lambda bi, hp: (bi, 0, ncol + hp)),
                  pl.BlockSpec((None, s, LANES), lambda bi, hp: (bi, 0, 2 * ncol + hp)),
                  pl.BlockSpec((None, None, s, hps), lambda bi, hp: (bi, hp, 0, 0)),
                  pl.BlockSpec((None, None, hps, s // FOX_TK, FOX_TK), lambda bi, hp: (bi, hp, 0, 0, 0))],
        out_specs=pl.BlockSpec((None, s, LANES), lambda bi, hp: (bi, 0, hp)),
        out_shape=jax.ShapeDtypeStruct((b, s, D_MODEL), BF16),
        compiler_params=_params("arbitrary", "arbitrary"),
        name="fox_attn",
    )(qkv, qkv, qkv, cb, cr)


def _deepnorm(xres, y, g, b, alpha):
    z = alpha * xres + y
    mu = jnp.mean(z, axis=-1, keepdims=True)
    zc = z - mu
    var = jnp.mean(zc * zc, axis=-1, keepdims=True)
    return zc * lax.rsqrt(var + LN_EPS) * g + b


def _oproj_ln_kernel(alpha, o_ref, xres_ref, w_ref, g_ref, b_ref, xo_ref, xob_ref, wb_ref):
    @pl.when(pl.program_id(0) == 0)
    def _():
        wb_ref[...] = w_ref[...].astype(BF16)

    y = jnp.dot(o_ref[...], wb_ref[...], preferred_element_type=F32)
    out = _deepnorm(xres_ref[...], y, g_ref[...], b_ref[...], alpha)
    xo_ref[...] = out
    xob_ref[...] = out.astype(BF16)


def _oproj_ln(o, xres, w, layer, g, b, alpha, tm, name):
    m, k = o.shape
    n = w.shape[2]
    return pl.pallas_call(
        functools.partial(_oproj_ln_kernel, alpha),
        grid=(m // tm,),
        in_specs=[pl.BlockSpec((tm, k), lambda i: (i, 0)),
                  pl.BlockSpec((tm, n), lambda i: (i, 0)),
                  pl.BlockSpec((None, k, n), lambda i: (layer, 0, 0), pipeline_mode=pl.Buffered(1)),
                  pl.BlockSpec((1, n), lambda i: (0, 0)),
                  pl.BlockSpec((1, n), lambda i: (0, 0))],
        out_specs=[pl.BlockSpec((tm, n), lambda i: (i, 0)),
                   pl.BlockSpec((tm, n), lambda i: (i, 0))],
        out_shape=[jax.ShapeDtypeStruct((m, n), F32), jax.ShapeDtypeStruct((m, n), BF16)],
        scratch_shapes=[pltpu.VMEM((k, n), BF16)],
        compiler_params=_params("arbitrary"),
        name=name,
    )(o, xres, w, g, b)


CONV_WIDTH = 3
FFN_TF = 512
FFN_UP_TM = 1024


def _ffn_up_kernel(tiles_per_seq, x_ref, wg_ref, wu_ref, cwg_ref, cwu_ref, cbg_ref, cbu_ref,
                   a_ref, wgb_ref, wub_ref, hg_ref, hu_ref):
    i = pl.program_id(1)
    tm = x_ref.shape[0]
    halo = SUBLANES

    @pl.when(i == 0)
    def _():
        wgb_ref[...] = wg_ref[...].astype(BF16)
        wub_ref[...] = wu_ref[...].astype(BF16)

    @pl.when(i % tiles_per_seq == 0)
    def _():
        hg_ref[0:halo, :] = jnp.zeros((halo, hg_ref.shape[1]), F32)
        hu_ref[0:halo, :] = jnp.zeros((halo, hu_ref.shape[1]), F32)

    x = x_ref[...]
    hg_ref[halo:halo + tm, :] = jnp.dot(x, wgb_ref[...], preferred_element_type=F32)
    hu_ref[halo:halo + tm, :] = jnp.dot(x, wub_ref[...], preferred_element_type=F32)

    def conv(h_ref, cw_ref, cb_ref):
        out = cb_ref[...]
        for tap in range(CONV_WIDTH):
            start = halo - (CONV_WIDTH - 1) + tap
            out = out + cw_ref[tap:tap + 1, :] * h_ref[start:start + tm, :]
        return out

    g = conv(hg_ref, cwg_ref, cbg_ref)
    u = conv(hu_ref, cwu_ref, cbu_ref)
    a_ref[...] = (g * jax.nn.sigmoid(g) * u).astype(a_ref.dtype)
    hg_ref[0:halo, :] = hg_ref[tm:tm + halo, :]
    hu_ref[0:halo, :] = hu_ref[tm:tm + halo, :]


def _ffn_up(xb, w_up, conv_w, conv_b, layer, seq):
    m, k = xb.shape
    tm, tf = FFN_UP_TM, FFN_TF
    nf = D_FF // tf
    cb2 = conv_b.reshape(conv_b.shape[0], 1, 2 * D_FF)
    return pl.pallas_call(
        functools.partial(_ffn_up_kernel, seq // tm),
        grid=(nf, m // tm),
        in_specs=[pl.BlockSpec((tm, k), lambda j, i: (i, 0)),
                  pl.BlockSpec((None, k, tf), lambda j, i: (layer, 0, j)),
                  pl.BlockSpec((None, k, tf), lambda j, i: (layer, 0, j + nf)),
                  pl.BlockSpec((None, CONV_WIDTH, tf), lambda j, i: (layer, 0, j)),
                  pl.BlockSpec((None, CONV_WIDTH, tf), lambda j, i: (layer, 0, j + nf)),
                  pl.BlockSpec((None, 1, tf), lambda j, i: (layer, 0, j)),
                  pl.BlockSpec((None, 1, tf), lambda j, i: (layer, 0, j + nf))],
        out_specs=pl.BlockSpec((tm, tf), lambda j, i: (i, j)),
        out_shape=jax.ShapeDtypeStruct((m, D_FF), BF16),
        scratch_shapes=[pltpu.VMEM((k, tf), BF16), pltpu.VMEM((k, tf), BF16),
                        pltpu.VMEM((tm + 2 * SUBLANES, tf), F32), pltpu.VMEM((tm + 2 * SUBLANES, tf), F32)],
        compiler_params=_params("arbitrary", "arbitrary"),
        name="ffn_up",
    )(xb, w_up, w_up, conv_w, conv_w, cb2, cb2)


FFN_DOWN_TM = 256
FFN_DOWN_WCHUNK = 512


def _ffn_down_ln_kernel(alpha, n_load, emit_bf16, a_ref, xres_ref, w_ref, g_ref, b_ref, *rest):
    if emit_bf16:
        xo_ref, xob_ref, wb_ref = rest
    else:
        (xo_ref, wb_ref), xob_ref = rest, None
    step = pl.program_id(0)
    rows = w_ref.shape[0]

    @pl.when(step < n_load)
    def _():
        off = pl.multiple_of(step * rows, rows)
        wb_ref[pl.ds(off, rows), :] = w_ref[...].astype(BF16)

    @pl.when(step >= n_load)
    def _():
        y = jnp.dot(a_ref[...], wb_ref[...], preferred_element_type=F32)
        out = _deepnorm(xres_ref[...], y, g_ref[...], b_ref[...], alpha)
        xo_ref[...] = out
        if emit_bf16:
            xob_ref[...] = out.astype(BF16)


def _ffn_down_ln(a, xres, w, layer, g, b, alpha, emit_bf16, name):
    m, k = a.shape
    n = w.shape[2]
    tm, rows = FFN_DOWN_TM, FFN_DOWN_WCHUNK
    n_load = k // rows
    tile = lambda s: (jnp.maximum(s - n_load, 0), 0)
    out_specs = [pl.BlockSpec((tm, n), tile)]
    out_shape = [jax.ShapeDtypeStruct((m, n), F32)]
    if emit_bf16:
        out_specs.append(pl.BlockSpec((tm, n), tile))
        out_shape.append(jax.ShapeDtypeStruct((m, n), BF16))
    return pl.pallas_call(
        functools.partial(_ffn_down_ln_kernel, alpha, n_load, emit_bf16),
        grid=(n_load + m // tm,),
        in_specs=[pl.BlockSpec((tm, k), tile),
                  pl.BlockSpec((tm, n), tile),
                  pl.BlockSpec((None, rows, n), lambda s: (layer, jnp.minimum(s, n_load - 1), 0)),
                  pl.BlockSpec((1, n), lambda s: (0, 0)),
                  pl.BlockSpec((1, n), lambda s: (0, 0))],
        out_specs=out_specs,
        out_shape=out_shape,
        scratch_shapes=[pltpu.VMEM((k, n), BF16)],
        compiler_params=_params("arbitrary"),
        name=name,
    )(a, xres, w, g, b)


def kernel(x, a_w_qkv, a_w_o, a_sinks, rel_bias, b_w_qkvf, b_f_bias, b_w_o,
           ffn_w_up, ffn_conv_w, ffn_conv_b, ffn_w_down, ln_g, ln_b):
    batch, seq, d = x.shape
    depth = ffn_w_up.shape[0]
    alpha = (2 * depth) ** 0.25
    m = batch * seq
    xf = x.reshape(m, d)
    xb = None
    biasmask = _swa_bias(rel_bias)
    for layer in range(depth):
        j = layer // 2
        g = ln_g[layer].reshape(2, 1, d)
        bt = ln_b[layer].reshape(2, 1, d)
        if layer % 2 == 0:
            n_qkv = D_MODEL + 2 * A_KV_DIM
            src = xf if xb is None else xb
            qkv = _ws_matmul(src, a_w_qkv, j, n_qkv, n_qkv // 2, 512, "swa_qkv")
            o = _swa_attention(qkv.reshape(batch, seq, n_qkv), a_sinks[j], biasmask)
            w_o = a_w_o
        else:
            qkv = _ws_matmul(xb, b_w_qkvf, j, 3 * D_MODEL, 1024, 1024, "fox_qkv")
            cum = _fox_gate(xb, b_w_qkvf[j][:, 3 * D_MODEL:], b_f_bias[j].reshape(1, N_HEADS), batch, seq)
            o = _fox_attention(qkv.reshape(batch, seq, 3 * D_MODEL), cum)
            w_o = b_w_o
        xf, xb = _oproj_ln(o.reshape(m, d), xf, w_o, j, g[0], bt[0], alpha, 256, "oproj_ln")
        a = _ffn_up(xb, ffn_w_up, ffn_conv_w, ffn_conv_b, layer, seq)
        last = layer == depth - 1
        res = _ffn_down_ln(a, xf, ffn_w_down, layer, g[1], bt[1], alpha, not last, "ffn_down_ln")
        xf, xb = (res[0], None) if last else res
    return xf.reshape(batch, seq, d)
```

```python
import functools
import math

import jax
import jax.numpy as jnp
import numpy as np
from jax import lax
from jax.experimental import pallas as pl
from jax.experimental.pallas import tpu as pltpu

D_MODEL = 2048
HEAD_DIM = 64
BLOCK = 128
N_HEADS = D_MODEL // HEAD_DIM
A_KV_HEADS = 4
A_GROUP = N_HEADS // A_KV_HEADS
A_KV_DIM = A_KV_HEADS * HEAD_DIM
REL_BUCKETS = 32
REL_MAX_DIST = 128
D_FF = 5632
LN_EPS = 1e-5
SCALE = HEAD_DIM ** -0.5

SUBLANES = 8
LANES = 128

BF16 = jnp.bfloat16
F32 = jnp.float32
NEG_INF = float("-inf")

_NT_DIMS = (((1,), (1,)), ((), ()))


def _params(*sem):
    return pltpu.CompilerParams(dimension_semantics=sem)


def _ws_matmul_kernel(x_ref, w_ref, o_ref, wb_ref):
    @pl.when(pl.program_id(1) == 0)
    def _():
        wb_ref[...] = w_ref[...].astype(BF16)

    x = x_ref[...].astype(BF16)
    o_ref[...] = jnp.dot(x, wb_ref[...], preferred_element_type=F32).astype(o_ref.dtype)


def _ws_matmul(x, w, layer, n_out, tn, tm, name):
    m, k = x.shape
    return pl.pallas_call(
        _ws_matmul_kernel,
        grid=(n_out // tn, m // tm),
        in_specs=[pl.BlockSpec((tm, k), lambda j, i: (i, 0)),
                  pl.BlockSpec((None, k, tn), lambda j, i: (layer, 0, j))],
        out_specs=pl.BlockSpec((tm, tn), lambda j, i: (i, j)),
        out_shape=jax.ShapeDtypeStruct((m, n_out), BF16),
        scratch_shapes=[pltpu.VMEM((k, tn), BF16)],
        compiler_params=_params("arbitrary", "arbitrary"),
        name=name,
    )(x, w)


def _t5_bucket_map():
    qi = np.arange(BLOCK)[:, None]
    kj = np.arange(2 * BLOCK)[None, :]
    dist = BLOCK + qi - kj
    n = np.maximum(dist, 0)
    max_exact = REL_BUCKETS // 2
    nf = np.maximum(n, 1).astype(np.float32)
    large = max_exact + (np.log(nf / max_exact) / math.log(REL_MAX_DIST / max_exact)
                         * (REL_BUCKETS - max_exact)).astype(np.int32)
    large = np.minimum(large, REL_BUCKETS - 1)
    bucket = np.where(n < max_exact, n, large)
    visible = (dist >= 0) & (dist < BLOCK)
    return np.where(visible, bucket, -1).astype(np.int32)


def _swa_bias_kernel(rb_ref, bk_ref, o_ref):
    bk = bk_ref[...]

    def head(h, _):
        acc = jnp.full(bk.shape, NEG_INF, F32)
        for b in range(REL_BUCKETS):
            acc = jnp.where(bk == b, rb_ref[b, h], acc)
        o_ref[h] = acc
        return 0

    lax.fori_loop(0, N_HEADS, head, 0)


def _swa_bias(rel_bias):
    bk = jnp.asarray(_t5_bucket_map())
    return pl.pallas_call(
        _swa_bias_kernel,
        in_specs=[pl.BlockSpec(memory_space=pltpu.SMEM),
                  pl.BlockSpec(memory_space=pltpu.VMEM)],
        out_specs=pl.BlockSpec(memory_space=pltpu.VMEM),
        out_shape=jax.ShapeDtypeStruct((N_HEADS, BLOCK, 2 * BLOCK), F32),
        name="swa_bias",
    )(rel_bias, bk)


def _swa_kernel(sink_ref, q_ref, kp_ref, kc_ref, vp_ref, vc_ref, bm_ref, o_ref):
    first = pl.program_id(1) == 0
    col = lax.broadcasted_iota(jnp.int32, (BLOCK, 2 * BLOCK), 1)
    kill = jnp.logical_and(first, col < BLOCK)
    for kh in range(A_KV_HEADS):
        ks = slice(kh * HEAD_DIM, (kh + 1) * HEAD_DIM)
        k2 = jnp.concatenate([kp_ref[:, ks], kc_ref[:, ks]], axis=0)
        v2 = jnp.concatenate([vp_ref[:, ks], vc_ref[:, ks]], axis=0)
        for g in range(A_GROUP):
            h = kh * A_GROUP + g
            hs = slice(h * HEAD_DIM, (h + 1) * HEAD_DIM)
            q = q_ref[:, hs] * SCALE
            s = lax.dot_general(q, k2, _NT_DIMS, preferred_element_type=F32)
            s = jnp.where(kill, NEG_INF, s + bm_ref[h])
            sink = sink_ref[h]
            m = jnp.maximum(jnp.max(s, axis=-1, keepdims=True), sink)
            p = jnp.exp(s - m)
            denom = jnp.sum(p, axis=-1, keepdims=True) + jnp.exp(sink - m)
            o = jnp.dot(p.astype(BF16), v2, preferred_element_type=F32)
            o_ref[:, hs] = (o / denom).astype(o_ref.dtype)


def _swa_attention(qkv, sinks, biasmask):
    b, s, _ = qkv.shape
    nb = s // BLOCK
    kcol = D_MODEL // A_KV_DIM
    prev = lambda n: jnp.maximum(n - 1, 0)
    return pl.pallas_call(
        _swa_kernel,
        grid=(b, nb),
        in_specs=[pl.BlockSpec(memory_space=pltpu.SMEM),
                  pl.BlockSpec((None, BLOCK, D_MODEL), lambda bi, n: (bi, n, 0)),
                  pl.BlockSpec((None, BLOCK, A_KV_DIM), lambda bi, n: (bi, prev(n), kcol)),
                  pl.BlockSpec((None, BLOCK, A_KV_DIM), lambda bi, n: (bi, n, kcol)),
                  pl.BlockSpec((None, BLOCK, A_KV_DIM), lambda bi, n: (bi, prev(n), kcol + 1)),
                  pl.BlockSpec((None, BLOCK, A_KV_DIM), lambda bi, n: (bi, n, kcol + 1)),
                  pl.BlockSpec((N_HEADS, BLOCK, 2 * BLOCK), lambda bi, n: (0, 0, 0))],
        out_specs=pl.BlockSpec((None, BLOCK, D_MODEL), lambda bi, n: (bi, n, 0)),
        out_shape=jax.ShapeDtypeStruct((b, s, D_MODEL), BF16),
        compiler_params=_params("arbitrary", "arbitrary"),
        name="swa_attn",
    )(sinks, qkv, qkv, qkv, qkv, qkv, biasmask)


FOX_CUM_BLOCK = 256
FOX_TQ = 256
FOX_HEADS_PER_STEP = LANES // HEAD_DIM
FOX_PIECES = 3
BF16_ONE_PAIR = 0x3F803F80


def _fox_ext_lane(head, piece):
    pair, odd = divmod(head, FOX_HEADS_PER_STEP)
    return pair * LANES + (1 - odd) * HEAD_DIM + piece


def _fox_placement():
    pm = np.zeros((FOX_PIECES * N_HEADS, D_MODEL), np.float32)
    for t in range(FOX_PIECES):
        for h in range(N_HEADS):
            pm[t * N_HEADS + h, _fox_ext_lane(h, t)] = -1.0
    return pm


def _fox_lane_words():
    w = np.zeros((SUBLANES, LANES), np.uint32)
    for hh in range(FOX_HEADS_PER_STEP):
        own = np.zeros(LANES, bool)
        own[hh * HEAD_DIM:(hh + 1) * HEAD_DIM] = True
        ext = np.zeros(LANES, bool)
        first = _fox_ext_lane(hh, 0)
        ext[first:first + FOX_PIECES] = True
        w[3 * hh + 0] = np.where(own, 0xFFFFFFFF, 0)
        w[3 * hh + 1] = np.where(ext, 0xFFFFFFFF, 0)
        w[3 * hh + 2] = np.where(ext, BF16_ONE_PAIR, 0)
    return w


def _log_sigmoid(x):
    return jnp.minimum(x, 0.0) - jnp.log1p(jnp.exp(-jnp.abs(x)))


def _round_bf16(x):
    return x.astype(BF16).astype(F32)


def _fox_gate_kernel(x_ref, wf_ref, fb_ref, pm_ref, cum_ref, kext_ref):
    fz = jnp.dot(x_ref[...], wf_ref[...].astype(BF16), preferred_element_type=F32) + fb_ref[...]
    lf = _log_sigmoid(fz)
    cb = FOX_CUM_BLOCK
    row = lax.broadcasted_iota(jnp.int32, (cb, cb), 0)
    col = lax.broadcasted_iota(jnp.int32, (cb, cb), 1)
    tri = jnp.where(row >= col, 1.0, 0.0).astype(BF16)
    group = lax.broadcasted_iota(jnp.int32, (cb, lf.shape[1]), 1) // N_HEADS
    pm = pm_ref[...]
    carry = jnp.zeros((1, lf.shape[1]), F32)
    for blk in range(lf.shape[0] // cb):
        rows = slice(blk * cb, (blk + 1) * cb)
        x = lf[rows]
        hi = _round_bf16(x)
        r1 = x - hi
        mid = _round_bf16(r1)
        c = (jnp.dot(tri, hi.astype(BF16), preferred_element_type=F32)
             + jnp.dot(tri, mid.astype(BF16), preferred_element_type=F32)
             + jnp.dot(tri, (r1 - mid).astype(BF16), preferred_element_type=F32)) + carry
        carry = c[cb - 1:cb]
        cum_ref[rows, :] = c[:, :N_HEADS]
        chi = _round_bf16(c)
        cr1 = c - chi
        cmid = _round_bf16(cr1)
        piece = jnp.where(group == 0, chi, jnp.where(group == 1, cmid, cr1 - cmid)).astype(BF16)
        kext_ref[rows, :] = jnp.dot(piece, pm, preferred_element_type=F32).astype(BF16)


def _fox_gate(xb, wf3, fb3, batch, seq):
    pm = jnp.asarray(_fox_placement(), BF16)
    n3 = FOX_PIECES * N_HEADS
    return pl.pallas_call(
        _fox_gate_kernel,
        grid=(batch,),
        in_specs=[pl.BlockSpec((seq, D_MODEL), lambda bi: (bi, 0)),
                  pl.BlockSpec((D_MODEL, n3), lambda bi: (0, 0)),
                  pl.BlockSpec((1, n3), lambda bi: (0, 0)),
                  pl.BlockSpec((n3, D_MODEL), lambda bi: (0, 0))],
        out_specs=[pl.BlockSpec((None, seq, N_HEADS), lambda bi: (bi, 0, 0)),
                   pl.BlockSpec((None, seq, D_MODEL), lambda bi: (bi, 0, 0))],
        out_shape=[jax.ShapeDtypeStruct((batch, seq, N_HEADS), F32),
                   jax.ShapeDtypeStruct((batch, seq, D_MODEL), BF16)],
        compiler_params=_params("arbitrary"),
        name="fox_gate",
    )(xb, wf3, fb3, pm)


def _fox_kernel(q_ref, k_ref, v_ref, kext_ref, cb_ref, lw_ref, o_ref, qa_ref, ka_ref, va_ref):
    seq = q_ref.shape[0]
    tq = FOX_TQ
    row = lax.broadcasted_iota(jnp.int32, (tq, tq), 0)
    col = lax.broadcasted_iota(jnp.int32, (tq, tq), 1)
    causal = col <= row
    words = lambda x: pltpu.bitcast(x, jnp.uint32)
    q = words(q_ref[...] * SCALE)
    k = words(k_ref[...])
    v = words(v_ref[...])
    kext = words(kext_ref[...])
    for hh in range(FOX_HEADS_PER_STEP):
        own = lw_ref[3 * hh + 0:3 * hh + 1, :]
        ext = lw_ref[3 * hh + 1:3 * hh + 2, :]
        ones = lw_ref[3 * hh + 2:3 * hh + 3, :]
        qa_ref[hh] = pltpu.bitcast((q & own) | ones, BF16)
        ka_ref[hh] = pltpu.bitcast((k & ~ext) | (kext & ext), BF16)
        va_ref[hh] = pltpu.bitcast(v & own, BF16)

    for i in range(seq // tq):
        r0 = i * tq
        out = None
        for hh in range(FOX_HEADS_PER_STEP):
            qa = qa_ref[hh, r0:r0 + tq, :]
            cbq = cb_ref[r0:r0 + tq, hh:hh + 1]
            s_d = lax.dot_general(qa, ka_ref[hh, r0:r0 + tq, :], _NT_DIMS, preferred_element_type=F32)
            s_d = jnp.where(causal, s_d, NEG_INF)
            m = jnp.max(s_d, axis=-1, keepdims=True)
            if i > 0:
                s_o = lax.dot_general(qa, ka_ref[hh, 0:r0, :], _NT_DIMS, preferred_element_type=F32)
                m = jnp.maximum(m, jnp.max(s_o, axis=-1, keepdims=True))
            shift = cbq - (m + cbq)
            p_d = jnp.exp(s_d + shift)
            l = jnp.sum(p_d, axis=-1, keepdims=True)
            o = jnp.dot(p_d.astype(BF16), va_ref[hh, r0:r0 + tq, :], preferred_element_type=F32)
            if i > 0:
                p_o = jnp.exp(s_o + shift)
                l = l + jnp.sum(p_o, axis=-1, keepdims=True)
                o = o + jnp.dot(p_o.astype(BF16), va_ref[hh, 0:r0, :], preferred_element_type=F32)
            o = o / l
            out = o if out is None else out + o
        o_ref[r0:r0 + tq, :] = out.astype(o_ref.dtype)


def _fox_attention(qkv, cum, kext):
    b, s, _ = qkv.shape
    hps = FOX_HEADS_PER_STEP
    npair = N_HEADS // hps
    cb = cum.reshape(b, s, npair, hps).transpose(0, 2, 1, 3)
    lw = jnp.asarray(_fox_lane_words())
    ncol = D_MODEL // LANES
    return pl.pallas_call(
        _fox_kernel,
        grid=(b, npair),
        in_specs=[pl.BlockSpec((None, s, LANES), lambda bi, hp: (bi, 0, hp)),
                  pl.BlockSpec((None, s, LANES), lambda bi, hp: (bi, 0, ncol + hp)),
                  pl.BlockSpec((None, s, LANES), lambda bi, hp: (bi, 0, 2 * ncol + hp)),
                  pl.BlockSpec((None, s, LANES), lambda bi, hp: (bi, 0, hp)),
                  pl.BlockSpec((None, None, s, hps), lambda bi, hp: (bi, hp, 0, 0)),
                  pl.BlockSpec((SUBLANES, LANES), lambda bi, hp: (0, 0))],
        out_specs=pl.BlockSpec((None, s, LANES), lambda bi, hp: (bi, 0, hp)),
        out_shape=jax.ShapeDtypeStruct((b, s, D_MODEL), BF16),
        scratch_shapes=[pltpu.VMEM((hps, s, LANES), BF16)] * 3,
        compiler_params=_params("arbitrary", "arbitrary"),
        name="fox_attn",
    )(qkv, qkv, qkv, kext, cb, lw)


def _deepnorm(xres, y, g, b, alpha):
    z = alpha * xres + y
    mu = jnp.mean(z, axis=-1, keepdims=True)
    zc = z - mu
    var = jnp.mean(zc * zc, axis=-1, keepdims=True)
    return zc * lax.rsqrt(var + LN_EPS) * g + b


def _oproj_ln_kernel(alpha, o_ref, xres_ref, w_ref, g_ref, b_ref, xo_ref, xob_ref, wb_ref):
    @pl.when(pl.program_id(0) == 0)
    def _():
        wb_ref[...] = w_ref[...].astype(BF16)

    y = jnp.dot(o_ref[...], wb_ref[...], preferred_element_type=F32)
    out = _deepnorm(xres_ref[...], y, g_ref[...], b_ref[...], alpha)
    xo_ref[...] = out
    xob_ref[...] = out.astype(BF16)


def _oproj_ln(o, xres, w, layer, g, b, alpha, tm, name):
    m, k = o.shape
    n = w.shape[2]
    return pl.pallas_call(
        functools.partial(_oproj_ln_kernel, alpha),
        grid=(m // tm,),
        in_specs=[pl.BlockSpec((tm, k), lambda i: (i, 0)),
                  pl.BlockSpec((tm, n), lambda i: (i, 0)),
                  pl.BlockSpec((None, k, n), lambda i: (layer, 0, 0), pipeline_mode=pl.Buffered(1)),
                  pl.BlockSpec((1, n), lambda i: (0, 0)),
                  pl.BlockSpec((1, n), lambda i: (0, 0))],
        out_specs=[pl.BlockSpec((tm, n), lambda i: (i, 0)),
                   pl.BlockSpec((tm, n), lambda i: (i, 0))],
        out_shape=[jax.ShapeDtypeStruct((m, n), F32), jax.ShapeDtypeStruct((m, n), BF16)],
        scratch_shapes=[pltpu.VMEM((k, n), BF16)],
        compiler_params=_params("arbitrary"),
        name=name,
    )(o, xres, w, g, b)


CONV_WIDTH = 3
FFN_TF = 512
FFN_UP_TM = 1024


def _ffn_up_kernel(tiles_per_seq, x_ref, wg_ref, wu_ref, cwg_ref, cwu_ref, cbg_ref, cbu_ref,
                   a_ref, wgb_ref, wub_ref, hg_ref, hu_ref):
    i = pl.program_id(1)
    tm = x_ref.shape[0]
    halo = SUBLANES

    @pl.when(i == 0)
    def _():
        wgb_ref[...] = wg_ref[...].astype(BF16)
        wub_ref[...] = wu_ref[...].astype(BF16)

    @pl.when(i % tiles_per_seq == 0)
    def _():
        hg_ref[0:halo, :] = jnp.zeros((halo, hg_ref.shape[1]), F32)
        hu_ref[0:halo, :] = jnp.zeros((halo, hu_ref.shape[1]), F32)

    x = x_ref[...]
    hg_ref[halo:halo + tm, :] = jnp.dot(x, wgb_ref[...], preferred_element_type=F32)
    hu_ref[halo:halo + tm, :] = jnp.dot(x, wub_ref[...], preferred_element_type=F32)

    def conv(h_ref, cw_ref, cb_ref):
        out = cb_ref[...]
        for tap in range(CONV_WIDTH):
            start = halo - (CONV_WIDTH - 1) + tap
            out = out + cw_ref[tap:tap + 1, :] * h_ref[start:start + tm, :]
        return out

    g = conv(hg_ref, cwg_ref, cbg_ref)
    u = conv(hu_ref, cwu_ref, cbu_ref)
    a_ref[...] = (g * jax.nn.sigmoid(g) * u).astype(a_ref.dtype)
    hg_ref[0:halo, :] = hg_ref[tm:tm + halo, :]
    hu_ref[0:halo, :] = hu_ref[tm:tm + halo, :]


def _ffn_up(xb, w_up, conv_w, conv_b, layer, seq):
    m, k = xb.shape
    tm, tf = FFN_UP_TM, FFN_TF
    nf = D_FF // tf
    cb2 = conv_b.reshape(conv_b.shape[0], 1, 2 * D_FF)
    return pl.pallas_call(
        functools.partial(_ffn_up_kernel, seq // tm),
        grid=(nf, m // tm),
        in_specs=[pl.BlockSpec((tm, k), lambda j, i: (i, 0)),
                  pl.BlockSpec((None, k, tf), lambda j, i: (layer, 0, j)),
                  pl.BlockSpec((None, k, tf), lambda j, i: (layer, 0, j + nf)),
                  pl.BlockSpec((None, CONV_WIDTH, tf), lambda j, i: (layer, 0, j)),
                  pl.BlockSpec((None, CONV_WIDTH, tf), lambda j, i: (layer, 0, j + nf)),
                  pl.BlockSpec((None, 1, tf), lambda j, i: (layer, 0, j)),
                  pl.BlockSpec((None, 1, tf), lambda j, i: (layer, 0, j + nf))],
        out_specs=pl.BlockSpec((tm, tf), lambda j, i: (i, j)),
        out_shape=jax.ShapeDtypeStruct((m, D_FF), BF16),
        scratch_shapes=[pltpu.VMEM((k, tf), BF16), pltpu.VMEM((k, tf), BF16),
                        pltpu.VMEM((tm + 2 * SUBLANES, tf), F32), pltpu.VMEM((tm + 2 * SUBLANES, tf), F32)],
        compiler_params=_params("arbitrary", "arbitrary"),
        name="ffn_up",
    )(xb, w_up, w_up, conv_w, conv_w, cb2, cb2)


FFN_DOWN_TM = 256
FFN_DOWN_WCHUNK = 512


def _ffn_down_ln_kernel(alpha, n_load, emit_bf16, a_ref, xres_ref, w_ref, g_ref, b_ref, *rest):
    if emit_bf16:
        xo_ref, xob_ref, wb_ref = rest
    else:
        (xo_ref, wb_ref), xob_ref = rest, None
    step = pl.program_id(0)
    rows = w_ref.shape[0]

    @pl.when(step < n_load)
    def _():
        off = pl.multiple_of(step * rows, rows)
        wb_ref[pl.ds(off, rows), :] = w_ref[...].astype(BF16)

    @pl.when(step >= n_load)
    def _():
        y = jnp.dot(a_ref[...], wb_ref[...], preferred_element_type=F32)
        out = _deepnorm(xres_ref[...], y, g_ref[...], b_ref[...], alpha)
        xo_ref[...] = out
        if emit_bf16:
            xob_ref[...] = out.astype(BF16)


def _ffn_down_ln(a, xres, w, layer, g, b, alpha, emit_bf16, name):
    m, k = a.shape
    n = w.shape[2]
    tm, rows = FFN_DOWN_TM, FFN_DOWN_WCHUNK
    n_load = k // rows
    tile = lambda s: (jnp.maximum(s - n_load, 0), 0)
    out_specs = [pl.BlockSpec((tm, n), tile)]
    out_shape = [jax.ShapeDtypeStruct((m, n), F32)]
    if emit_bf16:
        out_specs.append(pl.BlockSpec((tm, n), tile))
        out_shape.append(jax.ShapeDtypeStruct((m, n), BF16))
    return pl.pallas_call(
        functools.partial(_ffn_down_ln_kernel, alpha, n_load, emit_bf16),
        grid=(n_load + m // tm,),
        in_specs=[pl.BlockSpec((tm, k), tile),
                  pl.BlockSpec((tm, n), tile),
                  pl.BlockSpec((None, rows, n), lambda s: (layer, jnp.minimum(s, n_load - 1), 0)),
                  pl.BlockSpec((1, n), lambda s: (0, 0)),
                  pl.BlockSpec((1, n), lambda s: (0, 0))],
        out_specs=out_specs,
        out_shape=out_shape,
        scratch_shapes=[pltpu.VMEM((k, n), BF16)],
        compiler_params=_params("arbitrary"),
        name=name,
    )(a, xres, w, g, b)


def kernel(x, a_w_qkv, a_w_o, a_sinks, rel_bias, b_w_qkvf, b_f_bias, b_w_o,
           ffn_w_up, ffn_conv_w, ffn_conv_b, ffn_w_down, ln_g, ln_b):
    batch, seq, d = x.shape
    depth = ffn_w_up.shape[0]
    alpha = (2 * depth) ** 0.25
    m = batch * seq
    xf = x.reshape(m, d)
    xb = None
    biasmask = _swa_bias(rel_bias)
    for layer in range(depth):
        j = layer // 2
        g = ln_g[layer].reshape(2, 1, d)
        bt = ln_b[layer].reshape(2, 1, d)
        if layer % 2 == 0:
            n_qkv = D_MODEL + 2 * A_KV_DIM
            src = xf if xb is None else xb
            qkv = _ws_matmul(src, a_w_qkv, j, n_qkv, n_qkv // 2, 512, "swa_qkv")
            o = _swa_attention(qkv.reshape(batch, seq, n_qkv), a_sinks[j], biasmask)
            w_o = a_w_o
        else:
            qkv = _ws_matmul(xb, b_w_qkvf, j, 3 * D_MODEL, 1024, 1024, "fox_qkv")
            wf3 = jnp.tile(b_w_qkvf[j][:, 3 * D_MODEL:], (1, FOX_PIECES))
            fb3 = jnp.tile(b_f_bias[j].reshape(1, N_HEADS), (1, FOX_PIECES))
            cum, kext = _fox_gate(xb, wf3, fb3, batch, seq)
            o = _fox_attention(qkv.reshape(batch, seq, 3 * D_MODEL), cum, kext)
            w_o = b_w_o
        xf, xb = _oproj_ln(o.reshape(m, d), xf, w_o, j, g[0], bt[0], alpha, 256, "oproj_ln")
        a = _ffn_up(xb, ffn_w_up, ffn_conv_w, ffn_conv_b, layer, seq)
        last = layer == depth - 1
        res = _ffn_down_ln(a, xf, ffn_w_down, layer, g[1], bt[1], alpha, not last, "ffn_down_ln")
        xf, xb = (res[0], None) if last else res
    return xf.reshape(batch, seq, d)
```

```python
import functools
import math

import jax
import jax.numpy as jnp
import numpy as np
from jax import lax
from jax.experimental import pallas as pl
from jax.experimental.pallas import tpu as pltpu

D_MODEL = 2048
HEAD_DIM = 64
BLOCK = 128
N_HEADS = D_MODEL // HEAD_DIM
A_KV_HEADS = 4
A_GROUP = N_HEADS // A_KV_HEADS
A_KV_DIM = A_KV_HEADS * HEAD_DIM
REL_BUCKETS = 32
REL_MAX_DIST = 128
D_FF = 5632
LN_EPS = 1e-5
SCALE = HEAD_DIM ** -0.5

SUBLANES = 8
LANES = 128

BF16 = jnp.bfloat16
F32 = jnp.float32
NEG_INF = float("-inf")

_NT_DIMS = (((1,), (1,)), ((), ()))


def _params(*sem):
    return pltpu.CompilerParams(dimension_semantics=sem)


def _ws_matmul_kernel(x_ref, w_ref, o_ref, wb_ref):
    @pl.when(pl.program_id(1) == 0)
    def _():
        wb_ref[...] = w_ref[...].astype(BF16)

    x = x_ref[...].astype(BF16)
    o_ref[...] = jnp.dot(x, wb_ref[...], preferred_element_type=F32).astype(o_ref.dtype)


def _ws_matmul(x, w, layer, n_out, tn, tm, name):
    m, k = x.shape
    return pl.pallas_call(
        _ws_matmul_kernel,
        grid=(n_out // tn, m // tm),
        in_specs=[pl.BlockSpec((tm, k), lambda j, i: (i, 0)),
                  pl.BlockSpec((None, k, tn), lambda j, i: (layer, 0, j))],
        out_specs=pl.BlockSpec((tm, tn), lambda j, i: (i, j)),
        out_shape=jax.ShapeDtypeStruct((m, n_out), BF16),
        scratch_shapes=[pltpu.VMEM((k, tn), BF16)],
        compiler_params=_params("arbitrary", "arbitrary"),
        name=name,
    )(x, w)


def _t5_bucket_map():
    qi = np.arange(BLOCK)[None, :]
    kj = np.arange(2 * BLOCK)[:, None]
    dist = BLOCK + qi - kj
    n = np.maximum(dist, 0)
    max_exact = REL_BUCKETS // 2
    nf = np.maximum(n, 1).astype(np.float32)
    large = max_exact + (np.log(nf / max_exact) / math.log(REL_MAX_DIST / max_exact)
                         * (REL_BUCKETS - max_exact)).astype(np.int32)
    large = np.minimum(large, REL_BUCKETS - 1)
    bucket = np.where(n < max_exact, n, large)
    visible = (dist >= 0) & (dist < BLOCK)
    return np.where(visible, bucket, -1).astype(np.int32)


def _swa_bias_kernel(rb_ref, bk_ref, o_ref):
    bk = bk_ref[...]
    key = lax.broadcasted_iota(jnp.int32, bk.shape, 0)

    def head(h, _):
        acc = jnp.full(bk.shape, NEG_INF, F32)
        for b in range(REL_BUCKETS):
            acc = jnp.where(bk == b, rb_ref[b, h], acc)
        o_ref[0, h] = acc
        o_ref[1, h] = jnp.where(key < BLOCK, NEG_INF, acc)
        return 0

    lax.fori_loop(0, N_HEADS, head, 0)


def _swa_bias(rel_bias):
    bk = jnp.asarray(_t5_bucket_map())
    return pl.pallas_call(
        _swa_bias_kernel,
        in_specs=[pl.BlockSpec(memory_space=pltpu.SMEM),
                  pl.BlockSpec(memory_space=pltpu.VMEM)],
        out_specs=pl.BlockSpec(memory_space=pltpu.VMEM),
        out_shape=jax.ShapeDtypeStruct((2, N_HEADS, 2 * BLOCK, BLOCK), F32),
        name="swa_bias",
    )(rel_bias, bk)


_TN_DIMS = (((0,), (0,)), ((), ()))


def _swa_kernel(sink_ref, q_ref, kp_ref, kc_ref, vp_ref, vc_ref, bm_ref, o_ref,
                qs_ref, s_ref, p_ref, linv_ref):
    for kh in range(A_KV_HEADS):
        ks = slice(kh * HEAD_DIM, (kh + 1) * HEAD_DIM)
        for g in range(A_GROUP):
            h = kh * A_GROUP + g
            qs_ref[kh, g * BLOCK:(g + 1) * BLOCK, :] = q_ref[:, h * HEAD_DIM:(h + 1) * HEAD_DIM] * SCALE
        k2 = jnp.concatenate([kp_ref[:, ks], kc_ref[:, ks]], axis=0)
        s_ref[kh] = lax.dot_general(k2, qs_ref[kh], _NT_DIMS, preferred_element_type=F32)
    for kh in range(A_KV_HEADS):
        for g in range(A_GROUP):
            h = kh * A_GROUP + g
            cols = slice(g * BLOCK, (g + 1) * BLOCK)
            s = s_ref[kh, :, cols] + bm_ref[h]
            sink = sink_ref[h]
            m = jnp.maximum(jnp.max(s, axis=0, keepdims=True), sink)
            p = jnp.exp(s - m)
            denom = jnp.sum(p, axis=0, keepdims=True) + jnp.exp(sink - m)
            p_ref[kh, :, cols] = p.astype(BF16)
            linv_ref[kh, :, cols] = 1.0 / denom
    for kh in range(A_KV_HEADS):
        ks = slice(kh * HEAD_DIM, (kh + 1) * HEAD_DIM)
        v2 = jnp.concatenate([vp_ref[:, ks], vc_ref[:, ks]], axis=0)
        ot = lax.dot_general(v2, p_ref[kh], _TN_DIMS, preferred_element_type=F32) * linv_ref[kh]
        for g in range(A_GROUP):
            h = kh * A_GROUP + g
            o_ref[:, h * HEAD_DIM:(h + 1) * HEAD_DIM] = ot[:, g * BLOCK:(g + 1) * BLOCK].T.astype(o_ref.dtype)


def _swa_attention(qkv, sinks, biasmask):
    b, s, _ = qkv.shape
    nb = s // BLOCK
    gq = A_GROUP * BLOCK
    kcol = D_MODEL // A_KV_DIM
    prev = lambda n: jnp.maximum(n - 1, 0)
    return pl.pallas_call(
        _swa_kernel,
        grid=(b, nb),
        in_specs=[pl.BlockSpec(memory_space=pltpu.SMEM),
                  pl.BlockSpec((None, BLOCK, D_MODEL), lambda bi, n: (bi, n, 0)),
                  pl.BlockSpec((None, BLOCK, A_KV_DIM), lambda bi, n: (bi, prev(n), kcol)),
                  pl.BlockSpec((None, BLOCK, A_KV_DIM), lambda bi, n: (bi, n, kcol)),
                  pl.BlockSpec((None, BLOCK, A_KV_DIM), lambda bi, n: (bi, prev(n), kcol + 1)),
                  pl.BlockSpec((None, BLOCK, A_KV_DIM), lambda bi, n: (bi, n, kcol + 1)),
                  pl.BlockSpec((None, N_HEADS, 2 * BLOCK, BLOCK),
                               lambda bi, n: (jnp.where(n == 0, 1, 0), 0, 0, 0))],
        out_specs=pl.BlockSpec((None, BLOCK, D_MODEL), lambda bi, n: (bi, n, 0)),
        out_shape=jax.ShapeDtypeStruct((b, s, D_MODEL), BF16),
        scratch_shapes=[pltpu.VMEM((A_KV_HEADS, gq, HEAD_DIM), BF16),
                        pltpu.VMEM((A_KV_HEADS, 2 * BLOCK, gq), F32),
                        pltpu.VMEM((A_KV_HEADS, 2 * BLOCK, gq), BF16),
                        pltpu.VMEM((A_KV_HEADS, 1, gq), F32)],
        compiler_params=_params("arbitrary", "arbitrary"),
        name="swa_attn",
    )(sinks, qkv, qkv, qkv, qkv, qkv, biasmask)


FOX_CUM_BLOCK = 256
FOX_TQ = 256
FOX_HEADS_PER_STEP = LANES // HEAD_DIM
FOX_PIECES = 3
BF16_ONE_PAIR = 0x3F803F80


def _fox_ext_lane(head, piece):
    pair, odd = divmod(head, FOX_HEADS_PER_STEP)
    return pair * LANES + (1 - odd) * HEAD_DIM + piece


def _fox_placement():
    pm = np.zeros((FOX_PIECES * N_HEADS, D_MODEL), np.float32)
    for t in range(FOX_PIECES):
        for h in range(N_HEADS):
            pm[t * N_HEADS + h, _fox_ext_lane(h, t)] = -1.0
    return pm


def _fox_lane_words():
    w = np.zeros((SUBLANES, LANES), np.uint32)
    for hh in range(FOX_HEADS_PER_STEP):
        own = np.zeros(LANES, bool)
        own[hh * HEAD_DIM:(hh + 1) * HEAD_DIM] = True
        ext = np.zeros(LANES, bool)
        first = _fox_ext_lane(hh, 0)
        ext[first:first + FOX_PIECES] = True
        w[3 * hh + 0] = np.where(own, 0xFFFFFFFF, 0)
        w[3 * hh + 1] = np.where(ext, 0xFFFFFFFF, 0)
        w[3 * hh + 2] = np.where(ext, BF16_ONE_PAIR, 0)
    return w


def _log_sigmoid(x):
    return jnp.minimum(x, 0.0) - jnp.log1p(jnp.exp(-jnp.abs(x)))


def _round_bf16(x):
    return x.astype(BF16).astype(F32)


def _fox_gate_kernel(x_ref, wf_ref, fb_ref, pm_ref, cum_ref, kext_ref, w3_ref):
    @pl.when(pl.program_id(0) == 0)
    def _():
        w = wf_ref[:, :N_HEADS].astype(BF16)
        for t in range(FOX_PIECES):
            w3_ref[:, t * N_HEADS:(t + 1) * N_HEADS] = w

    fz = jnp.dot(x_ref[...], w3_ref[...], preferred_element_type=F32) + fb_ref[...]
    lf = _log_sigmoid(fz)
    cb = FOX_CUM_BLOCK
    row = lax.broadcasted_iota(jnp.int32, (cb, cb), 0)
    col = lax.broadcasted_iota(jnp.int32, (cb, cb), 1)
    tri = jnp.where(row >= col, 1.0, 0.0).astype(BF16)
    group = lax.broadcasted_iota(jnp.int32, (cb, lf.shape[1]), 1) // N_HEADS
    pm = pm_ref[...]
    carry = jnp.zeros((1, lf.shape[1]), F32)
    for blk in range(lf.shape[0] // cb):
        rows = slice(blk * cb, (blk + 1) * cb)
        x = lf[rows]
        hi = _round_bf16(x)
        r1 = x - hi
        mid = _round_bf16(r1)
        c = (jnp.dot(tri, hi.astype(BF16), preferred_element_type=F32)
             + jnp.dot(tri, mid.astype(BF16), preferred_element_type=F32)
             + jnp.dot(tri, (r1 - mid).astype(BF16), preferred_element_type=F32)) + carry
        carry = c[cb - 1:cb]
        cum_ref[rows, :] = c[:, :N_HEADS]
        chi = _round_bf16(c)
        cr1 = c - chi
        cmid = _round_bf16(cr1)
        piece = jnp.where(group == 0, chi, jnp.where(group == 1, cmid, cr1 - cmid)).astype(BF16)
        kext_ref[rows, :] = jnp.dot(piece, pm, preferred_element_type=F32).astype(BF16)


def _fox_gate(xb, w_qkvf, layer, fb3, batch, seq):
    pm = jnp.asarray(_fox_placement(), BF16)
    n3 = FOX_PIECES * N_HEADS
    gate_col = 3 * D_MODEL // LANES
    return pl.pallas_call(
        _fox_gate_kernel,
        grid=(batch,),
        in_specs=[pl.BlockSpec((seq, D_MODEL), lambda bi: (bi, 0)),
                  pl.BlockSpec((None, D_MODEL, LANES), lambda bi: (layer, 0, gate_col)),
                  pl.BlockSpec((1, n3), lambda bi: (0, 0)),
                  pl.BlockSpec((n3, D_MODEL), lambda bi: (0, 0))],
        out_specs=[pl.BlockSpec((None, seq, N_HEADS), lambda bi: (bi, 0, 0)),
                   pl.BlockSpec((None, seq, D_MODEL), lambda bi: (bi, 0, 0))],
        out_shape=[jax.ShapeDtypeStruct((batch, seq, N_HEADS), F32),
                   jax.ShapeDtypeStruct((batch, seq, D_MODEL), BF16)],
        scratch_shapes=[pltpu.VMEM((D_MODEL, n3), BF16)],
        compiler_params=_params("arbitrary"),
        name="fox_gate",
    )(xb, w_qkvf, fb3, pm)


def _fox_kernel(q_ref, k_ref, v_ref, kext_ref, cb_ref, lw_ref, o_ref, qa_ref, ka_ref, va_ref):
    seq = q_ref.shape[0]
    tq = FOX_TQ
    row = lax.broadcasted_iota(jnp.int32, (tq, tq), 0)
    col = lax.broadcasted_iota(jnp.int32, (tq, tq), 1)
    causal = col <= row
    words = lambda x: pltpu.bitcast(x, jnp.uint32)
    q = words(q_ref[...] * SCALE)
    k = words(k_ref[...])
    v = words(v_ref[...])
    kext = words(kext_ref[...])
    for hh in range(FOX_HEADS_PER_STEP):
        own = lw_ref[3 * hh + 0:3 * hh + 1, :]
        ext = lw_ref[3 * hh + 1:3 * hh + 2, :]
        ones = lw_ref[3 * hh + 2:3 * hh + 3, :]
        qa_ref[hh] = pltpu.bitcast((q & own) | ones, BF16)
        ka_ref[hh] = pltpu.bitcast((k & ~ext) | (kext & ext), BF16)
        va_ref[hh] = pltpu.bitcast(v & own, BF16)

    head_lane = lax.broadcasted_iota(jnp.int32, (tq, N_HEADS), 1)
    for i in range(seq // tq):
        r0 = i * tq
        out = None
        for hh in range(FOX_HEADS_PER_STEP):
            qa = qa_ref[hh, r0:r0 + tq, :]
            head = pl.program_id(1) * FOX_HEADS_PER_STEP + hh
            cbq = jnp.sum(jnp.where(head_lane == head, cb_ref[r0:r0 + tq, :], 0.0),
                          axis=-1, keepdims=True)
            s_d = lax.dot_general(qa, ka_ref[hh, r0:r0 + tq, :], _NT_DIMS, preferred_element_type=F32)
            s_d = jnp.where(causal, s_d, NEG_INF)
            m = jnp.max(s_d, axis=-1, keepdims=True)
            if i > 0:
                s_o = lax.dot_general(qa, ka_ref[hh, 0:r0, :], _NT_DIMS, preferred_element_type=F32)
                m = jnp.maximum(m, jnp.max(s_o, axis=-1, keepdims=True))
            shift = cbq - (m + cbq)
            p_d = jnp.exp(s_d + shift)
            l = jnp.sum(p_d, axis=-1, keepdims=True)
            o = jnp.dot(p_d.astype(BF16), va_ref[hh, r0:r0 + tq, :], preferred_element_type=F32)
            if i > 0:
                p_o = jnp.exp(s_o + shift)
                l = l + jnp.sum(p_o, axis=-1, keepdims=True)
                o = o + jnp.dot(p_o.astype(BF16), va_ref[hh, 0:r0, :], preferred_element_type=F32)
            o = o / l
            out = o if out is None else out + o
        o_ref[r0:r0 + tq, :] = out.astype(o_ref.dtype)


def _fox_attention(qkv, cum, kext):
    b, s, _ = qkv.shape
    hps = FOX_HEADS_PER_STEP
    npair = N_HEADS // hps
    lw = jnp.asarray(_fox_lane_words())
    ncol = D_MODEL // LANES
    return pl.pallas_call(
        _fox_kernel,
        grid=(b, npair),
        in_specs=[pl.BlockSpec((None, s, LANES), lambda bi, hp: (bi, 0, hp)),
                  pl.BlockSpec((None, s, LANES), lambda bi, hp: (bi, 0, ncol + hp)),
                  pl.BlockSpec((None, s, LANES), lambda bi, hp: (bi, 0, 2 * ncol + hp)),
                  pl.BlockSpec((None, s, LANES), lambda bi, hp: (bi, 0, hp)),
                  pl.BlockSpec((None, s, N_HEADS), lambda bi, hp: (bi, 0, 0)),
                  pl.BlockSpec((SUBLANES, LANES), lambda bi, hp: (0, 0))],
        out_specs=pl.BlockSpec((None, s, LANES), lambda bi, hp: (bi, 0, hp)),
        out_shape=jax.ShapeDtypeStruct((b, s, D_MODEL), BF16),
        scratch_shapes=[pltpu.VMEM((hps, s, LANES), BF16)] * 3,
        compiler_params=_params("arbitrary", "arbitrary"),
        name="fox_attn",
    )(qkv, qkv, qkv, kext, cum, lw)


def _deepnorm(xres, y, g, b, alpha):
    z = alpha * xres + y
    mu = jnp.mean(z, axis=-1, keepdims=True)
    zc = z - mu
    var = jnp.mean(zc * zc, axis=-1, keepdims=True)
    return zc * lax.rsqrt(var + LN_EPS) * g + b


def _oproj_ln_kernel(alpha, o_ref, xres_ref, w_ref, g_ref, b_ref, xo_ref, xob_ref, wb_ref):
    @pl.when(pl.program_id(0) == 0)
    def _():
        wb_ref[...] = w_ref[...].astype(BF16)

    y = jnp.dot(o_ref[...], wb_ref[...], preferred_element_type=F32)
    out = _deepnorm(xres_ref[...], y, g_ref[...], b_ref[...], alpha)
    xo_ref[...] = out
    xob_ref[...] = out.astype(BF16)


def _oproj_ln(o, xres, w, layer, g, b, alpha, tm, name):
    m, k = o.shape
    n = w.shape[2]
    return pl.pallas_call(
        functools.partial(_oproj_ln_kernel, alpha),
        grid=(m // tm,),
        in_specs=[pl.BlockSpec((tm, k), lambda i: (i, 0)),
                  pl.BlockSpec((tm, n), lambda i: (i, 0)),
                  pl.BlockSpec((None, k, n), lambda i: (layer, 0, 0), pipeline_mode=pl.Buffered(1)),
                  pl.BlockSpec((1, n), lambda i: (0, 0)),
                  pl.BlockSpec((1, n), lambda i: (0, 0))],
        out_specs=[pl.BlockSpec((tm, n), lambda i: (i, 0)),
                   pl.BlockSpec((tm, n), lambda i: (i, 0))],
        out_shape=[jax.ShapeDtypeStruct((m, n), F32), jax.ShapeDtypeStruct((m, n), BF16)],
        scratch_shapes=[pltpu.VMEM((k, n), BF16)],
        compiler_params=_params("arbitrary"),
        name=name,
    )(o, xres, w, g, b)


CONV_WIDTH = 3
FFN_TF = 512
FFN_UP_TM = 1024


def _ffn_up_kernel(tiles_per_seq, x_ref, wg_ref, wu_ref, cwg_ref, cwu_ref, cbg_ref, cbu_ref,
                   a_ref, wgb_ref, wub_ref, hg_ref, hu_ref):
    i = pl.program_id(1)
    tm = x_ref.shape[0]
    halo = SUBLANES

    @pl.when(i == 0)
    def _():
        wgb_ref[...] = wg_ref[...].astype(BF16)
        wub_ref[...] = wu_ref[...].astype(BF16)

    @pl.when(i % tiles_per_seq == 0)
    def _():
        hg_ref[0:halo, :] = jnp.zeros((halo, hg_ref.shape[1]), F32)
        hu_ref[0:halo, :] = jnp.zeros((halo, hu_ref.shape[1]), F32)

    x = x_ref[...]
    hg_ref[halo:halo + tm, :] = jnp.dot(x, wgb_ref[...], preferred_element_type=F32)
    hu_ref[halo:halo + tm, :] = jnp.dot(x, wub_ref[...], preferred_element_type=F32)

    def conv(h_ref, cw_ref, cb_ref):
        out = cb_ref[...]
        for tap in range(CONV_WIDTH):
            start = halo - (CONV_WIDTH - 1) + tap
            out = out + cw_ref[tap:tap + 1, :] * h_ref[start:start + tm, :]
        return out

    g = conv(hg_ref, cwg_ref, cbg_ref)
    u = conv(hu_ref, cwu_ref, cbu_ref)
    a_ref[...] = (g * jax.nn.sigmoid(g) * u).astype(a_ref.dtype)
    hg_ref[0:halo, :] = hg_ref[tm:tm + halo, :]
    hu_ref[0:halo, :] = hu_ref[tm:tm + halo, :]


def _ffn_up(xb, w_up, conv_w, conv_b, layer, seq):
    m, k = xb.shape
    tm, tf = FFN_UP_TM, FFN_TF
    nf = D_FF // tf
    cb2 = conv_b.reshape(conv_b.shape[0], 1, 2 * D_FF)
    return pl.pallas_call(
        functools.partial(_ffn_up_kernel, seq // tm),
        grid=(nf, m // tm),
        in_specs=[pl.BlockSpec((tm, k), lambda j, i: (i, 0)),
                  pl.BlockSpec((None, k, tf), lambda j, i: (layer, 0, j)),
                  pl.BlockSpec((None, k, tf), lambda j, i: (layer, 0, j + nf)),
                  pl.BlockSpec((None, CONV_WIDTH, tf), lambda j, i: (layer, 0, j)),
                  pl.BlockSpec((None, CONV_WIDTH, tf), lambda j, i: (layer, 0, j + nf)),
                  pl.BlockSpec((None, 1, tf), lambda j, i: (layer, 0, j)),
                  pl.BlockSpec((None, 1, tf), lambda j, i: (layer, 0, j + nf))],
        out_specs=pl.BlockSpec((tm, tf), lambda j, i: (i, j)),
        out_shape=jax.ShapeDtypeStruct((m, D_FF), BF16),
        scratch_shapes=[pltpu.VMEM((k, tf), BF16), pltpu.VMEM((k, tf), BF16),
                        pltpu.VMEM((tm + 2 * SUBLANES, tf), F32), pltpu.VMEM((tm + 2 * SUBLANES, tf), F32)],
        compiler_params=_params("arbitrary", "arbitrary"),
        name="ffn_up",
    )(xb, w_up, w_up, conv_w, conv_w, cb2, cb2)


FFN_DOWN_TM = 256
FFN_DOWN_WCHUNK = 512


def _ffn_down_ln_kernel(alpha, n_load, emit_bf16, a_ref, xres_ref, w_ref, g_ref, b_ref, *rest):
    if emit_bf16:
        xo_ref, xob_ref, wb_ref = rest
    else:
        (xo_ref, wb_ref), xob_ref = rest, None
    step = pl.program_id(0)
    rows = w_ref.shape[0]

    @pl.when(step < n_load)
    def _():
        off = pl.multiple_of(step * rows, rows)
        wb_ref[pl.ds(off, rows), :] = w_ref[...].astype(BF16)

    @pl.when(step >= n_load)
    def _():
        y = jnp.dot(a_ref[...], wb_ref[...], preferred_element_type=F32)
        out = _deepnorm(xres_ref[...], y, g_ref[...], b_ref[...], alpha)
        xo_ref[...] = out
        if emit_bf16:
            xob_ref[...] = out.astype(BF16)


def _ffn_down_ln(a, xres, w, layer, g, b, alpha, emit_bf16, name):
    m, k = a.shape
    n = w.shape[2]
    tm, rows = FFN_DOWN_TM, FFN_DOWN_WCHUNK
    n_load = k // rows
    tile = lambda s: (jnp.maximum(s - n_load, 0), 0)
    out_specs = [pl.BlockSpec((tm, n), tile)]
    out_shape = [jax.ShapeDtypeStruct((m, n), F32)]
    if emit_bf16:
        out_specs.append(pl.BlockSpec((tm, n), tile))
        out_shape.append(jax.ShapeDtypeStruct((m, n), BF16))
    return pl.pallas_call(
        functools.partial(_ffn_down_ln_kernel, alpha, n_load, emit_bf16),
        grid=(n_load + m // tm,),
        in_specs=[pl.BlockSpec((tm, k), tile),
                  pl.BlockSpec((tm, n), tile),
                  pl.BlockSpec((None, rows, n), lambda s: (layer, jnp.minimum(s, n_load - 1), 0)),
                  pl.BlockSpec((1, n), lambda s: (0, 0)),
                  pl.BlockSpec((1, n), lambda s: (0, 0))],
        out_specs=out_specs,
        out_shape=out_shape,
        scratch_shapes=[pltpu.VMEM((k, n), BF16)],
        compiler_params=_params("arbitrary"),
        name=name,
    )(a, xres, w, g, b)


def kernel(x, a_w_qkv, a_w_o, a_sinks, rel_bias, b_w_qkvf, b_f_bias, b_w_o,
           ffn_w_up, ffn_conv_w, ffn_conv_b, ffn_w_down, ln_g, ln_b):
    batch, seq, d = x.shape
    depth = ffn_w_up.shape[0]
    alpha = (2 * depth) ** 0.25
    m = batch * seq
    xf = x.reshape(m, d)
    xb = None
    biasmask = _swa_bias(rel_bias)
    for layer in range(depth):
        j = layer // 2
        g = ln_g[layer].reshape(2, 1, d)
        bt = ln_b[layer].reshape(2, 1, d)
        if layer % 2 == 0:
            n_qkv = D_MODEL + 2 * A_KV_DIM
            src = xf if xb is None else xb
            qkv = _ws_matmul(src, a_w_qkv, j, n_qkv, n_qkv // 2, 512, "swa_qkv")
            o = _swa_attention(qkv.reshape(batch, seq, n_qkv), a_sinks[j], biasmask)
            w_o = a_w_o
        else:
            qkv = _ws_matmul(xb, b_w_qkvf, j, 3 * D_MODEL, 1024, 1024, "fox_qkv")
            fb3 = jnp.tile(b_f_bias[j].reshape(1, N_HEADS), (1, FOX_PIECES))
            cum, kext = _fox_gate(xb, b_w_qkvf, j, fb3, batch, seq)
            o = _fox_attention(qkv.reshape(batch, seq, 3 * D_MODEL), cum, kext)
            w_o = b_w_o
        xf, xb = _oproj_ln(o.reshape(m, d), xf, w_o, j, g[0], bt[0], alpha, 256, "oproj_ln")
        a = _ffn_up(xb, ffn_w_up, ffn_conv_w, ffn_conv_b, layer, seq)
        last = layer == depth - 1
        res = _ffn_down_ln(a, xf, ffn_w_down, layer, g[1], bt[1], alpha, not last, "ffn_down_ln")
        xf, xb = (res[0], None) if last else res
    return xf.reshape(batch, seq, d)
```

```python
import functools
import math

import jax
import jax.numpy as jnp
import numpy as np
from jax import lax
from jax.experimental import pallas as pl
from jax.experimental.pallas import tpu as pltpu

D_MODEL = 2048
HEAD_DIM = 64
BLOCK = 128
N_HEADS = D_MODEL // HEAD_DIM
A_KV_HEADS = 4
A_GROUP = N_HEADS // A_KV_HEADS
A_KV_DIM = A_KV_HEADS * HEAD_DIM
REL_BUCKETS = 32
REL_MAX_DIST = 128
D_FF = 5632
LN_EPS = 1e-5
SCALE = HEAD_DIM ** -0.5

SUBLANES = 8
LANES = 128

BF16 = jnp.bfloat16
F32 = jnp.float32
NEG_INF = float("-inf")

_NT_DIMS = (((1,), (1,)), ((), ()))


def _params(*sem):
    return pltpu.CompilerParams(dimension_semantics=sem)


def _ws_matmul_kernel(w_is_transposed, x_ref, w_ref, o_ref, wb_ref):
    @pl.when(pl.program_id(1) == 0)
    def _():
        wb_ref[...] = w_ref[...].astype(BF16)

    x = x_ref[...].astype(BF16)
    if w_is_transposed:
        y = lax.dot_general(x, wb_ref[...], _NT_DIMS, preferred_element_type=F32)
    else:
        y = jnp.dot(x, wb_ref[...], preferred_element_type=F32)
    o_ref[...] = y.astype(o_ref.dtype)


def _ws_matmul(x, w, layer, n_out, tn, tm, name, w_is_transposed=False):
    m, k = x.shape
    if w_is_transposed:
        w_spec = pl.BlockSpec((None, tn, k), lambda j, i: (layer, j, 0))
        wb_shape = (tn, k)
    else:
        w_spec = pl.BlockSpec((None, k, tn), lambda j, i: (layer, 0, j))
        wb_shape = (k, tn)
    return pl.pallas_call(
        functools.partial(_ws_matmul_kernel, w_is_transposed),
        grid=(n_out // tn, m // tm),
        in_specs=[pl.BlockSpec((tm, k), lambda j, i: (i, 0)), w_spec],
        out_specs=pl.BlockSpec((tm, tn), lambda j, i: (i, j)),
        out_shape=jax.ShapeDtypeStruct((m, n_out), BF16),
        scratch_shapes=[pltpu.VMEM(wb_shape, BF16)],
        compiler_params=_params("arbitrary", "arbitrary"),
        name=name,
    )(x, w)


def _t5_bucket_map():
    qi = np.arange(BLOCK)[None, :]
    kj = np.arange(2 * BLOCK)[:, None]
    dist = BLOCK + qi - kj
    n = np.maximum(dist, 0)
    max_exact = REL_BUCKETS // 2
    nf = np.maximum(n, 1).astype(np.float32)
    large = max_exact + (np.log(nf / max_exact) / math.log(REL_MAX_DIST / max_exact)
                         * (REL_BUCKETS - max_exact)).astype(np.int32)
    large = np.minimum(large, REL_BUCKETS - 1)
    bucket = np.where(n < max_exact, n, large)
    visible = (dist >= 0) & (dist < BLOCK)
    return np.where(visible, bucket, -1).astype(np.int32)


def _swa_bias_kernel(rb_ref, bk_ref, o_ref):
    bk = bk_ref[...]
    key = lax.broadcasted_iota(jnp.int32, bk.shape, 0)

    def head(h, _):
        acc = jnp.full(bk.shape, NEG_INF, F32)
        for b in range(REL_BUCKETS):
            acc = jnp.where(bk == b, rb_ref[b, h], acc)
        o_ref[0, h] = acc
        o_ref[1, h] = jnp.where(key < BLOCK, NEG_INF, acc)
        return 0

    lax.fori_loop(0, N_HEADS, head, 0)


def _swa_bias(rel_bias):
    bk = jnp.asarray(_t5_bucket_map())
    return pl.pallas_call(
        _swa_bias_kernel,
        in_specs=[pl.BlockSpec(memory_space=pltpu.SMEM),
                  pl.BlockSpec(memory_space=pltpu.VMEM)],
        out_specs=pl.BlockSpec(memory_space=pltpu.VMEM),
        out_shape=jax.ShapeDtypeStruct((2, N_HEADS, 2 * BLOCK, BLOCK), F32),
        name="swa_bias",
    )(rel_bias, bk)


_TN_DIMS = (((0,), (0,)), ((), ()))


def _swa_kernel(sink_ref, q_ref, kp_ref, kc_ref, vp_ref, vc_ref, bm_ref, o_ref,
                qs_ref, s_ref, p_ref, linv_ref):
    for kh in range(A_KV_HEADS):
        ks = slice(kh * HEAD_DIM, (kh + 1) * HEAD_DIM)
        for g in range(A_GROUP):
            h = kh * A_GROUP + g
            qs_ref[kh, g * BLOCK:(g + 1) * BLOCK, :] = q_ref[:, h * HEAD_DIM:(h + 1) * HEAD_DIM] * SCALE
        k2 = jnp.concatenate([kp_ref[:, ks], kc_ref[:, ks]], axis=0)
        s_ref[kh] = lax.dot_general(k2, qs_ref[kh], _NT_DIMS, preferred_element_type=F32)
    for kh in range(A_KV_HEADS):
        for g in range(A_GROUP):
            h = kh * A_GROUP + g
            cols = slice(g * BLOCK, (g + 1) * BLOCK)
            s = s_ref[kh, :, cols] + bm_ref[h]
            sink = sink_ref[h]
            m = jnp.maximum(jnp.max(s, axis=0, keepdims=True), sink)
            p = jnp.exp(s - m)
            denom = jnp.sum(p, axis=0, keepdims=True) + jnp.exp(sink - m)
            p_ref[kh, :, cols] = p.astype(BF16)
            linv_ref[kh, :, cols] = 1.0 / denom
    for kh in range(A_KV_HEADS):
        ks = slice(kh * HEAD_DIM, (kh + 1) * HEAD_DIM)
        v2 = jnp.concatenate([vp_ref[:, ks], vc_ref[:, ks]], axis=0)
        ot = lax.dot_general(v2, p_ref[kh], _TN_DIMS, preferred_element_type=F32) * linv_ref[kh]
        for g in range(A_GROUP):
            h = kh * A_GROUP + g
            o_ref[:, h * HEAD_DIM:(h + 1) * HEAD_DIM] = ot[:, g * BLOCK:(g + 1) * BLOCK].T.astype(o_ref.dtype)


def _swa_attention(qkv, sinks, biasmask):
    b, s, _ = qkv.shape
    nb = s // BLOCK
    gq = A_GROUP * BLOCK
    kcol = D_MODEL // A_KV_DIM
    prev = lambda n: jnp.maximum(n - 1, 0)
    return pl.pallas_call(
        _swa_kernel,
        grid=(b, nb),
        in_specs=[pl.BlockSpec(memory_space=pltpu.SMEM),
                  pl.BlockSpec((None, BLOCK, D_MODEL), lambda bi, n: (bi, n, 0)),
                  pl.BlockSpec((None, BLOCK, A_KV_DIM), lambda bi, n: (bi, prev(n), kcol)),
                  pl.BlockSpec((None, BLOCK, A_KV_DIM), lambda bi, n: (bi, n, kcol)),
                  pl.BlockSpec((None, BLOCK, A_KV_DIM), lambda bi, n: (bi, prev(n), kcol + 1)),
                  pl.BlockSpec((None, BLOCK, A_KV_DIM), lambda bi, n: (bi, n, kcol + 1)),
                  pl.BlockSpec((None, N_HEADS, 2 * BLOCK, BLOCK),
                               lambda bi, n: (jnp.where(n == 0, 1, 0), 0, 0, 0))],
        out_specs=pl.BlockSpec((None, BLOCK, D_MODEL), lambda bi, n: (bi, n, 0)),
        out_shape=jax.ShapeDtypeStruct((b, s, D_MODEL), BF16),
        scratch_shapes=[pltpu.VMEM((A_KV_HEADS, gq, HEAD_DIM), BF16),
                        pltpu.VMEM((A_KV_HEADS, 2 * BLOCK, gq), F32),
                        pltpu.VMEM((A_KV_HEADS, 2 * BLOCK, gq), BF16),
                        pltpu.VMEM((A_KV_HEADS, 1, gq), F32)],
        compiler_params=_params("arbitrary", "arbitrary"),
        name="swa_attn",
    )(sinks, qkv, qkv, qkv, qkv, qkv, biasmask)


FOX_CUM_BLOCK = 256
FOX_TQ = 256
FOX_HEADS_PER_STEP = LANES // HEAD_DIM
FOX_PIECES = 3
BF16_ONE_PAIR = 0x3F803F80


def _fox_ext_lane(head, piece):
    pair, odd = divmod(head, FOX_HEADS_PER_STEP)
    return pair * LANES + (1 - odd) * HEAD_DIM + piece


def _fox_placement():
    pm = np.zeros((FOX_PIECES * N_HEADS, D_MODEL), np.float32)
    for t in range(FOX_PIECES):
        for h in range(N_HEADS):
            pm[t * N_HEADS + h, _fox_ext_lane(h, t)] = -1.0
    return pm


def _fox_lane_words():
    w = np.zeros((SUBLANES, LANES), np.uint32)
    for hh in range(FOX_HEADS_PER_STEP):
        own = np.zeros(LANES, bool)
        own[hh * HEAD_DIM:(hh + 1) * HEAD_DIM] = True
        ext = np.zeros(LANES, bool)
        first = _fox_ext_lane(hh, 0)
        ext[first:first + FOX_PIECES] = True
        w[3 * hh + 0] = np.where(own, 0xFFFFFFFF, 0)
        w[3 * hh + 1] = np.where(ext, 0xFFFFFFFF, 0)
        w[3 * hh + 2] = np.where(ext, BF16_ONE_PAIR, 0)
    return w


def _log_sigmoid(x):
    return jnp.minimum(x, 0.0) - jnp.log1p(jnp.exp(-jnp.abs(x)))


def _round_bf16(x):
    return x.astype(BF16).astype(F32)


def _fox_gate_kernel(x_ref, wf_ref, fb_ref, pm_ref, cumt_ref, kext_ref, w3_ref):
    @pl.when(pl.program_id(0) == 0)
    def _():
        w = wf_ref[...].astype(BF16)
        for t in range(FOX_PIECES):
            w3_ref[t * N_HEADS:(t + 1) * N_HEADS, :] = w

    fz = lax.dot_general(x_ref[...], w3_ref[...], _NT_DIMS, preferred_element_type=F32) + fb_ref[...]
    lf = _log_sigmoid(fz)
    cb = FOX_CUM_BLOCK
    row = lax.broadcasted_iota(jnp.int32, (cb, cb), 0)
    col = lax.broadcasted_iota(jnp.int32, (cb, cb), 1)
    tri = jnp.where(row >= col, 1.0, 0.0).astype(BF16)
    group = lax.broadcasted_iota(jnp.int32, (cb, lf.shape[1]), 1) // N_HEADS
    pm = pm_ref[...]
    carry = jnp.zeros((1, lf.shape[1]), F32)
    for blk in range(lf.shape[0] // cb):
        rows = slice(blk * cb, (blk + 1) * cb)
        x = lf[rows]
        hi = _round_bf16(x)
        r1 = x - hi
        mid = _round_bf16(r1)
        c = (jnp.dot(tri, hi.astype(BF16), preferred_element_type=F32)
             + jnp.dot(tri, mid.astype(BF16), preferred_element_type=F32)
             + jnp.dot(tri, (r1 - mid).astype(BF16), preferred_element_type=F32)) + carry
        carry = c[cb - 1:cb]
        cumt_ref[:, rows] = c[:, :N_HEADS].T
        chi = _round_bf16(c)
        cr1 = c - chi
        cmid = _round_bf16(cr1)
        piece = jnp.where(group == 0, chi, jnp.where(group == 1, cmid, cr1 - cmid)).astype(BF16)
        kext_ref[rows, :] = jnp.dot(piece, pm, preferred_element_type=F32).astype(BF16)


def _fox_gate(xb, w_qkvf_t, layer, fb3, batch, seq):
    pm = jnp.asarray(_fox_placement(), BF16)
    n3 = FOX_PIECES * N_HEADS
    gate_row = 3 * D_MODEL // N_HEADS
    return pl.pallas_call(
        _fox_gate_kernel,
        grid=(batch,),
        in_specs=[pl.BlockSpec((seq, D_MODEL), lambda bi: (bi, 0)),
                  pl.BlockSpec((None, N_HEADS, D_MODEL), lambda bi: (layer, gate_row, 0)),
                  pl.BlockSpec((1, n3), lambda bi: (0, 0)),
                  pl.BlockSpec((n3, D_MODEL), lambda bi: (0, 0))],
        out_specs=[pl.BlockSpec((None, N_HEADS, seq), lambda bi: (bi, 0, 0)),
                   pl.BlockSpec((None, seq, D_MODEL), lambda bi: (bi, 0, 0))],
        out_shape=[jax.ShapeDtypeStruct((batch, N_HEADS, seq), F32),
                   jax.ShapeDtypeStruct((batch, seq, D_MODEL), BF16)],
        scratch_shapes=[pltpu.VMEM((n3, D_MODEL), BF16)],
        compiler_params=_params("arbitrary"),
        name="fox_gate",
    )(xb, w_qkvf_t, fb3, pm)


def _fox_kernel(q_ref, k_ref, v_ref, kext_ref, cbt_ref, lw_ref, o_ref, qa_ref, ka_ref, vt_ref,
                s_ref, p_ref):
    seq = q_ref.shape[0]
    tq = FOX_TQ
    key = lax.broadcasted_iota(jnp.int32, (tq, tq), 0)
    qry = lax.broadcasted_iota(jnp.int32, (tq, tq), 1)
    causal = key <= qry
    words = lambda x: pltpu.bitcast(x, jnp.uint32)
    q = words(q_ref[...] * SCALE)
    k = words(k_ref[...])
    kext = words(kext_ref[...])
    for hh in range(FOX_HEADS_PER_STEP):
        own = lw_ref[3 * hh + 0:3 * hh + 1, :]
        ext = lw_ref[3 * hh + 1:3 * hh + 2, :]
        ones = lw_ref[3 * hh + 2:3 * hh + 3, :]
        qa_ref[hh] = pltpu.bitcast((q & own) | ones, BF16)
        ka_ref[hh] = pltpu.bitcast((k & ~ext) | (kext & ext), BF16)
    vt_ref[...] = v_ref[...].T

    units = [(i, hh) for i in range(seq // tq) for hh in range(FOX_HEADS_PER_STEP)]

    def scores(i, hh):
        r0, r1 = i * tq, (i + 1) * tq
        qa = qa_ref[hh, r0:r1, :]
        s_d = lax.dot_general(ka_ref[hh, r0:r1, :], qa, _NT_DIMS, preferred_element_type=F32)
        s_ref[hh, r0:r1, :] = jnp.where(causal, s_d, NEG_INF)
        if i > 0:
            s_ref[hh, 0:r0, :] = lax.dot_general(ka_ref[hh, 0:r0, :], qa, _NT_DIMS,
                                                 preferred_element_type=F32)

    def attend(i, hh):
        r0, r1 = i * tq, (i + 1) * tq
        head = pl.program_id(1) * FOX_HEADS_PER_STEP + hh
        cbq = cbt_ref[pl.ds(head, 1), r0:r1]
        m = jnp.max(s_ref[hh, 0:r1, :], axis=0, keepdims=True)
        shift = cbq - (m + cbq)
        p = jnp.exp(s_ref[hh, 0:r1, :] + shift)
        l = jnp.sum(p, axis=0, keepdims=True)
        p_ref[hh, 0:r1, :] = p.astype(BF16)
        vrows = slice(hh * HEAD_DIM, (hh + 1) * HEAD_DIM)
        ot = jnp.dot(vt_ref[vrows, 0:r1], p_ref[hh, 0:r1, :], preferred_element_type=F32)
        return ot * (1.0 / l)

    scores(*units[0])
    outs = []
    for u, (i, hh) in enumerate(units):
        if u + 1 < len(units):
            scores(*units[u + 1])
        outs.append(attend(i, hh))
        if hh == FOX_HEADS_PER_STEP - 1:
            o_ref[i * tq:(i + 1) * tq, :] = jnp.concatenate(outs, axis=0).T.astype(o_ref.dtype)
            outs = []


def _fox_attention(qkv, cumt, kext):
    b, s, _ = qkv.shape
    hps = FOX_HEADS_PER_STEP
    npair = N_HEADS // hps
    lw = jnp.asarray(_fox_lane_words())
    ncol = D_MODEL // LANES
    return pl.pallas_call(
        _fox_kernel,
        grid=(b, npair),
        in_specs=[pl.BlockSpec((None, s, LANES), lambda bi, hp: (bi, 0, hp)),
                  pl.BlockSpec((None, s, LANES), lambda bi, hp: (bi, 0, ncol + hp)),
                  pl.BlockSpec((None, s, LANES), lambda bi, hp: (bi, 0, 2 * ncol + hp)),
                  pl.BlockSpec((None, s, LANES), lambda bi, hp: (bi, 0, hp)),
                  pl.BlockSpec((None, N_HEADS, s), lambda bi, hp: (bi, 0, 0)),
                  pl.BlockSpec((SUBLANES, LANES), lambda bi, hp: (0, 0))],
        out_specs=pl.BlockSpec((None, s, LANES), lambda bi, hp: (bi, 0, hp)),
        out_shape=jax.ShapeDtypeStruct((b, s, D_MODEL), BF16),
        scratch_shapes=[pltpu.VMEM((hps, s, LANES), BF16), pltpu.VMEM((hps, s, LANES), BF16),
                        pltpu.VMEM((LANES, s), BF16),
                        pltpu.VMEM((hps, s, FOX_TQ), F32), pltpu.VMEM((hps, s, FOX_TQ), BF16)],
        compiler_params=_params("arbitrary", "arbitrary"),
        name="fox_attn",
    )(qkv, qkv, qkv, kext, cumt, lw)


def _deepnorm(xres, y, g, b, alpha):
    z = alpha * xres + y
    mu = jnp.mean(z, axis=-1, keepdims=True)
    zc = z - mu
    var = jnp.mean(zc * zc, axis=-1, keepdims=True)
    return zc * lax.rsqrt(var + LN_EPS) * g + b


def _oproj_ln_kernel(alpha, o_ref, xres_ref, w_ref, g_ref, b_ref, xo_ref, xob_ref, wb_ref):
    @pl.when(pl.program_id(0) == 0)
    def _():
        wb_ref[...] = w_ref[...].astype(BF16)

    y = jnp.dot(o_ref[...], wb_ref[...], preferred_element_type=F32)
    out = _deepnorm(xres_ref[...], y, g_ref[...], b_ref[...], alpha)
    xo_ref[...] = out
    xob_ref[...] = out.astype(BF16)


def _oproj_ln(o, xres, w, layer, g, b, alpha, tm, name):
    m, k = o.shape
    n = w.shape[2]
    return pl.pallas_call(
        functools.partial(_oproj_ln_kernel, alpha),
        grid=(m // tm,),
        in_specs=[pl.BlockSpec((tm, k), lambda i: (i, 0)),
                  pl.BlockSpec((tm, n), lambda i: (i, 0)),
                  pl.BlockSpec((None, k, n), lambda i: (layer, 0, 0), pipeline_mode=pl.Buffered(1)),
                  pl.BlockSpec((1, n), lambda i: (0, 0)),
                  pl.BlockSpec((1, n), lambda i: (0, 0))],
        out_specs=[pl.BlockSpec((tm, n), lambda i: (i, 0)),
                   pl.BlockSpec((tm, n), lambda i: (i, 0))],
        out_shape=[jax.ShapeDtypeStruct((m, n), F32), jax.ShapeDtypeStruct((m, n), BF16)],
        scratch_shapes=[pltpu.VMEM((k, n), BF16)],
        compiler_params=_params("arbitrary"),
        name=name,
    )(o, xres, w, g, b)


CONV_WIDTH = 3
FFN_TF = 512
FFN_UP_TM = 1024


def _ffn_up_kernel(tiles_per_seq, x_ref, wg_ref, wu_ref, cwg_ref, cwu_ref, cbg_ref, cbu_ref,
                   a_ref, wgb_ref, wub_ref, hg_ref, hu_ref):
    i = pl.program_id(1)
    tm = x_ref.shape[0]
    halo = SUBLANES

    @pl.when(i == 0)
    def _():
        wgb_ref[...] = wg_ref[...].astype(BF16)
        wub_ref[...] = wu_ref[...].astype(BF16)

    @pl.when(i % tiles_per_seq == 0)
    def _():
        hg_ref[0:halo, :] = jnp.zeros((halo, hg_ref.shape[1]), F32)
        hu_ref[0:halo, :] = jnp.zeros((halo, hu_ref.shape[1]), F32)

    x = x_ref[...]
    hg_ref[halo:halo + tm, :] = jnp.dot(x, wgb_ref[...], preferred_element_type=F32)
    hu_ref[halo:halo + tm, :] = jnp.dot(x, wub_ref[...], preferred_element_type=F32)

    def conv(h_ref, cw_ref, cb_ref):
        out = cb_ref[...]
        for tap in range(CONV_WIDTH):
            start = halo - (CONV_WIDTH - 1) + tap
            out = out + cw_ref[tap:tap + 1, :] * h_ref[start:start + tm, :]
        return out

    g = conv(hg_ref, cwg_ref, cbg_ref)
    u = conv(hu_ref, cwu_ref, cbu_ref)
    a_ref[...] = (g * jax.nn.sigmoid(g) * u).astype(a_ref.dtype)
    hg_ref[0:halo, :] = hg_ref[tm:tm + halo, :]
    hu_ref[0:halo, :] = hu_ref[tm:tm + halo, :]


def _ffn_up(xb, w_up, conv_w, conv_b, layer, seq):
    m, k = xb.shape
    tm, tf = FFN_UP_TM, FFN_TF
    nf = D_FF // tf
    cb2 = conv_b.reshape(conv_b.shape[0], 1, 2 * D_FF)
    return pl.pallas_call(
        functools.partial(_ffn_up_kernel, seq // tm),
        grid=(nf, m // tm),
        in_specs=[pl.BlockSpec((tm, k), lambda j, i: (i, 0)),
                  pl.BlockSpec((None, k, tf), lambda j, i: (layer, 0, j)),
                  pl.BlockSpec((None, k, tf), lambda j, i: (layer, 0, j + nf)),
                  pl.BlockSpec((None, CONV_WIDTH, tf), lambda j, i: (layer, 0, j)),
                  pl.BlockSpec((None, CONV_WIDTH, tf), lambda j, i: (layer, 0, j + nf)),
                  pl.BlockSpec((None, 1, tf), lambda j, i: (layer, 0, j)),
                  pl.BlockSpec((None, 1, tf), lambda j, i: (layer, 0, j + nf))],
        out_specs=pl.BlockSpec((tm, tf), lambda j, i: (i, j)),
        out_shape=jax.ShapeDtypeStruct((m, D_FF), BF16),
        scratch_shapes=[pltpu.VMEM((k, tf), BF16), pltpu.VMEM((k, tf), BF16),
                        pltpu.VMEM((tm + 2 * SUBLANES, tf), F32), pltpu.VMEM((tm + 2 * SUBLANES, tf), F32)],
        compiler_params=_params("arbitrary", "arbitrary"),
        name="ffn_up",
    )(xb, w_up, w_up, conv_w, conv_w, cb2, cb2)


FFN_DOWN_TM = 256
FFN_DOWN_WCHUNK = 512


def _ffn_down_ln_kernel(alpha, n_load, emit_bf16, a_ref, xres_ref, w_ref, g_ref, b_ref, *rest):
    if emit_bf16:
        xo_ref, xob_ref, wb_ref = rest
    else:
        (xo_ref, wb_ref), xob_ref = rest, None
    step = pl.program_id(0)
    rows = w_ref.shape[0]

    @pl.when(step < n_load)
    def _():
        off = pl.multiple_of(step * rows, rows)
        wb_ref[pl.ds(off, rows), :] = w_ref[...].astype(BF16)

    @pl.when(step >= n_load)
    def _():
        y = jnp.dot(a_ref[...], wb_ref[...], preferred_element_type=F32)
        out = _deepnorm(xres_ref[...], y, g_ref[...], b_ref[...], alpha)
        xo_ref[...] = out
        if emit_bf16:
            xob_ref[...] = out.astype(BF16)


def _ffn_down_ln(a, xres, w, layer, g, b, alpha, emit_bf16, name):
    m, k = a.shape
    n = w.shape[2]
    tm, rows = FFN_DOWN_TM, FFN_DOWN_WCHUNK
    n_load = k // rows
    tile = lambda s: (jnp.maximum(s - n_load, 0), 0)
    out_specs = [pl.BlockSpec((tm, n), tile)]
    out_shape = [jax.ShapeDtypeStruct((m, n), F32)]
    if emit_bf16:
        out_specs.append(pl.BlockSpec((tm, n), tile))
        out_shape.append(jax.ShapeDtypeStruct((m, n), BF16))
    return pl.pallas_call(
        functools.partial(_ffn_down_ln_kernel, alpha, n_load, emit_bf16),
        grid=(n_load + m // tm,),
        in_specs=[pl.BlockSpec((tm, k), tile),
                  pl.BlockSpec((tm, n), tile),
                  pl.BlockSpec((None, rows, n), lambda s: (layer, jnp.minimum(s, n_load - 1), 0)),
                  pl.BlockSpec((1, n), lambda s: (0, 0)),
                  pl.BlockSpec((1, n), lambda s: (0, 0))],
        out_specs=out_specs,
        out_shape=out_shape,
        scratch_shapes=[pltpu.VMEM((k, n), BF16)],
        compiler_params=_params("arbitrary"),
        name=name,
    )(a, xres, w, g, b)


def kernel(x, a_w_qkv, a_w_o, a_sinks, rel_bias, b_w_qkvf, b_f_bias, b_w_o,
           ffn_w_up, ffn_conv_w, ffn_conv_b, ffn_w_down, ln_g, ln_b):
    batch, seq, d = x.shape
    depth = ffn_w_up.shape[0]
    alpha = (2 * depth) ** 0.25
    m = batch * seq
    xf = x.reshape(m, d)
    xb = None
    biasmask = _swa_bias(rel_bias)
    for layer in range(depth):
        j = layer // 2
        g = ln_g[layer].reshape(2, 1, d)
        bt = ln_b[layer].reshape(2, 1, d)
        if layer % 2 == 0:
            n_qkv = D_MODEL + 2 * A_KV_DIM
            src = xf if xb is None else xb
            qkv = _ws_matmul(src, a_w_qkv, j, n_qkv, n_qkv // 2, 512, "swa_qkv")
            o = _swa_attention(qkv.reshape(batch, seq, n_qkv), a_sinks[j], biasmask)
            w_o = a_w_o
        else:
            w_t = jnp.swapaxes(b_w_qkvf, 1, 2)
            qkv = _ws_matmul(xb, w_t, j, 3 * D_MODEL, 1024, 1024, "fox_qkv", w_is_transposed=True)
            fb3 = jnp.tile(b_f_bias[j].reshape(1, N_HEADS), (1, FOX_PIECES))
            cumt, kext = _fox_gate(xb, w_t, j, fb3, batch, seq)
            o = _fox_attention(qkv.reshape(batch, seq, 3 * D_MODEL), cumt, kext)
            w_o = b_w_o
        xf, xb = _oproj_ln(o.reshape(m, d), xf, w_o, j, g[0], bt[0], alpha, 256, "oproj_ln")
        a = _ffn_up(xb, ffn_w_up, ffn_conv_w, ffn_conv_b, layer, seq)
        last = layer == depth - 1
        res = _ffn_down_ln(a, xf, ffn_w_down, layer, g[1], bt[1], alpha, not last, "ffn_down_ln")
        xf, xb = (res[0], None) if last else res
    return xf.reshape(batch, seq, d)
```

```python
import functools
import math

import jax
import jax.numpy as jnp
import numpy as np
from jax import lax
from jax.experimental import pallas as pl
from jax.experimental.pallas import tpu as pltpu

D_MODEL = 2048
HEAD_DIM = 64
BLOCK = 128
N_HEADS = D_MODEL // HEAD_DIM
A_KV_HEADS = 4
A_GROUP = N_HEADS // A_KV_HEADS
A_KV_DIM = A_KV_HEADS * HEAD_DIM
REL_BUCKETS = 32
REL_MAX_DIST = 128
D_FF = 5632
LN_EPS = 1e-5
SCALE = HEAD_DIM ** -0.5

SUBLANES = 8
LANES = 128

BF16 = jnp.bfloat16
F32 = jnp.float32
NEG_INF = float("-inf")

_NT_DIMS = (((1,), (1,)), ((), ()))


def _params(*sem):
    return pltpu.CompilerParams(dimension_semantics=sem)


LOG2E = math.log2(math.e)
Q_SCALE = SCALE * LOG2E


def _ws_matmul_kernel(w_is_transposed, q_cols, x_ref, w_ref, o_ref, wb_ref):
    @pl.when(pl.program_id(1) == 0)
    def _():
        wb_ref[...] = w_ref[...].astype(BF16)

    x = x_ref[...].astype(BF16)
    if w_is_transposed:
        y = lax.dot_general(x, wb_ref[...], _NT_DIMS, preferred_element_type=F32)
    else:
        y = jnp.dot(x, wb_ref[...], preferred_element_type=F32)
    tn = o_ref.shape[1]
    col = pl.program_id(0) * tn + lax.broadcasted_iota(jnp.int32, (1, tn), 1)
    y = y * jnp.where(col < q_cols, Q_SCALE, 1.0)
    o_ref[...] = y.astype(o_ref.dtype)


def _ws_matmul(x, w, layer, n_out, tn, tm, name, w_is_transposed=False):
    m, k = x.shape
    if w_is_transposed:
        w_spec = pl.BlockSpec((None, tn, k), lambda j, i: (layer, j, 0))
        wb_shape = (tn, k)
    else:
        w_spec = pl.BlockSpec((None, k, tn), lambda j, i: (layer, 0, j))
        wb_shape = (k, tn)
    return pl.pallas_call(
        functools.partial(_ws_matmul_kernel, w_is_transposed, D_MODEL),
        grid=(n_out // tn, m // tm),
        in_specs=[pl.BlockSpec((tm, k), lambda j, i: (i, 0)), w_spec],
        out_specs=pl.BlockSpec((tm, tn), lambda j, i: (i, j)),
        out_shape=jax.ShapeDtypeStruct((m, n_out), BF16),
        scratch_shapes=[pltpu.VMEM(wb_shape, BF16)],
        compiler_params=_params("arbitrary", "arbitrary"),
        name=name,
    )(x, w)


def _t5_bucket_map():
    qi = np.arange(BLOCK)[None, :]
    kj = np.arange(2 * BLOCK)[:, None]
    dist = BLOCK + qi - kj
    n = np.maximum(dist, 0)
    max_exact = REL_BUCKETS // 2
    nf = np.maximum(n, 1).astype(np.float32)
    large = max_exact + (np.log(nf / max_exact) / math.log(REL_MAX_DIST / max_exact)
                         * (REL_BUCKETS - max_exact)).astype(np.int32)
    large = np.minimum(large, REL_BUCKETS - 1)
    bucket = np.where(n < max_exact, n, large)
    visible = (dist >= 0) & (dist < BLOCK)
    return np.where(visible, bucket, -1).astype(np.int32)


def _swa_bias_kernel(rb_ref, bk_ref, o_ref):
    bk = bk_ref[...]
    key = lax.broadcasted_iota(jnp.int32, bk.shape, 0)

    def head(h, _):
        acc = jnp.full(bk.shape, NEG_INF, F32)
        for b in range(REL_BUCKETS):
            acc = jnp.where(bk == b, rb_ref[b, h] * LOG2E, acc)
        o_ref[0, h] = acc
        o_ref[1, h] = jnp.where(key < BLOCK, NEG_INF, acc)
        return 0

    lax.fori_loop(0, N_HEADS, head, 0)


def _swa_bias(rel_bias):
    bk = jnp.asarray(_t5_bucket_map())
    return pl.pallas_call(
        _swa_bias_kernel,
        in_specs=[pl.BlockSpec(memory_space=pltpu.SMEM),
                  pl.BlockSpec(memory_space=pltpu.VMEM)],
        out_specs=pl.BlockSpec(memory_space=pltpu.VMEM),
        out_shape=jax.ShapeDtypeStruct((2, N_HEADS, 2 * BLOCK, BLOCK), F32),
        name="swa_bias",
    )(rel_bias, bk)


_TN_DIMS = (((0,), (0,)), ((), ()))


def _swa_kernel(sink_ref, q_ref, kp_ref, kc_ref, vp_ref, vc_ref, bm_ref, o_ref,
                qs_ref, s_ref, p_ref, linv_ref):
    for kh in range(A_KV_HEADS):
        ks = slice(kh * HEAD_DIM, (kh + 1) * HEAD_DIM)
        for g in range(A_GROUP):
            h = kh * A_GROUP + g
            qs_ref[kh, g * BLOCK:(g + 1) * BLOCK, :] = q_ref[:, h * HEAD_DIM:(h + 1) * HEAD_DIM]
        k2 = jnp.concatenate([kp_ref[:, ks], kc_ref[:, ks]], axis=0)
        s_ref[kh] = lax.dot_general(k2, qs_ref[kh], _NT_DIMS, preferred_element_type=F32)
    for kh in range(A_KV_HEADS):
        for g in range(A_GROUP):
            h = kh * A_GROUP + g
            cols = slice(g * BLOCK, (g + 1) * BLOCK)
            s = s_ref[kh, :, cols] + bm_ref[h]
            sink = sink_ref[h] * LOG2E
            m = jnp.maximum(jnp.max(s, axis=0, keepdims=True), sink)
            p = jnp.exp2(s - m)
            denom = jnp.sum(p, axis=0, keepdims=True) + jnp.exp2(sink - m)
            p_ref[kh, :, cols] = p.astype(BF16)
            linv_ref[kh, :, cols] = 1.0 / denom
    for kh in range(A_KV_HEADS):
        ks = slice(kh * HEAD_DIM, (kh + 1) * HEAD_DIM)
        v2 = jnp.concatenate([vp_ref[:, ks], vc_ref[:, ks]], axis=0)
        ot = lax.dot_general(v2, p_ref[kh], _TN_DIMS, preferred_element_type=F32) * linv_ref[kh]
        for g in range(A_GROUP):
            h = kh * A_GROUP + g
            o_ref[:, h * HEAD_DIM:(h + 1) * HEAD_DIM] = ot[:, g * BLOCK:(g + 1) * BLOCK].T.astype(o_ref.dtype)


def _swa_attention(qkv, sinks, biasmask):
    b, s, _ = qkv.shape
    nb = s // BLOCK
    gq = A_GROUP * BLOCK
    kcol = D_MODEL // A_KV_DIM
    prev = lambda n: jnp.maximum(n - 1, 0)
    return pl.pallas_call(
        _swa_kernel,
        grid=(b, nb),
        in_specs=[pl.BlockSpec(memory_space=pltpu.SMEM),
                  pl.BlockSpec((None, BLOCK, D_MODEL), lambda bi, n: (bi, n, 0)),
                  pl.BlockSpec((None, BLOCK, A_KV_DIM), lambda bi, n: (bi, prev(n), kcol)),
                  pl.BlockSpec((None, BLOCK, A_KV_DIM), lambda bi, n: (bi, n, kcol)),
                  pl.BlockSpec((None, BLOCK, A_KV_DIM), lambda bi, n: (bi, prev(n), kcol + 1)),
                  pl.BlockSpec((None, BLOCK, A_KV_DIM), lambda bi, n: (bi, n, kcol + 1)),
                  pl.BlockSpec((None, N_HEADS, 2 * BLOCK, BLOCK),
                               lambda bi, n: (jnp.where(n == 0, 1, 0), 0, 0, 0))],
        out_specs=pl.BlockSpec((None, BLOCK, D_MODEL), lambda bi, n: (bi, n, 0)),
        out_shape=jax.ShapeDtypeStruct((b, s, D_MODEL), BF16),
        scratch_shapes=[pltpu.VMEM((A_KV_HEADS, gq, HEAD_DIM), BF16),
                        pltpu.VMEM((A_KV_HEADS, 2 * BLOCK, gq), F32),
                        pltpu.VMEM((A_KV_HEADS, 2 * BLOCK, gq), BF16),
                        pltpu.VMEM((A_KV_HEADS, 1, gq), F32)],
        compiler_params=_params("arbitrary", "arbitrary"),
        name="swa_attn",
    )(sinks, qkv, qkv, qkv, qkv, qkv, biasmask)


FOX_CUM_BLOCK = 256
FOX_TQ = 256
FOX_HEADS_PER_STEP = LANES // HEAD_DIM
FOX_PIECES = 3
FOX_SUM_ROWS = 16
FOX_SCORE_SLOTS = 4
BF16_ONE_PAIR = 0x3F803F80


def _fox_ext_lane(head, piece):
    pair, odd = divmod(head, FOX_HEADS_PER_STEP)
    return pair * LANES + (1 - odd) * HEAD_DIM + piece


def _fox_placement():
    pm = np.zeros((FOX_PIECES * N_HEADS, D_MODEL), np.float32)
    for t in range(FOX_PIECES):
        for h in range(N_HEADS):
            pm[t * N_HEADS + h, _fox_ext_lane(h, t)] = -1.0
    return pm


def _fox_lane_words():
    w = np.zeros((SUBLANES, LANES), np.uint32)
    for hh in range(FOX_HEADS_PER_STEP):
        own = np.zeros(LANES, bool)
        own[hh * HEAD_DIM:(hh + 1) * HEAD_DIM] = True
        ext = np.zeros(LANES, bool)
        first = _fox_ext_lane(hh, 0)
        ext[first:first + FOX_PIECES] = True
        w[3 * hh + 0] = np.where(own, 0xFFFFFFFF, 0)
        w[3 * hh + 1] = np.where(ext, 0xFFFFFFFF, 0)
        w[3 * hh + 2] = np.where(ext, BF16_ONE_PAIR, 0)
    return w


def _log_sigmoid(x):
    return jnp.minimum(x, 0.0) - jnp.log1p(jnp.exp(-jnp.abs(x)))


def _round_bf16(x):
    return x.astype(BF16).astype(F32)


def _fox_gate_kernel(x_ref, wf_ref, fb_ref, pm_ref, cumt_ref, kext_ref, w3_ref):
    @pl.when(pl.program_id(0) == 0)
    def _():
        w = wf_ref[...].astype(BF16)
        for t in range(FOX_PIECES):
            w3_ref[t * N_HEADS:(t + 1) * N_HEADS, :] = w

    fz = lax.dot_general(x_ref[...], w3_ref[...], _NT_DIMS, preferred_element_type=F32) + fb_ref[...]
    lf = _log_sigmoid(fz)
    cb = FOX_CUM_BLOCK
    row = lax.broadcasted_iota(jnp.int32, (cb, cb), 0)
    col = lax.broadcasted_iota(jnp.int32, (cb, cb), 1)
    tri = jnp.where(row >= col, 1.0, 0.0).astype(BF16)
    group = lax.broadcasted_iota(jnp.int32, (cb, lf.shape[1]), 1) // N_HEADS
    pm = pm_ref[...]
    carry = jnp.zeros((1, lf.shape[1]), F32)
    for blk in range(lf.shape[0] // cb):
        rows = slice(blk * cb, (blk + 1) * cb)
        x = lf[rows]
        hi = _round_bf16(x)
        r1 = x - hi
        mid = _round_bf16(r1)
        c = (jnp.dot(tri, hi.astype(BF16), preferred_element_type=F32)
             + jnp.dot(tri, mid.astype(BF16), preferred_element_type=F32)
             + jnp.dot(tri, (r1 - mid).astype(BF16), preferred_element_type=F32)) + carry
        carry = c[cb - 1:cb]
        c2 = c * LOG2E
        cumt_ref[:, rows] = c2[:, :N_HEADS].T
        chi = _round_bf16(c2)
        cr1 = c2 - chi
        cmid = _round_bf16(cr1)
        piece = jnp.where(group == 0, chi, jnp.where(group == 1, cmid, cr1 - cmid)).astype(BF16)
        kext_ref[rows, :] = jnp.dot(piece, pm, preferred_element_type=F32).astype(BF16)


def _fox_gate(xb, w_qkvf_t, layer, fb3, batch, seq):
    pm = jnp.asarray(_fox_placement(), BF16)
    n3 = FOX_PIECES * N_HEADS
    gate_row = 3 * D_MODEL // N_HEADS
    return pl.pallas_call(
        _fox_gate_kernel,
        grid=(batch,),
        in_specs=[pl.BlockSpec((seq, D_MODEL), lambda bi: (bi, 0)),
                  pl.BlockSpec((None, N_HEADS, D_MODEL), lambda bi: (layer, gate_row, 0)),
                  pl.BlockSpec((1, n3), lambda bi: (0, 0)),
                  pl.BlockSpec((n3, D_MODEL), lambda bi: (0, 0))],
        out_specs=[pl.BlockSpec((None, N_HEADS, seq), lambda bi: (bi, 0, 0)),
                   pl.BlockSpec((None, seq, D_MODEL), lambda bi: (bi, 0, 0))],
        out_shape=[jax.ShapeDtypeStruct((batch, N_HEADS, seq), F32),
                   jax.ShapeDtypeStruct((batch, seq, D_MODEL), BF16)],
        scratch_shapes=[pltpu.VMEM((n3, D_MODEL), BF16)],
        compiler_params=_params("arbitrary"),
        name="fox_gate",
    )(xb, w_qkvf_t, fb3, pm)


def _fox_kernel(q_ref, k_ref, v_ref, kext_ref, cbt_ref, lw_ref, o_ref, qa_ref, ka_ref, vt_ref,
                s_ref, p_ref):
    seq = q_ref.shape[0]
    tq = FOX_TQ
    key = lax.broadcasted_iota(jnp.int32, (tq, tq), 0)
    qry = lax.broadcasted_iota(jnp.int32, (tq, tq), 1)
    causal = key <= qry
    words = lambda x: pltpu.bitcast(x, jnp.uint32)
    q = words(q_ref[...])
    k = words(k_ref[...])
    kext = words(kext_ref[...])
    vt = v_ref[...].T
    for hh in range(FOX_HEADS_PER_STEP):
        own = lw_ref[3 * hh + 0:3 * hh + 1, :]
        ext = lw_ref[3 * hh + 1:3 * hh + 2, :]
        ones = lw_ref[3 * hh + 2:3 * hh + 3, :]
        qa_ref[hh] = pltpu.bitcast((q & own) | ones, BF16)
        ka_ref[hh] = pltpu.bitcast((k & ~ext) | (kext & ext), BF16)
        vt_ref[hh, 0:HEAD_DIM, :] = vt[hh * HEAD_DIM:(hh + 1) * HEAD_DIM]
        vt_ref[hh, HEAD_DIM:, :] = jnp.ones((FOX_SUM_ROWS, seq), BF16)

    units = [(i, hh) for i in range(seq // tq) for hh in range(FOX_HEADS_PER_STEP)]

    def scores(u):
        i, hh = units[u]
        slot = u % FOX_SCORE_SLOTS
        r0, r1 = i * tq, (i + 1) * tq
        qa = qa_ref[hh, r0:r1, :]
        s_d = lax.dot_general(ka_ref[hh, r0:r1, :], qa, _NT_DIMS, preferred_element_type=F32)
        s_ref[slot, r0:r1, :] = jnp.where(causal, s_d, NEG_INF)
        if i > 0:
            s_ref[slot, 0:r0, :] = lax.dot_general(ka_ref[hh, 0:r0, :], qa, _NT_DIMS,
                                                   preferred_element_type=F32)

    def attend(u):
        i, hh = units[u]
        slot = u % FOX_SCORE_SLOTS
        r0, r1 = i * tq, (i + 1) * tq
        head = pl.program_id(1) * FOX_HEADS_PER_STEP + hh
        cbq = cbt_ref[pl.ds(head, 1), r0:r1]
        m = jnp.max(s_ref[slot, 0:r1, :], axis=0, keepdims=True)
        shift = cbq - (m + cbq)
        p_ref[hh, 0:r1, :] = jnp.exp2(s_ref[slot, 0:r1, :] + shift).astype(BF16)
        ot = jnp.dot(vt_ref[hh, :, 0:r1], p_ref[hh, 0:r1, :], preferred_element_type=F32)
        return ot[0:HEAD_DIM] * (1.0 / ot[HEAD_DIM:HEAD_DIM + 1])

    ahead = FOX_SCORE_SLOTS - 1
    for u in range(min(ahead, len(units))):
        scores(u)
    outs = []
    for u, (i, hh) in enumerate(units):
        if u + ahead < len(units):
            scores(u + ahead)
        outs.append(attend(u))
        if hh == FOX_HEADS_PER_STEP - 1:
            o_ref[i * tq:(i + 1) * tq, :] = jnp.concatenate(outs, axis=0).T.astype(o_ref.dtype)
            outs = []


def _fox_attention(qkv, cumt, kext):
    b, s, _ = qkv.shape
    hps = FOX_HEADS_PER_STEP
    npair = N_HEADS // hps
    lw = jnp.asarray(_fox_lane_words())
    ncol = D_MODEL // LANES
    return pl.pallas_call(
        _fox_kernel,
        grid=(b, npair),
        in_specs=[pl.BlockSpec((None, s, LANES), lambda bi, hp: (bi, 0, hp)),
                  pl.BlockSpec((None, s, LANES), lambda bi, hp: (bi, 0, ncol + hp)),
                  pl.BlockSpec((None, s, LANES), lambda bi, hp: (bi, 0, 2 * ncol + hp)),
                  pl.BlockSpec((None, s, LANES), lambda bi, hp: (bi, 0, hp)),
                  pl.BlockSpec((None, N_HEADS, s), lambda bi, hp: (bi, 0, 0)),
                  pl.BlockSpec((SUBLANES, LANES), lambda bi, hp: (0, 0))],
        out_specs=pl.BlockSpec((None, s, LANES), lambda bi, hp: (bi, 0, hp)),
        out_shape=jax.ShapeDtypeStruct((b, s, D_MODEL), BF16),
        scratch_shapes=[pltpu.VMEM((hps, s, LANES), BF16), pltpu.VMEM((hps, s, LANES), BF16),
                        pltpu.VMEM((hps, HEAD_DIM + FOX_SUM_ROWS, s), BF16),
                        pltpu.VMEM((FOX_SCORE_SLOTS, s, FOX_TQ), F32), pltpu.VMEM((hps, s, FOX_TQ), BF16)],
        compiler_params=_params("arbitrary", "arbitrary"),
        name="fox_attn",
    )(qkv, qkv, qkv, kext, cumt, lw)


def _deepnorm(xres, y, g, b, alpha):
    z = alpha * xres + y
    mu = jnp.mean(z, axis=-1, keepdims=True)
    zc = z - mu
    var = jnp.mean(zc * zc, axis=-1, keepdims=True)
    return zc * lax.rsqrt(var + LN_EPS) * g + b


def _oproj_ln_kernel(alpha, o_ref, xres_ref, w_ref, g_ref, b_ref, xo_ref, xob_ref, wb_ref):
    @pl.when(pl.program_id(0) == 0)
    def _():
        wb_ref[...] = w_ref[...].astype(BF16)

    y = jnp.dot(o_ref[...], wb_ref[...], preferred_element_type=F32)
    out = _deepnorm(xres_ref[...], y, g_ref[...], b_ref[...], alpha)
    xo_ref[...] = out
    xob_ref[...] = out.astype(BF16)


def _oproj_ln(o, xres, w, layer, g, b, alpha, tm, name):
    m, k = o.shape
    n = w.shape[2]
    return pl.pallas_call(
        functools.partial(_oproj_ln_kernel, alpha),
        grid=(m // tm,),
        in_specs=[pl.BlockSpec((tm, k), lambda i: (i, 0)),
                  pl.BlockSpec((tm, n), lambda i: (i, 0)),
                  pl.BlockSpec((None, k, n), lambda i: (layer, 0, 0), pipeline_mode=pl.Buffered(1)),
                  pl.BlockSpec((1, n), lambda i: (0, 0)),
                  pl.BlockSpec((1, n), lambda i: (0, 0))],
        out_specs=[pl.BlockSpec((tm, n), lambda i: (i, 0)),
                   pl.BlockSpec((tm, n), lambda i: (i, 0))],
        out_shape=[jax.ShapeDtypeStruct((m, n), F32), jax.ShapeDtypeStruct((m, n), BF16)],
        scratch_shapes=[pltpu.VMEM((k, n), BF16)],
        compiler_params=_params("arbitrary"),
        name=name,
    )(o, xres, w, g, b)


CONV_WIDTH = 3
FFN_TF = 512
FFN_UP_TM = 1024


def _ffn_up_kernel(tiles_per_seq, x_ref, wg_ref, wu_ref, cwg_ref, cwu_ref, cbg_ref, cbu_ref,
                   a_ref, wb_ref, h_ref):
    i = pl.program_id(1)
    tm = x_ref.shape[0]
    tf = a_ref.shape[1]
    halo = SUBLANES

    @pl.when(i == 0)
    def _():
        wb_ref[:, 0:tf] = wg_ref[...].astype(BF16)
        wb_ref[:, tf:2 * tf] = wu_ref[...].astype(BF16)

    @pl.when(i % tiles_per_seq == 0)
    def _():
        h_ref[0:halo, :] = jnp.zeros((halo, h_ref.shape[1]), F32)

    h_ref[halo:halo + tm, :] = jnp.dot(x_ref[...], wb_ref[...], preferred_element_type=F32)

    def conv(cols, cw_ref, cb_ref):
        out = cb_ref[...]
        for tap in range(CONV_WIDTH):
            start = halo - (CONV_WIDTH - 1) + tap
            out = out + cw_ref[tap:tap + 1, :] * h_ref[start:start + tm, cols]
        return out

    g = conv(slice(0, tf), cwg_ref, cbg_ref)
    u = conv(slice(tf, 2 * tf), cwu_ref, cbu_ref)
    a_ref[...] = (g * jax.nn.sigmoid(g) * u).astype(a_ref.dtype)
    h_ref[0:halo, :] = h_ref[tm:tm + halo, :]


def _ffn_up(xb, w_up, conv_w, conv_b, layer, seq):
    m, k = xb.shape
    tm, tf = FFN_UP_TM, FFN_TF
    nf = D_FF // tf
    cb2 = conv_b.reshape(conv_b.shape[0], 1, 2 * D_FF)
    return pl.pallas_call(
        functools.partial(_ffn_up_kernel, seq // tm),
        grid=(nf, m // tm),
        in_specs=[pl.BlockSpec((tm, k), lambda j, i: (i, 0)),
                  pl.BlockSpec((None, k, tf), lambda j, i: (layer, 0, j)),
                  pl.BlockSpec((None, k, tf), lambda j, i: (layer, 0, j + nf)),
                  pl.BlockSpec((None, CONV_WIDTH, tf), lambda j, i: (layer, 0, j)),
                  pl.BlockSpec((None, CONV_WIDTH, tf), lambda j, i: (layer, 0, j + nf)),
                  pl.BlockSpec((None, 1, tf), lambda j, i: (layer, 0, j)),
                  pl.BlockSpec((None, 1, tf), lambda j, i: (layer, 0, j + nf))],
        out_specs=pl.BlockSpec((tm, tf), lambda j, i: (i, j)),
        out_shape=jax.ShapeDtypeStruct((m, D_FF), BF16),
        scratch_shapes=[pltpu.VMEM((k, 2 * tf), BF16), pltpu.VMEM((tm + 2 * SUBLANES, 2 * tf), F32)],
        compiler_params=_params("arbitrary", "arbitrary"),
        name="ffn_up",
    )(xb, w_up, w_up, conv_w, conv_w, cb2, cb2)


FFN_DOWN_TM = 256
FFN_DOWN_WCHUNK = 512


def _ffn_down_ln_kernel(alpha, n_load, emit_bf16, a_ref, xres_ref, w_ref, g_ref, b_ref, *rest):
    if emit_bf16:
        xo_ref, xob_ref, wb_ref = rest
    else:
        (xo_ref, wb_ref), xob_ref = rest, None
    step = pl.program_id(0)
    rows = w_ref.shape[0]

    @pl.when(step < n_load)
    def _():
        off = pl.multiple_of(step * rows, rows)
        wb_ref[pl.ds(off, rows), :] = w_ref[...].astype(BF16)

    @pl.when(step >= n_load)
    def _():
        y = jnp.dot(a_ref[...], wb_ref[...], preferred_element_type=F32)
        out = _deepnorm(xres_ref[...], y, g_ref[...], b_ref[...], alpha)
        xo_ref[...] = out
        if emit_bf16:
            xob_ref[...] = out.astype(BF16)


def _ffn_down_ln(a, xres, w, layer, g, b, alpha, emit_bf16, name):
    m, k = a.shape
    n = w.shape[2]
    tm, rows = FFN_DOWN_TM, FFN_DOWN_WCHUNK
    n_load = k // rows
    tile = lambda s: (jnp.maximum(s - n_load, 0), 0)
    out_specs = [pl.BlockSpec((tm, n), tile)]
    out_shape = [jax.ShapeDtypeStruct((m, n), F32)]
    if emit_bf16:
        out_specs.append(pl.BlockSpec((tm, n), tile))
        out_shape.append(jax.ShapeDtypeStruct((m, n), BF16))
    return pl.pallas_call(
        functools.partial(_ffn_down_ln_kernel, alpha, n_load, emit_bf16),
        grid=(n_load + m // tm,),
        in_specs=[pl.BlockSpec((tm, k), tile),
                  pl.BlockSpec((tm, n), tile),
                  pl.BlockSpec((None, rows, n), lambda s: (layer, jnp.minimum(s, n_load - 1), 0)),
                  pl.BlockSpec((1, n), lambda s: (0, 0)),
                  pl.BlockSpec((1, n), lambda s: (0, 0))],
        out_specs=out_specs,
        out_shape=out_shape,
        scratch_shapes=[pltpu.VMEM((k, n), BF16)],
        compiler_params=_params("arbitrary"),
        name=name,
    )(a, xres, w, g, b)


def kernel(x, a_w_qkv, a_w_o, a_sinks, rel_bias, b_w_qkvf, b_f_bias, b_w_o,
           ffn_w_up, ffn_conv_w, ffn_conv_b, ffn_w_down, ln_g, ln_b):
    batch, seq, d = x.shape
    depth = ffn_w_up.shape[0]
    alpha = (2 * depth) ** 0.25
    m = batch * seq
    xf = x.reshape(m, d)
    xb = None
    biasmask = _swa_bias(rel_bias)
    for layer in range(depth):
        j = layer // 2
        g = ln_g[layer].reshape(2, 1, d)
        bt = ln_b[layer].reshape(2, 1, d)
        if layer % 2 == 0:
            n_qkv = D_MODEL + 2 * A_KV_DIM
            src = xf if xb is None else xb
            qkv = _ws_matmul(src, a_w_qkv, j, n_qkv, n_qkv // 2, 512, "swa_qkv")
            o = _swa_attention(qkv.reshape(batch, seq, n_qkv), a_sinks[j], biasmask)
            w_o = a_w_o
        else:
            w_t = jnp.swapaxes(b_w_qkvf, 1, 2)
            qkv = _ws_matmul(xb, w_t, j, 3 * D_MODEL, 1024, 1024, "fox_qkv", w_is_transposed=True)
            fb3 = jnp.tile(b_f_bias[j].reshape(1, N_HEADS), (1, FOX_PIECES))
            cumt, kext = _fox_gate(xb, w_t, j, fb3, batch, seq)
            o = _fox_attention(qkv.reshape(batch, seq, 3 * D_MODEL), cumt, kext)
            w_o = b_w_o
        xf, xb = _oproj_ln(o.reshape(m, d), xf, w_o, j, g[0], bt[0], alpha, 256, "oproj_ln")
        a = _ffn_up(xb, ffn_w_up, ffn_conv_w, ffn_conv_b, layer, seq)
        last = layer == depth - 1
        res = _ffn_down_ln(a, xf, ffn_w_down, layer, g[1], bt[1], alpha, not last, "ffn_down_ln")
        xf, xb = (res[0], None) if last else res
    return xf.reshape(batch, seq, d)
```

```python
import functools
import math

import jax
import jax.numpy as jnp
import numpy as np
from jax import lax
from jax.experimental import pallas as pl
from jax.experimental.pallas import tpu as pltpu

D_MODEL = 2048
HEAD_DIM = 64
BLOCK = 128
N_HEADS = D_MODEL // HEAD_DIM
A_KV_HEADS = 4
A_GROUP = N_HEADS // A_KV_HEADS
A_KV_DIM = A_KV_HEADS * HEAD_DIM
REL_BUCKETS = 32
REL_MAX_DIST = 128
D_FF = 5632
LN_EPS = 1e-5
SCALE = HEAD_DIM ** -0.5

SUBLANES = 8
LANES = 128

BF16 = jnp.bfloat16
F32 = jnp.float32
NEG_INF = float("-inf")

_NT_DIMS = (((1,), (1,)), ((), ()))


def _params(*sem):
    return pltpu.CompilerParams(dimension_semantics=sem)


LOG2E = math.log2(math.e)
Q_SCALE = SCALE * LOG2E


def _ws_matmul_kernel(w_is_transposed, q_cols, x_ref, w_ref, o_ref, wb_ref):
    @pl.when(pl.program_id(1) == 0)
    def _():
        wb_ref[...] = w_ref[...].astype(BF16)

    x = x_ref[...].astype(BF16)
    if w_is_transposed:
        y = lax.dot_general(x, wb_ref[...], _NT_DIMS, preferred_element_type=F32)
    else:
        y = jnp.dot(x, wb_ref[...], preferred_element_type=F32)
    tn = o_ref.shape[1]
    col = pl.program_id(0) * tn + lax.broadcasted_iota(jnp.int32, (1, tn), 1)
    y = y * jnp.where(col < q_cols, Q_SCALE, 1.0)
    o_ref[...] = y.astype(o_ref.dtype)


def _ws_matmul(x, w, layer, n_out, tn, tm, name, w_is_transposed=False):
    m, k = x.shape
    if w_is_transposed:
        w_spec = pl.BlockSpec((None, tn, k), lambda j, i: (layer, j, 0))
        wb_shape = (tn, k)
    else:
        w_spec = pl.BlockSpec((None, k, tn), lambda j, i: (layer, 0, j))
        wb_shape = (k, tn)
    return pl.pallas_call(
        functools.partial(_ws_matmul_kernel, w_is_transposed, D_MODEL),
        grid=(n_out // tn, m // tm),
        in_specs=[pl.BlockSpec((tm, k), lambda j, i: (i, 0)), w_spec],
        out_specs=pl.BlockSpec((tm, tn), lambda j, i: (i, j)),
        out_shape=jax.ShapeDtypeStruct((m, n_out), BF16),
        scratch_shapes=[pltpu.VMEM(wb_shape, BF16)],
        compiler_params=_params("arbitrary", "arbitrary"),
        name=name,
    )(x, w)


def _t5_bucket_map():
    qi = np.arange(BLOCK)[None, :]
    kj = np.arange(2 * BLOCK)[:, None]
    dist = BLOCK + qi - kj
    n = np.maximum(dist, 0)
    max_exact = REL_BUCKETS // 2
    nf = np.maximum(n, 1).astype(np.float32)
    large = max_exact + (np.log(nf / max_exact) / math.log(REL_MAX_DIST / max_exact)
                         * (REL_BUCKETS - max_exact)).astype(np.int32)
    large = np.minimum(large, REL_BUCKETS - 1)
    bucket = np.where(n < max_exact, n, large)
    visible = (dist >= 0) & (dist < BLOCK)
    return np.where(visible, bucket, -1).astype(np.int32)


def _swa_bias_kernel(rb_ref, bk_ref, o_ref):
    bk = bk_ref[...]
    key = lax.broadcasted_iota(jnp.int32, bk.shape, 0)

    def head(h, _):
        acc = jnp.full(bk.shape, NEG_INF, F32)
        for b in range(REL_BUCKETS):
            acc = jnp.where(bk == b, rb_ref[b, h] * LOG2E, acc)
        o_ref[0, h] = acc
        o_ref[1, h] = jnp.where(key < BLOCK, NEG_INF, acc)
        return 0

    lax.fori_loop(0, N_HEADS, head, 0)


def _swa_bias(rel_bias):
    bk = jnp.asarray(_t5_bucket_map())
    return pl.pallas_call(
        _swa_bias_kernel,
        in_specs=[pl.BlockSpec(memory_space=pltpu.SMEM),
                  pl.BlockSpec(memory_space=pltpu.VMEM)],
        out_specs=pl.BlockSpec(memory_space=pltpu.VMEM),
        out_shape=jax.ShapeDtypeStruct((2, N_HEADS, 2 * BLOCK, BLOCK), F32),
        name="swa_bias",
    )(rel_bias, bk)


_TN_DIMS = (((0,), (0,)), ((), ()))
SWA_SCORES_AHEAD = 2


def _swa_kernel(sink_ref, q_ref, kp_ref, kc_ref, vp_ref, vc_ref, bm_ref, o_ref,
                qs_ref, s_ref, p_ref, linv_ref):
    def scores(kh):
        ks = slice(kh * HEAD_DIM, (kh + 1) * HEAD_DIM)
        for g in range(A_GROUP):
            h = kh * A_GROUP + g
            qs_ref[kh, g * BLOCK:(g + 1) * BLOCK, :] = q_ref[:, h * HEAD_DIM:(h + 1) * HEAD_DIM]
        k2 = jnp.concatenate([kp_ref[:, ks], kc_ref[:, ks]], axis=0)
        s_ref[kh] = lax.dot_general(k2, qs_ref[kh], _NT_DIMS, preferred_element_type=F32)

    def attend(kh):
        for g in range(A_GROUP):
            h = kh * A_GROUP + g
            cols = slice(g * BLOCK, (g + 1) * BLOCK)
            s = s_ref[kh, :, cols] + bm_ref[h]
            sink = sink_ref[h] * LOG2E
            m = jnp.maximum(jnp.max(s, axis=0, keepdims=True), sink)
            p = jnp.exp2(s - m)
            denom = jnp.sum(p, axis=0, keepdims=True) + jnp.exp2(sink - m)
            p_ref[kh, :, cols] = p.astype(BF16)
            linv_ref[kh, :, cols] = 1.0 / denom
        ks = slice(kh * HEAD_DIM, (kh + 1) * HEAD_DIM)
        v2 = jnp.concatenate([vp_ref[:, ks], vc_ref[:, ks]], axis=0)
        ot = lax.dot_general(v2, p_ref[kh], _TN_DIMS, preferred_element_type=F32) * linv_ref[kh]
        for g in range(A_GROUP):
            h = kh * A_GROUP + g
            o_ref[:, h * HEAD_DIM:(h + 1) * HEAD_DIM] = ot[:, g * BLOCK:(g + 1) * BLOCK].T.astype(o_ref.dtype)

    for kh in range(min(SWA_SCORES_AHEAD, A_KV_HEADS)):
        scores(kh)
    for kh in range(A_KV_HEADS):
        if kh + SWA_SCORES_AHEAD < A_KV_HEADS:
            scores(kh + SWA_SCORES_AHEAD)
        attend(kh)


def _swa_attention(qkv, sinks, biasmask):
    b, s, _ = qkv.shape
    nb = s // BLOCK
    gq = A_GROUP * BLOCK
    kcol = D_MODEL // A_KV_DIM
    prev = lambda n: jnp.maximum(n - 1, 0)
    return pl.pallas_call(
        _swa_kernel,
        grid=(b, nb),
        in_specs=[pl.BlockSpec(memory_space=pltpu.SMEM),
                  pl.BlockSpec((None, BLOCK, D_MODEL), lambda bi, n: (bi, n, 0)),
                  pl.BlockSpec((None, BLOCK, A_KV_DIM), lambda bi, n: (bi, prev(n), kcol)),
                  pl.BlockSpec((None, BLOCK, A_KV_DIM), lambda bi, n: (bi, n, kcol)),
                  pl.BlockSpec((None, BLOCK, A_KV_DIM), lambda bi, n: (bi, prev(n), kcol + 1)),
                  pl.BlockSpec((None, BLOCK, A_KV_DIM), lambda bi, n: (bi, n, kcol + 1)),
                  pl.BlockSpec((None, N_HEADS, 2 * BLOCK, BLOCK),
                               lambda bi, n: (jnp.where(n == 0, 1, 0), 0, 0, 0))],
        out_specs=pl.BlockSpec((None, BLOCK, D_MODEL), lambda bi, n: (bi, n, 0)),
        out_shape=jax.ShapeDtypeStruct((b, s, D_MODEL), BF16),
        scratch_shapes=[pltpu.VMEM((A_KV_HEADS, gq, HEAD_DIM), BF16),
                        pltpu.VMEM((A_KV_HEADS, 2 * BLOCK, gq), F32),
                        pltpu.VMEM((A_KV_HEADS, 2 * BLOCK, gq), BF16),
                        pltpu.VMEM((A_KV_HEADS, 1, gq), F32)],
        compiler_params=_params("arbitrary", "arbitrary"),
        name="swa_attn",
    )(sinks, qkv, qkv, qkv, qkv, qkv, biasmask)


FOX_CUM_BLOCK = 256
FOX_TQ = 256
FOX_HEADS_PER_PAIR = LANES // HEAD_DIM
FOX_PAIRS_PER_STEP = 2
FOX_PIECES = 3
FOX_SUM_ROWS = 16
FOX_SCORE_SLOTS = 4
FOX_PROB_SLOTS = 2
BF16_ONE_PAIR = 0x3F803F80


def _fox_ext_lane(head, piece):
    pair, odd = divmod(head, FOX_HEADS_PER_PAIR)
    return pair * LANES + (1 - odd) * HEAD_DIM + piece


def _fox_placement():
    pm = np.zeros((FOX_PIECES * N_HEADS, D_MODEL), np.float32)
    for t in range(FOX_PIECES):
        for h in range(N_HEADS):
            pm[t * N_HEADS + h, _fox_ext_lane(h, t)] = -1.0
    return pm


def _fox_lane_words():
    w = np.zeros((SUBLANES, LANES), np.uint32)
    for hh in range(FOX_HEADS_PER_PAIR):
        own = np.zeros(LANES, bool)
        own[hh * HEAD_DIM:(hh + 1) * HEAD_DIM] = True
        ext = np.zeros(LANES, bool)
        first = _fox_ext_lane(hh, 0)
        ext[first:first + FOX_PIECES] = True
        w[3 * hh + 0] = np.where(own, 0xFFFFFFFF, 0)
        w[3 * hh + 1] = np.where(ext, 0xFFFFFFFF, 0)
        w[3 * hh + 2] = np.where(ext, BF16_ONE_PAIR, 0)
    return w


def _log_sigmoid(x):
    return jnp.minimum(x, 0.0) - jnp.log1p(jnp.exp(-jnp.abs(x)))


def _round_bf16(x):
    return x.astype(BF16).astype(F32)


def _fox_gate_kernel(x_ref, wf_ref, fb_ref, pm_ref, cumt_ref, kext_ref, w3_ref):
    @pl.when(pl.program_id(0) == 0)
    def _():
        w = wf_ref[...].astype(BF16)
        for t in range(FOX_PIECES):
            w3_ref[t * N_HEADS:(t + 1) * N_HEADS, :] = w

    fz = lax.dot_general(x_ref[...], w3_ref[...], _NT_DIMS, preferred_element_type=F32) + fb_ref[...]
    lf = _log_sigmoid(fz)
    cb = FOX_CUM_BLOCK
    row = lax.broadcasted_iota(jnp.int32, (cb, cb), 0)
    col = lax.broadcasted_iota(jnp.int32, (cb, cb), 1)
    tri = jnp.where(row >= col, 1.0, 0.0).astype(BF16)
    tri3 = jnp.concatenate([tri] * FOX_PIECES, axis=1)
    group = lax.broadcasted_iota(jnp.int32, (cb, lf.shape[1]), 1) // N_HEADS
    pm = pm_ref[...]
    n_blocks = lf.shape[0] // cb
    local = []
    for blk in range(n_blocks):
        x = lf[blk * cb:(blk + 1) * cb]
        hi = _round_bf16(x)
        r1 = x - hi
        mid = _round_bf16(r1)
        pieces = jnp.concatenate([hi, mid, r1 - mid], axis=0).astype(BF16)
        local.append(jnp.dot(tri3, pieces, preferred_element_type=F32))
    carry = jnp.zeros((1, lf.shape[1]), F32)
    for blk in range(n_blocks):
        rows = slice(blk * cb, (blk + 1) * cb)
        c = local[blk] + carry
        carry = carry + local[blk][cb - 1:cb]
        c2 = c * LOG2E
        cumt_ref[:, rows] = c2[:, :N_HEADS].T
        chi = _round_bf16(c2)
        cr1 = c2 - chi
        cmid = _round_bf16(cr1)
        piece = jnp.where(group == 0, chi, jnp.where(group == 1, cmid, cr1 - cmid)).astype(BF16)
        kext_ref[rows, :] = jnp.dot(piece, pm, preferred_element_type=F32).astype(BF16)


def _fox_gate(xb, w_qkvf_t, layer, fb3, batch, seq):
    pm = jnp.asarray(_fox_placement(), BF16)
    n3 = FOX_PIECES * N_HEADS
    gate_row = 3 * D_MODEL // N_HEADS
    return pl.pallas_call(
        _fox_gate_kernel,
        grid=(batch,),
        in_specs=[pl.BlockSpec((seq, D_MODEL), lambda bi: (bi, 0)),
                  pl.BlockSpec((None, N_HEADS, D_MODEL), lambda bi: (layer, gate_row, 0)),
                  pl.BlockSpec((1, n3), lambda bi: (0, 0)),
                  pl.BlockSpec((n3, D_MODEL), lambda bi: (0, 0))],
        out_specs=[pl.BlockSpec((None, N_HEADS, seq), lambda bi: (bi, 0, 0)),
                   pl.BlockSpec((None, seq, D_MODEL), lambda bi: (bi, 0, 0))],
        out_shape=[jax.ShapeDtypeStruct((batch, N_HEADS, seq), F32),
                   jax.ShapeDtypeStruct((batch, seq, D_MODEL), BF16)],
        scratch_shapes=[pltpu.VMEM((n3, D_MODEL), BF16)],
        compiler_params=_params("arbitrary"),
        name="fox_gate",
    )(xb, w_qkvf_t, fb3, pm)


def _fox_kernel(q_ref, k_ref, v_ref, kext_ref, cbt_ref, lw_ref, o_ref, qa_ref, ka_ref, vt_ref,
                s_ref, p_ref):
    seq = q_ref.shape[0]
    tq = FOX_TQ
    key = lax.broadcasted_iota(jnp.int32, (tq, tq), 0)
    qry = lax.broadcasted_iota(jnp.int32, (tq, tq), 1)
    causal = key <= qry
    words = lambda x: pltpu.bitcast(x, jnp.uint32)
    for pair in range(FOX_PAIRS_PER_STEP):
        lanes = slice(pair * LANES, (pair + 1) * LANES)
        q = words(q_ref[:, lanes])
        k = words(k_ref[:, lanes])
        kext = words(kext_ref[:, lanes])
        vt = v_ref[:, lanes].T
        for hh in range(FOX_HEADS_PER_PAIR):
            h = pair * FOX_HEADS_PER_PAIR + hh
            own = lw_ref[3 * hh + 0:3 * hh + 1, :]
            ext = lw_ref[3 * hh + 1:3 * hh + 2, :]
            ones = lw_ref[3 * hh + 2:3 * hh + 3, :]
            qa_ref[h] = pltpu.bitcast((q & own) | ones, BF16)
            ka_ref[h] = pltpu.bitcast((k & ~ext) | (kext & ext), BF16)
            vt_ref[h, 0:HEAD_DIM, :] = vt[hh * HEAD_DIM:(hh + 1) * HEAD_DIM]
            vt_ref[h, HEAD_DIM:, :] = jnp.ones((FOX_SUM_ROWS, seq), BF16)

    heads = FOX_PAIRS_PER_STEP * FOX_HEADS_PER_PAIR
    units = [(i, h) for i in range(seq // tq) for h in range(heads)]

    def scores(u):
        i, h = units[u]
        slot = u % FOX_SCORE_SLOTS
        r0, r1 = i * tq, (i + 1) * tq
        qa = qa_ref[h, r0:r1, :]
        s_d = lax.dot_general(ka_ref[h, r0:r1, :], qa, _NT_DIMS, preferred_element_type=F32)
        s_ref[slot, r0:r1, :] = jnp.where(causal, s_d, NEG_INF)
        if i > 0:
            s_ref[slot, 0:r0, :] = lax.dot_general(ka_ref[h, 0:r0, :], qa, _NT_DIMS,
                                                   preferred_element_type=F32)

    def attend(u):
        i, h = units[u]
        slot = u % FOX_SCORE_SLOTS
        pslot = u % FOX_PROB_SLOTS
        r0, r1 = i * tq, (i + 1) * tq
        head = pl.program_id(1) * heads + h
        cbq = cbt_ref[pl.ds(head, 1), r0:r1]
        m = jnp.max(s_ref[slot, 0:r1, :], axis=0, keepdims=True)
        shift = cbq - (m + cbq)
        p_ref[pslot, 0:r1, :] = jnp.exp2(s_ref[slot, 0:r1, :] + shift).astype(BF16)
        ot = jnp.dot(vt_ref[h, :, 0:r1], p_ref[pslot, 0:r1, :], preferred_element_type=F32)
        return ot[0:HEAD_DIM] * (1.0 / ot[HEAD_DIM:HEAD_DIM + 1])

    ahead = FOX_SCORE_SLOTS - 1
    for u in range(min(ahead, len(units))):
        scores(u)
    outs = []
    for u, (i, h) in enumerate(units):
        if u + ahead < len(units):
            scores(u + ahead)
        outs.append(attend(u))
        if h == heads - 1:
            o_ref[i * tq:(i + 1) * tq, :] = jnp.concatenate(outs, axis=0).T.astype(o_ref.dtype)
            outs = []


def _fox_attention(qkv, cumt, kext):
    b, s, _ = qkv.shape
    heads = FOX_PAIRS_PER_STEP * FOX_HEADS_PER_PAIR
    width = FOX_PAIRS_PER_STEP * LANES
    lw = jnp.asarray(_fox_lane_words())
    ncol = D_MODEL // width
    return pl.pallas_call(
        _fox_kernel,
        grid=(b, N_HEADS // heads),
        in_specs=[pl.BlockSpec((None, s, width), lambda bi, hp: (bi, 0, hp)),
                  pl.BlockSpec((None, s, width), lambda bi, hp: (bi, 0, ncol + hp)),
                  pl.BlockSpec((None, s, width), lambda bi, hp: (bi, 0, 2 * ncol + hp)),
                  pl.BlockSpec((None, s, width), lambda bi, hp: (bi, 0, hp)),
                  pl.BlockSpec((None, N_HEADS, s), lambda bi, hp: (bi, 0, 0)),
                  pl.BlockSpec((SUBLANES, LANES), lambda bi, hp: (0, 0))],
        out_specs=pl.BlockSpec((None, s, width), lambda bi, hp: (bi, 0, hp)),
        out_shape=jax.ShapeDtypeStruct((b, s, D_MODEL), BF16),
        scratch_shapes=[pltpu.VMEM((heads, s, LANES), BF16), pltpu.VMEM((heads, s, LANES), BF16),
                        pltpu.VMEM((heads, HEAD_DIM + FOX_SUM_ROWS, s), BF16),
                        pltpu.VMEM((FOX_SCORE_SLOTS, s, FOX_TQ), F32),
                        pltpu.VMEM((FOX_PROB_SLOTS, s, FOX_TQ), BF16)],
        compiler_params=_params("arbitrary", "arbitrary"),
        name="fox_attn",
    )(qkv, qkv, qkv, kext, cumt, lw)


def _deepnorm(xres, y, g, b, alpha):
    z = alpha * xres + y
    mu = jnp.mean(z, axis=-1, keepdims=True)
    zc = z - mu
    var = jnp.mean(zc * zc, axis=-1, keepdims=True)
    return zc * lax.rsqrt(var + LN_EPS) * g + b


def _oproj_ln_kernel(alpha, o_ref, xres_ref, w_ref, g_ref, b_ref, xo_ref, xob_ref, wb_ref):
    @pl.when(pl.program_id(0) == 0)
    def _():
        wb_ref[...] = w_ref[...].astype(BF16)

    y = jnp.dot(o_ref[...], wb_ref[...], preferred_element_type=F32)
    out = _deepnorm(xres_ref[...], y, g_ref[...], b_ref[...], alpha)
    xo_ref[...] = out
    xob_ref[...] = out.astype(BF16)


def _oproj_ln(o, xres, w, layer, g, b, alpha, tm, name):
    m, k = o.shape
    n = w.shape[2]
    return pl.pallas_call(
        functools.partial(_oproj_ln_kernel, alpha),
        grid=(m // tm,),
        in_specs=[pl.BlockSpec((tm, k), lambda i: (i, 0)),
                  pl.BlockSpec((tm, n), lambda i: (i, 0)),
                  pl.BlockSpec((None, k, n), lambda i: (layer, 0, 0), pipeline_mode=pl.Buffered(1)),
                  pl.BlockSpec((1, n), lambda i: (0, 0)),
                  pl.BlockSpec((1, n), lambda i: (0, 0))],
        out_specs=[pl.BlockSpec((tm, n), lambda i: (i, 0)),
                   pl.BlockSpec((tm, n), lambda i: (i, 0))],
        out_shape=[jax.ShapeDtypeStruct((m, n), F32), jax.ShapeDtypeStruct((m, n), BF16)],
        scratch_shapes=[pltpu.VMEM((k, n), BF16)],
        compiler_params=_params("arbitrary"),
        name=name,
    )(o, xres, w, g, b)


CONV_WIDTH = 3
FFN_TF = 512
FFN_UP_TM = 1024


def _ffn_up_kernel(tiles_per_seq, x_ref, wg_ref, wu_ref, cwg_ref, cwu_ref, cbg_ref, cbu_ref,
                   a_ref, wb_ref, h_ref):
    i = pl.program_id(1)
    tm = x_ref.shape[0]
    tf = a_ref.shape[1]
    halo = SUBLANES

    @pl.when(i == 0)
    def _():
        wb_ref[:, 0:tf] = wg_ref[...].astype(BF16)
        wb_ref[:, tf:2 * tf] = wu_ref[...].astype(BF16)

    @pl.when(i % tiles_per_seq == 0)
    def _():
        h_ref[0:halo, :] = jnp.zeros((halo, h_ref.shape[1]), F32)

    h_ref[halo:halo + tm, :] = jnp.dot(x_ref[...], wb_ref[...], preferred_element_type=F32)

    def conv(cols, cw_ref, cb_ref):
        out = cb_ref[...]
        for tap in range(CONV_WIDTH):
            start = halo - (CONV_WIDTH - 1) + tap
            out = out + cw_ref[tap:tap + 1, :] * h_ref[start:start + tm, cols]
        return out

    g = conv(slice(0, tf), cwg_ref, cbg_ref)
    u = conv(slice(tf, 2 * tf), cwu_ref, cbu_ref)
    a_ref[...] = (g * jax.nn.sigmoid(g) * u).astype(a_ref.dtype)
    h_ref[0:halo, :] = h_ref[tm:tm + halo, :]


def _ffn_up(xb, w_up, conv_w, conv_b, layer, seq):
    m, k = xb.shape
    tm, tf = FFN_UP_TM, FFN_TF
    nf = D_FF // tf
    cb2 = conv_b.reshape(conv_b.shape[0], 1, 2 * D_FF)
    return pl.pallas_call(
        functools.partial(_ffn_up_kernel, seq // tm),
        grid=(nf, m // tm),
        in_specs=[pl.BlockSpec((tm, k), lambda j, i: (i, 0)),
                  pl.BlockSpec((None, k, tf), lambda j, i: (layer, 0, j)),
                  pl.BlockSpec((None, k, tf), lambda j, i: (layer, 0, j + nf)),
                  pl.BlockSpec((None, CONV_WIDTH, tf), lambda j, i: (layer, 0, j)),
                  pl.BlockSpec((None, CONV_WIDTH, tf), lambda j, i: (layer, 0, j + nf)),
                  pl.BlockSpec((None, 1, tf), lambda j, i: (layer, 0, j)),
                  pl.BlockSpec((None, 1, tf), lambda j, i: (layer, 0, j + nf))],
        out_specs=pl.BlockSpec((tm, tf), lambda j, i: (i, j)),
        out_shape=jax.ShapeDtypeStruct((m, D_FF), BF16),
        scratch_shapes=[pltpu.VMEM((k, 2 * tf), BF16), pltpu.VMEM((tm + 2 * SUBLANES, 2 * tf), F32)],
        compiler_params=_params("arbitrary", "arbitrary"),
        name="ffn_up",
    )(xb, w_up, w_up, conv_w, conv_w, cb2, cb2)


FFN_DOWN_TM = 256
FFN_DOWN_WCHUNK = 512


def _ffn_down_ln_kernel(alpha, n_load, emit_bf16, a_ref, xres_ref, w_ref, g_ref, b_ref, *rest):
    if emit_bf16:
        xo_ref, xob_ref, wb_ref = rest
    else:
        (xo_ref, wb_ref), xob_ref = rest, None
    step = pl.program_id(0)
    rows = w_ref.shape[0]

    @pl.when(step < n_load)
    def _():
        off = pl.multiple_of(step * rows, rows)
        wb_ref[pl.ds(off, rows), :] = w_ref[...].astype(BF16)

    @pl.when(step >= n_load)
    def _():
        y = jnp.dot(a_ref[...], wb_ref[...], preferred_element_type=F32)
        out = _deepnorm(xres_ref[...], y, g_ref[...], b_ref[...], alpha)
        xo_ref[...] = out
        if emit_bf16:
            xob_ref[...] = out.astype(BF16)


def _ffn_down_ln(a, xres, w, layer, g, b, alpha, emit_bf16, name):
    m, k = a.shape
    n = w.shape[2]
    tm, rows = FFN_DOWN_TM, FFN_DOWN_WCHUNK
    n_load = k // rows
    tile = lambda s: (jnp.maximum(s - n_load, 0), 0)
    out_specs = [pl.BlockSpec((tm, n), tile)]
    out_shape = [jax.ShapeDtypeStruct((m, n), F32)]
    if emit_bf16:
        out_specs.append(pl.BlockSpec((tm, n), tile))
        out_shape.append(jax.ShapeDtypeStruct((m, n), BF16))
    return pl.pallas_call(
        functools.partial(_ffn_down_ln_kernel, alpha, n_load, emit_bf16),
        grid=(n_load + m // tm,),
        in_specs=[pl.BlockSpec((tm, k), tile),
                  pl.BlockSpec((tm, n), tile),
                  pl.BlockSpec((None, rows, n), lambda s: (layer, jnp.minimum(s, n_load - 1), 0)),
                  pl.BlockSpec((1, n), lambda s: (0, 0)),
                  pl.BlockSpec((1, n), lambda s: (0, 0))],
        out_specs=out_specs,
        out_shape=out_shape,
        scratch_shapes=[pltpu.VMEM((k, n), BF16)],
        compiler_params=_params("arbitrary"),
        name=name,
    )(a, xres, w, g, b)


def kernel(x, a_w_qkv, a_w_o, a_sinks, rel_bias, b_w_qkvf, b_f_bias, b_w_o,
           ffn_w_up, ffn_conv_w, ffn_conv_b, ffn_w_down, ln_g, ln_b):
    batch, seq, d = x.shape
    depth = ffn_w_up.shape[0]
    alpha = (2 * depth) ** 0.25
    m = batch * seq
    xf = x.reshape(m, d)
    xb = None
    biasmask = _swa_bias(rel_bias)
    for layer in range(depth):
        j = layer // 2
        g = ln_g[layer].reshape(2, 1, d)
        bt = ln_b[layer].reshape(2, 1, d)
        if layer % 2 == 0:
            n_qkv = D_MODEL + 2 * A_KV_DIM
            src = xf if xb is None else xb
            qkv = _ws_matmul(src, a_w_qkv, j, n_qkv, n_qkv // 2, 512, "swa_qkv")
            o = _swa_attention(qkv.reshape(batch, seq, n_qkv), a_sinks[j], biasmask)
            w_o = a_w_o
        else:
            w_t = jnp.swapaxes(b_w_qkvf, 1, 2)
            qkv = _ws_matmul(xb, w_t, j, 3 * D_MODEL, 1024, 1024, "fox_qkv", w_is_transposed=True)
            fb3 = jnp.tile(b_f_bias[j].reshape(1, N_HEADS), (1, FOX_PIECES))
            cumt, kext = _fox_gate(xb, w_t, j, fb3, batch, seq)
            o = _fox_attention(qkv.reshape(batch, seq, 3 * D_MODEL), cumt, kext)
            w_o = b_w_o
        xf, xb = _oproj_ln(o.reshape(m, d), xf, w_o, j, g[0], bt[0], alpha, 256, "oproj_ln")
        a = _ffn_up(xb, ffn_w_up, ffn_conv_w, ffn_conv_b, layer, seq)
        last = layer == depth - 1
        res = _ffn_down_ln(a, xf, ffn_w_down, layer, g[1], bt[1], alpha, not last, "ffn_down_ln")
        xf, xb = (res[0], None) if last else res
    return xf.reshape(batch, seq, d)
```

```python
import functools
import math

import jax
import jax.numpy as jnp
import numpy as np
from jax import lax
from jax.experimental import pallas as pl
from jax.experimental.pallas import tpu as pltpu

D_MODEL = 2048
HEAD_DIM = 64
BLOCK = 128
N_HEADS = D_MODEL // HEAD_DIM
A_KV_HEADS = 4
A_GROUP = N_HEADS // A_KV_HEADS
A_KV_DIM = A_KV_HEADS * HEAD_DIM
REL_BUCKETS = 32
REL_MAX_DIST = 128
D_FF = 5632
LN_EPS = 1e-5
SCALE = HEAD_DIM ** -0.5

SUBLANES = 8
LANES = 128

BF16 = jnp.bfloat16
F32 = jnp.float32
NEG_INF = float("-inf")

_NT_DIMS = (((1,), (1,)), ((), ()))


def _params(*sem):
    return pltpu.CompilerParams(dimension_semantics=sem)


LOG2E = math.log2(math.e)
Q_SCALE = SCALE * LOG2E


def _ws_matmul_kernel(w_is_transposed, q_cols, x_ref, w_ref, o_ref, wb_ref):
    @pl.when(pl.program_id(1) == 0)
    def _():
        wb_ref[...] = w_ref[...].astype(BF16)

    x = x_ref[...].astype(BF16)
    if w_is_transposed:
        y = lax.dot_general(x, wb_ref[...], _NT_DIMS, preferred_element_type=F32)
    else:
        y = jnp.dot(x, wb_ref[...], preferred_element_type=F32)
    tn = o_ref.shape[1]
    col = pl.program_id(0) * tn + lax.broadcasted_iota(jnp.int32, (1, tn), 1)
    y = y * jnp.where(col < q_cols, Q_SCALE, 1.0)
    o_ref[...] = y.astype(o_ref.dtype)


def _ws_matmul(x, w, layer, n_out, tn, tm, name, w_is_transposed=False):
    m, k = x.shape
    if w_is_transposed:
        w_spec = pl.BlockSpec((None, tn, k), lambda j, i: (layer, j, 0))
        wb_shape = (tn, k)
    else:
        w_spec = pl.BlockSpec((None, k, tn), lambda j, i: (layer, 0, j))
        wb_shape = (k, tn)
    return pl.pallas_call(
        functools.partial(_ws_matmul_kernel, w_is_transposed, D_MODEL),
        grid=(n_out // tn, m // tm),
        in_specs=[pl.BlockSpec((tm, k), lambda j, i: (i, 0)), w_spec],
        out_specs=pl.BlockSpec((tm, tn), lambda j, i: (i, j)),
        out_shape=jax.ShapeDtypeStruct((m, n_out), BF16),
        scratch_shapes=[pltpu.VMEM(wb_shape, BF16)],
        compiler_params=_params("arbitrary", "arbitrary"),
        name=name,
    )(x, w)


def _t5_bucket_map():
    qi = np.arange(BLOCK)[None, :]
    kj = np.arange(2 * BLOCK)[:, None]
    dist = BLOCK + qi - kj
    n = np.maximum(dist, 0)
    max_exact = REL_BUCKETS // 2
    nf = np.maximum(n, 1).astype(np.float32)
    large = max_exact + (np.log(nf / max_exact) / math.log(REL_MAX_DIST / max_exact)
                         * (REL_BUCKETS - max_exact)).astype(np.int32)
    large = np.minimum(large, REL_BUCKETS - 1)
    bucket = np.where(n < max_exact, n, large)
    visible = (dist >= 0) & (dist < BLOCK)
    return np.where(visible, bucket, -1).astype(np.int32)


def _swa_bias_kernel(rb_ref, bk_ref, o_ref):
    bk = bk_ref[...]
    key = lax.broadcasted_iota(jnp.int32, bk.shape, 0)

    def head(h, _):
        acc = jnp.full(bk.shape, NEG_INF, F32)
        for b in range(REL_BUCKETS):
            acc = jnp.where(bk == b, rb_ref[b, h] * LOG2E, acc)
        o_ref[0, h] = acc
        o_ref[1, h] = jnp.where(key < BLOCK, NEG_INF, acc)
        return 0

    lax.fori_loop(0, N_HEADS, head, 0)


def _swa_bias(rel_bias):
    bk = jnp.asarray(_t5_bucket_map())
    return pl.pallas_call(
        _swa_bias_kernel,
        in_specs=[pl.BlockSpec(memory_space=pltpu.SMEM),
                  pl.BlockSpec(memory_space=pltpu.VMEM)],
        out_specs=pl.BlockSpec(memory_space=pltpu.VMEM),
        out_shape=jax.ShapeDtypeStruct((2, N_HEADS, 2 * BLOCK, BLOCK), F32),
        name="swa_bias",
    )(rel_bias, bk)


_TN_DIMS = (((0,), (0,)), ((), ()))
SWA_SCORES_AHEAD = 2


def _swa_kernel(sink_ref, q_ref, kp_ref, kc_ref, vp_ref, vc_ref, bm_ref, o_ref,
                qs_ref, s_ref, p_ref, linv_ref):
    def scores(kh):
        ks = slice(kh * HEAD_DIM, (kh + 1) * HEAD_DIM)
        for g in range(A_GROUP):
            h = kh * A_GROUP + g
            qs_ref[kh, g * BLOCK:(g + 1) * BLOCK, :] = q_ref[:, h * HEAD_DIM:(h + 1) * HEAD_DIM]
        k2 = jnp.concatenate([kp_ref[:, ks], kc_ref[:, ks]], axis=0)
        s_ref[kh] = lax.dot_general(k2, qs_ref[kh], _NT_DIMS, preferred_element_type=F32)

    def attend(kh):
        for g in range(A_GROUP):
            h = kh * A_GROUP + g
            cols = slice(g * BLOCK, (g + 1) * BLOCK)
            s = s_ref[kh, :, cols] + bm_ref[h]
            sink = sink_ref[h] * LOG2E
            m = jnp.maximum(jnp.max(s, axis=0, keepdims=True), sink)
            p = jnp.exp2(s - m)
            denom = jnp.sum(p, axis=0, keepdims=True) + jnp.exp2(sink - m)
            p_ref[kh, :, cols] = p.astype(BF16)
            linv_ref[kh, :, cols] = 1.0 / denom
        ks = slice(kh * HEAD_DIM, (kh + 1) * HEAD_DIM)
        v2 = jnp.concatenate([vp_ref[:, ks], vc_ref[:, ks]], axis=0)
        ot = lax.dot_general(v2, p_ref[kh], _TN_DIMS, preferred_element_type=F32) * linv_ref[kh]
        for g in range(A_GROUP):
            h = kh * A_GROUP + g
            o_ref[:, h * HEAD_DIM:(h + 1) * HEAD_DIM] = ot[:, g * BLOCK:(g + 1) * BLOCK].T.astype(o_ref.dtype)

    for kh in range(min(SWA_SCORES_AHEAD, A_KV_HEADS)):
        scores(kh)
    for kh in range(A_KV_HEADS):
        if kh + SWA_SCORES_AHEAD < A_KV_HEADS:
            scores(kh + SWA_SCORES_AHEAD)
        attend(kh)


def _swa_attention(qkv, sinks, biasmask):
    b, s, _ = qkv.shape
    nb = s // BLOCK
    gq = A_GROUP * BLOCK
    kcol = D_MODEL // A_KV_DIM
    prev = lambda n: jnp.maximum(n - 1, 0)
    return pl.pallas_call(
        _swa_kernel,
        grid=(b, nb),
        in_specs=[pl.BlockSpec(memory_space=pltpu.SMEM),
                  pl.BlockSpec((None, BLOCK, D_MODEL), lambda bi, n: (bi, n, 0)),
                  pl.BlockSpec((None, BLOCK, A_KV_DIM), lambda bi, n: (bi, prev(n), kcol)),
                  pl.BlockSpec((None, BLOCK, A_KV_DIM), lambda bi, n: (bi, n, kcol)),
                  pl.BlockSpec((None, BLOCK, A_KV_DIM), lambda bi, n: (bi, prev(n), kcol + 1)),
                  pl.BlockSpec((None, BLOCK, A_KV_DIM), lambda bi, n: (bi, n, kcol + 1)),
                  pl.BlockSpec((None, N_HEADS, 2 * BLOCK, BLOCK),
                               lambda bi, n: (jnp.where(n == 0, 1, 0), 0, 0, 0))],
        out_specs=pl.BlockSpec((None, BLOCK, D_MODEL), lambda bi, n: (bi, n, 0)),
        out_shape=jax.ShapeDtypeStruct((b, s, D_MODEL), BF16),
        scratch_shapes=[pltpu.VMEM((A_KV_HEADS, gq, HEAD_DIM), BF16),
                        pltpu.VMEM((A_KV_HEADS, 2 * BLOCK, gq), F32),
                        pltpu.VMEM((A_KV_HEADS, 2 * BLOCK, gq), BF16),
                        pltpu.VMEM((A_KV_HEADS, 1, gq), F32)],
        compiler_params=_params("arbitrary", "arbitrary"),
        name="swa_attn",
    )(sinks, qkv, qkv, qkv, qkv, qkv, biasmask)


FOX_CUM_BLOCK = 256
FOX_TQ = 256
FOX_HEADS_PER_PAIR = LANES // HEAD_DIM
FOX_PAIRS_PER_STEP = 2
FOX_PIECES = 3
FOX_SUM_ROWS = 16
FOX_SCORE_SLOTS = 4
FOX_PROB_SLOTS = 2
BF16_ONE_PAIR = 0x3F803F80


def _fox_ext_lane(head, piece):
    pair, odd = divmod(head, FOX_HEADS_PER_PAIR)
    return pair * LANES + (1 - odd) * HEAD_DIM + piece


def _fox_placement():
    pm = np.zeros((FOX_PIECES * N_HEADS, D_MODEL), np.float32)
    for t in range(FOX_PIECES):
        for h in range(N_HEADS):
            pm[t * N_HEADS + h, _fox_ext_lane(h, t)] = -1.0
    return pm


def _fox_lane_words():
    w = np.zeros((SUBLANES, LANES), np.uint32)
    for hh in range(FOX_HEADS_PER_PAIR):
        own = np.zeros(LANES, bool)
        own[hh * HEAD_DIM:(hh + 1) * HEAD_DIM] = True
        ext = np.zeros(LANES, bool)
        first = _fox_ext_lane(hh, 0)
        ext[first:first + FOX_PIECES] = True
        w[3 * hh + 0] = np.where(own, 0xFFFFFFFF, 0)
        w[3 * hh + 1] = np.where(ext, 0xFFFFFFFF, 0)
        w[3 * hh + 2] = np.where(ext, BF16_ONE_PAIR, 0)
    return w


def _log_sigmoid(x):
    return jnp.minimum(x, 0.0) - jnp.log1p(jnp.exp(-jnp.abs(x)))


def _round_bf16(x):
    return x.astype(BF16).astype(F32)


def _fox_gate_kernel(x_ref, wf_ref, fb_ref, pm_ref, cumt_ref, kext_ref, w3_ref):
    @pl.when(pl.program_id(0) == 0)
    def _():
        w = wf_ref[...].astype(BF16)
        for t in range(FOX_PIECES):
            w3_ref[t * N_HEADS:(t + 1) * N_HEADS, :] = w

    fz = lax.dot_general(x_ref[...], w3_ref[...], _NT_DIMS, preferred_element_type=F32) + fb_ref[...]
    lf = _log_sigmoid(fz)
    cb = FOX_CUM_BLOCK
    row = lax.broadcasted_iota(jnp.int32, (cb, cb), 0)
    col = lax.broadcasted_iota(jnp.int32, (cb, cb), 1)
    tri = jnp.where(row >= col, 1.0, 0.0).astype(BF16)
    tri3 = jnp.concatenate([tri] * FOX_PIECES, axis=1)
    group = lax.broadcasted_iota(jnp.int32, (cb, lf.shape[1]), 1) // N_HEADS
    pm = pm_ref[...]
    n_blocks = lf.shape[0] // cb
    local = []
    for blk in range(n_blocks):
        x = lf[blk * cb:(blk + 1) * cb]
        hi = _round_bf16(x)
        r1 = x - hi
        mid = _round_bf16(r1)
        pieces = jnp.concatenate([hi, mid, r1 - mid], axis=0).astype(BF16)
        local.append(jnp.dot(tri3, pieces, preferred_element_type=F32))
    carry = jnp.zeros((1, lf.shape[1]), F32)
    for blk in range(n_blocks):
        rows = slice(blk * cb, (blk + 1) * cb)
        c = local[blk] + carry
        carry = carry + local[blk][cb - 1:cb]
        c2 = c * LOG2E
        cumt_ref[:, rows] = c2[:, :N_HEADS].T
        chi = _round_bf16(c2)
        cr1 = c2 - chi
        cmid = _round_bf16(cr1)
        piece = jnp.where(group == 0, chi, jnp.where(group == 1, cmid, cr1 - cmid)).astype(BF16)
        kext_ref[rows, :] = jnp.dot(piece, pm, preferred_element_type=F32).astype(BF16)


def _fox_gate(xb, w_qkvf_t, layer, fb3, batch, seq):
    pm = jnp.asarray(_fox_placement(), BF16)
    n3 = FOX_PIECES * N_HEADS
    gate_row = 3 * D_MODEL // N_HEADS
    return pl.pallas_call(
        _fox_gate_kernel,
        grid=(batch,),
        in_specs=[pl.BlockSpec((seq, D_MODEL), lambda bi: (bi, 0)),
                  pl.BlockSpec((None, N_HEADS, D_MODEL), lambda bi: (layer, gate_row, 0)),
                  pl.BlockSpec((1, n3), lambda bi: (0, 0)),
                  pl.BlockSpec((n3, D_MODEL), lambda bi: (0, 0))],
        out_specs=[pl.BlockSpec((None, N_HEADS, seq), lambda bi: (bi, 0, 0)),
                   pl.BlockSpec((None, seq, D_MODEL), lambda bi: (bi, 0, 0))],
        out_shape=[jax.ShapeDtypeStruct((batch, N_HEADS, seq), F32),
                   jax.ShapeDtypeStruct((batch, seq, D_MODEL), BF16)],
        scratch_shapes=[pltpu.VMEM((n3, D_MODEL), BF16)],
        compiler_params=_params("arbitrary"),
        name="fox_gate",
    )(xb, w_qkvf_t, fb3, pm)


def _fox_kernel(q_ref, k_ref, v_ref, kext_ref, cbt_ref, lw_ref, o_ref, qa_ref, ka_ref, vt_ref,
                s_ref, p_ref):
    seq = q_ref.shape[0]
    tq = FOX_TQ
    key = lax.broadcasted_iota(jnp.int32, (tq, tq), 0)
    qry = lax.broadcasted_iota(jnp.int32, (tq, tq), 1)
    causal = key <= qry
    words = lambda x: pltpu.bitcast(x, jnp.uint32)
    for pair in range(FOX_PAIRS_PER_STEP):
        lanes = slice(pair * LANES, (pair + 1) * LANES)
        q = words(q_ref[:, lanes])
        k = words(k_ref[:, lanes])
        kext = words(kext_ref[:, lanes])
        vt = v_ref[:, lanes].T
        for hh in range(FOX_HEADS_PER_PAIR):
            h = pair * FOX_HEADS_PER_PAIR + hh
            own = lw_ref[3 * hh + 0:3 * hh + 1, :]
            ext = lw_ref[3 * hh + 1:3 * hh + 2, :]
            ones = lw_ref[3 * hh + 2:3 * hh + 3, :]
            qa_ref[h] = pltpu.bitcast((q & own) | ones, BF16)
            ka_ref[h] = pltpu.bitcast((k & ~ext) | (kext & ext), BF16)
            vt_ref[h, 0:HEAD_DIM, :] = vt[hh * HEAD_DIM:(hh + 1) * HEAD_DIM]
            vt_ref[h, HEAD_DIM:, :] = jnp.ones((FOX_SUM_ROWS, seq), BF16)

    heads = FOX_PAIRS_PER_STEP * FOX_HEADS_PER_PAIR
    units = [(i, h) for i in range(seq // tq) for h in range(heads)]

    def scores(u):
        i, h = units[u]
        slot = u % FOX_SCORE_SLOTS
        r0, r1 = i * tq, (i + 1) * tq
        qa = qa_ref[h, r0:r1, :]
        s_d = lax.dot_general(ka_ref[h, r0:r1, :], qa, _NT_DIMS, preferred_element_type=F32)
        s_ref[slot, r0:r1, :] = jnp.where(causal, s_d, NEG_INF)
        if i > 0:
            s_ref[slot, 0:r0, :] = lax.dot_general(ka_ref[h, 0:r0, :], qa, _NT_DIMS,
                                                   preferred_element_type=F32)

    def attend(u):
        i, h = units[u]
        slot = u % FOX_SCORE_SLOTS
        pslot = u % FOX_PROB_SLOTS
        r0, r1 = i * tq, (i + 1) * tq
        head = pl.program_id(1) * heads + h
        cbq = cbt_ref[pl.ds(head, 1), r0:r1]
        m = jnp.max(s_ref[slot, 0:r1, :], axis=0, keepdims=True)
        shift = cbq - (m + cbq)
        p_ref[pslot, 0:r1, :] = jnp.exp2(s_ref[slot, 0:r1, :] + shift).astype(BF16)
        ot = jnp.dot(vt_ref[h, :, 0:r1], p_ref[pslot, 0:r1, :], preferred_element_type=F32)
        return ot[0:HEAD_DIM] * (1.0 / ot[HEAD_DIM:HEAD_DIM + 1])

    ahead = FOX_SCORE_SLOTS - 1
    for u in range(min(ahead, len(units))):
        scores(u)
    outs = []
    for u, (i, h) in enumerate(units):
        if u + ahead < len(units):
            scores(u + ahead)
        outs.append(attend(u))
        if h == heads - 1:
            o_ref[i * tq:(i + 1) * tq, :] = jnp.concatenate(outs, axis=0).T.astype(o_ref.dtype)
            outs = []


def _fox_attention(qkv, cumt, kext):
    b, s, _ = qkv.shape
    heads = FOX_PAIRS_PER_STEP * FOX_HEADS_PER_PAIR
    width = FOX_PAIRS_PER_STEP * LANES
    lw = jnp.asarray(_fox_lane_words())
    ncol = D_MODEL // width
    return pl.pallas_call(
        _fox_kernel,
        grid=(b, N_HEADS // heads),
        in_specs=[pl.BlockSpec((None, s, width), lambda bi, hp: (bi, 0, hp)),
                  pl.BlockSpec((None, s, width), lambda bi, hp: (bi, 0, ncol + hp)),
                  pl.BlockSpec((None, s, width), lambda bi, hp: (bi, 0, 2 * ncol + hp)),
                  pl.BlockSpec((None, s, width), lambda bi, hp: (bi, 0, hp)),
                  pl.BlockSpec((None, N_HEADS, s), lambda bi, hp: (bi, 0, 0)),
                  pl.BlockSpec((SUBLANES, LANES), lambda bi, hp: (0, 0))],
        out_specs=pl.BlockSpec((None, s, width), lambda bi, hp: (bi, 0, hp)),
        out_shape=jax.ShapeDtypeStruct((b, s, D_MODEL), BF16),
        scratch_shapes=[pltpu.VMEM((heads, s, LANES), BF16), pltpu.VMEM((heads, s, LANES), BF16),
                        pltpu.VMEM((heads, HEAD_DIM + FOX_SUM_ROWS, s), BF16),
                        pltpu.VMEM((FOX_SCORE_SLOTS, s, FOX_TQ), F32),
                        pltpu.VMEM((FOX_PROB_SLOTS, s, FOX_TQ), BF16)],
        compiler_params=_params("arbitrary", "arbitrary"),
        name="fox_attn",
    )(qkv, qkv, qkv, kext, cumt, lw)


def _deepnorm(xres, y, g, b, alpha):
    z = alpha * xres + y
    mu = jnp.mean(z, axis=-1, keepdims=True)
    zc = z - mu
    var = jnp.mean(zc * zc, axis=-1, keepdims=True)
    return zc * lax.rsqrt(var + LN_EPS) * g + b


LN_SUBTILES = 2


def _oproj_ln_kernel(alpha, o_ref, xres_ref, w_ref, g_ref, b_ref, xo_ref, xob_ref, wb_ref):
    @pl.when(pl.program_id(0) == 0)
    def _():
        wb_ref[...] = w_ref[...].astype(BF16)

    ts = o_ref.shape[0] // LN_SUBTILES
    ys = [jnp.dot(o_ref[r * ts:(r + 1) * ts, :], wb_ref[...], preferred_element_type=F32)
          for r in range(LN_SUBTILES)]
    for r, y in enumerate(ys):
        rows = slice(r * ts, (r + 1) * ts)
        out = _deepnorm(xres_ref[rows, :], y, g_ref[...], b_ref[...], alpha)
        xo_ref[rows, :] = out
        xob_ref[rows, :] = out.astype(BF16)


def _oproj_ln(o, xres, w, layer, g, b, alpha, tm, name):
    m, k = o.shape
    n = w.shape[2]
    return pl.pallas_call(
        functools.partial(_oproj_ln_kernel, alpha),
        grid=(m // tm,),
        in_specs=[pl.BlockSpec((tm, k), lambda i: (i, 0)),
                  pl.BlockSpec((tm, n), lambda i: (i, 0)),
                  pl.BlockSpec((None, k, n), lambda i: (layer, 0, 0), pipeline_mode=pl.Buffered(1)),
                  pl.BlockSpec((1, n), lambda i: (0, 0)),
                  pl.BlockSpec((1, n), lambda i: (0, 0))],
        out_specs=[pl.BlockSpec((tm, n), lambda i: (i, 0)),
                   pl.BlockSpec((tm, n), lambda i: (i, 0))],
        out_shape=[jax.ShapeDtypeStruct((m, n), F32), jax.ShapeDtypeStruct((m, n), BF16)],
        scratch_shapes=[pltpu.VMEM((k, n), BF16)],
        compiler_params=_params("arbitrary"),
        name=name,
    )(o, xres, w, g, b)


CONV_WIDTH = 3
FFN_TF = 512
FFN_UP_TM = 1024


def _ffn_up_kernel(tiles_per_seq, x_ref, wg_ref, wu_ref, cwg_ref, cwu_ref, cbg_ref, cbu_ref,
                   a_ref, wb_ref, h_ref):
    i = pl.program_id(1)
    tm = x_ref.shape[0]
    tf = a_ref.shape[1]
    halo = SUBLANES

    @pl.when(i == 0)
    def _():
        wb_ref[:, 0:tf] = wg_ref[...].astype(BF16)
        wb_ref[:, tf:2 * tf] = wu_ref[...].astype(BF16)

    @pl.when(i % tiles_per_seq == 0)
    def _():
        h_ref[0:halo, :] = jnp.zeros((halo, h_ref.shape[1]), F32)

    h_ref[halo:halo + tm, :] = jnp.dot(x_ref[...], wb_ref[...], preferred_element_type=F32)

    def conv(cols, cw_ref, cb_ref):
        out = cb_ref[...]
        for tap in range(CONV_WIDTH):
            start = halo - (CONV_WIDTH - 1) + tap
            out = out + cw_ref[tap:tap + 1, :] * h_ref[start:start + tm, cols]
        return out

    g = conv(slice(0, tf), cwg_ref, cbg_ref)
    u = conv(slice(tf, 2 * tf), cwu_ref, cbu_ref)
    a_ref[...] = (g * jax.nn.sigmoid(g) * u).astype(a_ref.dtype)
    h_ref[0:halo, :] = h_ref[tm:tm + halo, :]


def _ffn_up(xb, w_up, conv_w, conv_b, layer, seq):
    m, k = xb.shape
    tm, tf = FFN_UP_TM, FFN_TF
    nf = D_FF // tf
    cb2 = conv_b.reshape(conv_b.shape[0], 1, 2 * D_FF)
    return pl.pallas_call(
        functools.partial(_ffn_up_kernel, seq // tm),
        grid=(nf, m // tm),
        in_specs=[pl.BlockSpec((tm, k), lambda j, i: (i, 0)),
                  pl.BlockSpec((None, k, tf), lambda j, i: (layer, 0, j)),
                  pl.BlockSpec((None, k, tf), lambda j, i: (layer, 0, j + nf)),
                  pl.BlockSpec((None, CONV_WIDTH, tf), lambda j, i: (layer, 0, j)),
                  pl.BlockSpec((None, CONV_WIDTH, tf), lambda j, i: (layer, 0, j + nf)),
                  pl.BlockSpec((None, 1, tf), lambda j, i: (layer, 0, j)),
                  pl.BlockSpec((None, 1, tf), lambda j, i: (layer, 0, j + nf))],
        out_specs=pl.BlockSpec((tm, tf), lambda j, i: (i, j)),
        out_shape=jax.ShapeDtypeStruct((m, D_FF), BF16),
        scratch_shapes=[pltpu.VMEM((k, 2 * tf), BF16), pltpu.VMEM((tm + 2 * SUBLANES, 2 * tf), F32)],
        compiler_params=_params("arbitrary", "arbitrary"),
        name="ffn_up",
    )(xb, w_up, w_up, conv_w, conv_w, cb2, cb2)


FFN_DOWN_TM = 256
FFN_DOWN_WCHUNK = 512


def _ffn_down_ln_kernel(alpha, n_load, emit_bf16, a_ref, xres_ref, w_ref, g_ref, b_ref, *rest):
    if emit_bf16:
        xo_ref, xob_ref, wb_ref = rest
    else:
        (xo_ref, wb_ref), xob_ref = rest, None
    step = pl.program_id(0)
    rows = w_ref.shape[0]

    @pl.when(step < n_load)
    def _():
        off = pl.multiple_of(step * rows, rows)
        wb_ref[pl.ds(off, rows), :] = w_ref[...].astype(BF16)

    @pl.when(step >= n_load)
    def _():
        y = jnp.dot(a_ref[...], wb_ref[...], preferred_element_type=F32)
        out = _deepnorm(xres_ref[...], y, g_ref[...], b_ref[...], alpha)
        xo_ref[...] = out
        if emit_bf16:
            xob_ref[...] = out.astype(BF16)


def _ffn_down_ln(a, xres, w, layer, g, b, alpha, emit_bf16, name):
    m, k = a.shape
    n = w.shape[2]
    tm, rows = FFN_DOWN_TM, FFN_DOWN_WCHUNK
    n_load = k // rows
    tile = lambda s: (jnp.maximum(s - n_load, 0), 0)
    out_specs = [pl.BlockSpec((tm, n), tile)]
    out_shape = [jax.ShapeDtypeStruct((m, n), F32)]
    if emit_bf16:
        out_specs.append(pl.BlockSpec((tm, n), tile))
        out_shape.append(jax.ShapeDtypeStruct((m, n), BF16))
    return pl.pallas_call(
        functools.partial(_ffn_down_ln_kernel, alpha, n_load, emit_bf16),
        grid=(n_load + m // tm,),
        in_specs=[pl.BlockSpec((tm, k), tile),
                  pl.BlockSpec((tm, n), tile),
                  pl.BlockSpec((None, rows, n), lambda s: (layer, jnp.minimum(s, n_load - 1), 0)),
                  pl.BlockSpec((1, n), lambda s: (0, 0)),
                  pl.BlockSpec((1, n), lambda s: (0, 0))],
        out_specs=out_specs,
        out_shape=out_shape,
        scratch_shapes=[pltpu.VMEM((k, n), BF16)],
        compiler_params=_params("arbitrary"),
        name=name,
    )(a, xres, w, g, b)


def kernel(x, a_w_qkv, a_w_o, a_sinks, rel_bias, b_w_qkvf, b_f_bias, b_w_o,
           ffn_w_up, ffn_conv_w, ffn_conv_b, ffn_w_down, ln_g, ln_b):
    batch, seq, d = x.shape
    depth = ffn_w_up.shape[0]
    alpha = (2 * depth) ** 0.25
    m = batch * seq
    xf = x.reshape(m, d)
    xb = None
    biasmask = _swa_bias(rel_bias)
    for layer in range(depth):
        j = layer // 2
        g = ln_g[layer].reshape(2, 1, d)
        bt = ln_b[layer].reshape(2, 1, d)
        if layer % 2 == 0:
            n_qkv = D_MODEL + 2 * A_KV_DIM
            src = xf if xb is None else xb
            qkv = _ws_matmul(src, a_w_qkv, j, n_qkv, n_qkv // 2, 1024, "swa_qkv")
            o = _swa_attention(qkv.reshape(batch, seq, n_qkv), a_sinks[j], biasmask)
            w_o = a_w_o
        else:
            w_t = jnp.swapaxes(b_w_qkvf, 1, 2)
            qkv = _ws_matmul(xb, w_t, j, 3 * D_MODEL, 1024, 2048, "fox_qkv", w_is_transposed=True)
            fb3 = jnp.tile(b_f_bias[j].reshape(1, N_HEADS), (1, FOX_PIECES))
            cumt, kext = _fox_gate(xb, w_t, j, fb3, batch, seq)
            o = _fox_attention(qkv.reshape(batch, seq, 3 * D_MODEL), cumt, kext)
            w_o = b_w_o
        xf, xb = _oproj_ln(o.reshape(m, d), xf, w_o, j, g[0], bt[0], alpha, 512, "oproj_ln")
        a = _ffn_up(xb, ffn_w_up, ffn_conv_w, ffn_conv_b, layer, seq)
        last = layer == depth - 1
        res = _ffn_down_ln(a, xf, ffn_w_down, layer, g[1], bt[1], alpha, not last, "ffn_down_ln")
        xf, xb = (res[0], None) if last else res
    return xf.reshape(batch, seq, d)
```

```python
import functools
import math

import jax
import jax.numpy as jnp
import numpy as np
from jax import lax
from jax.experimental import pallas as pl
from jax.experimental.pallas import tpu as pltpu

D_MODEL = 2048
HEAD_DIM = 64
BLOCK = 128
N_HEADS = D_MODEL // HEAD_DIM
A_KV_HEADS = 4
A_GROUP = N_HEADS // A_KV_HEADS
A_KV_DIM = A_KV_HEADS * HEAD_DIM
REL_BUCKETS = 32
REL_MAX_DIST = 128
D_FF = 5632
LN_EPS = 1e-5
SCALE = HEAD_DIM ** -0.5

SUBLANES = 8
LANES = 128

BF16 = jnp.bfloat16
F32 = jnp.float32
NEG_INF = float("-inf")

_NT_DIMS = (((1,), (1,)), ((), ()))


def _params(*sem):
    return pltpu.CompilerParams(dimension_semantics=sem)


LOG2E = math.log2(math.e)
Q_SCALE = SCALE * LOG2E


def _ws_matmul_kernel(w_is_transposed, q_cols, x_ref, w_ref, o_ref, wb_ref):
    @pl.when(pl.program_id(1) == 0)
    def _():
        wb_ref[...] = w_ref[...].astype(BF16)

    x = x_ref[...].astype(BF16)
    if w_is_transposed:
        y = lax.dot_general(x, wb_ref[...], _NT_DIMS, preferred_element_type=F32)
    else:
        y = jnp.dot(x, wb_ref[...], preferred_element_type=F32)
    tn = o_ref.shape[1]
    col = pl.program_id(0) * tn + lax.broadcasted_iota(jnp.int32, (1, tn), 1)
    y = y * jnp.where(col < q_cols, Q_SCALE, 1.0)
    o_ref[...] = y.astype(o_ref.dtype)


def _ws_matmul(x, w, layer, n_out, tn, tm, name, w_is_transposed=False):
    m, k = x.shape
    if w_is_transposed:
        w_spec = pl.BlockSpec((None, tn, k), lambda j, i: (layer, j, 0))
        wb_shape = (tn, k)
    else:
        w_spec = pl.BlockSpec((None, k, tn), lambda j, i: (layer, 0, j))
        wb_shape = (k, tn)
    return pl.pallas_call(
        functools.partial(_ws_matmul_kernel, w_is_transposed, D_MODEL),
        grid=(n_out // tn, m // tm),
        in_specs=[pl.BlockSpec((tm, k), lambda j, i: (i, 0)), w_spec],
        out_specs=pl.BlockSpec((tm, tn), lambda j, i: (i, j)),
        out_shape=jax.ShapeDtypeStruct((m, n_out), BF16),
        scratch_shapes=[pltpu.VMEM(wb_shape, BF16)],
        compiler_params=_params("arbitrary", "arbitrary"),
        name=name,
    )(x, w)


def _t5_bucket_map():
    qi = np.arange(BLOCK)[None, :]
    kj = np.arange(2 * BLOCK)[:, None]
    dist = BLOCK + qi - kj
    n = np.maximum(dist, 0)
    max_exact = REL_BUCKETS // 2
    nf = np.maximum(n, 1).astype(np.float32)
    large = max_exact + (np.log(nf / max_exact) / math.log(REL_MAX_DIST / max_exact)
                         * (REL_BUCKETS - max_exact)).astype(np.int32)
    large = np.minimum(large, REL_BUCKETS - 1)
    bucket = np.where(n < max_exact, n, large)
    visible = (dist >= 0) & (dist < BLOCK)
    return np.where(visible, bucket, -1).astype(np.int32)


def _swa_bias_kernel(rb_ref, bk_ref, o_ref):
    bk = bk_ref[...]
    key = lax.broadcasted_iota(jnp.int32, bk.shape, 0)

    def head(h, _):
        acc = jnp.full(bk.shape, NEG_INF, F32)
        for b in range(REL_BUCKETS):
            acc = jnp.where(bk == b, rb_ref[b, h] * LOG2E, acc)
        o_ref[0, h] = acc
        o_ref[1, h] = jnp.where(key < BLOCK, NEG_INF, acc)
        return 0

    lax.fori_loop(0, N_HEADS, head, 0)


def _swa_bias(rel_bias):
    bk = jnp.asarray(_t5_bucket_map())
    return pl.pallas_call(
        _swa_bias_kernel,
        in_specs=[pl.BlockSpec(memory_space=pltpu.SMEM),
                  pl.BlockSpec(memory_space=pltpu.VMEM)],
        out_specs=pl.BlockSpec(memory_space=pltpu.VMEM),
        out_shape=jax.ShapeDtypeStruct((2, N_HEADS, 2 * BLOCK, BLOCK), F32),
        name="swa_bias",
    )(rel_bias, bk)


_TN_DIMS = (((0,), (0,)), ((), ()))
SWA_SCORES_AHEAD = 2


def _cast_rider(w, layer, n_steps, step_of):
    rows = w.shape[1] // n_steps
    assert rows * n_steps == w.shape[1] and rows % (2 * SUBLANES) == 0
    in_spec = pl.BlockSpec((None, rows, w.shape[2]), lambda *g: (layer, step_of(*g), 0))
    out_spec = pl.BlockSpec((rows, w.shape[2]), lambda *g: (step_of(*g), 0))
    return in_spec, out_spec, jax.ShapeDtypeStruct(w.shape[1:], BF16)


def _swa_kernel(sink_ref, q_ref, kp_ref, kc_ref, vp_ref, vc_ref, bm_ref, wsrc_ref, o_ref, wdst_ref,
                qs_ref, s_ref, p_ref, linv_ref):
    wdst_ref[...] = wsrc_ref[...].astype(BF16)

    def scores(kh):
        ks = slice(kh * HEAD_DIM, (kh + 1) * HEAD_DIM)
        for g in range(A_GROUP):
            h = kh * A_GROUP + g
            qs_ref[kh, g * BLOCK:(g + 1) * BLOCK, :] = q_ref[:, h * HEAD_DIM:(h + 1) * HEAD_DIM]
        k2 = jnp.concatenate([kp_ref[:, ks], kc_ref[:, ks]], axis=0)
        s_ref[kh] = lax.dot_general(k2, qs_ref[kh], _NT_DIMS, preferred_element_type=F32)

    def attend(kh):
        for g in range(A_GROUP):
            h = kh * A_GROUP + g
            cols = slice(g * BLOCK, (g + 1) * BLOCK)
            s = s_ref[kh, :, cols] + bm_ref[h]
            sink = sink_ref[h] * LOG2E
            m = jnp.maximum(jnp.max(s, axis=0, keepdims=True), sink)
            p = jnp.exp2(s - m)
            denom = jnp.sum(p, axis=0, keepdims=True) + jnp.exp2(sink - m)
            p_ref[kh, :, cols] = p.astype(BF16)
            linv_ref[kh, :, cols] = 1.0 / denom
        ks = slice(kh * HEAD_DIM, (kh + 1) * HEAD_DIM)
        v2 = jnp.concatenate([vp_ref[:, ks], vc_ref[:, ks]], axis=0)
        ot = lax.dot_general(v2, p_ref[kh], _TN_DIMS, preferred_element_type=F32) * linv_ref[kh]
        for g in range(A_GROUP):
            h = kh * A_GROUP + g
            o_ref[:, h * HEAD_DIM:(h + 1) * HEAD_DIM] = ot[:, g * BLOCK:(g + 1) * BLOCK].T.astype(o_ref.dtype)

    for kh in range(min(SWA_SCORES_AHEAD, A_KV_HEADS)):
        scores(kh)
    for kh in range(A_KV_HEADS):
        if kh + SWA_SCORES_AHEAD < A_KV_HEADS:
            scores(kh + SWA_SCORES_AHEAD)
        attend(kh)


def _swa_attention(qkv, sinks, biasmask, w_cast, w_layer):
    b, s, _ = qkv.shape
    nb = s // BLOCK
    gq = A_GROUP * BLOCK
    kcol = D_MODEL // A_KV_DIM
    prev = lambda n: jnp.maximum(n - 1, 0)
    w_in, w_out, w_shape = _cast_rider(w_cast, w_layer, b * nb, lambda bi, n: bi * nb + n)
    return pl.pallas_call(
        _swa_kernel,
        grid=(b, nb),
        in_specs=[pl.BlockSpec(memory_space=pltpu.SMEM),
                  pl.BlockSpec((None, BLOCK, D_MODEL), lambda bi, n: (bi, n, 0)),
                  pl.BlockSpec((None, BLOCK, A_KV_DIM), lambda bi, n: (bi, prev(n), kcol)),
                  pl.BlockSpec((None, BLOCK, A_KV_DIM), lambda bi, n: (bi, n, kcol)),
                  pl.BlockSpec((None, BLOCK, A_KV_DIM), lambda bi, n: (bi, prev(n), kcol + 1)),
                  pl.BlockSpec((None, BLOCK, A_KV_DIM), lambda bi, n: (bi, n, kcol + 1)),
                  pl.BlockSpec((None, N_HEADS, 2 * BLOCK, BLOCK),
                               lambda bi, n: (jnp.where(n == 0, 1, 0), 0, 0, 0)),
                  w_in],
        out_specs=[pl.BlockSpec((None, BLOCK, D_MODEL), lambda bi, n: (bi, n, 0)), w_out],
        out_shape=[jax.ShapeDtypeStruct((b, s, D_MODEL), BF16), w_shape],
        scratch_shapes=[pltpu.VMEM((A_KV_HEADS, gq, HEAD_DIM), BF16),
                        pltpu.VMEM((A_KV_HEADS, 2 * BLOCK, gq), F32),
                        pltpu.VMEM((A_KV_HEADS, 2 * BLOCK, gq), BF16),
                        pltpu.VMEM((A_KV_HEADS, 1, gq), F32)],
        compiler_params=_params("arbitrary", "arbitrary"),
        name="swa_attn",
    )(sinks, qkv, qkv, qkv, qkv, qkv, biasmask, w_cast)


FOX_CUM_BLOCK = 256
FOX_TQ = 256
FOX_HEADS_PER_PAIR = LANES // HEAD_DIM
FOX_PAIRS_PER_STEP = 2
FOX_PIECES = 3
FOX_SUM_ROWS = 16
FOX_SCORE_SLOTS = 4
FOX_PROB_SLOTS = 2
BF16_ONE_PAIR = 0x3F803F80


def _fox_ext_lane(head, piece):
    pair, odd = divmod(head, FOX_HEADS_PER_PAIR)
    return pair * LANES + (1 - odd) * HEAD_DIM + piece


def _fox_placement():
    pm = np.zeros((FOX_PIECES * N_HEADS, D_MODEL), np.float32)
    for t in range(FOX_PIECES):
        for h in range(N_HEADS):
            pm[t * N_HEADS + h, _fox_ext_lane(h, t)] = -1.0
    return pm


def _fox_lane_words():
    w = np.zeros((SUBLANES, LANES), np.uint32)
    for hh in range(FOX_HEADS_PER_PAIR):
        own = np.zeros(LANES, bool)
        own[hh * HEAD_DIM:(hh + 1) * HEAD_DIM] = True
        ext = np.zeros(LANES, bool)
        first = _fox_ext_lane(hh, 0)
        ext[first:first + FOX_PIECES] = True
        w[3 * hh + 0] = np.where(own, 0xFFFFFFFF, 0)
        w[3 * hh + 1] = np.where(ext, 0xFFFFFFFF, 0)
        w[3 * hh + 2] = np.where(ext, BF16_ONE_PAIR, 0)
    return w


def _log_sigmoid(x):
    return jnp.minimum(x, 0.0) - jnp.log1p(jnp.exp(-jnp.abs(x)))


def _round_bf16(x):
    return x.astype(BF16).astype(F32)


def _fox_gate_kernel(x_ref, wf_ref, fb_ref, pm_ref, cumt_ref, kext_ref, w3_ref):
    @pl.when(pl.program_id(0) == 0)
    def _():
        w = wf_ref[...].astype(BF16)
        for t in range(FOX_PIECES):
            w3_ref[t * N_HEADS:(t + 1) * N_HEADS, :] = w

    fz = lax.dot_general(x_ref[...], w3_ref[...], _NT_DIMS, preferred_element_type=F32) + fb_ref[...]
    lf = _log_sigmoid(fz)
    cb = FOX_CUM_BLOCK
    row = lax.broadcasted_iota(jnp.int32, (cb, cb), 0)
    col = lax.broadcasted_iota(jnp.int32, (cb, cb), 1)
    tri = jnp.where(row >= col, 1.0, 0.0).astype(BF16)
    tri3 = jnp.concatenate([tri] * FOX_PIECES, axis=1)
    group = lax.broadcasted_iota(jnp.int32, (cb, lf.shape[1]), 1) // N_HEADS
    pm = pm_ref[...]
    n_blocks = lf.shape[0] // cb
    local = []
    for blk in range(n_blocks):
        x = lf[blk * cb:(blk + 1) * cb]
        hi = _round_bf16(x)
        r1 = x - hi
        mid = _round_bf16(r1)
        pieces = jnp.concatenate([hi, mid, r1 - mid], axis=0).astype(BF16)
        local.append(jnp.dot(tri3, pieces, preferred_element_type=F32))
    carry = jnp.zeros((1, lf.shape[1]), F32)
    for blk in range(n_blocks):
        rows = slice(blk * cb, (blk + 1) * cb)
        c = local[blk] + carry
        carry = carry + local[blk][cb - 1:cb]
        c2 = c * LOG2E
        cumt_ref[:, rows] = c2[:, :N_HEADS].T
        chi = _round_bf16(c2)
        cr1 = c2 - chi
        cmid = _round_bf16(cr1)
        piece = jnp.where(group == 0, chi, jnp.where(group == 1, cmid, cr1 - cmid)).astype(BF16)
        kext_ref[rows, :] = jnp.dot(piece, pm, preferred_element_type=F32).astype(BF16)


def _fox_gate(xb, w_qkvf_t, layer, fb3, batch, seq):
    pm = jnp.asarray(_fox_placement(), BF16)
    n3 = FOX_PIECES * N_HEADS
    gate_row = 3 * D_MODEL // N_HEADS
    return pl.pallas_call(
        _fox_gate_kernel,
        grid=(batch,),
        in_specs=[pl.BlockSpec((seq, D_MODEL), lambda bi: (bi, 0)),
                  pl.BlockSpec((None, N_HEADS, D_MODEL), lambda bi: (layer, gate_row, 0)),
                  pl.BlockSpec((1, n3), lambda bi: (0, 0)),
                  pl.BlockSpec((n3, D_MODEL), lambda bi: (0, 0))],
        out_specs=[pl.BlockSpec((None, N_HEADS, seq), lambda bi: (bi, 0, 0)),
                   pl.BlockSpec((None, seq, D_MODEL), lambda bi: (bi, 0, 0))],
        out_shape=[jax.ShapeDtypeStruct((batch, N_HEADS, seq), F32),
                   jax.ShapeDtypeStruct((batch, seq, D_MODEL), BF16)],
        scratch_shapes=[pltpu.VMEM((n3, D_MODEL), BF16)],
        compiler_params=_params("arbitrary"),
        name="fox_gate",
    )(xb, w_qkvf_t, fb3, pm)


def _fox_kernel(q_ref, k_ref, v_ref, kext_ref, cbt_ref, lw_ref, wsrc_ref, o_ref, wdst_ref,
                qa_ref, ka_ref, vt_ref, s_ref, p_ref):
    wdst_ref[...] = wsrc_ref[...].astype(BF16)
    seq = q_ref.shape[0]
    tq = FOX_TQ
    key = lax.broadcasted_iota(jnp.int32, (tq, tq), 0)
    qry = lax.broadcasted_iota(jnp.int32, (tq, tq), 1)
    causal = key <= qry
    words = lambda x: pltpu.bitcast(x, jnp.uint32)
    for pair in range(FOX_PAIRS_PER_STEP):
        lanes = slice(pair * LANES, (pair + 1) * LANES)
        q = words(q_ref[:, lanes])
        k = words(k_ref[:, lanes])
        kext = words(kext_ref[:, lanes])
        vt = v_ref[:, lanes].T
        for hh in range(FOX_HEADS_PER_PAIR):
            h = pair * FOX_HEADS_PER_PAIR + hh
            own = lw_ref[3 * hh + 0:3 * hh + 1, :]
            ext = lw_ref[3 * hh + 1:3 * hh + 2, :]
            ones = lw_ref[3 * hh + 2:3 * hh + 3, :]
            qa_ref[h] = pltpu.bitcast((q & own) | ones, BF16)
            ka_ref[h] = pltpu.bitcast((k & ~ext) | (kext & ext), BF16)
            vt_ref[h, 0:HEAD_DIM, :] = vt[hh * HEAD_DIM:(hh + 1) * HEAD_DIM]
            vt_ref[h, HEAD_DIM:, :] = jnp.ones((FOX_SUM_ROWS, seq), BF16)

    heads = FOX_PAIRS_PER_STEP * FOX_HEADS_PER_PAIR
    units = [(i, h) for i in range(seq // tq) for h in range(heads)]

    def scores(u):
        i, h = units[u]
        slot = u % FOX_SCORE_SLOTS
        r0, r1 = i * tq, (i + 1) * tq
        qa = qa_ref[h, r0:r1, :]
        s_d = lax.dot_general(ka_ref[h, r0:r1, :], qa, _NT_DIMS, preferred_element_type=F32)
        s_d = jnp.where(causal, s_d, NEG_INF)
        s_ref[slot, r0:r1, :] = s_d
        m = jnp.max(s_d, axis=0, keepdims=True)
        if i > 0:
            s_o = lax.dot_general(ka_ref[h, 0:r0, :], qa, _NT_DIMS, preferred_element_type=F32)
            s_ref[slot, 0:r0, :] = s_o
            m = jnp.maximum(m, jnp.max(s_o, axis=0, keepdims=True))
        head = pl.program_id(1) * heads + h
        cbq = cbt_ref[pl.ds(head, 1), r0:r1]
        return cbq - (m + cbq)

    def attend(u, shift):
        i, h = units[u]
        slot = u % FOX_SCORE_SLOTS
        pslot = u % FOX_PROB_SLOTS
        r1 = (i + 1) * tq
        p_ref[pslot, 0:r1, :] = jnp.exp2(s_ref[slot, 0:r1, :] + shift).astype(BF16)
        ot = jnp.dot(vt_ref[h, :, 0:r1], p_ref[pslot, 0:r1, :], preferred_element_type=F32)
        return ot[0:HEAD_DIM] * (1.0 / ot[HEAD_DIM:HEAD_DIM + 1])

    ahead = FOX_SCORE_SLOTS - 1
    shifts = {u: scores(u) for u in range(min(ahead, len(units)))}
    outs = []
    for u, (i, h) in enumerate(units):
        if u + ahead < len(units):
            shifts[u + ahead] = scores(u + ahead)
        outs.append(attend(u, shifts.pop(u)))
        if h == heads - 1:
            o_ref[i * tq:(i + 1) * tq, :] = jnp.concatenate(outs, axis=0).T.astype(o_ref.dtype)
            outs = []


def _fox_attention(qkv, cumt, kext, w_cast, w_layer):
    b, s, _ = qkv.shape
    heads = FOX_PAIRS_PER_STEP * FOX_HEADS_PER_PAIR
    width = FOX_PAIRS_PER_STEP * LANES
    lw = jnp.asarray(_fox_lane_words())
    ncol = D_MODEL // width
    steps = N_HEADS // heads
    w_in, w_out, w_shape = _cast_rider(w_cast, w_layer, b * steps, lambda bi, hp: bi * steps + hp)
    return pl.pallas_call(
        _fox_kernel,
        grid=(b, steps),
        in_specs=[pl.BlockSpec((None, s, width), lambda bi, hp: (bi, 0, hp)),
                  pl.BlockSpec((None, s, width), lambda bi, hp: (bi, 0, ncol + hp)),
                  pl.BlockSpec((None, s, width), lambda bi, hp: (bi, 0, 2 * ncol + hp)),
                  pl.BlockSpec((None, s, width), lambda bi, hp: (bi, 0, hp)),
                  pl.BlockSpec((None, N_HEADS, s), lambda bi, hp: (bi, 0, 0)),
                  pl.BlockSpec((SUBLANES, LANES), lambda bi, hp: (0, 0)),
                  w_in],
        out_specs=[pl.BlockSpec((None, s, width), lambda bi, hp: (bi, 0, hp)), w_out],
        out_shape=[jax.ShapeDtypeStruct((b, s, D_MODEL), BF16), w_shape],
        scratch_shapes=[pltpu.VMEM((heads, s, LANES), BF16), pltpu.VMEM((heads, s, LANES), BF16),
                        pltpu.VMEM((heads, HEAD_DIM + FOX_SUM_ROWS, s), BF16),
                        pltpu.VMEM((FOX_SCORE_SLOTS, s, FOX_TQ), F32),
                        pltpu.VMEM((FOX_PROB_SLOTS, s, FOX_TQ), BF16)],
        compiler_params=_params("arbitrary", "arbitrary"),
        name="fox_attn",
    )(qkv, qkv, qkv, kext, cumt, lw, w_cast)


def _deepnorm(xres, y, g, b, alpha):
    z = alpha * xres + y
    mu = jnp.mean(z, axis=-1, keepdims=True)
    zc = z - mu
    var = jnp.mean(zc * zc, axis=-1, keepdims=True)
    return zc * lax.rsqrt(var + LN_EPS) * g + b


LN_SUBTILES = 2


def _oproj_ln_kernel(alpha, o_ref, xres_ref, w_ref, g_ref, b_ref, xo_ref, xob_ref, wb_ref):
    @pl.when(pl.program_id(0) == 0)
    def _():
        wb_ref[...] = w_ref[...].astype(BF16)

    ts = o_ref.shape[0] // LN_SUBTILES
    ys = [jnp.dot(o_ref[r * ts:(r + 1) * ts, :], wb_ref[...], preferred_element_type=F32)
          for r in range(LN_SUBTILES)]
    for r, y in enumerate(ys):
        rows = slice(r * ts, (r + 1) * ts)
        out = _deepnorm(xres_ref[rows, :], y, g_ref[...], b_ref[...], alpha)
        xo_ref[rows, :] = out
        xob_ref[rows, :] = out.astype(BF16)


def _oproj_ln(o, xres, w, layer, g, b, alpha, tm, name):
    m, k = o.shape
    n = w.shape[2]
    return pl.pallas_call(
        functools.partial(_oproj_ln_kernel, alpha),
        grid=(m // tm,),
        in_specs=[pl.BlockSpec((tm, k), lambda i: (i, 0)),
                  pl.BlockSpec((tm, n), lambda i: (i, 0)),
                  pl.BlockSpec((None, k, n), lambda i: (layer, 0, 0), pipeline_mode=pl.Buffered(1)),
                  pl.BlockSpec((1, n), lambda i: (0, 0)),
                  pl.BlockSpec((1, n), lambda i: (0, 0))],
        out_specs=[pl.BlockSpec((tm, n), lambda i: (i, 0)),
                   pl.BlockSpec((tm, n), lambda i: (i, 0))],
        out_shape=[jax.ShapeDtypeStruct((m, n), F32), jax.ShapeDtypeStruct((m, n), BF16)],
        scratch_shapes=[pltpu.VMEM((k, n), BF16)],
        compiler_params=_params("arbitrary"),
        name=name,
    )(o, xres, w, g, b)


CONV_WIDTH = 3
FFN_TF = 512
FFN_UP_ROW_SPLIT = 2


def _ffn_up_kernel(x_ref, wg_ref, wu_ref, cwg_ref, cwu_ref, cbg_ref, cbu_ref, wsrc_ref,
                   a_ref, wdst_ref, hg_ref, hu_ref):
    wdst_ref[...] = wsrc_ref[...].astype(BF16)
    tm = x_ref.shape[0]
    halo = SUBLANES
    for h_ref in (hg_ref, hu_ref):
        h_ref[0:halo, :] = jnp.zeros((halo, h_ref.shape[1]), F32)
    tr = tm // FFN_UP_ROW_SPLIT
    for r in range(FFN_UP_ROW_SPLIT):
        x = x_ref[r * tr:(r + 1) * tr, :]
        for h_ref, w_ref in ((hg_ref, wg_ref), (hu_ref, wu_ref)):
            h_ref[halo + r * tr:halo + (r + 1) * tr, :] = jnp.dot(x, w_ref[...], preferred_element_type=F32)

    def conv(h_ref, cw_ref, cb_ref):
        out = cb_ref[...]
        for tap in range(CONV_WIDTH):
            start = halo - (CONV_WIDTH - 1) + tap
            out = out + cw_ref[tap:tap + 1, :] * h_ref[start:start + tm, :]
        return out

    g = conv(hg_ref, cwg_ref, cbg_ref)
    u = conv(hu_ref, cwu_ref, cbu_ref)
    a_ref[...] = (g * jax.nn.sigmoid(g) * u).astype(a_ref.dtype)


def _ffn_up(xb, w_up_bf16, conv_w, conv_b, layer, seq, w_cast, w_layer):
    m, k = xb.shape
    tf = FFN_TF
    nf = D_FF // tf
    n_seq = m // seq
    cb2 = conv_b.reshape(conv_b.shape[0], 1, 2 * D_FF)
    w_in, w_out, w_shape = _cast_rider(w_cast, w_layer, nf * n_seq, lambda j, i: j * n_seq + i)
    return pl.pallas_call(
        _ffn_up_kernel,
        grid=(nf, n_seq),
        in_specs=[pl.BlockSpec((seq, k), lambda j, i: (i, 0)),
                  pl.BlockSpec((k, tf), lambda j, i: (0, j)),
                  pl.BlockSpec((k, tf), lambda j, i: (0, j + nf)),
                  pl.BlockSpec((None, CONV_WIDTH, tf), lambda j, i: (layer, 0, j)),
                  pl.BlockSpec((None, CONV_WIDTH, tf), lambda j, i: (layer, 0, j + nf)),
                  pl.BlockSpec((None, 1, tf), lambda j, i: (layer, 0, j)),
                  pl.BlockSpec((None, 1, tf), lambda j, i: (layer, 0, j + nf)),
                  w_in],
        out_specs=[pl.BlockSpec((seq, tf), lambda j, i: (i, j)), w_out],
        out_shape=[jax.ShapeDtypeStruct((m, D_FF), BF16), w_shape],
        scratch_shapes=[pltpu.VMEM((seq + SUBLANES, tf), F32), pltpu.VMEM((seq + SUBLANES, tf), F32)],
        compiler_params=_params("arbitrary", "arbitrary"),
        name="ffn_up",
    )(xb, w_up_bf16, w_up_bf16, conv_w, conv_w, cb2, cb2, w_cast)


FFN_DOWN_TM = 256


def _ffn_down_ln_kernel(alpha, emit_bf16, a_ref, xres_ref, w_ref, g_ref, b_ref, xo_ref, *rest):
    y = jnp.dot(a_ref[...], w_ref[...], preferred_element_type=F32)
    out = _deepnorm(xres_ref[...], y, g_ref[...], b_ref[...], alpha)
    xo_ref[...] = out
    if emit_bf16:
        rest[0][...] = out.astype(BF16)


def _ffn_down_ln(a, xres, w_bf16, g, b, alpha, emit_bf16, name):
    m, k = a.shape
    n = w_bf16.shape[1]
    tm = FFN_DOWN_TM
    out_specs = [pl.BlockSpec((tm, n), lambda i: (i, 0))]
    out_shape = [jax.ShapeDtypeStruct((m, n), F32)]
    if emit_bf16:
        out_specs.append(pl.BlockSpec((tm, n), lambda i: (i, 0)))
        out_shape.append(jax.ShapeDtypeStruct((m, n), BF16))
    return pl.pallas_call(
        functools.partial(_ffn_down_ln_kernel, alpha, emit_bf16),
        grid=(m // tm,),
        in_specs=[pl.BlockSpec((tm, k), lambda i: (i, 0)),
                  pl.BlockSpec((tm, n), lambda i: (i, 0)),
                  pl.BlockSpec((k, n), lambda i: (0, 0), pipeline_mode=pl.Buffered(1)),
                  pl.BlockSpec((1, n), lambda i: (0, 0)),
                  pl.BlockSpec((1, n), lambda i: (0, 0))],
        out_specs=out_specs,
        out_shape=out_shape,
        compiler_params=_params("arbitrary"),
        name=name,
    )(a, xres, w_bf16, g, b)


def kernel(x, a_w_qkv, a_w_o, a_sinks, rel_bias, b_w_qkvf, b_f_bias, b_w_o,
           ffn_w_up, ffn_conv_w, ffn_conv_b, ffn_w_down, ln_g, ln_b):
    batch, seq, d = x.shape
    depth = ffn_w_up.shape[0]
    alpha = (2 * depth) ** 0.25
    m = batch * seq
    xf = x.reshape(m, d)
    xb = None
    biasmask = _swa_bias(rel_bias)
    for layer in range(depth):
        j = layer // 2
        g = ln_g[layer].reshape(2, 1, d)
        bt = ln_b[layer].reshape(2, 1, d)
        if layer % 2 == 0:
            n_qkv = D_MODEL + 2 * A_KV_DIM
            src = xf if xb is None else xb
            qkv = _ws_matmul(src, a_w_qkv, j, n_qkv, n_qkv // 2, 1024, "swa_qkv")
            o, w_up = _swa_attention(qkv.reshape(batch, seq, n_qkv), a_sinks[j], biasmask,
                                     ffn_w_up, layer)
            w_o = a_w_o
        else:
            w_t = jnp.swapaxes(b_w_qkvf, 1, 2)
            qkv = _ws_matmul(xb, w_t, j, 3 * D_MODEL, 1024, 2048, "fox_qkv", w_is_transposed=True)
            fb3 = jnp.tile(b_f_bias[j].reshape(1, N_HEADS), (1, FOX_PIECES))
            cumt, kext = _fox_gate(xb, w_t, j, fb3, batch, seq)
            o, w_up = _fox_attention(qkv.reshape(batch, seq, 3 * D_MODEL), cumt, kext, ffn_w_up, layer)
            w_o = b_w_o
        xf, xb = _oproj_ln(o.reshape(m, d), xf, w_o, j, g[0], bt[0], alpha, 512, "oproj_ln")
        a, w_down = _ffn_up(xb, w_up, ffn_conv_w, ffn_conv_b, layer, seq, ffn_w_down, layer)
        last = layer == depth - 1
        res = _ffn_down_ln(a, xf, w_down, g[1], bt[1], alpha, not last, "ffn_down_ln")
        xf, xb = (res[0], None) if last else res
    return xf.reshape(batch, seq, d)
```

```python
import functools
import math

import jax
import jax.numpy as jnp
import numpy as np
from jax import lax
from jax.experimental import pallas as pl
from jax.experimental.pallas import tpu as pltpu

D_MODEL = 2048
HEAD_DIM = 64
BLOCK = 128
N_HEADS = D_MODEL // HEAD_DIM
A_KV_HEADS = 4
A_GROUP = N_HEADS // A_KV_HEADS
A_KV_DIM = A_KV_HEADS * HEAD_DIM
REL_BUCKETS = 32
REL_MAX_DIST = 128
D_FF = 5632
LN_EPS = 1e-5
SCALE = HEAD_DIM ** -0.5

SUBLANES = 8
LANES = 128

BF16 = jnp.bfloat16
F32 = jnp.float32
NEG_INF = float("-inf")

_NT_DIMS = (((1,), (1,)), ((), ()))


def _params(*sem):
    return pltpu.CompilerParams(dimension_semantics=sem)


LOG2E = math.log2(math.e)
Q_SCALE = SCALE * LOG2E


def _ws_matmul_kernel(w_is_transposed, q_cols, x_ref, w_ref, o_ref, wb_ref):
    @pl.when(pl.program_id(1) == 0)
    def _():
        wb_ref[...] = w_ref[...].astype(BF16)

    x = x_ref[...].astype(BF16)
    if w_is_transposed:
        y = lax.dot_general(x, wb_ref[...], _NT_DIMS, preferred_element_type=F32)
    else:
        y = jnp.dot(x, wb_ref[...], preferred_element_type=F32)
    tn = o_ref.shape[1]
    col = pl.program_id(0) * tn + lax.broadcasted_iota(jnp.int32, (1, tn), 1)
    y = y * jnp.where(col < q_cols, Q_SCALE, 1.0)
    o_ref[...] = y.astype(o_ref.dtype)


def _ws_matmul(x, w, layer, n_out, tn, tm, name, w_is_transposed=False):
    m, k = x.shape
    if w_is_transposed:
        w_spec = pl.BlockSpec((None, tn, k), lambda j, i: (layer, j, 0))
        wb_shape = (tn, k)
    else:
        w_spec = pl.BlockSpec((None, k, tn), lambda j, i: (layer, 0, j))
        wb_shape = (k, tn)
    return pl.pallas_call(
        functools.partial(_ws_matmul_kernel, w_is_transposed, D_MODEL),
        grid=(n_out // tn, m // tm),
        in_specs=[pl.BlockSpec((tm, k), lambda j, i: (i, 0)), w_spec],
        out_specs=pl.BlockSpec((tm, tn), lambda j, i: (i, j)),
        out_shape=jax.ShapeDtypeStruct((m, n_out), BF16),
        scratch_shapes=[pltpu.VMEM(wb_shape, BF16)],
        compiler_params=_params("arbitrary", "arbitrary"),
        name=name,
    )(x, w)


def _t5_bucket_map():
    qi = np.arange(BLOCK)[None, :]
    kj = np.arange(2 * BLOCK)[:, None]
    dist = BLOCK + qi - kj
    n = np.maximum(dist, 0)
    max_exact = REL_BUCKETS // 2
    nf = np.maximum(n, 1).astype(np.float32)
    large = max_exact + (np.log(nf / max_exact) / math.log(REL_MAX_DIST / max_exact)
                         * (REL_BUCKETS - max_exact)).astype(np.int32)
    large = np.minimum(large, REL_BUCKETS - 1)
    bucket = np.where(n < max_exact, n, large)
    visible = (dist >= 0) & (dist < BLOCK)
    return np.where(visible, bucket, -1).astype(np.int32)


def _swa_bias_kernel(rb_ref, bk_ref, o_ref):
    bk = bk_ref[...]
    key = lax.broadcasted_iota(jnp.int32, bk.shape, 0)

    def head(h, _):
        acc = jnp.full(bk.shape, NEG_INF, F32)
        for b in range(REL_BUCKETS):
            acc = jnp.where(bk == b, rb_ref[b, h] * LOG2E, acc)
        o_ref[0, h] = acc
        o_ref[1, h] = jnp.where(key < BLOCK, NEG_INF, acc)
        return 0

    lax.fori_loop(0, N_HEADS, head, 0)


def _swa_bias(rel_bias):
    bk = jnp.asarray(_t5_bucket_map())
    return pl.pallas_call(
        _swa_bias_kernel,
        in_specs=[pl.BlockSpec(memory_space=pltpu.SMEM),
                  pl.BlockSpec(memory_space=pltpu.VMEM)],
        out_specs=pl.BlockSpec(memory_space=pltpu.VMEM),
        out_shape=jax.ShapeDtypeStruct((2, N_HEADS, 2 * BLOCK, BLOCK), F32),
        name="swa_bias",
    )(rel_bias, bk)


_TN_DIMS = (((0,), (0,)), ((), ()))
SWA_SCORES_AHEAD = 2


def _swa_kernel(sink_ref, q_ref, kp_ref, kc_ref, vp_ref, vc_ref, bm_ref, o_ref,
                qs_ref, s_ref, p_ref, linv_ref):
    def scores(kh):
        ks = slice(kh * HEAD_DIM, (kh + 1) * HEAD_DIM)
        for g in range(A_GROUP):
            h = kh * A_GROUP + g
            qs_ref[kh, g * BLOCK:(g + 1) * BLOCK, :] = q_ref[:, h * HEAD_DIM:(h + 1) * HEAD_DIM]
        k2 = jnp.concatenate([kp_ref[:, ks], kc_ref[:, ks]], axis=0)
        s_ref[kh] = lax.dot_general(k2, qs_ref[kh], _NT_DIMS, preferred_element_type=F32)

    def softmax(kh):
        for g in range(A_GROUP):
            h = kh * A_GROUP + g
            cols = slice(g * BLOCK, (g + 1) * BLOCK)
            s = s_ref[kh, :, cols] + bm_ref[h]
            sink = sink_ref[h] * LOG2E
            m = jnp.maximum(jnp.max(s, axis=0, keepdims=True), sink)
            p = jnp.exp2(s - m)
            denom = jnp.sum(p, axis=0, keepdims=True) + jnp.exp2(sink - m)
            p_ref[kh, :, cols] = p.astype(BF16)
            linv_ref[kh, :, cols] = 1.0 / denom

    def weighted_values(kh):
        ks = slice(kh * HEAD_DIM, (kh + 1) * HEAD_DIM)
        v2 = jnp.concatenate([vp_ref[:, ks], vc_ref[:, ks]], axis=0)
        ot = lax.dot_general(v2, p_ref[kh], _TN_DIMS, preferred_element_type=F32) * linv_ref[kh]
        for g in range(A_GROUP):
            h = kh * A_GROUP + g
            o_ref[:, h * HEAD_DIM:(h + 1) * HEAD_DIM] = ot[:, g * BLOCK:(g + 1) * BLOCK].T.astype(o_ref.dtype)

    for kh in range(min(SWA_SCORES_AHEAD, A_KV_HEADS)):
        scores(kh)
    for kh in range(A_KV_HEADS):
        if kh + SWA_SCORES_AHEAD < A_KV_HEADS:
            scores(kh + SWA_SCORES_AHEAD)
        softmax(kh)
        weighted_values(kh)


def _swa_attention(qkv, sinks, biasmask):
    b, s, _ = qkv.shape
    nb = s // BLOCK
    gq = A_GROUP * BLOCK
    kcol = D_MODEL // A_KV_DIM
    prev = lambda n: jnp.maximum(n - 1, 0)
    return pl.pallas_call(
        _swa_kernel,
        grid=(b, nb),
        in_specs=[pl.BlockSpec(memory_space=pltpu.SMEM),
                  pl.BlockSpec((None, BLOCK, D_MODEL), lambda bi, n: (bi, n, 0)),
                  pl.BlockSpec((None, BLOCK, A_KV_DIM), lambda bi, n: (bi, prev(n), kcol)),
                  pl.BlockSpec((None, BLOCK, A_KV_DIM), lambda bi, n: (bi, n, kcol)),
                  pl.BlockSpec((None, BLOCK, A_KV_DIM), lambda bi, n: (bi, prev(n), kcol + 1)),
                  pl.BlockSpec((None, BLOCK, A_KV_DIM), lambda bi, n: (bi, n, kcol + 1)),
                  pl.BlockSpec((None, N_HEADS, 2 * BLOCK, BLOCK),
                               lambda bi, n: (jnp.where(n == 0, 1, 0), 0, 0, 0))],
        out_specs=pl.BlockSpec((None, BLOCK, D_MODEL), lambda bi, n: (bi, n, 0)),
        out_shape=jax.ShapeDtypeStruct((b, s, D_MODEL), BF16),
        scratch_shapes=[pltpu.VMEM((A_KV_HEADS, gq, HEAD_DIM), BF16),
                        pltpu.VMEM((A_KV_HEADS, 2 * BLOCK, gq), F32),
                        pltpu.VMEM((A_KV_HEADS, 2 * BLOCK, gq), BF16),
                        pltpu.VMEM((A_KV_HEADS, 1, gq), F32)],
        compiler_params=_params("arbitrary", "arbitrary"),
        name="swa_attn",
    )(sinks, qkv, qkv, qkv, qkv, qkv, biasmask)


FOX_CUM_BLOCK = 256
FOX_TQ = 256
FOX_HEADS_PER_PAIR = LANES // HEAD_DIM
FOX_PAIRS_PER_STEP = 2
FOX_PIECES = 3
FOX_SUM_ROWS = 16
FOX_SCORE_SLOTS = 4
FOX_PV_LAG = 1
FOX_PROB_SLOTS = FOX_PV_LAG + 1
BF16_ONE_PAIR = 0x3F803F80


def _fox_ext_lane(head, piece):
    pair, odd = divmod(head, FOX_HEADS_PER_PAIR)
    return pair * LANES + (1 - odd) * HEAD_DIM + piece


def _fox_placement():
    pm = np.zeros((FOX_PIECES * N_HEADS, D_MODEL), np.float32)
    for t in range(FOX_PIECES):
        for h in range(N_HEADS):
            pm[t * N_HEADS + h, _fox_ext_lane(h, t)] = -1.0
    return pm


def _fox_lane_words():
    w = np.zeros((SUBLANES, LANES), np.uint32)
    for hh in range(FOX_HEADS_PER_PAIR):
        own = np.zeros(LANES, bool)
        own[hh * HEAD_DIM:(hh + 1) * HEAD_DIM] = True
        ext = np.zeros(LANES, bool)
        first = _fox_ext_lane(hh, 0)
        ext[first:first + FOX_PIECES] = True
        w[3 * hh + 0] = np.where(own, 0xFFFFFFFF, 0)
        w[3 * hh + 1] = np.where(ext, 0xFFFFFFFF, 0)
        w[3 * hh + 2] = np.where(ext, BF16_ONE_PAIR, 0)
    return w


def _log_sigmoid(x):
    return jnp.minimum(x, 0.0) - jnp.log1p(jnp.exp(-jnp.abs(x)))


def _round_bf16(x):
    return x.astype(BF16).astype(F32)


def _fox_gate_kernel(x_ref, wf_ref, fb_ref, pm_ref, cumt_ref, kext_ref, w3_ref):
    @pl.when(pl.program_id(0) == 0)
    def _():
        w = wf_ref[...].astype(BF16)
        for t in range(FOX_PIECES):
            w3_ref[t * N_HEADS:(t + 1) * N_HEADS, :] = w

    fz = lax.dot_general(x_ref[...], w3_ref[...], _NT_DIMS, preferred_element_type=F32) + fb_ref[...]
    lf = _log_sigmoid(fz)
    cb = FOX_CUM_BLOCK
    row = lax.broadcasted_iota(jnp.int32, (cb, cb), 0)
    col = lax.broadcasted_iota(jnp.int32, (cb, cb), 1)
    tri = jnp.where(row >= col, 1.0, 0.0).astype(BF16)
    tri3 = jnp.concatenate([tri] * FOX_PIECES, axis=1)
    group = lax.broadcasted_iota(jnp.int32, (cb, lf.shape[1]), 1) // N_HEADS
    pm = pm_ref[...]
    n_blocks = lf.shape[0] // cb
    local = []
    for blk in range(n_blocks):
        x = lf[blk * cb:(blk + 1) * cb]
        hi = _round_bf16(x)
        r1 = x - hi
        mid = _round_bf16(r1)
        pieces = jnp.concatenate([hi, mid, r1 - mid], axis=0).astype(BF16)
        local.append(jnp.dot(tri3, pieces, preferred_element_type=F32))
    carry = jnp.zeros((1, lf.shape[1]), F32)
    for blk in range(n_blocks):
        rows = slice(blk * cb, (blk + 1) * cb)
        c = local[blk] + carry
        carry = carry + local[blk][cb - 1:cb]
        c2 = c * LOG2E
        cumt_ref[:, rows] = c2[:, :N_HEADS].T
        chi = _round_bf16(c2)
        cr1 = c2 - chi
        cmid = _round_bf16(cr1)
        piece = jnp.where(group == 0, chi, jnp.where(group == 1, cmid, cr1 - cmid)).astype(BF16)
        kext_ref[rows, :] = jnp.dot(piece, pm, preferred_element_type=F32).astype(BF16)


def _fox_gate(xb, w_qkvf_t, layer, fb3, batch, seq):
    pm = jnp.asarray(_fox_placement(), BF16)
    n3 = FOX_PIECES * N_HEADS
    gate_row = 3 * D_MODEL // N_HEADS
    return pl.pallas_call(
        _fox_gate_kernel,
        grid=(batch,),
        in_specs=[pl.BlockSpec((seq, D_MODEL), lambda bi: (bi, 0)),
                  pl.BlockSpec((None, N_HEADS, D_MODEL), lambda bi: (layer, gate_row, 0)),
                  pl.BlockSpec((1, n3), lambda bi: (0, 0)),
                  pl.BlockSpec((n3, D_MODEL), lambda bi: (0, 0))],
        out_specs=[pl.BlockSpec((None, N_HEADS, seq), lambda bi: (bi, 0, 0)),
                   pl.BlockSpec((None, seq, D_MODEL), lambda bi: (bi, 0, 0))],
        out_shape=[jax.ShapeDtypeStruct((batch, N_HEADS, seq), F32),
                   jax.ShapeDtypeStruct((batch, seq, D_MODEL), BF16)],
        scratch_shapes=[pltpu.VMEM((n3, D_MODEL), BF16)],
        compiler_params=_params("arbitrary"),
        name="fox_gate",
    )(xb, w_qkvf_t, fb3, pm)


def _fox_kernel(q_ref, k_ref, v_ref, kext_ref, cbt_ref, lw_ref, o_ref, qa_ref, ka_ref, vt_ref,
                s_ref, p_ref):
    seq = q_ref.shape[0]
    tq = FOX_TQ
    key = lax.broadcasted_iota(jnp.int32, (tq, tq), 0)
    qry = lax.broadcasted_iota(jnp.int32, (tq, tq), 1)
    causal = key <= qry
    words = lambda x: pltpu.bitcast(x, jnp.uint32)
    for pair in range(FOX_PAIRS_PER_STEP):
        lanes = slice(pair * LANES, (pair + 1) * LANES)
        q = words(q_ref[:, lanes])
        k = words(k_ref[:, lanes])
        kext = words(kext_ref[:, lanes])
        vt = v_ref[:, lanes].T
        for hh in range(FOX_HEADS_PER_PAIR):
            h = pair * FOX_HEADS_PER_PAIR + hh
            own = lw_ref[3 * hh + 0:3 * hh + 1, :]
            ext = lw_ref[3 * hh + 1:3 * hh + 2, :]
            ones = lw_ref[3 * hh + 2:3 * hh + 3, :]
            qa_ref[h] = pltpu.bitcast((q & own) | ones, BF16)
            ka_ref[h] = pltpu.bitcast((k & ~ext) | (kext & ext), BF16)
            vt_ref[h, 0:HEAD_DIM, :] = vt[hh * HEAD_DIM:(hh + 1) * HEAD_DIM]
            vt_ref[h, HEAD_DIM:, :] = jnp.ones((FOX_SUM_ROWS, seq), BF16)

    heads = FOX_PAIRS_PER_STEP * FOX_HEADS_PER_PAIR
    units = [(i, h) for i in range(seq // tq) for h in range(heads)]

    def scores(u):
        i, h = units[u]
        slot = u % FOX_SCORE_SLOTS
        r0, r1 = i * tq, (i + 1) * tq
        qa = qa_ref[h, r0:r1, :]
        s_d = lax.dot_general(ka_ref[h, r0:r1, :], qa, _NT_DIMS, preferred_element_type=F32)
        s_ref[slot, r0:r1, :] = jnp.where(causal, s_d, NEG_INF)
        if i > 0:
            s_ref[slot, 0:r0, :] = lax.dot_general(ka_ref[h, 0:r0, :], qa, _NT_DIMS,
                                                   preferred_element_type=F32)

    def softmax(u):
        i, h = units[u]
        slot = u % FOX_SCORE_SLOTS
        r0, r1 = i * tq, (i + 1) * tq
        head = pl.program_id(1) * heads + h
        cbq = cbt_ref[pl.ds(head, 1), r0:r1]
        m = jnp.max(s_ref[slot, 0:r1, :], axis=0, keepdims=True)
        shift = cbq - (m + cbq)
        p_ref[u % FOX_PROB_SLOTS, 0:r1, :] = jnp.exp2(s_ref[slot, 0:r1, :] + shift).astype(BF16)

    outs = []

    def weighted_values(u):
        i, h = units[u]
        r1 = (i + 1) * tq
        ot = jnp.dot(vt_ref[h, :, 0:r1], p_ref[u % FOX_PROB_SLOTS, 0:r1, :], preferred_element_type=F32)
        outs.append(ot[0:HEAD_DIM] * (1.0 / ot[HEAD_DIM:HEAD_DIM + 1]))
        if h == heads - 1:
            o_ref[i * tq:r1, :] = jnp.concatenate(outs, axis=0).T.astype(o_ref.dtype)
            outs.clear()

    ahead = FOX_SCORE_SLOTS - 1
    for u in range(min(ahead, len(units))):
        scores(u)
    for u in range(len(units)):
        if u + ahead < len(units):
            scores(u + ahead)
        softmax(u)
        if u >= FOX_PV_LAG:
            weighted_values(u - FOX_PV_LAG)
    for u in range(len(units) - FOX_PV_LAG, len(units)):
        weighted_values(u)


def _fox_attention(qkv, cumt, kext):
    b, s, _ = qkv.shape
    heads = FOX_PAIRS_PER_STEP * FOX_HEADS_PER_PAIR
    width = FOX_PAIRS_PER_STEP * LANES
    lw = jnp.asarray(_fox_lane_words())
    ncol = D_MODEL // width
    return pl.pallas_call(
        _fox_kernel,
        grid=(b, N_HEADS // heads),
        in_specs=[pl.BlockSpec((None, s, width), lambda bi, hp: (bi, 0, hp)),
                  pl.BlockSpec((None, s, width), lambda bi, hp: (bi, 0, ncol + hp)),
                  pl.BlockSpec((None, s, width), lambda bi, hp: (bi, 0, 2 * ncol + hp)),
                  pl.BlockSpec((None, s, width), lambda bi, hp: (bi, 0, hp)),
                  pl.BlockSpec((None, N_HEADS, s), lambda bi, hp: (bi, 0, 0)),
                  pl.BlockSpec((SUBLANES, LANES), lambda bi, hp: (0, 0))],
        out_specs=pl.BlockSpec((None, s, width), lambda bi, hp: (bi, 0, hp)),
        out_shape=jax.ShapeDtypeStruct((b, s, D_MODEL), BF16),
        scratch_shapes=[pltpu.VMEM((heads, s, LANES), BF16), pltpu.VMEM((heads, s, LANES), BF16),
                        pltpu.VMEM((heads, HEAD_DIM + FOX_SUM_ROWS, s), BF16),
                        pltpu.VMEM((FOX_SCORE_SLOTS, s, FOX_TQ), F32),
                        pltpu.VMEM((FOX_PROB_SLOTS, s, FOX_TQ), BF16)],
        compiler_params=_params("arbitrary", "arbitrary"),
        name="fox_attn",
    )(qkv, qkv, qkv, kext, cumt, lw)


def _deepnorm(xres, y, g, b, alpha):
    z = alpha * xres + y
    mu = jnp.mean(z, axis=-1, keepdims=True)
    zc = z - mu
    var = jnp.mean(zc * zc, axis=-1, keepdims=True)
    return zc * lax.rsqrt(var + LN_EPS) * g + b


LN_SUBTILES = 2


def _oproj_ln_kernel(alpha, o_ref, xres_ref, w_ref, g_ref, b_ref, xo_ref, xob_ref, wb_ref):
    @pl.when(pl.program_id(0) == 0)
    def _():
        wb_ref[...] = w_ref[...].astype(BF16)

    ts = o_ref.shape[0] // LN_SUBTILES
    ys = [jnp.dot(o_ref[r * ts:(r + 1) * ts, :], wb_ref[...], preferred_element_type=F32)
          for r in range(LN_SUBTILES)]
    for r, y in enumerate(ys):
        rows = slice(r * ts, (r + 1) * ts)
        out = _deepnorm(xres_ref[rows, :], y, g_ref[...], b_ref[...], alpha)
        xo_ref[rows, :] = out
        xob_ref[rows, :] = out.astype(BF16)


def _oproj_ln(o, xres, w, layer, g, b, alpha, tm, name):
    m, k = o.shape
    n = w.shape[2]
    return pl.pallas_call(
        functools.partial(_oproj_ln_kernel, alpha),
        grid=(m // tm,),
        in_specs=[pl.BlockSpec((tm, k), lambda i: (i, 0)),
                  pl.BlockSpec((tm, n), lambda i: (i, 0)),
                  pl.BlockSpec((None, k, n), lambda i: (layer, 0, 0), pipeline_mode=pl.Buffered(1)),
                  pl.BlockSpec((1, n), lambda i: (0, 0)),
                  pl.BlockSpec((1, n), lambda i: (0, 0))],
        out_specs=[pl.BlockSpec((tm, n), lambda i: (i, 0)),
                   pl.BlockSpec((tm, n), lambda i: (i, 0))],
        out_shape=[jax.ShapeDtypeStruct((m, n), F32), jax.ShapeDtypeStruct((m, n), BF16)],
        scratch_shapes=[pltpu.VMEM((k, n), BF16)],
        compiler_params=_params("arbitrary"),
        name=name,
    )(o, xres, w, g, b)


CONV_WIDTH = 3
FFN_TF = 512
FFN_UP_TM = 1024


def _ffn_up_kernel(tiles_per_seq, x_ref, wg_ref, wu_ref, cwg_ref, cwu_ref, cbg_ref, cbu_ref,
                   a_ref, wb_ref, h_ref):
    i = pl.program_id(1)
    tm = x_ref.shape[0]
    tf = a_ref.shape[1]
    halo = SUBLANES

    @pl.when(i == 0)
    def _():
        wb_ref[:, 0:tf] = wg_ref[...].astype(BF16)
        wb_ref[:, tf:2 * tf] = wu_ref[...].astype(BF16)

    @pl.when(i % tiles_per_seq == 0)
    def _():
        h_ref[0:halo, :] = jnp.zeros((halo, h_ref.shape[1]), F32)

    h_ref[halo:halo + tm, :] = jnp.dot(x_ref[...], wb_ref[...], preferred_element_type=F32)

    def conv(cols, cw_ref, cb_ref):
        out = cb_ref[...]
        for tap in range(CONV_WIDTH):
            start = halo - (CONV_WIDTH - 1) + tap
            out = out + cw_ref[tap:tap + 1, :] * h_ref[start:start + tm, cols]
        return out

    g = conv(slice(0, tf), cwg_ref, cbg_ref)
    u = conv(slice(tf, 2 * tf), cwu_ref, cbu_ref)
    a_ref[...] = (g * jax.nn.sigmoid(g) * u).astype(a_ref.dtype)
    h_ref[0:halo, :] = h_ref[tm:tm + halo, :]


def _ffn_up(xb, w_up, conv_w, conv_b, layer, seq):
    m, k = xb.shape
    tm, tf = FFN_UP_TM, FFN_TF
    nf = D_FF // tf
    cb2 = conv_b.reshape(conv_b.shape[0], 1, 2 * D_FF)
    return pl.pallas_call(
        functools.partial(_ffn_up_kernel, seq // tm),
        grid=(nf, m // tm),
        in_specs=[pl.BlockSpec((tm, k), lambda j, i: (i, 0)),
                  pl.BlockSpec((None, k, tf), lambda j, i: (layer, 0, j)),
                  pl.BlockSpec((None, k, tf), lambda j, i: (layer, 0, j + nf)),
                  pl.BlockSpec((None, CONV_WIDTH, tf), lambda j, i: (layer, 0, j)),
                  pl.BlockSpec((None, CONV_WIDTH, tf), lambda j, i: (layer, 0, j + nf)),
                  pl.BlockSpec((None, 1, tf), lambda j, i: (layer, 0, j)),
                  pl.BlockSpec((None, 1, tf), lambda j, i: (layer, 0, j + nf))],
        out_specs=pl.BlockSpec((tm, tf), lambda j, i: (i, j)),
        out_shape=jax.ShapeDtypeStruct((m, D_FF), BF16),
        scratch_shapes=[pltpu.VMEM((k, 2 * tf), BF16), pltpu.VMEM((tm + 2 * SUBLANES, 2 * tf), F32)],
        compiler_params=_params("arbitrary", "arbitrary"),
        name="ffn_up",
    )(xb, w_up, w_up, conv_w, conv_w, cb2, cb2)


FFN_DOWN_TM = 256
FFN_DOWN_WCHUNK = 512


def _ffn_down_ln_kernel(alpha, n_load, emit_bf16, a_ref, xres_ref, w_ref, g_ref, b_ref, *rest):
    if emit_bf16:
        xo_ref, xob_ref, wb_ref = rest
    else:
        (xo_ref, wb_ref), xob_ref = rest, None
    step = pl.program_id(0)
    rows = w_ref.shape[0]

    @pl.when(step < n_load)
    def _():
        off = pl.multiple_of(step * rows, rows)
        wb_ref[pl.ds(off, rows), :] = w_ref[...].astype(BF16)

    @pl.when(step >= n_load)
    def _():
        y = jnp.dot(a_ref[...], wb_ref[...], preferred_element_type=F32)
        out = _deepnorm(xres_ref[...], y, g_ref[...], b_ref[...], alpha)
        xo_ref[...] = out
        if emit_bf16:
            xob_ref[...] = out.astype(BF16)


def _ffn_down_ln(a, xres, w, layer, g, b, alpha, emit_bf16, name):
    m, k = a.shape
    n = w.shape[2]
    tm, rows = FFN_DOWN_TM, FFN_DOWN_WCHUNK
    n_load = k // rows
    tile = lambda s: (jnp.maximum(s - n_load, 0), 0)
    out_specs = [pl.BlockSpec((tm, n), tile)]
    out_shape = [jax.ShapeDtypeStruct((m, n), F32)]
    if emit_bf16:
        out_specs.append(pl.BlockSpec((tm, n), tile))
        out_shape.append(jax.ShapeDtypeStruct((m, n), BF16))
    return pl.pallas_call(
        functools.partial(_ffn_down_ln_kernel, alpha, n_load, emit_bf16),
        grid=(n_load + m // tm,),
        in_specs=[pl.BlockSpec((tm, k), tile),
                  pl.BlockSpec((tm, n), tile),
                  pl.BlockSpec((None, rows, n), lambda s: (layer, jnp.minimum(s, n_load - 1), 0)),
                  pl.BlockSpec((1, n), lambda s: (0, 0)),
                  pl.BlockSpec((1, n), lambda s: (0, 0))],
        out_specs=out_specs,
        out_shape=out_shape,
        scratch_shapes=[pltpu.VMEM((k, n), BF16)],
        compiler_params=_params("arbitrary"),
        name=name,
    )(a, xres, w, g, b)


def kernel(x, a_w_qkv, a_w_o, a_sinks, rel_bias, b_w_qkvf, b_f_bias, b_w_o,
           ffn_w_up, ffn_conv_w, ffn_conv_b, ffn_w_down, ln_g, ln_b):
    batch, seq, d = x.shape
    depth = ffn_w_up.shape[0]
    alpha = (2 * depth) ** 0.25
    m = batch * seq
    xf = x.reshape(m, d)
    xb = None
    biasmask = _swa_bias(rel_bias)
    for layer in range(depth):
        j = layer // 2
        g = ln_g[layer].reshape(2, 1, d)
        bt = ln_b[layer].reshape(2, 1, d)
        if layer % 2 == 0:
            n_qkv = D_MODEL + 2 * A_KV_DIM
            src = xf if xb is None else xb
            qkv = _ws_matmul(src, a_w_qkv, j, n_qkv, n_qkv // 2, 1024, "swa_qkv")
            o = _swa_attention(qkv.reshape(batch, seq, n_qkv), a_sinks[j], biasmask)
            w_o = a_w_o
        else:
            w_t = jnp.swapaxes(b_w_qkvf, 1, 2)
            qkv = _ws_matmul(xb, w_t, j, 3 * D_MODEL, 1024, 2048, "fox_qkv", w_is_transposed=True)
            fb3 = jnp.tile(b_f_bias[j].reshape(1, N_HEADS), (1, FOX_PIECES))
            cumt, kext = _fox_gate(xb, w_t, j, fb3, batch, seq)
            o = _fox_attention(qkv.reshape(batch, seq, 3 * D_MODEL), cumt, kext)
            w_o = b_w_o
        xf, xb = _oproj_ln(o.reshape(m, d), xf, w_o, j, g[0], bt[0], alpha, 512, "oproj_ln")
        a = _ffn_up(xb, ffn_w_up, ffn_conv_w, ffn_conv_b, layer, seq)
        last = layer == depth - 1
        res = _ffn_down_ln(a, xf, ffn_w_down, layer, g[1], bt[1], alpha, not last, "ffn_down_ln")
        xf, xb = (res[0], None) if last else res
    return xf.reshape(batch, seq, d)
```

```python
import functools
import math

import jax
import jax.numpy as jnp
import numpy as np
from jax import lax
from jax.experimental import pallas as pl
from jax.experimental.pallas import tpu as pltpu

D_MODEL = 2048
HEAD_DIM = 64
BLOCK = 128
N_HEADS = D_MODEL // HEAD_DIM
A_KV_HEADS = 4
A_GROUP = N_HEADS // A_KV_HEADS
A_KV_DIM = A_KV_HEADS * HEAD_DIM
REL_BUCKETS = 32
REL_MAX_DIST = 128
D_FF = 5632
LN_EPS = 1e-5
SCALE = HEAD_DIM ** -0.5

SUBLANES = 8
LANES = 128

BF16 = jnp.bfloat16
F32 = jnp.float32
NEG_INF = float("-inf")

_NT_DIMS = (((1,), (1,)), ((), ()))


def _params(*sem):
    return pltpu.CompilerParams(dimension_semantics=sem)


LOG2E = math.log2(math.e)
Q_SCALE = SCALE * LOG2E


def _ws_matmul_kernel(w_is_transposed, q_cols, x_ref, w_ref, o_ref, wb_ref):
    @pl.when(pl.program_id(1) == 0)
    def _():
        wb_ref[...] = w_ref[...].astype(BF16)

    x = x_ref[...].astype(BF16)
    if w_is_transposed:
        y = lax.dot_general(x, wb_ref[...], _NT_DIMS, preferred_element_type=F32)
    else:
        y = jnp.dot(x, wb_ref[...], preferred_element_type=F32)
    tn = o_ref.shape[1]
    col = pl.program_id(0) * tn + lax.broadcasted_iota(jnp.int32, (1, tn), 1)
    y = y * jnp.where(col < q_cols, Q_SCALE, 1.0)
    o_ref[...] = y.astype(o_ref.dtype)


def _ws_matmul(x, w, layer, n_out, tn, tm, name, w_is_transposed=False):
    m, k = x.shape
    if w_is_transposed:
        w_spec = pl.BlockSpec((None, tn, k), lambda j, i: (layer, j, 0))
        wb_shape = (tn, k)
    else:
        w_spec = pl.BlockSpec((None, k, tn), lambda j, i: (layer, 0, j))
        wb_shape = (k, tn)
    return pl.pallas_call(
        functools.partial(_ws_matmul_kernel, w_is_transposed, D_MODEL),
        grid=(n_out // tn, m // tm),
        in_specs=[pl.BlockSpec((tm, k), lambda j, i: (i, 0)), w_spec],
        out_specs=pl.BlockSpec((tm, tn), lambda j, i: (i, j)),
        out_shape=jax.ShapeDtypeStruct((m, n_out), BF16),
        scratch_shapes=[pltpu.VMEM(wb_shape, BF16)],
        compiler_params=_params("arbitrary", "arbitrary"),
        name=name,
    )(x, w)


def _t5_bucket_map():
    qi = np.arange(BLOCK)[None, :]
    kj = np.arange(2 * BLOCK)[:, None]
    dist = BLOCK + qi - kj
    n = np.maximum(dist, 0)
    max_exact = REL_BUCKETS // 2
    nf = np.maximum(n, 1).astype(np.float32)
    large = max_exact + (np.log(nf / max_exact) / math.log(REL_MAX_DIST / max_exact)
                         * (REL_BUCKETS - max_exact)).astype(np.int32)
    large = np.minimum(large, REL_BUCKETS - 1)
    bucket = np.where(n < max_exact, n, large)
    visible = (dist >= 0) & (dist < BLOCK)
    return np.where(visible, bucket, -1).astype(np.int32)


def _swa_bias_kernel(rb_ref, bk_ref, o_ref):
    bk = bk_ref[...]
    key = lax.broadcasted_iota(jnp.int32, bk.shape, 0)

    def head(h, _):
        acc = jnp.full(bk.shape, NEG_INF, F32)
        for b in range(REL_BUCKETS):
            acc = jnp.where(bk == b, rb_ref[b, h] * LOG2E, acc)
        o_ref[0, h] = acc
        o_ref[1, h] = jnp.where(key < BLOCK, NEG_INF, acc)
        return 0

    lax.fori_loop(0, N_HEADS, head, 0)


def _swa_bias(rel_bias):
    bk = jnp.asarray(_t5_bucket_map())
    return pl.pallas_call(
        _swa_bias_kernel,
        in_specs=[pl.BlockSpec(memory_space=pltpu.SMEM),
                  pl.BlockSpec(memory_space=pltpu.VMEM)],
        out_specs=pl.BlockSpec(memory_space=pltpu.VMEM),
        out_shape=jax.ShapeDtypeStruct((2, N_HEADS, 2 * BLOCK, BLOCK), F32),
        name="swa_bias",
    )(rel_bias, bk)


_TN_DIMS = (((0,), (0,)), ((), ()))
SWA_SCORES_AHEAD = 2


SWA_BLOCKS_PER_STEP = 2


def _swa_kernel(sink_ref, q_ref, kp_ref, kc_ref, vp_ref, vc_ref, bm_ref, o_ref,
                qs_ref, s_ref, p_ref, linv_ref):
    units = [(qb, kh) for qb in range(SWA_BLOCKS_PER_STEP) for kh in range(A_KV_HEADS)]
    first_table = jnp.where(pl.program_id(1) == 0, 1, 0)

    def keys_values(prev_ref, cur_ref, qb, ks):
        if qb == 0:
            return jnp.concatenate([prev_ref[:, ks], cur_ref[0:BLOCK, ks]], axis=0)
        return cur_ref[(qb - 1) * BLOCK:(qb + 1) * BLOCK, ks]

    def scores(u):
        qb, kh = units[u]
        ks = slice(kh * HEAD_DIM, (kh + 1) * HEAD_DIM)
        for g in range(A_GROUP):
            h = kh * A_GROUP + g
            qs_ref[u, g * BLOCK:(g + 1) * BLOCK, :] = q_ref[qb * BLOCK:(qb + 1) * BLOCK,
                                                            h * HEAD_DIM:(h + 1) * HEAD_DIM]
        k2 = keys_values(kp_ref, kc_ref, qb, ks)
        s_ref[u] = lax.dot_general(k2, qs_ref[u], _NT_DIMS, preferred_element_type=F32)

    def softmax(u):
        qb, kh = units[u]
        table = first_table if qb == 0 else 0
        for g in range(A_GROUP):
            h = kh * A_GROUP + g
            cols = slice(g * BLOCK, (g + 1) * BLOCK)
            s = s_ref[u, :, cols] + bm_ref[table, h]
            sink = sink_ref[h] * LOG2E
            m = jnp.maximum(jnp.max(s, axis=0, keepdims=True), sink)
            p = jnp.exp2(s - m)
            denom = jnp.sum(p, axis=0, keepdims=True) + jnp.exp2(sink - m)
            p_ref[u, :, cols] = p.astype(BF16)
            linv_ref[u, :, cols] = 1.0 / denom

    def weighted_values(u):
        qb, kh = units[u]
        ks = slice(kh * HEAD_DIM, (kh + 1) * HEAD_DIM)
        v2 = keys_values(vp_ref, vc_ref, qb, ks)
        ot = lax.dot_general(v2, p_ref[u], _TN_DIMS, preferred_element_type=F32) * linv_ref[u]
        for g in range(A_GROUP):
            h = kh * A_GROUP + g
            o_ref[qb * BLOCK:(qb + 1) * BLOCK, h * HEAD_DIM:(h + 1) * HEAD_DIM] = (
                ot[:, g * BLOCK:(g + 1) * BLOCK].T.astype(o_ref.dtype))

    for u in range(min(SWA_SCORES_AHEAD, len(units))):
        scores(u)
    for u in range(len(units)):
        if u + SWA_SCORES_AHEAD < len(units):
            scores(u + SWA_SCORES_AHEAD)
        softmax(u)
        weighted_values(u)


def _swa_attention(qkv, sinks, biasmask):
    b, s, _ = qkv.shape
    rows = SWA_BLOCKS_PER_STEP * BLOCK
    gq = A_GROUP * BLOCK
    n_units = SWA_BLOCKS_PER_STEP * A_KV_HEADS
    kcol = D_MODEL // A_KV_DIM
    prev = lambda n: jnp.maximum(SWA_BLOCKS_PER_STEP * n - 1, 0)
    return pl.pallas_call(
        _swa_kernel,
        grid=(b, s // rows),
        in_specs=[pl.BlockSpec(memory_space=pltpu.SMEM),
                  pl.BlockSpec((None, rows, D_MODEL), lambda bi, n: (bi, n, 0)),
                  pl.BlockSpec((None, BLOCK, A_KV_DIM), lambda bi, n: (bi, prev(n), kcol)),
                  pl.BlockSpec((None, rows, A_KV_DIM), lambda bi, n: (bi, n, kcol)),
                  pl.BlockSpec((None, BLOCK, A_KV_DIM), lambda bi, n: (bi, prev(n), kcol + 1)),
                  pl.BlockSpec((None, rows, A_KV_DIM), lambda bi, n: (bi, n, kcol + 1)),
                  pl.BlockSpec((2, N_HEADS, 2 * BLOCK, BLOCK), lambda bi, n: (0, 0, 0, 0),
                               pipeline_mode=pl.Buffered(1))],
        out_specs=pl.BlockSpec((None, rows, D_MODEL), lambda bi, n: (bi, n, 0)),
        out_shape=jax.ShapeDtypeStruct((b, s, D_MODEL), BF16),
        scratch_shapes=[pltpu.VMEM((n_units, gq, HEAD_DIM), BF16),
                        pltpu.VMEM((n_units, 2 * BLOCK, gq), F32),
                        pltpu.VMEM((n_units, 2 * BLOCK, gq), BF16),
                        pltpu.VMEM((n_units, 1, gq), F32)],
        compiler_params=_params("arbitrary", "arbitrary"),
        name="swa_attn",
    )(sinks, qkv, qkv, qkv, qkv, qkv, biasmask)


FOX_CUM_BLOCK = 256
FOX_TQ = 256
FOX_HEADS_PER_PAIR = LANES // HEAD_DIM
FOX_PAIRS_PER_STEP = 2
FOX_PIECES = 3
FOX_SUM_ROWS = 16
FOX_SCORE_SLOTS = 4
FOX_PV_LAG = 1
FOX_PROB_SLOTS = FOX_PV_LAG + 1
BF16_ONE_PAIR = 0x3F803F80


def _fox_ext_lane(head, piece):
    pair, odd = divmod(head, FOX_HEADS_PER_PAIR)
    return pair * LANES + (1 - odd) * HEAD_DIM + piece


def _fox_placement():
    pm = np.zeros((FOX_PIECES * N_HEADS, D_MODEL), np.float32)
    for t in range(FOX_PIECES):
        for h in range(N_HEADS):
            pm[t * N_HEADS + h, _fox_ext_lane(h, t)] = -1.0
    return pm


def _fox_lane_words():
    w = np.zeros((SUBLANES, LANES), np.uint32)
    for hh in range(FOX_HEADS_PER_PAIR):
        own = np.zeros(LANES, bool)
        own[hh * HEAD_DIM:(hh + 1) * HEAD_DIM] = True
        ext = np.zeros(LANES, bool)
        first = _fox_ext_lane(hh, 0)
        ext[first:first + FOX_PIECES] = True
        w[3 * hh + 0] = np.where(own, 0xFFFFFFFF, 0)
        w[3 * hh + 1] = np.where(ext, 0xFFFFFFFF, 0)
        w[3 * hh + 2] = np.where(ext, BF16_ONE_PAIR, 0)
    return w


def _log_sigmoid(x):
    return jnp.minimum(x, 0.0) - jnp.log1p(jnp.exp(-jnp.abs(x)))


def _round_bf16(x):
    return x.astype(BF16).astype(F32)


def _fox_gate_kernel(x_ref, wf_ref, fb_ref, pm_ref, cumt_ref, kext_ref, w3_ref):
    @pl.when(pl.program_id(0) == 0)
    def _():
        w = wf_ref[...].astype(BF16)
        for t in range(FOX_PIECES):
            w3_ref[t * N_HEADS:(t + 1) * N_HEADS, :] = w

    fz = lax.dot_general(w3_ref[...], x_ref[...], _NT_DIMS, preferred_element_type=F32) + fb_ref[...]
    lf = _log_sigmoid(fz)
    cb = FOX_CUM_BLOCK
    row = lax.broadcasted_iota(jnp.int32, (cb, cb), 0)
    col = lax.broadcasted_iota(jnp.int32, (cb, cb), 1)
    tri = jnp.where(row <= col, 1.0, 0.0).astype(BF16)
    tri3 = jnp.concatenate([tri] * FOX_PIECES, axis=0)
    group = lax.broadcasted_iota(jnp.int32, (lf.shape[0], cb), 0) // N_HEADS
    pm = pm_ref[...]
    n_blocks = lf.shape[1] // cb
    local = []
    for blk in range(n_blocks):
        x = lf[:, blk * cb:(blk + 1) * cb]
        hi = _round_bf16(x)
        r1 = x - hi
        mid = _round_bf16(r1)
        pieces = jnp.concatenate([hi, mid, r1 - mid], axis=1).astype(BF16)
        local.append(jnp.dot(pieces, tri3, preferred_element_type=F32))
    carry = jnp.zeros((lf.shape[0], 1), F32)
    for blk in range(n_blocks):
        cols = slice(blk * cb, (blk + 1) * cb)
        c = local[blk] + carry
        carry = carry + local[blk][:, cb - 1:cb]
        c2 = c * LOG2E
        cumt_ref[:, cols] = c2[:N_HEADS]
        chi = _round_bf16(c2)
        cr1 = c2 - chi
        cmid = _round_bf16(cr1)
        piece = jnp.where(group == 0, chi, jnp.where(group == 1, cmid, cr1 - cmid)).astype(BF16)
        kext_ref[cols, :] = lax.dot_general(piece, pm, _TN_DIMS, preferred_element_type=F32).astype(BF16)


def _fox_gate(xb, w_qkvf_t, layer, fb3, batch, seq):
    pm = jnp.asarray(_fox_placement(), BF16)
    n3 = FOX_PIECES * N_HEADS
    gate_row = 3 * D_MODEL // N_HEADS
    return pl.pallas_call(
        _fox_gate_kernel,
        grid=(batch,),
        in_specs=[pl.BlockSpec((seq, D_MODEL), lambda bi: (bi, 0)),
                  pl.BlockSpec((None, N_HEADS, D_MODEL), lambda bi: (layer, gate_row, 0)),
                  pl.BlockSpec((n3, 1), lambda bi: (0, 0)),
                  pl.BlockSpec((n3, D_MODEL), lambda bi: (0, 0))],
        out_specs=[pl.BlockSpec((None, N_HEADS, seq), lambda bi: (bi, 0, 0)),
                   pl.BlockSpec((None, seq, D_MODEL), lambda bi: (bi, 0, 0))],
        out_shape=[jax.ShapeDtypeStruct((batch, N_HEADS, seq), F32),
                   jax.ShapeDtypeStruct((batch, seq, D_MODEL), BF16)],
        scratch_shapes=[pltpu.VMEM((n3, D_MODEL), BF16)],
        compiler_params=_params("arbitrary"),
        name="fox_gate",
    )(xb, w_qkvf_t, fb3, pm)


def _fox_kernel(q_ref, k_ref, v_ref, kext_ref, cbt_ref, lw_ref, o_ref, qa_ref, ka_ref, vt_ref,
                s_ref, p_ref):
    seq = q_ref.shape[0]
    tq = FOX_TQ
    key = lax.broadcasted_iota(jnp.int32, (tq, tq), 0)
    qry = lax.broadcasted_iota(jnp.int32, (tq, tq), 1)
    causal = key <= qry
    words = lambda x: pltpu.bitcast(x, jnp.uint32)
    for pair in range(FOX_PAIRS_PER_STEP):
        lanes = slice(pair * LANES, (pair + 1) * LANES)
        q = words(q_ref[:, lanes])
        k = words(k_ref[:, lanes])
        kext = words(kext_ref[:, lanes])
        vt = v_ref[:, lanes].T
        for hh in range(FOX_HEADS_PER_PAIR):
            h = pair * FOX_HEADS_PER_PAIR + hh
            own = lw_ref[3 * hh + 0:3 * hh + 1, :]
            ext = lw_ref[3 * hh + 1:3 * hh + 2, :]
            ones = lw_ref[3 * hh + 2:3 * hh + 3, :]
            qa_ref[h] = pltpu.bitcast((q & own) | ones, BF16)
            ka_ref[h] = pltpu.bitcast((k & ~ext) | (kext & ext), BF16)
            vt_ref[h, 0:HEAD_DIM, :] = vt[hh * HEAD_DIM:(hh + 1) * HEAD_DIM]
            vt_ref[h, HEAD_DIM:, :] = jnp.ones((FOX_SUM_ROWS, seq), BF16)

    heads = FOX_PAIRS_PER_STEP * FOX_HEADS_PER_PAIR
    units = [(i, h) for i in range(seq // tq) for h in range(heads)]

    def scores(u):
        i, h = units[u]
        slot = u % FOX_SCORE_SLOTS
        r0, r1 = i * tq, (i + 1) * tq
        qa = qa_ref[h, r0:r1, :]
        s_d = lax.dot_general(ka_ref[h, r0:r1, :], qa, _NT_DIMS, preferred_element_type=F32)
        s_ref[slot, r0:r1, :] = jnp.where(causal, s_d, NEG_INF)
        if i > 0:
            s_ref[slot, 0:r0, :] = lax.dot_general(ka_ref[h, 0:r0, :], qa, _NT_DIMS,
                                                   preferred_element_type=F32)

    def softmax(u):
        i, h = units[u]
        slot = u % FOX_SCORE_SLOTS
        r0, r1 = i * tq, (i + 1) * tq
        head = pl.program_id(1) * heads + h
        cbq = cbt_ref[pl.ds(head, 1), r0:r1]
        m = jnp.max(s_ref[slot, 0:r1, :], axis=0, keepdims=True)
        shift = cbq - (m + cbq)
        p_ref[u % FOX_PROB_SLOTS, 0:r1, :] = jnp.exp2(s_ref[slot, 0:r1, :] + shift).astype(BF16)

    outs = []

    def weighted_values(u):
        i, h = units[u]
        r1 = (i + 1) * tq
        ot = jnp.dot(vt_ref[h, :, 0:r1], p_ref[u % FOX_PROB_SLOTS, 0:r1, :], preferred_element_type=F32)
        outs.append(ot[0:HEAD_DIM] * (1.0 / ot[HEAD_DIM:HEAD_DIM + 1]))
        if h == heads - 1:
            o_ref[i * tq:r1, :] = jnp.concatenate(outs, axis=0).T.astype(o_ref.dtype)
            outs.clear()

    ahead = FOX_SCORE_SLOTS - 1
    for u in range(min(ahead, len(units))):
        scores(u)
    for u in range(len(units)):
        if u + ahead < len(units):
            scores(u + ahead)
        softmax(u)
        if u >= FOX_PV_LAG:
            weighted_values(u - FOX_PV_LAG)
    for u in range(len(units) - FOX_PV_LAG, len(units)):
        weighted_values(u)


def _fox_attention(qkv, cumt, kext):
    b, s, _ = qkv.shape
    heads = FOX_PAIRS_PER_STEP * FOX_HEADS_PER_PAIR
    width = FOX_PAIRS_PER_STEP * LANES
    lw = jnp.asarray(_fox_lane_words())
    ncol = D_MODEL // width
    return pl.pallas_call(
        _fox_kernel,
        grid=(b, N_HEADS // heads),
        in_specs=[pl.BlockSpec((None, s, width), lambda bi, hp: (bi, 0, hp)),
                  pl.BlockSpec((None, s, width), lambda bi, hp: (bi, 0, ncol + hp)),
                  pl.BlockSpec((None, s, width), lambda bi, hp: (bi, 0, 2 * ncol + hp)),
                  pl.BlockSpec((None, s, width), lambda bi, hp: (bi, 0, hp)),
                  pl.BlockSpec((None, N_HEADS, s), lambda bi, hp: (bi, 0, 0)),
                  pl.BlockSpec((SUBLANES, LANES), lambda bi, hp: (0, 0))],
        out_specs=pl.BlockSpec((None, s, width), lambda bi, hp: (bi, 0, hp)),
        out_shape=jax.ShapeDtypeStruct((b, s, D_MODEL), BF16),
        scratch_shapes=[pltpu.VMEM((heads, s, LANES), BF16), pltpu.VMEM((heads, s, LANES), BF16),
                        pltpu.VMEM((heads, HEAD_DIM + FOX_SUM_ROWS, s), BF16),
                        pltpu.VMEM((FOX_SCORE_SLOTS, s, FOX_TQ), F32),
                        pltpu.VMEM((FOX_PROB_SLOTS, s, FOX_TQ), BF16)],
        compiler_params=_params("arbitrary", "arbitrary"),
        name="fox_attn",
    )(qkv, qkv, qkv, kext, cumt, lw)


def _deepnorm(xres, y, g, b, alpha):
    z = alpha * xres + y
    mu = jnp.mean(z, axis=-1, keepdims=True)
    zc = z - mu
    var = jnp.mean(zc * zc, axis=-1, keepdims=True)
    return zc * lax.rsqrt(var + LN_EPS) * g + b


LN_SUBTILES = 2


def _oproj_ln_kernel(alpha, o_ref, xres_ref, w_ref, g_ref, b_ref, xo_ref, xob_ref, wb_ref):
    @pl.when(pl.program_id(0) == 0)
    def _():
        wb_ref[...] = w_ref[...].astype(BF16)

    ts = o_ref.shape[0] // LN_SUBTILES
    ys = [jnp.dot(o_ref[r * ts:(r + 1) * ts, :], wb_ref[...], preferred_element_type=F32)
          for r in range(LN_SUBTILES)]
    for r, y in enumerate(ys):
        rows = slice(r * ts, (r + 1) * ts)
        out = _deepnorm(xres_ref[rows, :], y, g_ref[...], b_ref[...], alpha)
        xo_ref[rows, :] = out
        xob_ref[rows, :] = out.astype(BF16)


def _oproj_ln(o, xres, w, layer, g, b, alpha, tm, name):
    m, k = o.shape
    n = w.shape[2]
    return pl.pallas_call(
        functools.partial(_oproj_ln_kernel, alpha),
        grid=(m // tm,),
        in_specs=[pl.BlockSpec((tm, k), lambda i: (i, 0)),
                  pl.BlockSpec((tm, n), lambda i: (i, 0)),
                  pl.BlockSpec((None, k, n), lambda i: (layer, 0, 0), pipeline_mode=pl.Buffered(1)),
                  pl.BlockSpec((1, n), lambda i: (0, 0)),
                  pl.BlockSpec((1, n), lambda i: (0, 0))],
        out_specs=[pl.BlockSpec((tm, n), lambda i: (i, 0)),
                   pl.BlockSpec((tm, n), lambda i: (i, 0))],
        out_shape=[jax.ShapeDtypeStruct((m, n), F32), jax.ShapeDtypeStruct((m, n), BF16)],
        scratch_shapes=[pltpu.VMEM((k, n), BF16)],
        compiler_params=_params("arbitrary"),
        name=name,
    )(o, xres, w, g, b)


CONV_WIDTH = 3
FFN_TF = 512
FFN_UP_TM = 1024


def _ffn_up_kernel(tiles_per_seq, x_ref, wg_ref, wu_ref, cwg_ref, cwu_ref, cbg_ref, cbu_ref,
                   a_ref, wb_ref, h_ref):
    i = pl.program_id(1)
    tm = x_ref.shape[0]
    tf = a_ref.shape[1]
    halo = SUBLANES

    @pl.when(i == 0)
    def _():
        wb_ref[:, 0:tf] = wg_ref[...].astype(BF16)
        wb_ref[:, tf:2 * tf] = wu_ref[...].astype(BF16)

    @pl.when(i % tiles_per_seq == 0)
    def _():
        h_ref[0:halo, :] = jnp.zeros((halo, h_ref.shape[1]), F32)

    h_ref[halo:halo + tm, :] = jnp.dot(x_ref[...], wb_ref[...], preferred_element_type=F32)

    def conv(cols, cw_ref, cb_ref):
        out = cb_ref[...]
        for tap in range(CONV_WIDTH):
            start = halo - (CONV_WIDTH - 1) + tap
            out = out + cw_ref[tap:tap + 1, :] * h_ref[start:start + tm, cols]
        return out

    g = conv(slice(0, tf), cwg_ref, cbg_ref)
    u = conv(slice(tf, 2 * tf), cwu_ref, cbu_ref)
    a_ref[...] = (g * jax.nn.sigmoid(g) * u).astype(a_ref.dtype)
    h_ref[0:halo, :] = h_ref[tm:tm + halo, :]


def _ffn_up(xb, w_up, conv_w, conv_b, layer, seq):
    m, k = xb.shape
    tm, tf = FFN_UP_TM, FFN_TF
    nf = D_FF // tf
    cb2 = conv_b.reshape(conv_b.shape[0], 1, 2 * D_FF)
    return pl.pallas_call(
        functools.partial(_ffn_up_kernel, seq // tm),
        grid=(nf, m // tm),
        in_specs=[pl.BlockSpec((tm, k), lambda j, i: (i, 0)),
                  pl.BlockSpec((None, k, tf), lambda j, i: (layer, 0, j)),
                  pl.BlockSpec((None, k, tf), lambda j, i: (layer, 0, j + nf)),
                  pl.BlockSpec((None, CONV_WIDTH, tf), lambda j, i: (layer, 0, j)),
                  pl.BlockSpec((None, CONV_WIDTH, tf), lambda j, i: (layer, 0, j + nf)),
                  pl.BlockSpec((None, 1, tf), lambda j, i: (layer, 0, j)),
                  pl.BlockSpec((None, 1, tf), lambda j, i: (layer, 0, j + nf))],
        out_specs=pl.BlockSpec((tm, tf), lambda j, i: (i, j)),
        out_shape=jax.ShapeDtypeStruct((m, D_FF), BF16),
        scratch_shapes=[pltpu.VMEM((k, 2 * tf), BF16), pltpu.VMEM((tm + 2 * SUBLANES, 2 * tf), F32)],
        compiler_params=_params("arbitrary", "arbitrary"),
        name="ffn_up",
    )(xb, w_up, w_up, conv_w, conv_w, cb2, cb2)


FFN_DOWN_TM = 256
FFN_DOWN_WCHUNK = 512


def _ffn_down_ln_kernel(alpha, n_load, emit_bf16, a_ref, xres_ref, w_ref, g_ref, b_ref, *rest):
    if emit_bf16:
        xo_ref, xob_ref, wb_ref = rest
    else:
        (xo_ref, wb_ref), xob_ref = rest, None
    step = pl.program_id(0)
    rows = w_ref.shape[0]

    @pl.when(step < n_load)
    def _():
        off = pl.multiple_of(step * rows, rows)
        wb_ref[pl.ds(off, rows), :] = w_ref[...].astype(BF16)

    @pl.when(step >= n_load)
    def _():
        y = jnp.dot(a_ref[...], wb_ref[...], preferred_element_type=F32)
        out = _deepnorm(xres_ref[...], y, g_ref[...], b_ref[...], alpha)
        xo_ref[...] = out
        if emit_bf16:
            xob_ref[...] = out.astype(BF16)


def _ffn_down_ln(a, xres, w, layer, g, b, alpha, emit_bf16, name):
    m, k = a.shape
    n = w.shape[2]
    tm, rows = FFN_DOWN_TM, FFN_DOWN_WCHUNK
    n_load = k // rows
    tile = lambda s: (jnp.maximum(s - n_load, 0), 0)
    out_specs = [pl.BlockSpec((tm, n), tile)]
    out_shape = [jax.ShapeDtypeStruct((m, n), F32)]
    if emit_bf16:
        out_specs.append(pl.BlockSpec((tm, n), tile))
        out_shape.append(jax.ShapeDtypeStruct((m, n), BF16))
    return pl.pallas_call(
        functools.partial(_ffn_down_ln_kernel, alpha, n_load, emit_bf16),
        grid=(n_load + m // tm,),
        in_specs=[pl.BlockSpec((tm, k), tile),
                  pl.BlockSpec((tm, n), tile),
                  pl.BlockSpec((None, rows, n), lambda s: (layer, jnp.minimum(s, n_load - 1), 0)),
                  pl.BlockSpec((1, n), lambda s: (0, 0)),
                  pl.BlockSpec((1, n), lambda s: (0, 0))],
        out_specs=out_specs,
        out_shape=out_shape,
        scratch_shapes=[pltpu.VMEM((k, n), BF16)],
        compiler_params=_params("arbitrary"),
        name=name,
    )(a, xres, w, g, b)


def kernel(x, a_w_qkv, a_w_o, a_sinks, rel_bias, b_w_qkvf, b_f_bias, b_w_o,
           ffn_w_up, ffn_conv_w, ffn_conv_b, ffn_w_down, ln_g, ln_b):
    batch, seq, d = x.shape
    depth = ffn_w_up.shape[0]
    alpha = (2 * depth) ** 0.25
    m = batch * seq
    xf = x.reshape(m, d)
    xb = None
    biasmask = _swa_bias(rel_bias)
    for layer in range(depth):
        j = layer // 2
        g = ln_g[layer].reshape(2, 1, d)
        bt = ln_b[layer].reshape(2, 1, d)
        if layer % 2 == 0:
            n_qkv = D_MODEL + 2 * A_KV_DIM
            src = xf if xb is None else xb
            qkv = _ws_matmul(src, a_w_qkv, j, n_qkv, n_qkv // 2, 1024, "swa_qkv")
            o = _swa_attention(qkv.reshape(batch, seq, n_qkv), a_sinks[j], biasmask)
            w_o = a_w_o
        else:
            w_t = jnp.swapaxes(b_w_qkvf, 1, 2)
            qkv = _ws_matmul(xb, w_t, j, 3 * D_MODEL, 1024, 2048, "fox_qkv", w_is_transposed=True)
            fb3 = jnp.tile(b_f_bias[j].reshape(N_HEADS, 1), (FOX_PIECES, 1))
            cumt, kext = _fox_gate(xb, w_t, j, fb3, batch, seq)
            o = _fox_attention(qkv.reshape(batch, seq, 3 * D_MODEL), cumt, kext)
            w_o = b_w_o
        xf, xb = _oproj_ln(o.reshape(m, d), xf, w_o, j, g[0], bt[0], alpha, 512, "oproj_ln")
        a = _ffn_up(xb, ffn_w_up, ffn_conv_w, ffn_conv_b, layer, seq)
        last = layer == depth - 1
        res = _ffn_down_ln(a, xf, ffn_w_down, layer, g[1], bt[1], alpha, not last, "ffn_down_ln")
        xf, xb = (res[0], None) if last else res
    return xf.reshape(batch, seq, d)
```

```python
import functools
import math

import jax
import jax.numpy as jnp
import numpy as np
from jax import lax
from jax.experimental import pallas as pl
from jax.experimental.pallas import tpu as pltpu

D_MODEL = 2048
HEAD_DIM = 64
BLOCK = 128
N_HEADS = D_MODEL // HEAD_DIM
A_KV_HEADS = 4
A_GROUP = N_HEADS // A_KV_HEADS
A_KV_DIM = A_KV_HEADS * HEAD_DIM
REL_BUCKETS = 32
REL_MAX_DIST = 128
D_FF = 5632
LN_EPS = 1e-5
SCALE = HEAD_DIM ** -0.5

SUBLANES = 8
LANES = 128

BF16 = jnp.bfloat16
F32 = jnp.float32
NEG_INF = float("-inf")

_NT_DIMS = (((1,), (1,)), ((), ()))


def _params(*sem):
    return pltpu.CompilerParams(dimension_semantics=sem)


LOG2E = math.log2(math.e)
Q_SCALE = SCALE * LOG2E


def _ws_matmul_kernel(w_is_transposed, q_cols, x_ref, w_ref, o_ref, wb_ref):
    @pl.when(pl.program_id(1) == 0)
    def _():
        wb_ref[...] = w_ref[...].astype(BF16)

    x = x_ref[...].astype(BF16)
    if w_is_transposed:
        y = lax.dot_general(x, wb_ref[...], _NT_DIMS, preferred_element_type=F32)
    else:
        y = jnp.dot(x, wb_ref[...], preferred_element_type=F32)
    tn = o_ref.shape[1]
    col = pl.program_id(0) * tn + lax.broadcasted_iota(jnp.int32, (1, tn), 1)
    y = y * jnp.where(col < q_cols, Q_SCALE, 1.0)
    o_ref[...] = y.astype(o_ref.dtype)


def _ws_matmul(x, w, layer, n_out, tn, tm, name, w_is_transposed=False):
    m, k = x.shape
    if w_is_transposed:
        w_spec = pl.BlockSpec((None, tn, k), lambda j, i: (layer, j, 0))
        wb_shape = (tn, k)
    else:
        w_spec = pl.BlockSpec((None, k, tn), lambda j, i: (layer, 0, j))
        wb_shape = (k, tn)
    return pl.pallas_call(
        functools.partial(_ws_matmul_kernel, w_is_transposed, D_MODEL),
        grid=(n_out // tn, m // tm),
        in_specs=[pl.BlockSpec((tm, k), lambda j, i: (i, 0)), w_spec],
        out_specs=pl.BlockSpec((tm, tn), lambda j, i: (i, j)),
        out_shape=jax.ShapeDtypeStruct((m, n_out), BF16),
        scratch_shapes=[pltpu.VMEM(wb_shape, BF16)],
        compiler_params=_params("arbitrary", "arbitrary"),
        name=name,
    )(x, w)


def _t5_bucket_map():
    qi = np.arange(BLOCK)[None, :]
    kj = np.arange(2 * BLOCK)[:, None]
    dist = BLOCK + qi - kj
    n = np.maximum(dist, 0)
    max_exact = REL_BUCKETS // 2
    nf = np.maximum(n, 1).astype(np.float32)
    large = max_exact + (np.log(nf / max_exact) / math.log(REL_MAX_DIST / max_exact)
                         * (REL_BUCKETS - max_exact)).astype(np.int32)
    large = np.minimum(large, REL_BUCKETS - 1)
    bucket = np.where(n < max_exact, n, large)
    visible = (dist >= 0) & (dist < BLOCK)
    return np.where(visible, bucket, -1).astype(np.int32)


def _swa_bias_kernel(rb_ref, bk_ref, o_ref):
    bk = bk_ref[...]
    key = lax.broadcasted_iota(jnp.int32, bk.shape, 0)

    def head(h, _):
        acc = jnp.full(bk.shape, NEG_INF, F32)
        for b in range(REL_BUCKETS):
            acc = jnp.where(bk == b, rb_ref[b, h] * LOG2E, acc)
        o_ref[0, h] = acc
        o_ref[1, h] = jnp.where(key < BLOCK, NEG_INF, acc)
        return 0

    lax.fori_loop(0, N_HEADS, head, 0)


def _swa_bias(rel_bias):
    bk = jnp.asarray(_t5_bucket_map())
    return pl.pallas_call(
        _swa_bias_kernel,
        in_specs=[pl.BlockSpec(memory_space=pltpu.SMEM),
                  pl.BlockSpec(memory_space=pltpu.VMEM)],
        out_specs=pl.BlockSpec(memory_space=pltpu.VMEM),
        out_shape=jax.ShapeDtypeStruct((2, N_HEADS, 2 * BLOCK, BLOCK), F32),
        name="swa_bias",
    )(rel_bias, bk)


_TN_DIMS = (((0,), (0,)), ((), ()))
SWA_SCORES_AHEAD = 2


SWA_BLOCKS_PER_STEP = 2


def _swa_kernel(sink_ref, q_ref, kp_ref, kc_ref, vp_ref, vc_ref, bm_ref, o_ref,
                qs_ref, s_ref, p_ref, linv_ref):
    units = [(qb, kh) for qb in range(SWA_BLOCKS_PER_STEP) for kh in range(A_KV_HEADS)]
    first_table = jnp.where(pl.program_id(1) == 0, 1, 0)

    def keys_values(prev_ref, cur_ref, qb, ks):
        if qb == 0:
            return jnp.concatenate([prev_ref[:, ks], cur_ref[0:BLOCK, ks]], axis=0)
        return cur_ref[(qb - 1) * BLOCK:(qb + 1) * BLOCK, ks]

    def scores(u):
        qb, kh = units[u]
        ks = slice(kh * HEAD_DIM, (kh + 1) * HEAD_DIM)
        for g in range(A_GROUP):
            h = kh * A_GROUP + g
            qs_ref[u, g * BLOCK:(g + 1) * BLOCK, :] = q_ref[qb * BLOCK:(qb + 1) * BLOCK,
                                                            h * HEAD_DIM:(h + 1) * HEAD_DIM]
        k2 = keys_values(kp_ref, kc_ref, qb, ks)
        s = lax.dot_general(k2, qs_ref[u], _NT_DIMS, preferred_element_type=F32)
        for g in range(A_GROUP):
            s_ref[u, g] = s[:, g * BLOCK:(g + 1) * BLOCK]

    def softmax(u):
        qb, kh = units[u]
        table = first_table if qb == 0 else 0
        for g in range(A_GROUP):
            h = kh * A_GROUP + g
            cols = slice(g * BLOCK, (g + 1) * BLOCK)
            s = s_ref[u, g] + bm_ref[table, h]
            sink = sink_ref[h] * LOG2E
            m = jnp.maximum(jnp.max(s, axis=0, keepdims=True), sink)
            p = jnp.exp2(s - m)
            denom = jnp.sum(p, axis=0, keepdims=True) + jnp.exp2(sink - m)
            p_ref[u, g // 2, :, (g % 2) * BLOCK:(g % 2 + 1) * BLOCK] = p.astype(BF16)
            linv_ref[u, :, cols] = 1.0 / denom

    def weighted_values(u):
        qb, kh = units[u]
        ks = slice(kh * HEAD_DIM, (kh + 1) * HEAD_DIM)
        v2 = keys_values(vp_ref, vc_ref, qb, ks)
        for c in range(A_GROUP // 2):
            cols = slice(2 * c * BLOCK, 2 * (c + 1) * BLOCK)
            ot = (lax.dot_general(v2, p_ref[u, c], _TN_DIMS, preferred_element_type=F32)
                  * linv_ref[u, :, cols])
            for gg in range(2):
                h = kh * A_GROUP + 2 * c + gg
                o_ref[qb * BLOCK:(qb + 1) * BLOCK, h * HEAD_DIM:(h + 1) * HEAD_DIM] = (
                    ot[:, gg * BLOCK:(gg + 1) * BLOCK].T.astype(o_ref.dtype))

    for u in range(min(SWA_SCORES_AHEAD, len(units))):
        scores(u)
    for u in range(len(units)):
        if u + SWA_SCORES_AHEAD < len(units):
            scores(u + SWA_SCORES_AHEAD)
        softmax(u)
        weighted_values(u)


def _swa_attention(qkv, sinks, biasmask):
    b, s, _ = qkv.shape
    rows = SWA_BLOCKS_PER_STEP * BLOCK
    gq = A_GROUP * BLOCK
    n_units = SWA_BLOCKS_PER_STEP * A_KV_HEADS
    kcol = D_MODEL // A_KV_DIM
    prev = lambda n: jnp.maximum(SWA_BLOCKS_PER_STEP * n - 1, 0)
    return pl.pallas_call(
        _swa_kernel,
        grid=(b, s // rows),
        in_specs=[pl.BlockSpec(memory_space=pltpu.SMEM),
                  pl.BlockSpec((None, rows, D_MODEL), lambda bi, n: (bi, n, 0)),
                  pl.BlockSpec((None, BLOCK, A_KV_DIM), lambda bi, n: (bi, prev(n), kcol)),
                  pl.BlockSpec((None, rows, A_KV_DIM), lambda bi, n: (bi, n, kcol)),
                  pl.BlockSpec((None, BLOCK, A_KV_DIM), lambda bi, n: (bi, prev(n), kcol + 1)),
                  pl.BlockSpec((None, rows, A_KV_DIM), lambda bi, n: (bi, n, kcol + 1)),
                  pl.BlockSpec((2, N_HEADS, 2 * BLOCK, BLOCK), lambda bi, n: (0, 0, 0, 0),
                               pipeline_mode=pl.Buffered(1))],
        out_specs=pl.BlockSpec((None, rows, D_MODEL), lambda bi, n: (bi, n, 0)),
        out_shape=jax.ShapeDtypeStruct((b, s, D_MODEL), BF16),
        scratch_shapes=[pltpu.VMEM((n_units, gq, HEAD_DIM), BF16),
                        pltpu.VMEM((n_units, A_GROUP, 2 * BLOCK, BLOCK), F32),
                        pltpu.VMEM((n_units, A_GROUP // 2, 2 * BLOCK, 2 * BLOCK), BF16),
                        pltpu.VMEM((n_units, 1, gq), F32)],
        compiler_params=_params("arbitrary", "arbitrary"),
        name="swa_attn",
    )(sinks, qkv, qkv, qkv, qkv, qkv, biasmask)


FOX_CUM_BLOCK = 256
FOX_TQ = 256
FOX_HEADS_PER_PAIR = LANES // HEAD_DIM
FOX_PAIRS_PER_STEP = 2
FOX_PIECES = 3
FOX_SUM_ROWS = 16
FOX_SCORE_SLOTS = 4
FOX_PV_LAG = 1
FOX_PROB_SLOTS = FOX_PV_LAG + 1
BF16_ONE_PAIR = 0x3F803F80


def _fox_ext_lane(head, piece):
    pair, odd = divmod(head, FOX_HEADS_PER_PAIR)
    return pair * LANES + (1 - odd) * HEAD_DIM + piece


def _fox_placement():
    pm = np.zeros((FOX_PIECES * N_HEADS, D_MODEL), np.float32)
    for t in range(FOX_PIECES):
        for h in range(N_HEADS):
            pm[t * N_HEADS + h, _fox_ext_lane(h, t)] = -1.0
    return pm


def _fox_lane_words():
    w = np.zeros((SUBLANES, LANES), np.uint32)
    for hh in range(FOX_HEADS_PER_PAIR):
        own = np.zeros(LANES, bool)
        own[hh * HEAD_DIM:(hh + 1) * HEAD_DIM] = True
        ext = np.zeros(LANES, bool)
        first = _fox_ext_lane(hh, 0)
        ext[first:first + FOX_PIECES] = True
        w[3 * hh + 0] = np.where(own, 0xFFFFFFFF, 0)
        w[3 * hh + 1] = np.where(ext, 0xFFFFFFFF, 0)
        w[3 * hh + 2] = np.where(ext, BF16_ONE_PAIR, 0)
    return w


def _log_sigmoid(x):
    return jnp.minimum(x, 0.0) - jnp.log1p(jnp.exp(-jnp.abs(x)))


def _round_bf16(x):
    return x.astype(BF16).astype(F32)


def _fox_gate_kernel(x_ref, wf_ref, fb_ref, pm_ref, cumt_ref, kext_ref, w3_ref):
    @pl.when(pl.program_id(0) == 0)
    def _():
        w = wf_ref[...].astype(BF16)
        for t in range(FOX_PIECES):
            w3_ref[t * N_HEADS:(t + 1) * N_HEADS, :] = w

    fz = lax.dot_general(w3_ref[...], x_ref[...], _NT_DIMS, preferred_element_type=F32) + fb_ref[...]
    lf = _log_sigmoid(fz)
    cb = FOX_CUM_BLOCK
    row = lax.broadcasted_iota(jnp.int32, (cb, cb), 0)
    col = lax.broadcasted_iota(jnp.int32, (cb, cb), 1)
    tri = jnp.where(row <= col, 1.0, 0.0).astype(BF16)
    tri3 = jnp.concatenate([tri] * FOX_PIECES, axis=0)
    group = lax.broadcasted_iota(jnp.int32, (lf.shape[0], cb), 0) // N_HEADS
    pm = pm_ref[...]
    n_blocks = lf.shape[1] // cb
    local = []
    for blk in range(n_blocks):
        x = lf[:, blk * cb:(blk + 1) * cb]
        hi = _round_bf16(x)
        r1 = x - hi
        mid = _round_bf16(r1)
        pieces = jnp.concatenate([hi, mid, r1 - mid], axis=1).astype(BF16)
        local.append(jnp.dot(pieces, tri3, preferred_element_type=F32))
    carry = jnp.zeros((lf.shape[0], 1), F32)
    for blk in range(n_blocks):
        cols = slice(blk * cb, (blk + 1) * cb)
        c = local[blk] + carry
        carry = carry + local[blk][:, cb - 1:cb]
        c2 = c * LOG2E
        cumt_ref[:, cols] = c2[:N_HEADS]
        chi = _round_bf16(c2)
        cr1 = c2 - chi
        cmid = _round_bf16(cr1)
        piece = jnp.where(group == 0, chi, jnp.where(group == 1, cmid, cr1 - cmid)).astype(BF16)
        kext_ref[cols, :] = lax.dot_general(piece, pm, _TN_DIMS, preferred_element_type=F32).astype(BF16)


def _fox_gate(xb, w_qkvf_t, layer, fb3, batch, seq):
    pm = jnp.asarray(_fox_placement(), BF16)
    n3 = FOX_PIECES * N_HEADS
    gate_row = 3 * D_MODEL // N_HEADS
    return pl.pallas_call(
        _fox_gate_kernel,
        grid=(batch,),
        in_specs=[pl.BlockSpec((seq, D_MODEL), lambda bi: (bi, 0)),
                  pl.BlockSpec((None, N_HEADS, D_MODEL), lambda bi: (layer, gate_row, 0)),
                  pl.BlockSpec((n3, 1), lambda bi: (0, 0)),
                  pl.BlockSpec((n3, D_MODEL), lambda bi: (0, 0))],
        out_specs=[pl.BlockSpec((None, N_HEADS, seq), lambda bi: (bi, 0, 0)),
                   pl.BlockSpec((None, seq, D_MODEL), lambda bi: (bi, 0, 0))],
        out_shape=[jax.ShapeDtypeStruct((batch, N_HEADS, seq), F32),
                   jax.ShapeDtypeStruct((batch, seq, D_MODEL), BF16)],
        scratch_shapes=[pltpu.VMEM((n3, D_MODEL), BF16)],
        compiler_params=_params("arbitrary"),
        name="fox_gate",
    )(xb, w_qkvf_t, fb3, pm)


def _fox_kernel(q_ref, k_ref, v_ref, kext_ref, cbt_ref, lw_ref, o_ref, qa_ref, ka_ref, vt_ref,
                s_ref, p_ref):
    seq = q_ref.shape[0]
    tq = FOX_TQ
    key = lax.broadcasted_iota(jnp.int32, (tq, tq), 0)
    qry = lax.broadcasted_iota(jnp.int32, (tq, tq), 1)
    causal = key <= qry
    words = lambda x: pltpu.bitcast(x, jnp.uint32)
    for pair in range(FOX_PAIRS_PER_STEP):
        lanes = slice(pair * LANES, (pair + 1) * LANES)
        q = words(q_ref[:, lanes])
        k = words(k_ref[:, lanes])
        kext = words(kext_ref[:, lanes])
        vt = v_ref[:, lanes].T
        for hh in range(FOX_HEADS_PER_PAIR):
            h = pair * FOX_HEADS_PER_PAIR + hh
            own = lw_ref[3 * hh + 0:3 * hh + 1, :]
            ext = lw_ref[3 * hh + 1:3 * hh + 2, :]
            ones = lw_ref[3 * hh + 2:3 * hh + 3, :]
            qa_ref[h] = pltpu.bitcast((q & own) | ones, BF16)
            ka_ref[h] = pltpu.bitcast((k & ~ext) | (kext & ext), BF16)
            vt_ref[h, 0:HEAD_DIM, :] = vt[hh * HEAD_DIM:(hh + 1) * HEAD_DIM]
            vt_ref[h, HEAD_DIM:, :] = jnp.ones((FOX_SUM_ROWS, seq), BF16)

    heads = FOX_PAIRS_PER_STEP * FOX_HEADS_PER_PAIR
    units = [(i, h) for i in range(seq // tq) for h in range(heads)]

    def scores(u):
        i, h = units[u]
        slot = u % FOX_SCORE_SLOTS
        r0, r1 = i * tq, (i + 1) * tq
        qa = qa_ref[h, r0:r1, :]
        s_d = lax.dot_general(ka_ref[h, r0:r1, :], qa, _NT_DIMS, preferred_element_type=F32)
        s_ref[slot, r0:r1, :] = jnp.where(causal, s_d, NEG_INF)
        if i > 0:
            s_ref[slot, 0:r0, :] = lax.dot_general(ka_ref[h, 0:r0, :], qa, _NT_DIMS,
                                                   preferred_element_type=F32)

    def softmax(u):
        i, h = units[u]
        slot = u % FOX_SCORE_SLOTS
        r0, r1 = i * tq, (i + 1) * tq
        head = pl.program_id(1) * heads + h
        cbq = cbt_ref[pl.ds(head, 1), r0:r1]
        m = jnp.max(s_ref[slot, 0:r1, :], axis=0, keepdims=True)
        shift = cbq - (m + cbq)
        p_ref[u % FOX_PROB_SLOTS, 0:r1, :] = jnp.exp2(s_ref[slot, 0:r1, :] + shift).astype(BF16)

    outs = []

    def weighted_values(u):
        i, h = units[u]
        r1 = (i + 1) * tq
        ot = jnp.dot(vt_ref[h, :, 0:r1], p_ref[u % FOX_PROB_SLOTS, 0:r1, :], preferred_element_type=F32)
        outs.append(ot[0:HEAD_DIM] * (1.0 / ot[HEAD_DIM:HEAD_DIM + 1]))
        if h == heads - 1:
            o_ref[i * tq:r1, :] = jnp.concatenate(outs, axis=0).T.astype(o_ref.dtype)
            outs.clear()

    ahead = FOX_SCORE_SLOTS - 1
    for u in range(min(ahead, len(units))):
        scores(u)
    for u in range(len(units)):
        if u + ahead < len(units):
            scores(u + ahead)
        softmax(u)
        if u >= FOX_PV_LAG:
            weighted_values(u - FOX_PV_LAG)
    for u in range(len(units) - FOX_PV_LAG, len(units)):
        weighted_values(u)


def _fox_attention(qkv, cumt, kext):
    b, s, _ = qkv.shape
    heads = FOX_PAIRS_PER_STEP * FOX_HEADS_PER_PAIR
    width = FOX_PAIRS_PER_STEP * LANES
    lw = jnp.asarray(_fox_lane_words())
    ncol = D_MODEL // width
    return pl.pallas_call(
        _fox_kernel,
        grid=(b, N_HEADS // heads),
        in_specs=[pl.BlockSpec((None, s, width), lambda bi, hp: (bi, 0, hp)),
                  pl.BlockSpec((None, s, width), lambda bi, hp: (bi, 0, ncol + hp)),
                  pl.BlockSpec((None, s, width), lambda bi, hp: (bi, 0, 2 * ncol + hp)),
                  pl.BlockSpec((None, s, width), lambda bi, hp: (bi, 0, hp)),
                  pl.BlockSpec((None, N_HEADS, s), lambda bi, hp: (bi, 0, 0)),
                  pl.BlockSpec((SUBLANES, LANES), lambda bi, hp: (0, 0))],
        out_specs=pl.BlockSpec((None, s, width), lambda bi, hp: (bi, 0, hp)),
        out_shape=jax.ShapeDtypeStruct((b, s, D_MODEL), BF16),
        scratch_shapes=[pltpu.VMEM((heads, s, LANES), BF16), pltpu.VMEM((heads, s, LANES), BF16),
                        pltpu.VMEM((heads, HEAD_DIM + FOX_SUM_ROWS, s), BF16),
                        pltpu.VMEM((FOX_SCORE_SLOTS, s, FOX_TQ), F32),
                        pltpu.VMEM((FOX_PROB_SLOTS, s, FOX_TQ), BF16)],
        compiler_params=_params("arbitrary", "arbitrary"),
        name="fox_attn",
    )(qkv, qkv, qkv, kext, cumt, lw)


def _deepnorm(xres, y, g, b, alpha):
    z = alpha * xres + y
    mu = jnp.mean(z, axis=-1, keepdims=True)
    zc = z - mu
    var = jnp.mean(zc * zc, axis=-1, keepdims=True)
    return zc * lax.rsqrt(var + LN_EPS) * g + b


LN_SUBTILES = 2


def _oproj_ln_kernel(alpha, o_ref, xres_ref, w_ref, g_ref, b_ref, xo_ref, xob_ref, wb_ref):
    @pl.when(pl.program_id(0) == 0)
    def _():
        wb_ref[...] = w_ref[...].astype(BF16)

    ts = o_ref.shape[0] // LN_SUBTILES
    ys = [jnp.dot(o_ref[r * ts:(r + 1) * ts, :], wb_ref[...], preferred_element_type=F32)
          for r in range(LN_SUBTILES)]
    for r, y in enumerate(ys):
        rows = slice(r * ts, (r + 1) * ts)
        out = _deepnorm(xres_ref[rows, :], y, g_ref[...], b_ref[...], alpha)
        xo_ref[rows, :] = out
        xob_ref[rows, :] = out.astype(BF16)


def _oproj_ln(o, xres, w, layer, g, b, alpha, tm, name):
    m, k = o.shape
    n = w.shape[2]
    return pl.pallas_call(
        functools.partial(_oproj_ln_kernel, alpha),
        grid=(m // tm,),
        in_specs=[pl.BlockSpec((tm, k), lambda i: (i, 0)),
                  pl.BlockSpec((tm, n), lambda i: (i, 0)),
                  pl.BlockSpec((None, k, n), lambda i: (layer, 0, 0), pipeline_mode=pl.Buffered(1)),
                  pl.BlockSpec((1, n), lambda i: (0, 0)),
                  pl.BlockSpec((1, n), lambda i: (0, 0))],
        out_specs=[pl.BlockSpec((tm, n), lambda i: (i, 0)),
                   pl.BlockSpec((tm, n), lambda i: (i, 0))],
        out_shape=[jax.ShapeDtypeStruct((m, n), F32), jax.ShapeDtypeStruct((m, n), BF16)],
        scratch_shapes=[pltpu.VMEM((k, n), BF16)],
        compiler_params=_params("arbitrary"),
        name=name,
    )(o, xres, w, g, b)


CONV_WIDTH = 3
FFN_TF = 512
FFN_UP_TM = 1024


def _ffn_up_kernel(tiles_per_seq, x_ref, wg_ref, wu_ref, cwg_ref, cwu_ref, cbg_ref, cbu_ref,
                   a_ref, wb_ref, h_ref):
    i = pl.program_id(1)
    tm = x_ref.shape[0]
    tf = a_ref.shape[1]
    halo = SUBLANES

    @pl.when(i == 0)
    def _():
        wb_ref[:, 0:tf] = wg_ref[...].astype(BF16)
        wb_ref[:, tf:2 * tf] = wu_ref[...].astype(BF16)

    @pl.when(i % tiles_per_seq == 0)
    def _():
        h_ref[0:halo, :] = jnp.zeros((halo, h_ref.shape[1]), F32)

    h_ref[halo:halo + tm, :] = jnp.dot(x_ref[...], wb_ref[...], preferred_element_type=F32)

    def conv(cols, cw_ref, cb_ref):
        out = cb_ref[...]
        for tap in range(CONV_WIDTH):
            start = halo - (CONV_WIDTH - 1) + tap
            out = out + cw_ref[tap:tap + 1, :] * h_ref[start:start + tm, cols]
        return out

    g = conv(slice(0, tf), cwg_ref, cbg_ref)
    u = conv(slice(tf, 2 * tf), cwu_ref, cbu_ref)
    a_ref[...] = (g * jax.nn.sigmoid(g) * u).astype(a_ref.dtype)
    h_ref[0:halo, :] = h_ref[tm:tm + halo, :]


def _ffn_up(xb, w_up, conv_w, conv_b, layer, seq):
    m, k = xb.shape
    tm, tf = FFN_UP_TM, FFN_TF
    nf = D_FF // tf
    cb2 = conv_b.reshape(conv_b.shape[0], 1, 2 * D_FF)
    return pl.pallas_call(
        functools.partial(_ffn_up_kernel, seq // tm),
        grid=(nf, m // tm),
        in_specs=[pl.BlockSpec((tm, k), lambda j, i: (i, 0)),
                  pl.BlockSpec((None, k, tf), lambda j, i: (layer, 0, j)),
                  pl.BlockSpec((None, k, tf), lambda j, i: (layer, 0, j + nf)),
                  pl.BlockSpec((None, CONV_WIDTH, tf), lambda j, i: (layer, 0, j)),
                  pl.BlockSpec((None, CONV_WIDTH, tf), lambda j, i: (layer, 0, j + nf)),
                  pl.BlockSpec((None, 1, tf), lambda j, i: (layer, 0, j)),
                  pl.BlockSpec((None, 1, tf), lambda j, i: (layer, 0, j + nf))],
        out_specs=pl.BlockSpec((tm, tf), lambda j, i: (i, j)),
        out_shape=jax.ShapeDtypeStruct((m, D_FF), BF16),
        scratch_shapes=[pltpu.VMEM((k, 2 * tf), BF16), pltpu.VMEM((tm + 2 * SUBLANES, 2 * tf), F32)],
        compiler_params=_params("arbitrary", "arbitrary"),
        name="ffn_up",
    )(xb, w_up, w_up, conv_w, conv_w, cb2, cb2)


FFN_DOWN_TM = 256
FFN_DOWN_WCHUNK = 512


def _ffn_down_ln_kernel(alpha, n_load, emit_bf16, a_ref, xres_ref, w_ref, g_ref, b_ref, *rest):
    if emit_bf16:
        xo_ref, xob_ref, wb_ref = rest
    else:
        (xo_ref, wb_ref), xob_ref = rest, None
    step = pl.program_id(0)
    rows = w_ref.shape[0]

    @pl.when(step < n_load)
    def _():
        off = pl.multiple_of(step * rows, rows)
        wb_ref[pl.ds(off, rows), :] = w_ref[...].astype(BF16)

    @pl.when(step >= n_load)
    def _():
        y = jnp.dot(a_ref[...], wb_ref[...], preferred_element_type=F32)
        out = _deepnorm(xres_ref[...], y, g_ref[...], b_ref[...], alpha)
        xo_ref[...] = out
        if emit_bf16:
            xob_ref[...] = out.astype(BF16)


def _ffn_down_ln(a, xres, w, layer, g, b, alpha, emit_bf16, name):
    m, k = a.shape
    n = w.shape[2]
    tm, rows = FFN_DOWN_TM, FFN_DOWN_WCHUNK
    n_load = k // rows
    tile = lambda s: (jnp.maximum(s - n_load, 0), 0)
    out_specs = [pl.BlockSpec((tm, n), tile)]
    out_shape = [jax.ShapeDtypeStruct((m, n), F32)]
    if emit_bf16:
        out_specs.append(pl.BlockSpec((tm, n), tile))
        out_shape.append(jax.ShapeDtypeStruct((m, n), BF16))
    return pl.pallas_call(
        functools.partial(_ffn_down_ln_kernel, alpha, n_load, emit_bf16),
        grid=(n_load + m // tm,),
        in_specs=[pl.BlockSpec((tm, k), tile),
                  pl.BlockSpec((tm, n), tile),
                  pl.BlockSpec((None, rows, n), lambda s: (layer, jnp.minimum(s, n_load - 1), 0)),
                  pl.BlockSpec((1, n), lambda s: (0, 0)),
                  pl.BlockSpec((1, n), lambda s: (0, 0))],
        out_specs=out_specs,
        out_shape=out_shape,
        scratch_shapes=[pltpu.VMEM((k, n), BF16)],
        compiler_params=_params("arbitrary"),
        name=name,
    )(a, xres, w, g, b)


def kernel(x, a_w_qkv, a_w_o, a_sinks, rel_bias, b_w_qkvf, b_f_bias, b_w_o,
           ffn_w_up, ffn_conv_w, ffn_conv_b, ffn_w_down, ln_g, ln_b):
    batch, seq, d = x.shape
    depth = ffn_w_up.shape[0]
    alpha = (2 * depth) ** 0.25
    m = batch * seq
    xf = x.reshape(m, d)
    xb = None
    biasmask = _swa_bias(rel_bias)
    for layer in range(depth):
        j = layer // 2
        g = ln_g[layer].reshape(2, 1, d)
        bt = ln_b[layer].reshape(2, 1, d)
        if layer % 2 == 0:
            n_qkv = D_MODEL + 2 * A_KV_DIM
            src = xf if xb is None else xb
            qkv = _ws_matmul(src, a_w_qkv, j, n_qkv, n_qkv // 2, 1024, "swa_qkv")
            o = _swa_attention(qkv.reshape(batch, seq, n_qkv), a_sinks[j], biasmask)
            w_o = a_w_o
        else:
            w_t = jnp.swapaxes(b_w_qkvf, 1, 2)
            qkv = _ws_matmul(xb, w_t, j, 3 * D_MODEL, 1024, 2048, "fox_qkv", w_is_transposed=True)
            fb3 = jnp.tile(b_f_bias[j].reshape(N_HEADS, 1), (FOX_PIECES, 1))
            cumt, kext = _fox_gate(xb, w_t, j, fb3, batch, seq)
            o = _fox_attention(qkv.reshape(batch, seq, 3 * D_MODEL), cumt, kext)
            w_o = b_w_o
        xf, xb = _oproj_ln(o.reshape(m, d), xf, w_o, j, g[0], bt[0], alpha, 512, "oproj_ln")
        a = _ffn_up(xb, ffn_w_up, ffn_conv_w, ffn_conv_b, layer, seq)
        last = layer == depth - 1
        res = _ffn_down_ln(a, xf, ffn_w_down, layer, g[1], bt[1], alpha, not last, "ffn_down_ln")
        xf, xb = (res[0], None) if last else res
    return xf.reshape(batch, seq, d)
```

```python
import functools
import math

import jax
import jax.numpy as jnp
import numpy as np
from jax import lax
from jax.experimental import pallas as pl
from jax.experimental.pallas import tpu as pltpu

D_MODEL = 2048
HEAD_DIM = 64
BLOCK = 128
N_HEADS = D_MODEL // HEAD_DIM
A_KV_HEADS = 4
A_GROUP = N_HEADS // A_KV_HEADS
A_KV_DIM = A_KV_HEADS * HEAD_DIM
REL_BUCKETS = 32
REL_MAX_DIST = 128
D_FF = 5632
LN_EPS = 1e-5
SCALE = HEAD_DIM ** -0.5

SUBLANES = 8
LANES = 128

BF16 = jnp.bfloat16
F32 = jnp.float32
NEG_INF = float("-inf")

_NT_DIMS = (((1,), (1,)), ((), ()))


def _params(*sem):
    return pltpu.CompilerParams(dimension_semantics=sem)


LOG2E = math.log2(math.e)
Q_SCALE = SCALE * LOG2E


def _ws_matmul_kernel(w_is_transposed, q_cols, x_ref, w_ref, o_ref, wb_ref):
    @pl.when(pl.program_id(1) == 0)
    def _():
        wb_ref[...] = w_ref[...].astype(BF16)

    x = x_ref[...].astype(BF16)
    if w_is_transposed:
        y = lax.dot_general(x, wb_ref[...], _NT_DIMS, preferred_element_type=F32)
    else:
        y = jnp.dot(x, wb_ref[...], preferred_element_type=F32)
    tn = y.shape[1]
    col = pl.program_id(0) * tn + lax.broadcasted_iota(jnp.int32, (1, tn), 1)
    y = (y * jnp.where(col < q_cols, Q_SCALE, 1.0)).astype(o_ref.dtype)
    if len(o_ref.shape) == 2:
        o_ref[...] = y
    else:
        for c in range(o_ref.shape[0]):
            o_ref[c] = y[:, c * LANES:(c + 1) * LANES]


def _ws_matmul(x, w, layer, n_out, tn, tm, name, w_is_transposed=False, slab_batch=None):
    m, k = x.shape
    if w_is_transposed:
        w_spec = pl.BlockSpec((None, tn, k), lambda j, i: (layer, j, 0))
        wb_shape = (tn, k)
    else:
        w_spec = pl.BlockSpec((None, k, tn), lambda j, i: (layer, 0, j))
        wb_shape = (k, tn)
    if slab_batch is None:
        out_spec = pl.BlockSpec((tm, tn), lambda j, i: (i, j))
        out_shape = jax.ShapeDtypeStruct((m, n_out), BF16)
    else:
        batch, seq = slab_batch
        per_seq = seq // tm
        out_spec = pl.BlockSpec((None, tn // LANES, tm, LANES),
                                lambda j, i: (i // per_seq, j, i % per_seq, 0))
        out_shape = jax.ShapeDtypeStruct((batch, n_out // LANES, seq, LANES), BF16)
    return pl.pallas_call(
        functools.partial(_ws_matmul_kernel, w_is_transposed, D_MODEL),
        grid=(n_out // tn, m // tm),
        in_specs=[pl.BlockSpec((tm, k), lambda j, i: (i, 0)), w_spec],
        out_specs=out_spec,
        out_shape=out_shape,
        scratch_shapes=[pltpu.VMEM(wb_shape, BF16)],
        compiler_params=_params("arbitrary", "arbitrary"),
        name=name,
    )(x, w)


def _t5_bucket_map():
    qi = np.arange(BLOCK)[None, :]
    kj = np.arange(2 * BLOCK)[:, None]
    dist = BLOCK + qi - kj
    n = np.maximum(dist, 0)
    max_exact = REL_BUCKETS // 2
    nf = np.maximum(n, 1).astype(np.float32)
    large = max_exact + (np.log(nf / max_exact) / math.log(REL_MAX_DIST / max_exact)
                         * (REL_BUCKETS - max_exact)).astype(np.int32)
    large = np.minimum(large, REL_BUCKETS - 1)
    bucket = np.where(n < max_exact, n, large)
    visible = (dist >= 0) & (dist < BLOCK)
    return np.where(visible, bucket, -1).astype(np.int32)


def _swa_bias_kernel(rb_ref, bk_ref, o_ref):
    bk = bk_ref[...]
    key = lax.broadcasted_iota(jnp.int32, bk.shape, 0)

    def head(h, _):
        acc = jnp.full(bk.shape, NEG_INF, F32)
        for b in range(REL_BUCKETS):
            acc = jnp.where(bk == b, rb_ref[b, h] * LOG2E, acc)
        o_ref[0, h] = acc
        o_ref[1, h] = jnp.where(key < BLOCK, NEG_INF, acc)
        return 0

    lax.fori_loop(0, N_HEADS, head, 0)


def _swa_bias(rel_bias):
    bk = jnp.asarray(_t5_bucket_map())
    return pl.pallas_call(
        _swa_bias_kernel,
        in_specs=[pl.BlockSpec(memory_space=pltpu.SMEM),
                  pl.BlockSpec(memory_space=pltpu.VMEM)],
        out_specs=pl.BlockSpec(memory_space=pltpu.VMEM),
        out_shape=jax.ShapeDtypeStruct((2, N_HEADS, 2 * BLOCK, BLOCK), F32),
        name="swa_bias",
    )(rel_bias, bk)


_TN_DIMS = (((0,), (0,)), ((), ()))
SWA_SCORES_AHEAD = 2


SWA_BLOCKS_PER_STEP = 2


def _swa_kernel(sink_ref, q_ref, kp_ref, kc_ref, vp_ref, vc_ref, bm_ref, o_ref,
                qs_ref, s_ref, p_ref, linv_ref):
    units = [(qb, kh) for qb in range(SWA_BLOCKS_PER_STEP) for kh in range(A_KV_HEADS)]
    first_table = jnp.where(pl.program_id(1) == 0, 1, 0)

    def head_at(head):
        slab, half = divmod(head, LANES // HEAD_DIM)
        return slab, slice(half * HEAD_DIM, (half + 1) * HEAD_DIM)

    def keys_values(prev_ref, cur_ref, qb, kh):
        slab, lanes = head_at(kh)
        if qb == 0:
            return jnp.concatenate([prev_ref[slab, :, lanes], cur_ref[slab, 0:BLOCK, lanes]], axis=0)
        return cur_ref[slab, (qb - 1) * BLOCK:(qb + 1) * BLOCK, lanes]

    def scores(u):
        qb, kh = units[u]
        for g in range(A_GROUP):
            slab, lanes = head_at(kh * A_GROUP + g)
            qs_ref[u, g * BLOCK:(g + 1) * BLOCK, :] = q_ref[slab, qb * BLOCK:(qb + 1) * BLOCK, lanes]
        k2 = keys_values(kp_ref, kc_ref, qb, kh)
        s = lax.dot_general(k2, qs_ref[u], _NT_DIMS, preferred_element_type=F32)
        for g in range(A_GROUP):
            s_ref[u, g] = s[:, g * BLOCK:(g + 1) * BLOCK]

    def softmax(u):
        qb, kh = units[u]
        table = first_table if qb == 0 else 0
        for g in range(A_GROUP):
            h = kh * A_GROUP + g
            cols = slice(g * BLOCK, (g + 1) * BLOCK)
            s = s_ref[u, g] + bm_ref[table, h]
            sink = sink_ref[h] * LOG2E
            m = jnp.maximum(jnp.max(s, axis=0, keepdims=True), sink)
            p = jnp.exp2(s - m)
            denom = jnp.sum(p, axis=0, keepdims=True) + jnp.exp2(sink - m)
            p_ref[u, g // 2, :, (g % 2) * BLOCK:(g % 2 + 1) * BLOCK] = p.astype(BF16)
            linv_ref[u, :, cols] = 1.0 / denom

    def weighted_values(u):
        qb, kh = units[u]
        v2 = keys_values(vp_ref, vc_ref, qb, kh)
        for c in range(A_GROUP // 2):
            cols = slice(2 * c * BLOCK, 2 * (c + 1) * BLOCK)
            ot = (lax.dot_general(v2, p_ref[u, c], _TN_DIMS, preferred_element_type=F32)
                  * linv_ref[u, :, cols])
            for gg in range(2):
                slab, lanes = head_at(kh * A_GROUP + 2 * c + gg)
                o_ref[slab, qb * BLOCK:(qb + 1) * BLOCK, lanes] = (
                    ot[:, gg * BLOCK:(gg + 1) * BLOCK].T.astype(o_ref.dtype))

    for u in range(min(SWA_SCORES_AHEAD, len(units))):
        scores(u)
    for u in range(len(units)):
        if u + SWA_SCORES_AHEAD < len(units):
            scores(u + SWA_SCORES_AHEAD)
        softmax(u)
        weighted_values(u)


def _swa_attention(qkv, sinks, biasmask):
    b, _, s, _ = qkv.shape
    rows = SWA_BLOCKS_PER_STEP * BLOCK
    gq = A_GROUP * BLOCK
    n_units = SWA_BLOCKS_PER_STEP * A_KV_HEADS
    q_slabs = D_MODEL // LANES
    kv_slabs = A_KV_DIM // LANES
    kcol = q_slabs // kv_slabs
    prev = lambda n: jnp.maximum(SWA_BLOCKS_PER_STEP * n - 1, 0)
    return pl.pallas_call(
        _swa_kernel,
        grid=(b, s // rows),
        in_specs=[pl.BlockSpec(memory_space=pltpu.SMEM),
                  pl.BlockSpec((None, q_slabs, rows, LANES), lambda bi, n: (bi, 0, n, 0)),
                  pl.BlockSpec((None, kv_slabs, BLOCK, LANES), lambda bi, n: (bi, kcol, prev(n), 0)),
                  pl.BlockSpec((None, kv_slabs, rows, LANES), lambda bi, n: (bi, kcol, n, 0)),
                  pl.BlockSpec((None, kv_slabs, BLOCK, LANES), lambda bi, n: (bi, kcol + 1, prev(n), 0)),
                  pl.BlockSpec((None, kv_slabs, rows, LANES), lambda bi, n: (bi, kcol + 1, n, 0)),
                  pl.BlockSpec((2, N_HEADS, 2 * BLOCK, BLOCK), lambda bi, n: (0, 0, 0, 0),
                               pipeline_mode=pl.Buffered(1))],
        out_specs=pl.BlockSpec((None, q_slabs, rows, LANES), lambda bi, n: (bi, 0, n, 0)),
        out_shape=jax.ShapeDtypeStruct((b, q_slabs, s, LANES), BF16),
        scratch_shapes=[pltpu.VMEM((n_units, gq, HEAD_DIM), BF16),
                        pltpu.VMEM((n_units, A_GROUP, 2 * BLOCK, BLOCK), F32),
                        pltpu.VMEM((n_units, A_GROUP // 2, 2 * BLOCK, 2 * BLOCK), BF16),
                        pltpu.VMEM((n_units, 1, gq), F32)],
        compiler_params=_params("arbitrary", "arbitrary"),
        name="swa_attn",
    )(sinks, qkv, qkv, qkv, qkv, qkv, biasmask)


FOX_CUM_BLOCK = 256
FOX_TQ = 256
FOX_HEADS_PER_PAIR = LANES // HEAD_DIM
FOX_PAIRS_PER_STEP = 2
FOX_PIECES = 3
FOX_SUM_ROWS = 16
FOX_SCORE_SLOTS = 4
FOX_PV_LAG = 1
FOX_PROB_SLOTS = FOX_PV_LAG + 1
BF16_ONE_PAIR = 0x3F803F80


def _fox_ext_lane(head, piece):
    pair, odd = divmod(head, FOX_HEADS_PER_PAIR)
    return pair * LANES + (1 - odd) * HEAD_DIM + piece


def _fox_placement():
    pm = np.zeros((FOX_PIECES * N_HEADS, D_MODEL), np.float32)
    for t in range(FOX_PIECES):
        for h in range(N_HEADS):
            pm[t * N_HEADS + h, _fox_ext_lane(h, t)] = -1.0
    return pm


def _fox_lane_words():
    w = np.zeros((SUBLANES, LANES), np.uint32)
    for hh in range(FOX_HEADS_PER_PAIR):
        own = np.zeros(LANES, bool)
        own[hh * HEAD_DIM:(hh + 1) * HEAD_DIM] = True
        ext = np.zeros(LANES, bool)
        first = _fox_ext_lane(hh, 0)
        ext[first:first + FOX_PIECES] = True
        w[3 * hh + 0] = np.where(own, 0xFFFFFFFF, 0)
        w[3 * hh + 1] = np.where(ext, 0xFFFFFFFF, 0)
        w[3 * hh + 2] = np.where(ext, BF16_ONE_PAIR, 0)
    return w


def _log_sigmoid(x):
    return jnp.minimum(x, 0.0) - jnp.log1p(jnp.exp(-jnp.abs(x)))


def _round_bf16(x):
    return x.astype(BF16).astype(F32)


def _fox_gate_kernel(x_ref, wf_ref, fb_ref, pm_ref, cumt_ref, kext_ref, w3_ref):
    @pl.when(pl.program_id(0) == 0)
    def _():
        w = wf_ref[...].astype(BF16)
        for t in range(FOX_PIECES):
            w3_ref[t * N_HEADS:(t + 1) * N_HEADS, :] = w

    fz = lax.dot_general(w3_ref[...], x_ref[...], _NT_DIMS, preferred_element_type=F32) + fb_ref[...]
    lf = _log_sigmoid(fz)
    cb = FOX_CUM_BLOCK
    row = lax.broadcasted_iota(jnp.int32, (cb, cb), 0)
    col = lax.broadcasted_iota(jnp.int32, (cb, cb), 1)
    tri = jnp.where(row <= col, 1.0, 0.0).astype(BF16)
    tri3 = jnp.concatenate([tri] * FOX_PIECES, axis=0)
    group = lax.broadcasted_iota(jnp.int32, (lf.shape[0], cb), 0) // N_HEADS
    pm = pm_ref[...]
    n_blocks = lf.shape[1] // cb
    local = []
    for blk in range(n_blocks):
        x = lf[:, blk * cb:(blk + 1) * cb]
        hi = _round_bf16(x)
        r1 = x - hi
        mid = _round_bf16(r1)
        pieces = jnp.concatenate([hi, mid, r1 - mid], axis=1).astype(BF16)
        local.append(jnp.dot(pieces, tri3, preferred_element_type=F32))
    carry = jnp.zeros((lf.shape[0], 1), F32)
    for blk in range(n_blocks):
        cols = slice(blk * cb, (blk + 1) * cb)
        c = local[blk] + carry
        carry = carry + local[blk][:, cb - 1:cb]
        c2 = c * LOG2E
        cumt_ref[:, cols] = c2[:N_HEADS]
        chi = _round_bf16(c2)
        cr1 = c2 - chi
        cmid = _round_bf16(cr1)
        piece = jnp.where(group == 0, chi, jnp.where(group == 1, cmid, cr1 - cmid)).astype(BF16)
        kext_ref[cols, :] = lax.dot_general(piece, pm, _TN_DIMS, preferred_element_type=F32).astype(BF16)


def _fox_gate(xb, w_qkvf_t, layer, fb3, batch, seq):
    pm = jnp.asarray(_fox_placement(), BF16)
    n3 = FOX_PIECES * N_HEADS
    gate_row = 3 * D_MODEL // N_HEADS
    return pl.pallas_call(
        _fox_gate_kernel,
        grid=(batch,),
        in_specs=[pl.BlockSpec((seq, D_MODEL), lambda bi: (bi, 0)),
                  pl.BlockSpec((None, N_HEADS, D_MODEL), lambda bi: (layer, gate_row, 0)),
                  pl.BlockSpec((n3, 1), lambda bi: (0, 0)),
                  pl.BlockSpec((n3, D_MODEL), lambda bi: (0, 0))],
        out_specs=[pl.BlockSpec((None, N_HEADS, seq), lambda bi: (bi, 0, 0)),
                   pl.BlockSpec((None, seq, D_MODEL), lambda bi: (bi, 0, 0))],
        out_shape=[jax.ShapeDtypeStruct((batch, N_HEADS, seq), F32),
                   jax.ShapeDtypeStruct((batch, seq, D_MODEL), BF16)],
        scratch_shapes=[pltpu.VMEM((n3, D_MODEL), BF16)],
        compiler_params=_params("arbitrary"),
        name="fox_gate",
    )(xb, w_qkvf_t, fb3, pm)


def _fox_kernel(q_ref, k_ref, v_ref, kext_ref, cbt_ref, lw_ref, o_ref, qa_ref, ka_ref, vt_ref,
                s_ref, p_ref):
    seq = q_ref.shape[0]
    tq = FOX_TQ
    key = lax.broadcasted_iota(jnp.int32, (tq, tq), 0)
    qry = lax.broadcasted_iota(jnp.int32, (tq, tq), 1)
    causal = key <= qry
    words = lambda x: pltpu.bitcast(x, jnp.uint32)
    for pair in range(FOX_PAIRS_PER_STEP):
        lanes = slice(pair * LANES, (pair + 1) * LANES)
        q = words(q_ref[:, lanes])
        k = words(k_ref[:, lanes])
        kext = words(kext_ref[:, lanes])
        vt = v_ref[:, lanes].T
        for hh in range(FOX_HEADS_PER_PAIR):
            h = pair * FOX_HEADS_PER_PAIR + hh
            own = lw_ref[3 * hh + 0:3 * hh + 1, :]
            ext = lw_ref[3 * hh + 1:3 * hh + 2, :]
            ones = lw_ref[3 * hh + 2:3 * hh + 3, :]
            qa_ref[h] = pltpu.bitcast((q & own) | ones, BF16)
            ka_ref[h] = pltpu.bitcast((k & ~ext) | (kext & ext), BF16)
            vt_ref[h, 0:HEAD_DIM, :] = vt[hh * HEAD_DIM:(hh + 1) * HEAD_DIM]
            vt_ref[h, HEAD_DIM:, :] = jnp.ones((FOX_SUM_ROWS, seq), BF16)

    heads = FOX_PAIRS_PER_STEP * FOX_HEADS_PER_PAIR
    units = [(i, h) for i in range(seq // tq) for h in range(heads)]

    def scores(u):
        i, h = units[u]
        slot = u % FOX_SCORE_SLOTS
        r0, r1 = i * tq, (i + 1) * tq
        qa = qa_ref[h, r0:r1, :]
        s_d = lax.dot_general(ka_ref[h, r0:r1, :], qa, _NT_DIMS, preferred_element_type=F32)
        s_ref[slot, r0:r1, :] = jnp.where(causal, s_d, NEG_INF)
        if i > 0:
            s_ref[slot, 0:r0, :] = lax.dot_general(ka_ref[h, 0:r0, :], qa, _NT_DIMS,
                                                   preferred_element_type=F32)

    def softmax(u):
        i, h = units[u]
        slot = u % FOX_SCORE_SLOTS
        r0, r1 = i * tq, (i + 1) * tq
        head = pl.program_id(1) * heads + h
        cbq = cbt_ref[pl.ds(head, 1), r0:r1]
        m = jnp.max(s_ref[slot, 0:r1, :], axis=0, keepdims=True)
        shift = cbq - (m + cbq)
        p_ref[u % FOX_PROB_SLOTS, 0:r1, :] = jnp.exp2(s_ref[slot, 0:r1, :] + shift).astype(BF16)

    outs = []

    def weighted_values(u):
        i, h = units[u]
        r1 = (i + 1) * tq
        ot = jnp.dot(vt_ref[h, :, 0:r1], p_ref[u % FOX_PROB_SLOTS, 0:r1, :], preferred_element_type=F32)
        outs.append(ot[0:HEAD_DIM] * (1.0 / ot[HEAD_DIM:HEAD_DIM + 1]))
        if h == heads - 1:
            o_ref[i * tq:r1, :] = jnp.concatenate(outs, axis=0).T.astype(o_ref.dtype)
            outs.clear()

    ahead = FOX_SCORE_SLOTS - 1
    for u in range(min(ahead, len(units))):
        scores(u)
    for u in range(len(units)):
        if u + ahead < len(units):
            scores(u + ahead)
        softmax(u)
        if u >= FOX_PV_LAG:
            weighted_values(u - FOX_PV_LAG)
    for u in range(len(units) - FOX_PV_LAG, len(units)):
        weighted_values(u)


def _fox_attention(qkv, cumt, kext):
    b, s, _ = qkv.shape
    heads = FOX_PAIRS_PER_STEP * FOX_HEADS_PER_PAIR
    width = FOX_PAIRS_PER_STEP * LANES
    lw = jnp.asarray(_fox_lane_words())
    ncol = D_MODEL // width
    return pl.pallas_call(
        _fox_kernel,
        grid=(b, N_HEADS // heads),
        in_specs=[pl.BlockSpec((None, s, width), lambda bi, hp: (bi, 0, hp)),
                  pl.BlockSpec((None, s, width), lambda bi, hp: (bi, 0, ncol + hp)),
                  pl.BlockSpec((None, s, width), lambda bi, hp: (bi, 0, 2 * ncol + hp)),
                  pl.BlockSpec((None, s, width), lambda bi, hp: (bi, 0, hp)),
                  pl.BlockSpec((None, N_HEADS, s), lambda bi, hp: (bi, 0, 0)),
                  pl.BlockSpec((SUBLANES, LANES), lambda bi, hp: (0, 0))],
        out_specs=pl.BlockSpec((None, s, width), lambda bi, hp: (bi, 0, hp)),
        out_shape=jax.ShapeDtypeStruct((b, s, D_MODEL), BF16),
        scratch_shapes=[pltpu.VMEM((heads, s, LANES), BF16), pltpu.VMEM((heads, s, LANES), BF16),
                        pltpu.VMEM((heads, HEAD_DIM + FOX_SUM_ROWS, s), BF16),
                        pltpu.VMEM((FOX_SCORE_SLOTS, s, FOX_TQ), F32),
                        pltpu.VMEM((FOX_PROB_SLOTS, s, FOX_TQ), BF16)],
        compiler_params=_params("arbitrary", "arbitrary"),
        name="fox_attn",
    )(qkv, qkv, qkv, kext, cumt, lw)


def _deepnorm(xres, y, g, b, alpha):
    z = alpha * xres + y
    mu = jnp.mean(z, axis=-1, keepdims=True)
    zc = z - mu
    var = jnp.mean(zc * zc, axis=-1, keepdims=True)
    return zc * lax.rsqrt(var + LN_EPS) * g + b


LN_SUBTILES = 2


def _oproj_ln_kernel(alpha, o_ref, xres_ref, w_ref, g_ref, b_ref, xo_ref, xob_ref, wb_ref):
    @pl.when(pl.program_id(0) == 0)
    def _():
        wb_ref[...] = w_ref[...].astype(BF16)

    def lhs(rows):
        if len(o_ref.shape) == 2:
            return o_ref[rows, :]
        return jnp.concatenate([o_ref[c, rows, :] for c in range(o_ref.shape[0])], axis=1)

    ts = xres_ref.shape[0] // LN_SUBTILES
    ys = [jnp.dot(lhs(slice(r * ts, (r + 1) * ts)), wb_ref[...], preferred_element_type=F32)
          for r in range(LN_SUBTILES)]
    for r, y in enumerate(ys):
        rows = slice(r * ts, (r + 1) * ts)
        out = _deepnorm(xres_ref[rows, :], y, g_ref[...], b_ref[...], alpha)
        xo_ref[rows, :] = out
        xob_ref[rows, :] = out.astype(BF16)


def _oproj_ln(o, xres, w, layer, g, b, alpha, tm, name):
    m, n = xres.shape
    k = w.shape[1]
    if o.ndim == 2:
        o_spec = pl.BlockSpec((tm, k), lambda i: (i, 0))
    else:
        per_seq = o.shape[2] // tm
        o_spec = pl.BlockSpec((None, o.shape[1], tm, LANES), lambda i: (i // per_seq, 0, i % per_seq, 0))
    return pl.pallas_call(
        functools.partial(_oproj_ln_kernel, alpha),
        grid=(m // tm,),
        in_specs=[o_spec,
                  pl.BlockSpec((tm, n), lambda i: (i, 0)),
                  pl.BlockSpec((None, k, n), lambda i: (layer, 0, 0), pipeline_mode=pl.Buffered(1)),
                  pl.BlockSpec((1, n), lambda i: (0, 0)),
                  pl.BlockSpec((1, n), lambda i: (0, 0))],
        out_specs=[pl.BlockSpec((tm, n), lambda i: (i, 0)),
                   pl.BlockSpec((tm, n), lambda i: (i, 0))],
        out_shape=[jax.ShapeDtypeStruct((m, n), F32), jax.ShapeDtypeStruct((m, n), BF16)],
        scratch_shapes=[pltpu.VMEM((k, n), BF16)],
        compiler_params=_params("arbitrary"),
        name=name,
    )(o, xres, w, g, b)


CONV_WIDTH = 3
FFN_TF = 512
FFN_UP_TM = 1024


def _ffn_up_kernel(tiles_per_seq, x_ref, wg_ref, wu_ref, cwg_ref, cwu_ref, cbg_ref, cbu_ref,
                   a_ref, wb_ref, h_ref):
    i = pl.program_id(1)
    tm = x_ref.shape[0]
    tf = a_ref.shape[1]
    halo = SUBLANES

    @pl.when(i == 0)
    def _():
        wb_ref[:, 0:tf] = wg_ref[...].astype(BF16)
        wb_ref[:, tf:2 * tf] = wu_ref[...].astype(BF16)

    @pl.when(i % tiles_per_seq == 0)
    def _():
        h_ref[0:halo, :] = jnp.zeros((halo, h_ref.shape[1]), F32)

    h_ref[halo:halo + tm, :] = jnp.dot(x_ref[...], wb_ref[...], preferred_element_type=F32)

    def conv(cols, cw_ref, cb_ref):
        out = cb_ref[...]
        for tap in range(CONV_WIDTH):
            start = halo - (CONV_WIDTH - 1) + tap
            out = out + cw_ref[tap:tap + 1, :] * h_ref[start:start + tm, cols]
        return out

    g = conv(slice(0, tf), cwg_ref, cbg_ref)
    u = conv(slice(tf, 2 * tf), cwu_ref, cbu_ref)
    a_ref[...] = (g * jax.nn.sigmoid(g) * u).astype(a_ref.dtype)
    h_ref[0:halo, :] = h_ref[tm:tm + halo, :]


def _ffn_up(xb, w_up, conv_w, conv_b, layer, seq):
    m, k = xb.shape
    tm, tf = FFN_UP_TM, FFN_TF
    nf = D_FF // tf
    cb2 = conv_b.reshape(conv_b.shape[0], 1, 2 * D_FF)
    return pl.pallas_call(
        functools.partial(_ffn_up_kernel, seq // tm),
        grid=(nf, m // tm),
        in_specs=[pl.BlockSpec((tm, k), lambda j, i: (i, 0)),
                  pl.BlockSpec((None, k, tf), lambda j, i: (layer, 0, j)),
                  pl.BlockSpec((None, k, tf), lambda j, i: (layer, 0, j + nf)),
                  pl.BlockSpec((None, CONV_WIDTH, tf), lambda j, i: (layer, 0, j)),
                  pl.BlockSpec((None, CONV_WIDTH, tf), lambda j, i: (layer, 0, j + nf)),
                  pl.BlockSpec((None, 1, tf), lambda j, i: (layer, 0, j)),
                  pl.BlockSpec((None, 1, tf), lambda j, i: (layer, 0, j + nf))],
        out_specs=pl.BlockSpec((tm, tf), lambda j, i: (i, j)),
        out_shape=jax.ShapeDtypeStruct((m, D_FF), BF16),
        scratch_shapes=[pltpu.VMEM((k, 2 * tf), BF16), pltpu.VMEM((tm + 2 * SUBLANES, 2 * tf), F32)],
        compiler_params=_params("arbitrary", "arbitrary"),
        name="ffn_up",
    )(xb, w_up, w_up, conv_w, conv_w, cb2, cb2)


FFN_DOWN_TM = 256
FFN_DOWN_WCHUNK = 512


def _ffn_down_ln_kernel(alpha, n_load, emit_bf16, a_ref, xres_ref, w_ref, g_ref, b_ref, *rest):
    if emit_bf16:
        xo_ref, xob_ref, wb_ref = rest
    else:
        (xo_ref, wb_ref), xob_ref = rest, None
    step = pl.program_id(0)
    rows = w_ref.shape[0]

    @pl.when(step < n_load)
    def _():
        off = pl.multiple_of(step * rows, rows)
        wb_ref[pl.ds(off, rows), :] = w_ref[...].astype(BF16)

    @pl.when(step >= n_load)
    def _():
        y = jnp.dot(a_ref[...], wb_ref[...], preferred_element_type=F32)
        out = _deepnorm(xres_ref[...], y, g_ref[...], b_ref[...], alpha)
        xo_ref[...] = out
        if emit_bf16:
            xob_ref[...] = out.astype(BF16)


def _ffn_down_ln(a, xres, w, layer, g, b, alpha, emit_bf16, name):
    m, k = a.shape
    n = w.shape[2]
    tm, rows = FFN_DOWN_TM, FFN_DOWN_WCHUNK
    n_load = k // rows
    tile = lambda s: (jnp.maximum(s - n_load, 0), 0)
    out_specs = [pl.BlockSpec((tm, n), tile)]
    out_shape = [jax.ShapeDtypeStruct((m, n), F32)]
    if emit_bf16:
        out_specs.append(pl.BlockSpec((tm, n), tile))
        out_shape.append(jax.ShapeDtypeStruct((m, n), BF16))
    return pl.pallas_call(
        functools.partial(_ffn_down_ln_kernel, alpha, n_load, emit_bf16),
        grid=(n_load + m // tm,),
        in_specs=[pl.BlockSpec((tm, k), tile),
                  pl.BlockSpec((tm, n), tile),
                  pl.BlockSpec((None, rows, n), lambda s: (layer, jnp.minimum(s, n_load - 1), 0)),
                  pl.BlockSpec((1, n), lambda s: (0, 0)),
                  pl.BlockSpec((1, n), lambda s: (0, 0))],
        out_specs=out_specs,
        out_shape=out_shape,
        scratch_shapes=[pltpu.VMEM((k, n), BF16)],
        compiler_params=_params("arbitrary"),
        name=name,
    )(a, xres, w, g, b)


def kernel(x, a_w_qkv, a_w_o, a_sinks, rel_bias, b_w_qkvf, b_f_bias, b_w_o,
           ffn_w_up, ffn_conv_w, ffn_conv_b, ffn_w_down, ln_g, ln_b):
    batch, seq, d = x.shape
    depth = ffn_w_up.shape[0]
    alpha = (2 * depth) ** 0.25
    m = batch * seq
    xf = x.reshape(m, d)
    xb = None
    biasmask = _swa_bias(rel_bias)
    for layer in range(depth):
        j = layer // 2
        g = ln_g[layer].reshape(2, 1, d)
        bt = ln_b[layer].reshape(2, 1, d)
        if layer % 2 == 0:
            n_qkv = D_MODEL + 2 * A_KV_DIM
            src = xf if xb is None else xb
            qkv = _ws_matmul(src, a_w_qkv, j, n_qkv, n_qkv // 2, 1024, "swa_qkv", slab_batch=(batch, seq))
            o = _swa_attention(qkv, a_sinks[j], biasmask)
            w_o = a_w_o
        else:
            w_t = jnp.swapaxes(b_w_qkvf, 1, 2)
            qkv = _ws_matmul(xb, w_t, j, 3 * D_MODEL, 1024, 2048, "fox_qkv", w_is_transposed=True)
            fb3 = jnp.tile(b_f_bias[j].reshape(N_HEADS, 1), (FOX_PIECES, 1))
            cumt, kext = _fox_gate(xb, w_t, j, fb3, batch, seq)
            o = _fox_attention(qkv.reshape(batch, seq, 3 * D_MODEL), cumt, kext).reshape(m, d)
            w_o = b_w_o
        xf, xb = _oproj_ln(o, xf, w_o, j, g[0], bt[0], alpha, 512, "oproj_ln")
        a = _ffn_up(xb, ffn_w_up, ffn_conv_w, ffn_conv_b, layer, seq)
        last = layer == depth - 1
        res = _ffn_down_ln(a, xf, ffn_w_down, layer, g[1], bt[1], alpha, not last, "ffn_down_ln")
        xf, xb = (res[0], None) if last else res
    return xf.reshape(batch, seq, d)
```

```python
import functools
import math

import jax
import jax.numpy as jnp
import numpy as np
from jax import lax
from jax.experimental import pallas as pl
from jax.experimental.pallas import tpu as pltpu

D_MODEL = 2048
HEAD_DIM = 64
BLOCK = 128
N_HEADS = D_MODEL // HEAD_DIM
A_KV_HEADS = 4
A_GROUP = N_HEADS // A_KV_HEADS
A_KV_DIM = A_KV_HEADS * HEAD_DIM
REL_BUCKETS = 32
REL_MAX_DIST = 128
D_FF = 5632
LN_EPS = 1e-5
SCALE = HEAD_DIM ** -0.5

SUBLANES = 8
LANES = 128

BF16 = jnp.bfloat16
F32 = jnp.float32
NEG_INF = float("-inf")

_NT_DIMS = (((1,), (1,)), ((), ()))


def _params(*sem):
    return pltpu.CompilerParams(dimension_semantics=sem)


LOG2E = math.log2(math.e)
Q_SCALE = SCALE * LOG2E
SWA_QKV_COL_TILES = 2
SWA_QKV_TM = 1024
FOX_QKV_TN = 1024
FOX_QKV_TM = 2048
OPROJ_TM = 512


def _ws_matmul_kernel(w_is_transposed, x_ref, w_ref, o_ref, wb_ref):
    @pl.when(pl.program_id(1) == 0)
    def _():
        wb_ref[...] = w_ref[...].astype(BF16)

    x = x_ref[...].astype(BF16)
    if w_is_transposed:
        y = lax.dot_general(x, wb_ref[...], _NT_DIMS, preferred_element_type=F32)
    else:
        y = jnp.dot(x, wb_ref[...], preferred_element_type=F32)
    tn = y.shape[1]
    col = pl.program_id(0) * tn + lax.broadcasted_iota(jnp.int32, (1, tn), 1)
    y = (y * jnp.where(col < D_MODEL, Q_SCALE, 1.0)).astype(o_ref.dtype)
    if len(o_ref.shape) == 2:
        o_ref[...] = y
    else:
        for c in range(o_ref.shape[0]):
            o_ref[c] = y[:, c * LANES:(c + 1) * LANES]


def _ws_matmul(x, w, layer, n_out, tn, tm, name, w_is_transposed=False, slab_batch=None):
    m, k = x.shape
    if w_is_transposed:
        w_spec = pl.BlockSpec((None, tn, k), lambda j, i: (layer, j, 0))
        wb_shape = (tn, k)
    else:
        w_spec = pl.BlockSpec((None, k, tn), lambda j, i: (layer, 0, j))
        wb_shape = (k, tn)
    if slab_batch is None:
        out_spec = pl.BlockSpec((tm, tn), lambda j, i: (i, j))
        out_shape = jax.ShapeDtypeStruct((m, n_out), BF16)
    else:
        batch, seq = slab_batch
        per_seq = seq // tm
        out_spec = pl.BlockSpec((None, tn // LANES, tm, LANES),
                                lambda j, i: (i // per_seq, j, i % per_seq, 0))
        out_shape = jax.ShapeDtypeStruct((batch, n_out // LANES, seq, LANES), BF16)
    return pl.pallas_call(
        functools.partial(_ws_matmul_kernel, w_is_transposed),
        grid=(n_out // tn, m // tm),
        in_specs=[pl.BlockSpec((tm, k), lambda j, i: (i, 0)), w_spec],
        out_specs=out_spec,
        out_shape=out_shape,
        scratch_shapes=[pltpu.VMEM(wb_shape, BF16)],
        compiler_params=_params("arbitrary", "arbitrary"),
        name=name,
    )(x, w)


def _t5_bucket_map():
    qi = np.arange(BLOCK)[None, :]
    kj = np.arange(2 * BLOCK)[:, None]
    dist = BLOCK + qi - kj
    n = np.maximum(dist, 0)
    max_exact = REL_BUCKETS // 2
    nf = np.maximum(n, 1).astype(np.float32)
    large = max_exact + (np.log(nf / max_exact) / math.log(REL_MAX_DIST / max_exact)
                         * (REL_BUCKETS - max_exact)).astype(np.int32)
    large = np.minimum(large, REL_BUCKETS - 1)
    bucket = np.where(n < max_exact, n, large)
    visible = (dist >= 0) & (dist < BLOCK)
    return np.where(visible, bucket, -1).astype(np.int32)


def _swa_bias_kernel(rb_ref, bk_ref, o_ref):
    bk = bk_ref[...]
    key = lax.broadcasted_iota(jnp.int32, bk.shape, 0)

    def head(h, _):
        acc = jnp.full(bk.shape, NEG_INF, F32)
        for b in range(REL_BUCKETS):
            acc = jnp.where(bk == b, rb_ref[b, h] * LOG2E, acc)
        o_ref[0, h] = acc
        o_ref[1, h] = jnp.where(key < BLOCK, NEG_INF, acc)
        return 0

    lax.fori_loop(0, N_HEADS, head, 0)


def _swa_bias(rel_bias):
    bk = jnp.asarray(_t5_bucket_map())
    return pl.pallas_call(
        _swa_bias_kernel,
        in_specs=[pl.BlockSpec(memory_space=pltpu.SMEM),
                  pl.BlockSpec(memory_space=pltpu.VMEM)],
        out_specs=pl.BlockSpec(memory_space=pltpu.VMEM),
        out_shape=jax.ShapeDtypeStruct((2, N_HEADS, 2 * BLOCK, BLOCK), F32),
        name="swa_bias",
    )(rel_bias, bk)


_TN_DIMS = (((0,), (0,)), ((), ()))
SWA_SCORES_AHEAD = 2
SWA_BLOCKS_PER_STEP = 2


def _swa_kernel(sink_ref, q_ref, kp_ref, kc_ref, vp_ref, vc_ref, bm_ref, o_ref,
                qs_ref, s_ref, p_ref, linv_ref):
    units = [(qb, kh) for qb in range(SWA_BLOCKS_PER_STEP) for kh in range(A_KV_HEADS)]
    first_table = jnp.where(pl.program_id(1) == 0, 1, 0)

    def head_at(head):
        slab, half = divmod(head, LANES // HEAD_DIM)
        return slab, slice(half * HEAD_DIM, (half + 1) * HEAD_DIM)

    def keys_values(prev_ref, cur_ref, qb, kh):
        slab, lanes = head_at(kh)
        if qb == 0:
            return jnp.concatenate([prev_ref[slab, :, lanes], cur_ref[slab, 0:BLOCK, lanes]], axis=0)
        return cur_ref[slab, (qb - 1) * BLOCK:(qb + 1) * BLOCK, lanes]

    def scores(u):
        qb, kh = units[u]
        for g in range(A_GROUP):
            slab, lanes = head_at(kh * A_GROUP + g)
            qs_ref[u, g * BLOCK:(g + 1) * BLOCK, :] = q_ref[slab, qb * BLOCK:(qb + 1) * BLOCK, lanes]
        k2 = keys_values(kp_ref, kc_ref, qb, kh)
        s = lax.dot_general(k2, qs_ref[u], _NT_DIMS, preferred_element_type=F32)
        for g in range(A_GROUP):
            s_ref[u, g] = s[:, g * BLOCK:(g + 1) * BLOCK]

    def softmax(u):
        qb, kh = units[u]
        table = first_table if qb == 0 else 0
        for g in range(A_GROUP):
            h = kh * A_GROUP + g
            cols = slice(g * BLOCK, (g + 1) * BLOCK)
            s = s_ref[u, g] + bm_ref[table, h]
            sink = sink_ref[h] * LOG2E
            m = jnp.maximum(jnp.max(s, axis=0, keepdims=True), sink)
            p = jnp.exp2(s - m)
            denom = jnp.sum(p, axis=0, keepdims=True) + jnp.exp2(sink - m)
            p_ref[u, g // 2, :, (g % 2) * BLOCK:(g % 2 + 1) * BLOCK] = p.astype(BF16)
            linv_ref[u, :, cols] = 1.0 / denom

    def weighted_values(u):
        qb, kh = units[u]
        v2 = keys_values(vp_ref, vc_ref, qb, kh)
        for c in range(A_GROUP // 2):
            cols = slice(2 * c * BLOCK, 2 * (c + 1) * BLOCK)
            ot = (lax.dot_general(v2, p_ref[u, c], _TN_DIMS, preferred_element_type=F32)
                  * linv_ref[u, :, cols])
            for gg in range(2):
                slab, lanes = head_at(kh * A_GROUP + 2 * c + gg)
                o_ref[slab, qb * BLOCK:(qb + 1) * BLOCK, lanes] = (
                    ot[:, gg * BLOCK:(gg + 1) * BLOCK].T.astype(o_ref.dtype))

    for u in range(min(SWA_SCORES_AHEAD, len(units))):
        scores(u)
    for u in range(len(units)):
        if u + SWA_SCORES_AHEAD < len(units):
            scores(u + SWA_SCORES_AHEAD)
        softmax(u)
        weighted_values(u)


def _swa_attention(qkv, sinks, biasmask):
    b, _, s, _ = qkv.shape
    rows = SWA_BLOCKS_PER_STEP * BLOCK
    gq = A_GROUP * BLOCK
    n_units = SWA_BLOCKS_PER_STEP * A_KV_HEADS
    q_slabs = D_MODEL // LANES
    kv_slabs = A_KV_DIM // LANES
    kcol = q_slabs // kv_slabs
    prev = lambda n: jnp.maximum(SWA_BLOCKS_PER_STEP * n - 1, 0)
    return pl.pallas_call(
        _swa_kernel,
        grid=(b, s // rows),
        in_specs=[pl.BlockSpec(memory_space=pltpu.SMEM),
                  pl.BlockSpec((None, q_slabs, rows, LANES), lambda bi, n: (bi, 0, n, 0)),
                  pl.BlockSpec((None, kv_slabs, BLOCK, LANES), lambda bi, n: (bi, kcol, prev(n), 0)),
                  pl.BlockSpec((None, kv_slabs, rows, LANES), lambda bi, n: (bi, kcol, n, 0)),
                  pl.BlockSpec((None, kv_slabs, BLOCK, LANES), lambda bi, n: (bi, kcol + 1, prev(n), 0)),
                  pl.BlockSpec((None, kv_slabs, rows, LANES), lambda bi, n: (bi, kcol + 1, n, 0)),
                  pl.BlockSpec((2, N_HEADS, 2 * BLOCK, BLOCK), lambda bi, n: (0, 0, 0, 0),
                               pipeline_mode=pl.Buffered(1))],
        out_specs=pl.BlockSpec((None, q_slabs, rows, LANES), lambda bi, n: (bi, 0, n, 0)),
        out_shape=jax.ShapeDtypeStruct((b, q_slabs, s, LANES), BF16),
        scratch_shapes=[pltpu.VMEM((n_units, gq, HEAD_DIM), BF16),
                        pltpu.VMEM((n_units, A_GROUP, 2 * BLOCK, BLOCK), F32),
                        pltpu.VMEM((n_units, A_GROUP // 2, 2 * BLOCK, 2 * BLOCK), BF16),
                        pltpu.VMEM((n_units, 1, gq), F32)],
        compiler_params=_params("arbitrary", "arbitrary"),
        name="swa_attn",
    )(sinks, qkv, qkv, qkv, qkv, qkv, biasmask)


FOX_CUM_BLOCK = 256
FOX_TQ = 256
FOX_HEADS_PER_PAIR = LANES // HEAD_DIM
FOX_PAIRS_PER_STEP = 2
FOX_PIECES = 3
FOX_SUM_ROWS = 16
FOX_SCORE_SLOTS = 4
FOX_PV_LAG = 1
FOX_PROB_SLOTS = FOX_PV_LAG + 1
BF16_ONE_PAIR = 0x3F803F80


def _fox_ext_lane(head, piece):
    pair, odd = divmod(head, FOX_HEADS_PER_PAIR)
    return pair * LANES + (1 - odd) * HEAD_DIM + piece


def _fox_placement():
    pm = np.zeros((FOX_PIECES * N_HEADS, D_MODEL), np.float32)
    for t in range(FOX_PIECES):
        for h in range(N_HEADS):
            pm[t * N_HEADS + h, _fox_ext_lane(h, t)] = -1.0
    return pm


def _fox_lane_words():
    w = np.zeros((SUBLANES, LANES), np.uint32)
    for hh in range(FOX_HEADS_PER_PAIR):
        own = np.zeros(LANES, bool)
        own[hh * HEAD_DIM:(hh + 1) * HEAD_DIM] = True
        ext = np.zeros(LANES, bool)
        first = _fox_ext_lane(hh, 0)
        ext[first:first + FOX_PIECES] = True
        w[3 * hh + 0] = np.where(own, 0xFFFFFFFF, 0)
        w[3 * hh + 1] = np.where(ext, 0xFFFFFFFF, 0)
        w[3 * hh + 2] = np.where(ext, BF16_ONE_PAIR, 0)
    return w


def _log_sigmoid(x):
    return jnp.minimum(x, 0.0) - jnp.log1p(jnp.exp(-jnp.abs(x)))


def _round_bf16(x):
    return x.astype(BF16).astype(F32)


def _fox_gate_kernel(x_ref, wf_ref, fb_ref, pm_ref, cumt_ref, kext_ref, w3_ref):
    @pl.when(pl.program_id(0) == 0)
    def _():
        w = wf_ref[...].astype(BF16)
        for t in range(FOX_PIECES):
            w3_ref[t * N_HEADS:(t + 1) * N_HEADS, :] = w

    fz = lax.dot_general(w3_ref[...], x_ref[...], _NT_DIMS, preferred_element_type=F32) + fb_ref[...]
    lf = _log_sigmoid(fz)
    cb = FOX_CUM_BLOCK
    row = lax.broadcasted_iota(jnp.int32, (cb, cb), 0)
    col = lax.broadcasted_iota(jnp.int32, (cb, cb), 1)
    tri = jnp.where(row <= col, 1.0, 0.0).astype(BF16)
    tri3 = jnp.concatenate([tri] * FOX_PIECES, axis=0)
    group = lax.broadcasted_iota(jnp.int32, (lf.shape[0], cb), 0) // N_HEADS
    pm = pm_ref[...]
    n_blocks = lf.shape[1] // cb
    local = []
    for blk in range(n_blocks):
        x = lf[:, blk * cb:(blk + 1) * cb]
        hi = _round_bf16(x)
        r1 = x - hi
        mid = _round_bf16(r1)
        pieces = jnp.concatenate([hi, mid, r1 - mid], axis=1).astype(BF16)
        local.append(jnp.dot(pieces, tri3, preferred_element_type=F32))
    carry = jnp.zeros((lf.shape[0], 1), F32)
    for blk in range(n_blocks):
        cols = slice(blk * cb, (blk + 1) * cb)
        c = local[blk] + carry
        carry = carry + local[blk][:, cb - 1:cb]
        c2 = c * LOG2E
        cumt_ref[:, cols] = c2[:N_HEADS]
        chi = _round_bf16(c2)
        cr1 = c2 - chi
        cmid = _round_bf16(cr1)
        piece = jnp.where(group == 0, chi, jnp.where(group == 1, cmid, cr1 - cmid)).astype(BF16)
        kext_ref[cols, :] = lax.dot_general(piece, pm, _TN_DIMS, preferred_element_type=F32).astype(BF16)


def _fox_gate(xb, w_qkvf_t, layer, fb3, batch, seq):
    pm = jnp.asarray(_fox_placement(), BF16)
    n3 = FOX_PIECES * N_HEADS
    gate_row = 3 * D_MODEL // N_HEADS
    return pl.pallas_call(
        _fox_gate_kernel,
        grid=(batch,),
        in_specs=[pl.BlockSpec((seq, D_MODEL), lambda bi: (bi, 0)),
                  pl.BlockSpec((None, N_HEADS, D_MODEL), lambda bi: (layer, gate_row, 0)),
                  pl.BlockSpec((n3, 1), lambda bi: (0, 0)),
                  pl.BlockSpec((n3, D_MODEL), lambda bi: (0, 0))],
        out_specs=[pl.BlockSpec((None, N_HEADS, seq), lambda bi: (bi, 0, 0)),
                   pl.BlockSpec((None, seq, D_MODEL), lambda bi: (bi, 0, 0))],
        out_shape=[jax.ShapeDtypeStruct((batch, N_HEADS, seq), F32),
                   jax.ShapeDtypeStruct((batch, seq, D_MODEL), BF16)],
        scratch_shapes=[pltpu.VMEM((n3, D_MODEL), BF16)],
        compiler_params=_params("arbitrary"),
        name="fox_gate",
    )(xb, w_qkvf_t, fb3, pm)


def _fox_kernel(q_ref, k_ref, v_ref, kext_ref, cbt_ref, lw_ref, o_ref, qa_ref, ka_ref, vt_ref,
                s_ref, p_ref):
    seq = q_ref.shape[0]
    tq = FOX_TQ
    key = lax.broadcasted_iota(jnp.int32, (tq, tq), 0)
    qry = lax.broadcasted_iota(jnp.int32, (tq, tq), 1)
    causal = key <= qry
    words = lambda x: pltpu.bitcast(x, jnp.uint32)
    for pair in range(FOX_PAIRS_PER_STEP):
        lanes = slice(pair * LANES, (pair + 1) * LANES)
        q = words(q_ref[:, lanes])
        k = words(k_ref[:, lanes])
        kext = words(kext_ref[:, lanes])
        vt = v_ref[:, lanes].T
        for hh in range(FOX_HEADS_PER_PAIR):
            h = pair * FOX_HEADS_PER_PAIR + hh
            own = lw_ref[3 * hh + 0:3 * hh + 1, :]
            ext = lw_ref[3 * hh + 1:3 * hh + 2, :]
            ones = lw_ref[3 * hh + 2:3 * hh + 3, :]
            qa_ref[h] = pltpu.bitcast((q & own) | ones, BF16)
            ka_ref[h] = pltpu.bitcast((k & ~ext) | (kext & ext), BF16)
            vt_ref[h, 0:HEAD_DIM, :] = vt[hh * HEAD_DIM:(hh + 1) * HEAD_DIM]
            vt_ref[h, HEAD_DIM:, :] = jnp.ones((FOX_SUM_ROWS, seq), BF16)

    heads = FOX_PAIRS_PER_STEP * FOX_HEADS_PER_PAIR
    units = [(i, h) for i in range(seq // tq) for h in range(heads)]

    def scores(u):
        i, h = units[u]
        slot = u % FOX_SCORE_SLOTS
        r0, r1 = i * tq, (i + 1) * tq
        qa = qa_ref[h, r0:r1, :]
        s_d = lax.dot_general(ka_ref[h, r0:r1, :], qa, _NT_DIMS, preferred_element_type=F32)
        s_ref[slot, r0:r1, :] = jnp.where(causal, s_d, NEG_INF)
        if i > 0:
            s_ref[slot, 0:r0, :] = lax.dot_general(ka_ref[h, 0:r0, :], qa, _NT_DIMS,
                                                   preferred_element_type=F32)

    def softmax(u):
        i, h = units[u]
        slot = u % FOX_SCORE_SLOTS
        r0, r1 = i * tq, (i + 1) * tq
        head = pl.program_id(1) * heads + h
        cbq = cbt_ref[pl.ds(head, 1), r0:r1]
        m = jnp.max(s_ref[slot, 0:r1, :], axis=0, keepdims=True)
        shift = cbq - (m + cbq)
        p_ref[u % FOX_PROB_SLOTS, 0:r1, :] = jnp.exp2(s_ref[slot, 0:r1, :] + shift).astype(BF16)

    outs = []

    def weighted_values(u):
        i, h = units[u]
        r1 = (i + 1) * tq
        ot = jnp.dot(vt_ref[h, :, 0:r1], p_ref[u % FOX_PROB_SLOTS, 0:r1, :], preferred_element_type=F32)
        outs.append(ot[0:HEAD_DIM] * (1.0 / ot[HEAD_DIM:HEAD_DIM + 1]))
        if h == heads - 1:
            o_ref[i * tq:r1, :] = jnp.concatenate(outs, axis=0).T.astype(o_ref.dtype)
            outs.clear()

    ahead = FOX_SCORE_SLOTS - 1
    for u in range(min(ahead, len(units))):
        scores(u)
    for u in range(len(units)):
        if u + ahead < len(units):
            scores(u + ahead)
        softmax(u)
        if u >= FOX_PV_LAG:
            weighted_values(u - FOX_PV_LAG)
    for u in range(len(units) - FOX_PV_LAG, len(units)):
        weighted_values(u)


def _fox_attention(qkv, cumt, kext):
    b, s, _ = qkv.shape
    heads = FOX_PAIRS_PER_STEP * FOX_HEADS_PER_PAIR
    width = FOX_PAIRS_PER_STEP * LANES
    lw = jnp.asarray(_fox_lane_words())
    ncol = D_MODEL // width
    return pl.pallas_call(
        _fox_kernel,
        grid=(b, N_HEADS // heads),
        in_specs=[pl.BlockSpec((None, s, width), lambda bi, hp: (bi, 0, hp)),
                  pl.BlockSpec((None, s, width), lambda bi, hp: (bi, 0, ncol + hp)),
                  pl.BlockSpec((None, s, width), lambda bi, hp: (bi, 0, 2 * ncol + hp)),
                  pl.BlockSpec((None, s, width), lambda bi, hp: (bi, 0, hp)),
                  pl.BlockSpec((None, N_HEADS, s), lambda bi, hp: (bi, 0, 0)),
                  pl.BlockSpec((SUBLANES, LANES), lambda bi, hp: (0, 0))],
        out_specs=pl.BlockSpec((None, s, width), lambda bi, hp: (bi, 0, hp)),
        out_shape=jax.ShapeDtypeStruct((b, s, D_MODEL), BF16),
        scratch_shapes=[pltpu.VMEM((heads, s, LANES), BF16), pltpu.VMEM((heads, s, LANES), BF16),
                        pltpu.VMEM((heads, HEAD_DIM + FOX_SUM_ROWS, s), BF16),
                        pltpu.VMEM((FOX_SCORE_SLOTS, s, FOX_TQ), F32),
                        pltpu.VMEM((FOX_PROB_SLOTS, s, FOX_TQ), BF16)],
        compiler_params=_params("arbitrary", "arbitrary"),
        name="fox_attn",
    )(qkv, qkv, qkv, kext, cumt, lw)


def _deepnorm(xres, y, g, b, alpha):
    z = alpha * xres + y
    mu = jnp.mean(z, axis=-1, keepdims=True)
    zc = z - mu
    var = jnp.mean(zc * zc, axis=-1, keepdims=True)
    return zc * lax.rsqrt(var + LN_EPS) * g + b


LN_SUBTILES = 2


def _oproj_ln_kernel(alpha, o_ref, xres_ref, w_ref, g_ref, b_ref, xo_ref, xob_ref, wb_ref):
    @pl.when(pl.program_id(0) == 0)
    def _():
        wb_ref[...] = w_ref[...].astype(BF16)

    def lhs(rows):
        if len(o_ref.shape) == 2:
            return o_ref[rows, :]
        return jnp.concatenate([o_ref[c, rows, :] for c in range(o_ref.shape[0])], axis=1)

    ts = xres_ref.shape[0] // LN_SUBTILES
    ys = [jnp.dot(lhs(slice(r * ts, (r + 1) * ts)), wb_ref[...], preferred_element_type=F32)
          for r in range(LN_SUBTILES)]
    for r, y in enumerate(ys):
        rows = slice(r * ts, (r + 1) * ts)
        out = _deepnorm(xres_ref[rows, :], y, g_ref[...], b_ref[...], alpha)
        xo_ref[rows, :] = out
        xob_ref[rows, :] = out.astype(BF16)


def _oproj_ln(o, xres, w, layer, g, b, alpha, tm, name):
    m, n = xres.shape
    k = w.shape[1]
    if o.ndim == 2:
        o_spec = pl.BlockSpec((tm, k), lambda i: (i, 0))
    else:
        per_seq = o.shape[2] // tm
        o_spec = pl.BlockSpec((None, o.shape[1], tm, LANES), lambda i: (i // per_seq, 0, i % per_seq, 0))
    return pl.pallas_call(
        functools.partial(_oproj_ln_kernel, alpha),
        grid=(m // tm,),
        in_specs=[o_spec,
                  pl.BlockSpec((tm, n), lambda i: (i, 0)),
                  pl.BlockSpec((None, k, n), lambda i: (layer, 0, 0), pipeline_mode=pl.Buffered(1)),
                  pl.BlockSpec((1, n), lambda i: (0, 0)),
                  pl.BlockSpec((1, n), lambda i: (0, 0))],
        out_specs=[pl.BlockSpec((tm, n), lambda i: (i, 0)),
                   pl.BlockSpec((tm, n), lambda i: (i, 0))],
        out_shape=[jax.ShapeDtypeStruct((m, n), F32), jax.ShapeDtypeStruct((m, n), BF16)],
        scratch_shapes=[pltpu.VMEM((k, n), BF16)],
        compiler_params=_params("arbitrary"),
        name=name,
    )(o, xres, w, g, b)


CONV_WIDTH = 3
FFN_TF = 512
FFN_UP_TM = 1024


def _ffn_up_kernel(tiles_per_seq, x_ref, wg_ref, wu_ref, cwg_ref, cwu_ref, cbg_ref, cbu_ref,
                   a_ref, wb_ref, h_ref):
    i = pl.program_id(1)
    tm = x_ref.shape[0]
    tf = a_ref.shape[1]
    halo = SUBLANES

    @pl.when(i == 0)
    def _():
        wb_ref[:, 0:tf] = wg_ref[...].astype(BF16)
        wb_ref[:, tf:2 * tf] = wu_ref[...].astype(BF16)

    @pl.when(i % tiles_per_seq == 0)
    def _():
        h_ref[0:halo, :] = jnp.zeros((halo, h_ref.shape[1]), F32)

    h_ref[halo:halo + tm, :] = jnp.dot(x_ref[...], wb_ref[...], preferred_element_type=F32)

    def conv(cols, cw_ref, cb_ref):
        out = cb_ref[...]
        for tap in range(CONV_WIDTH):
            start = halo - (CONV_WIDTH - 1) + tap
            out = out + cw_ref[tap:tap + 1, :] * h_ref[start:start + tm, cols]
        return out

    g = conv(slice(0, tf), cwg_ref, cbg_ref)
    u = conv(slice(tf, 2 * tf), cwu_ref, cbu_ref)
    a_ref[...] = (g * jax.nn.sigmoid(g) * u).astype(a_ref.dtype)
    h_ref[0:halo, :] = h_ref[tm:tm + halo, :]


def _ffn_up(xb, w_up, conv_w, conv_b, layer, seq):
    m, k = xb.shape
    tm, tf = FFN_UP_TM, FFN_TF
    nf = D_FF // tf
    cb2 = conv_b.reshape(conv_b.shape[0], 1, 2 * D_FF)
    return pl.pallas_call(
        functools.partial(_ffn_up_kernel, seq // tm),
        grid=(nf, m // tm),
        in_specs=[pl.BlockSpec((tm, k), lambda j, i: (i, 0)),
                  pl.BlockSpec((None, k, tf), lambda j, i: (layer, 0, j)),
                  pl.BlockSpec((None, k, tf), lambda j, i: (layer, 0, j + nf)),
                  pl.BlockSpec((None, CONV_WIDTH, tf), lambda j, i: (layer, 0, j)),
                  pl.BlockSpec((None, CONV_WIDTH, tf), lambda j, i: (layer, 0, j + nf)),
                  pl.BlockSpec((None, 1, tf), lambda j, i: (layer, 0, j)),
                  pl.BlockSpec((None, 1, tf), lambda j, i: (layer, 0, j + nf))],
        out_specs=pl.BlockSpec((tm, tf), lambda j, i: (i, j)),
        out_shape=jax.ShapeDtypeStruct((m, D_FF), BF16),
        scratch_shapes=[pltpu.VMEM((k, 2 * tf), BF16), pltpu.VMEM((tm + 2 * SUBLANES, 2 * tf), F32)],
        compiler_params=_params("arbitrary", "arbitrary"),
        name="ffn_up",
    )(xb, w_up, w_up, conv_w, conv_w, cb2, cb2)


FFN_DOWN_TM = 256
FFN_DOWN_WCHUNK = 512


def _ffn_down_ln_kernel(alpha, n_load, emit_bf16, a_ref, xres_ref, w_ref, g_ref, b_ref, *rest):
    if emit_bf16:
        xo_ref, xob_ref, wb_ref = rest
    else:
        (xo_ref, wb_ref), xob_ref = rest, None
    step = pl.program_id(0)
    rows = w_ref.shape[0]

    @pl.when(step < n_load)
    def _():
        off = pl.multiple_of(step * rows, rows)
        wb_ref[pl.ds(off, rows), :] = w_ref[...].astype(BF16)

    @pl.when(step >= n_load)
    def _():
        y = jnp.dot(a_ref[...], wb_ref[...], preferred_element_type=F32)
        out = _deepnorm(xres_ref[...], y, g_ref[...], b_ref[...], alpha)
        xo_ref[...] = out
        if emit_bf16:
            xob_ref[...] = out.astype(BF16)


def _ffn_down_ln(a, xres, w, layer, g, b, alpha, emit_bf16, name):
    m, k = a.shape
    n = w.shape[2]
    tm, rows = FFN_DOWN_TM, FFN_DOWN_WCHUNK
    n_load = k // rows
    tile = lambda s: (jnp.maximum(s - n_load, 0), 0)
    out_specs = [pl.BlockSpec((tm, n), tile)]
    out_shape = [jax.ShapeDtypeStruct((m, n), F32)]
    if emit_bf16:
        out_specs.append(pl.BlockSpec((tm, n), tile))
        out_shape.append(jax.ShapeDtypeStruct((m, n), BF16))
    return pl.pallas_call(
        functools.partial(_ffn_down_ln_kernel, alpha, n_load, emit_bf16),
        grid=(n_load + m // tm,),
        in_specs=[pl.BlockSpec((tm, k), tile),
                  pl.BlockSpec((tm, n), tile),
                  pl.BlockSpec((None, rows, n), lambda s: (layer, jnp.minimum(s, n_load - 1), 0)),
                  pl.BlockSpec((1, n), lambda s: (0, 0)),
                  pl.BlockSpec((1, n), lambda s: (0, 0))],
        out_specs=out_specs,
        out_shape=out_shape,
        scratch_shapes=[pltpu.VMEM((k, n), BF16)],
        compiler_params=_params("arbitrary"),
        name=name,
    )(a, xres, w, g, b)


def kernel(x, a_w_qkv, a_w_o, a_sinks, rel_bias, b_w_qkvf, b_f_bias, b_w_o,
           ffn_w_up, ffn_conv_w, ffn_conv_b, ffn_w_down, ln_g, ln_b):
    batch, seq, d = x.shape
    depth = ffn_w_up.shape[0]
    alpha = (2 * depth) ** 0.25
    m = batch * seq
    xf = x.reshape(m, d)
    xb = None
    biasmask = _swa_bias(rel_bias)
    for layer in range(depth):
        j = layer // 2
        g = ln_g[layer].reshape(2, 1, d)
        bt = ln_b[layer].reshape(2, 1, d)
        if layer % 2 == 0:
            n_qkv = D_MODEL + 2 * A_KV_DIM
            src = xf if xb is None else xb
            qkv = _ws_matmul(src, a_w_qkv, j, n_qkv, n_qkv // SWA_QKV_COL_TILES, SWA_QKV_TM, "swa_qkv",
                             slab_batch=(batch, seq))
            o = _swa_attention(qkv, a_sinks[j], biasmask)
            w_o = a_w_o
        else:
            w_t = jnp.swapaxes(b_w_qkvf, 1, 2)
            qkv = _ws_matmul(xb, w_t, j, 3 * D_MODEL, FOX_QKV_TN, FOX_QKV_TM, "fox_qkv",
                             w_is_transposed=True)
            fb3 = jnp.tile(b_f_bias[j].reshape(N_HEADS, 1), (FOX_PIECES, 1))
            cumt, kext = _fox_gate(xb, w_t, j, fb3, batch, seq)
            o = _fox_attention(qkv.reshape(batch, seq, 3 * D_MODEL), cumt, kext).reshape(m, d)
            w_o = b_w_o
        xf, xb = _oproj_ln(o, xf, w_o, j, g[0], bt[0], alpha, OPROJ_TM, "oproj_ln")
        a = _ffn_up(xb, ffn_w_up, ffn_conv_w, ffn_conv_b, layer, seq)
        last = layer == depth - 1
        res = _ffn_down_ln(a, xf, ffn_w_down, layer, g[1], bt[1], alpha, not last, "ffn_down_ln")
        xf, xb = (res[0], None) if last else res
    return xf.reshape(batch, seq, d)
```

```python
import functools
import math

import jax
import jax.numpy as jnp
import numpy as np
from jax import lax
from jax.experimental import pallas as pl
from jax.experimental.pallas import tpu as pltpu

D_MODEL = 2048
HEAD_DIM = 64
BLOCK = 128
N_HEADS = D_MODEL // HEAD_DIM
A_KV_HEADS = 4
A_GROUP = N_HEADS // A_KV_HEADS
A_KV_DIM = A_KV_HEADS * HEAD_DIM
REL_BUCKETS = 32
REL_MAX_DIST = 128
D_FF = 5632
LN_EPS = 1e-5
SCALE = HEAD_DIM ** -0.5

SUBLANES = 8
LANES = 128

BF16 = jnp.bfloat16
F32 = jnp.float32
NEG_INF = float("-inf")

_NT_DIMS = (((1,), (1,)), ((), ()))


def _params(*sem):
    return pltpu.CompilerParams(dimension_semantics=sem)


LOG2E = math.log2(math.e)
Q_SCALE = SCALE * LOG2E
SWA_QKV_COL_TILES = 2
SWA_QKV_TM = 1024
FOX_QKV_TN = 1024
FOX_QKV_TM = 2048
OPROJ_TM = 512


def _ws_matmul_kernel(w_is_transposed, x_ref, w_ref, o_ref, wb_ref):
    @pl.when(pl.program_id(1) == 0)
    def _():
        wb_ref[...] = w_ref[...].astype(BF16)

    x = x_ref[...].astype(BF16)
    if w_is_transposed:
        y = lax.dot_general(x, wb_ref[...], _NT_DIMS, preferred_element_type=F32)
    else:
        y = jnp.dot(x, wb_ref[...], preferred_element_type=F32)
    tn = y.shape[1]
    col = pl.program_id(0) * tn + lax.broadcasted_iota(jnp.int32, (1, tn), 1)
    y = (y * jnp.where(col < D_MODEL, Q_SCALE, 1.0)).astype(o_ref.dtype)
    if len(o_ref.shape) == 2:
        o_ref[...] = y
    else:
        for c in range(o_ref.shape[0]):
            o_ref[c] = y[:, c * LANES:(c + 1) * LANES]


def _ws_matmul(x, w, layer, n_out, tn, tm, name, w_is_transposed=False, slab_batch=None):
    m, k = x.shape
    if w_is_transposed:
        w_spec = pl.BlockSpec((None, tn, k), lambda j, i: (layer, j, 0))
        wb_shape = (tn, k)
    else:
        w_spec = pl.BlockSpec((None, k, tn), lambda j, i: (layer, 0, j))
        wb_shape = (k, tn)
    if slab_batch is None:
        out_spec = pl.BlockSpec((tm, tn), lambda j, i: (i, j))
        out_shape = jax.ShapeDtypeStruct((m, n_out), BF16)
    else:
        batch, seq = slab_batch
        per_seq = seq // tm
        out_spec = pl.BlockSpec((None, tn // LANES, tm, LANES),
                                lambda j, i: (i // per_seq, j, i % per_seq, 0))
        out_shape = jax.ShapeDtypeStruct((batch, n_out // LANES, seq, LANES), BF16)
    return pl.pallas_call(
        functools.partial(_ws_matmul_kernel, w_is_transposed),
        grid=(n_out // tn, m // tm),
        in_specs=[pl.BlockSpec((tm, k), lambda j, i: (i, 0)), w_spec],
        out_specs=out_spec,
        out_shape=out_shape,
        scratch_shapes=[pltpu.VMEM(wb_shape, BF16)],
        compiler_params=_params("arbitrary", "arbitrary"),
        name=name,
    )(x, w)


def _t5_bucket_map():
    qi = np.arange(BLOCK)[None, :]
    kj = np.arange(2 * BLOCK)[:, None]
    dist = BLOCK + qi - kj
    n = np.maximum(dist, 0)
    max_exact = REL_BUCKETS // 2
    nf = np.maximum(n, 1).astype(np.float32)
    large = max_exact + (np.log(nf / max_exact) / math.log(REL_MAX_DIST / max_exact)
                         * (REL_BUCKETS - max_exact)).astype(np.int32)
    large = np.minimum(large, REL_BUCKETS - 1)
    bucket = np.where(n < max_exact, n, large)
    visible = (dist >= 0) & (dist < BLOCK)
    return np.where(visible, bucket, -1).astype(np.int32)


def _swa_bias_kernel(rb_ref, bk_ref, o_ref):
    bk = bk_ref[...]
    key = lax.broadcasted_iota(jnp.int32, bk.shape, 0)

    def head(h, _):
        acc = jnp.full(bk.shape, NEG_INF, F32)
        for b in range(REL_BUCKETS):
            acc = jnp.where(bk == b, rb_ref[b, h] * LOG2E, acc)
        o_ref[0, h] = acc
        o_ref[1, h] = jnp.where(key < BLOCK, NEG_INF, acc)
        return 0

    lax.fori_loop(0, N_HEADS, head, 0)


def _swa_bias(rel_bias):
    bk = jnp.asarray(_t5_bucket_map())
    return pl.pallas_call(
        _swa_bias_kernel,
        in_specs=[pl.BlockSpec(memory_space=pltpu.SMEM),
                  pl.BlockSpec(memory_space=pltpu.VMEM)],
        out_specs=pl.BlockSpec(memory_space=pltpu.VMEM),
        out_shape=jax.ShapeDtypeStruct((2, N_HEADS, 2 * BLOCK, BLOCK), F32),
        name="swa_bias",
    )(rel_bias, bk)


_TN_DIMS = (((0,), (0,)), ((), ()))
SWA_SCORES_AHEAD = 2
SWA_BLOCKS_PER_STEP = 2


def _swa_kernel(sink_ref, q_ref, kp_ref, kc_ref, vp_ref, vc_ref, bm_ref, o_ref,
                qs_ref, s_ref, p_ref, linv_ref):
    units = [(qb, kh) for qb in range(SWA_BLOCKS_PER_STEP) for kh in range(A_KV_HEADS)]
    first_table = jnp.where(pl.program_id(1) == 0, 1, 0)

    def head_at(head):
        slab, half = divmod(head, LANES // HEAD_DIM)
        return slab, slice(half * HEAD_DIM, (half + 1) * HEAD_DIM)

    def keys_values(prev_ref, cur_ref, qb, kh):
        slab, lanes = head_at(kh)
        if qb == 0:
            return jnp.concatenate([prev_ref[slab, :, lanes], cur_ref[slab, 0:BLOCK, lanes]], axis=0)
        return cur_ref[slab, (qb - 1) * BLOCK:(qb + 1) * BLOCK, lanes]

    def scores(u):
        qb, kh = units[u]
        for g in range(A_GROUP):
            slab, lanes = head_at(kh * A_GROUP + g)
            qs_ref[u, g * BLOCK:(g + 1) * BLOCK, :] = q_ref[slab, qb * BLOCK:(qb + 1) * BLOCK, lanes]
        k2 = keys_values(kp_ref, kc_ref, qb, kh)
        s = lax.dot_general(k2, qs_ref[u], _NT_DIMS, preferred_element_type=F32)
        for g in range(A_GROUP):
            s_ref[u, g] = s[:, g * BLOCK:(g + 1) * BLOCK]

    def softmax(u):
        qb, kh = units[u]
        table = first_table if qb == 0 else 0
        for g in range(A_GROUP):
            h = kh * A_GROUP + g
            cols = slice(g * BLOCK, (g + 1) * BLOCK)
            s = s_ref[u, g] + bm_ref[table, h]
            sink = sink_ref[h] * LOG2E
            m = jnp.maximum(jnp.max(s, axis=0, keepdims=True), sink)
            p = jnp.exp2(s - m)
            denom = jnp.sum(p, axis=0, keepdims=True) + jnp.exp2(sink - m)
            p_ref[u, g // 2, :, (g % 2) * BLOCK:(g % 2 + 1) * BLOCK] = p.astype(BF16)
            linv_ref[u, :, cols] = 1.0 / denom

    def weighted_values(u):
        qb, kh = units[u]
        v2 = keys_values(vp_ref, vc_ref, qb, kh)
        for c in range(A_GROUP // 2):
            cols = slice(2 * c * BLOCK, 2 * (c + 1) * BLOCK)
            ot = (lax.dot_general(v2, p_ref[u, c], _TN_DIMS, preferred_element_type=F32)
                  * linv_ref[u, :, cols])
            for gg in range(2):
                h = kh * A_GROUP + 2 * c + gg
                o_ref[h * HEAD_DIM:(h + 1) * HEAD_DIM, qb * BLOCK:(qb + 1) * BLOCK] = (
                    ot[:, gg * BLOCK:(gg + 1) * BLOCK].astype(o_ref.dtype))

    for u in range(min(SWA_SCORES_AHEAD, len(units))):
        scores(u)
    for u in range(len(units)):
        if u + SWA_SCORES_AHEAD < len(units):
            scores(u + SWA_SCORES_AHEAD)
        softmax(u)
        weighted_values(u)


def _swa_attention(qkv, sinks, biasmask):
    b, _, s, _ = qkv.shape
    rows = SWA_BLOCKS_PER_STEP * BLOCK
    gq = A_GROUP * BLOCK
    n_units = SWA_BLOCKS_PER_STEP * A_KV_HEADS
    q_slabs = D_MODEL // LANES
    kv_slabs = A_KV_DIM // LANES
    kcol = q_slabs // kv_slabs
    prev = lambda n: jnp.maximum(SWA_BLOCKS_PER_STEP * n - 1, 0)
    return pl.pallas_call(
        _swa_kernel,
        grid=(b, s // rows),
        in_specs=[pl.BlockSpec(memory_space=pltpu.SMEM),
                  pl.BlockSpec((None, q_slabs, rows, LANES), lambda bi, n: (bi, 0, n, 0)),
                  pl.BlockSpec((None, kv_slabs, BLOCK, LANES), lambda bi, n: (bi, kcol, prev(n), 0)),
                  pl.BlockSpec((None, kv_slabs, rows, LANES), lambda bi, n: (bi, kcol, n, 0)),
                  pl.BlockSpec((None, kv_slabs, BLOCK, LANES), lambda bi, n: (bi, kcol + 1, prev(n), 0)),
                  pl.BlockSpec((None, kv_slabs, rows, LANES), lambda bi, n: (bi, kcol + 1, n, 0)),
                  pl.BlockSpec((2, N_HEADS, 2 * BLOCK, BLOCK), lambda bi, n: (0, 0, 0, 0),
                               pipeline_mode=pl.Buffered(1))],
        out_specs=pl.BlockSpec((None, D_MODEL, rows), lambda bi, n: (bi, 0, n)),
        out_shape=jax.ShapeDtypeStruct((b, D_MODEL, s), BF16),
        scratch_shapes=[pltpu.VMEM((n_units, gq, HEAD_DIM), BF16),
                        pltpu.VMEM((n_units, A_GROUP, 2 * BLOCK, BLOCK), F32),
                        pltpu.VMEM((n_units, A_GROUP // 2, 2 * BLOCK, 2 * BLOCK), BF16),
                        pltpu.VMEM((n_units, 1, gq), F32)],
        compiler_params=_params("arbitrary", "arbitrary"),
        name="swa_attn",
    )(sinks, qkv, qkv, qkv, qkv, qkv, biasmask)


FOX_CUM_BLOCK = 256
FOX_TQ = 256
FOX_HEADS_PER_PAIR = LANES // HEAD_DIM
FOX_PAIRS_PER_STEP = 2
FOX_PIECES = 3
FOX_SUM_ROWS = 16
FOX_SCORE_SLOTS = 4
FOX_PV_LAG = 1
FOX_PROB_SLOTS = FOX_PV_LAG + 1
BF16_ONE_PAIR = 0x3F803F80


def _fox_ext_lane(head, piece):
    pair, odd = divmod(head, FOX_HEADS_PER_PAIR)
    return pair * LANES + (1 - odd) * HEAD_DIM + piece


def _fox_placement():
    pm = np.zeros((FOX_PIECES * N_HEADS, D_MODEL), np.float32)
    for t in range(FOX_PIECES):
        for h in range(N_HEADS):
            pm[t * N_HEADS + h, _fox_ext_lane(h, t)] = -1.0
    return pm


def _fox_lane_words():
    w = np.zeros((SUBLANES, LANES), np.uint32)
    for hh in range(FOX_HEADS_PER_PAIR):
        own = np.zeros(LANES, bool)
        own[hh * HEAD_DIM:(hh + 1) * HEAD_DIM] = True
        ext = np.zeros(LANES, bool)
        first = _fox_ext_lane(hh, 0)
        ext[first:first + FOX_PIECES] = True
        w[3 * hh + 0] = np.where(own, 0xFFFFFFFF, 0)
        w[3 * hh + 1] = np.where(ext, 0xFFFFFFFF, 0)
        w[3 * hh + 2] = np.where(ext, BF16_ONE_PAIR, 0)
    return w


def _log_sigmoid(x):
    return jnp.minimum(x, 0.0) - jnp.log1p(jnp.exp(-jnp.abs(x)))


def _round_bf16(x):
    return x.astype(BF16).astype(F32)


def _fox_gate_kernel(x_ref, wf_ref, fb_ref, pm_ref, cumt_ref, kext_ref, w3_ref):
    @pl.when(pl.program_id(0) == 0)
    def _():
        w = wf_ref[...].astype(BF16)
        for t in range(FOX_PIECES):
            w3_ref[t * N_HEADS:(t + 1) * N_HEADS, :] = w

    fz = lax.dot_general(w3_ref[...], x_ref[...], _NT_DIMS, preferred_element_type=F32) + fb_ref[...]
    lf = _log_sigmoid(fz)
    cb = FOX_CUM_BLOCK
    row = lax.broadcasted_iota(jnp.int32, (cb, cb), 0)
    col = lax.broadcasted_iota(jnp.int32, (cb, cb), 1)
    tri = jnp.where(row <= col, 1.0, 0.0).astype(BF16)
    tri3 = jnp.concatenate([tri] * FOX_PIECES, axis=0)
    group = lax.broadcasted_iota(jnp.int32, (lf.shape[0], cb), 0) // N_HEADS
    pm = pm_ref[...]
    n_blocks = lf.shape[1] // cb
    local = []
    for blk in range(n_blocks):
        x = lf[:, blk * cb:(blk + 1) * cb]
        hi = _round_bf16(x)
        r1 = x - hi
        mid = _round_bf16(r1)
        pieces = jnp.concatenate([hi, mid, r1 - mid], axis=1).astype(BF16)
        local.append(jnp.dot(pieces, tri3, preferred_element_type=F32))
    carry = jnp.zeros((lf.shape[0], 1), F32)
    for blk in range(n_blocks):
        cols = slice(blk * cb, (blk + 1) * cb)
        c = local[blk] + carry
        carry = carry + local[blk][:, cb - 1:cb]
        c2 = c * LOG2E
        cumt_ref[:, cols] = c2[:N_HEADS]
        chi = _round_bf16(c2)
        cr1 = c2 - chi
        cmid = _round_bf16(cr1)
        piece = jnp.where(group == 0, chi, jnp.where(group == 1, cmid, cr1 - cmid)).astype(BF16)
        kext_ref[cols, :] = lax.dot_general(piece, pm, _TN_DIMS, preferred_element_type=F32).astype(BF16)


def _fox_gate(xb, w_qkvf_t, layer, fb3, batch, seq):
    pm = jnp.asarray(_fox_placement(), BF16)
    n3 = FOX_PIECES * N_HEADS
    gate_row = 3 * D_MODEL // N_HEADS
    return pl.pallas_call(
        _fox_gate_kernel,
        grid=(batch,),
        in_specs=[pl.BlockSpec((seq, D_MODEL), lambda bi: (bi, 0)),
                  pl.BlockSpec((None, N_HEADS, D_MODEL), lambda bi: (layer, gate_row, 0)),
                  pl.BlockSpec((n3, 1), lambda bi: (0, 0)),
                  pl.BlockSpec((n3, D_MODEL), lambda bi: (0, 0))],
        out_specs=[pl.BlockSpec((None, N_HEADS, seq), lambda bi: (bi, 0, 0)),
                   pl.BlockSpec((None, seq, D_MODEL), lambda bi: (bi, 0, 0))],
        out_shape=[jax.ShapeDtypeStruct((batch, N_HEADS, seq), F32),
                   jax.ShapeDtypeStruct((batch, seq, D_MODEL), BF16)],
        scratch_shapes=[pltpu.VMEM((n3, D_MODEL), BF16)],
        compiler_params=_params("arbitrary"),
        name="fox_gate",
    )(xb, w_qkvf_t, fb3, pm)


def _fox_kernel(q_ref, k_ref, v_ref, kext_ref, cbt_ref, lw_ref, o_ref, qa_ref, ka_ref, vt_ref,
                s_ref, p_ref):
    seq = q_ref.shape[0]
    tq = FOX_TQ
    key = lax.broadcasted_iota(jnp.int32, (tq, tq), 0)
    qry = lax.broadcasted_iota(jnp.int32, (tq, tq), 1)
    causal = key <= qry
    words = lambda x: pltpu.bitcast(x, jnp.uint32)
    for pair in range(FOX_PAIRS_PER_STEP):
        lanes = slice(pair * LANES, (pair + 1) * LANES)
        q = words(q_ref[:, lanes])
        k = words(k_ref[:, lanes])
        kext = words(kext_ref[:, lanes])
        vt = v_ref[:, lanes].T
        for hh in range(FOX_HEADS_PER_PAIR):
            h = pair * FOX_HEADS_PER_PAIR + hh
            own = lw_ref[3 * hh + 0:3 * hh + 1, :]
            ext = lw_ref[3 * hh + 1:3 * hh + 2, :]
            ones = lw_ref[3 * hh + 2:3 * hh + 3, :]
            qa_ref[h] = pltpu.bitcast((q & own) | ones, BF16)
            ka_ref[h] = pltpu.bitcast((k & ~ext) | (kext & ext), BF16)
            vt_ref[h, 0:HEAD_DIM, :] = vt[hh * HEAD_DIM:(hh + 1) * HEAD_DIM]
            vt_ref[h, HEAD_DIM:, :] = jnp.ones((FOX_SUM_ROWS, seq), BF16)

    heads = FOX_PAIRS_PER_STEP * FOX_HEADS_PER_PAIR
    units = [(i, h) for i in range(seq // tq) for h in range(heads)]

    def scores(u):
        i, h = units[u]
        slot = u % FOX_SCORE_SLOTS
        r0, r1 = i * tq, (i + 1) * tq
        qa = qa_ref[h, r0:r1, :]
        s_d = lax.dot_general(ka_ref[h, r0:r1, :], qa, _NT_DIMS, preferred_element_type=F32)
        s_ref[slot, r0:r1, :] = jnp.where(causal, s_d, NEG_INF)
        if i > 0:
            s_ref[slot, 0:r0, :] = lax.dot_general(ka_ref[h, 0:r0, :], qa, _NT_DIMS,
                                                   preferred_element_type=F32)

    def softmax(u):
        i, h = units[u]
        slot = u % FOX_SCORE_SLOTS
        r0, r1 = i * tq, (i + 1) * tq
        head = pl.program_id(1) * heads + h
        cbq = cbt_ref[pl.ds(head, 1), r0:r1]
        m = jnp.max(s_ref[slot, 0:r1, :], axis=0, keepdims=True)
        shift = cbq - (m + cbq)
        p_ref[u % FOX_PROB_SLOTS, 0:r1, :] = jnp.exp2(s_ref[slot, 0:r1, :] + shift).astype(BF16)

    def weighted_values(u):
        i, h = units[u]
        r0, r1 = i * tq, (i + 1) * tq
        ot = jnp.dot(vt_ref[h, :, 0:r1], p_ref[u % FOX_PROB_SLOTS, 0:r1, :], preferred_element_type=F32)
        out = ot[0:HEAD_DIM] * (1.0 / ot[HEAD_DIM:HEAD_DIM + 1])
        o_ref[h * HEAD_DIM:(h + 1) * HEAD_DIM, r0:r1] = out.astype(o_ref.dtype)

    ahead = FOX_SCORE_SLOTS - 1
    for u in range(min(ahead, len(units))):
        scores(u)
    for u in range(len(units)):
        if u + ahead < len(units):
            scores(u + ahead)
        softmax(u)
        if u >= FOX_PV_LAG:
            weighted_values(u - FOX_PV_LAG)
    for u in range(len(units) - FOX_PV_LAG, len(units)):
        weighted_values(u)


def _fox_attention(qkv, cumt, kext):
    b, s, _ = qkv.shape
    heads = FOX_PAIRS_PER_STEP * FOX_HEADS_PER_PAIR
    width = FOX_PAIRS_PER_STEP * LANES
    lw = jnp.asarray(_fox_lane_words())
    ncol = D_MODEL // width
    return pl.pallas_call(
        _fox_kernel,
        grid=(b, N_HEADS // heads),
        in_specs=[pl.BlockSpec((None, s, width), lambda bi, hp: (bi, 0, hp)),
                  pl.BlockSpec((None, s, width), lambda bi, hp: (bi, 0, ncol + hp)),
                  pl.BlockSpec((None, s, width), lambda bi, hp: (bi, 0, 2 * ncol + hp)),
                  pl.BlockSpec((None, s, width), lambda bi, hp: (bi, 0, hp)),
                  pl.BlockSpec((None, N_HEADS, s), lambda bi, hp: (bi, 0, 0)),
                  pl.BlockSpec((SUBLANES, LANES), lambda bi, hp: (0, 0))],
        out_specs=pl.BlockSpec((None, heads * HEAD_DIM, s), lambda bi, hp: (bi, hp, 0)),
        out_shape=jax.ShapeDtypeStruct((b, D_MODEL, s), BF16),
        scratch_shapes=[pltpu.VMEM((heads, s, LANES), BF16), pltpu.VMEM((heads, s, LANES), BF16),
                        pltpu.VMEM((heads, HEAD_DIM + FOX_SUM_ROWS, s), BF16),
                        pltpu.VMEM((FOX_SCORE_SLOTS, s, FOX_TQ), F32),
                        pltpu.VMEM((FOX_PROB_SLOTS, s, FOX_TQ), BF16)],
        compiler_params=_params("arbitrary", "arbitrary"),
        name="fox_attn",
    )(qkv, qkv, qkv, kext, cumt, lw)


def _deepnorm(xres, y, g, b, alpha):
    z = alpha * xres + y
    mu = jnp.mean(z, axis=-1, keepdims=True)
    zc = z - mu
    var = jnp.mean(zc * zc, axis=-1, keepdims=True)
    return zc * lax.rsqrt(var + LN_EPS) * g + b


LN_SUBTILES = 2


def _oproj_ln_kernel(alpha, ot_ref, xres_ref, w_ref, g_ref, b_ref, xo_ref, xob_ref, wb_ref):
    @pl.when(pl.program_id(0) == 0)
    def _():
        wb_ref[...] = w_ref[...].astype(BF16)

    ts = xres_ref.shape[0] // LN_SUBTILES
    ys = [lax.dot_general(ot_ref[:, r * ts:(r + 1) * ts], wb_ref[...], _TN_DIMS,
                          preferred_element_type=F32) for r in range(LN_SUBTILES)]
    for r, y in enumerate(ys):
        rows = slice(r * ts, (r + 1) * ts)
        out = _deepnorm(xres_ref[rows, :], y, g_ref[...], b_ref[...], alpha)
        xo_ref[rows, :] = out
        xob_ref[rows, :] = out.astype(BF16)


def _oproj_ln(ot, xres, w, layer, g, b, alpha, tm, name):
    m, n = xres.shape
    k = w.shape[1]
    per_seq = ot.shape[2] // tm
    return pl.pallas_call(
        functools.partial(_oproj_ln_kernel, alpha),
        grid=(m // tm,),
        in_specs=[pl.BlockSpec((None, k, tm), lambda i: (i // per_seq, 0, i % per_seq)),
                  pl.BlockSpec((tm, n), lambda i: (i, 0)),
                  pl.BlockSpec((None, k, n), lambda i: (layer, 0, 0), pipeline_mode=pl.Buffered(1)),
                  pl.BlockSpec((1, n), lambda i: (0, 0)),
                  pl.BlockSpec((1, n), lambda i: (0, 0))],
        out_specs=[pl.BlockSpec((tm, n), lambda i: (i, 0)),
                   pl.BlockSpec((tm, n), lambda i: (i, 0))],
        out_shape=[jax.ShapeDtypeStruct((m, n), F32), jax.ShapeDtypeStruct((m, n), BF16)],
        scratch_shapes=[pltpu.VMEM((k, n), BF16)],
        compiler_params=_params("arbitrary"),
        name=name,
    )(ot, xres, w, g, b)


CONV_WIDTH = 3
FFN_TF = 512
FFN_UP_TM = 1024


def _ffn_up_kernel(tiles_per_seq, x_ref, wg_ref, wu_ref, cwg_ref, cwu_ref, cbg_ref, cbu_ref,
                   a_ref, wb_ref, h_ref):
    i = pl.program_id(1)
    tm = x_ref.shape[0]
    tf = a_ref.shape[1]
    halo = SUBLANES

    @pl.when(i == 0)
    def _():
        wb_ref[:, 0:tf] = wg_ref[...].astype(BF16)
        wb_ref[:, tf:2 * tf] = wu_ref[...].astype(BF16)

    @pl.when(i % tiles_per_seq == 0)
    def _():
        h_ref[0:halo, :] = jnp.zeros((halo, h_ref.shape[1]), F32)

    h_ref[halo:halo + tm, :] = jnp.dot(x_ref[...], wb_ref[...], preferred_element_type=F32)

    def conv(cols, cw_ref, cb_ref):
        out = cb_ref[...]
        for tap in range(CONV_WIDTH):
            start = halo - (CONV_WIDTH - 1) + tap
            out = out + cw_ref[tap:tap + 1, :] * h_ref[start:start + tm, cols]
        return out

    g = conv(slice(0, tf), cwg_ref, cbg_ref)
    u = conv(slice(tf, 2 * tf), cwu_ref, cbu_ref)
    a_ref[...] = (g * jax.nn.sigmoid(g) * u).astype(a_ref.dtype)
    h_ref[0:halo, :] = h_ref[tm:tm + halo, :]


def _ffn_up(xb, w_up, conv_w, conv_b, layer, seq):
    m, k = xb.shape
    tm, tf = FFN_UP_TM, FFN_TF
    nf = D_FF // tf
    cb2 = conv_b.reshape(conv_b.shape[0], 1, 2 * D_FF)
    return pl.pallas_call(
        functools.partial(_ffn_up_kernel, seq // tm),
        grid=(nf, m // tm),
        in_specs=[pl.BlockSpec((tm, k), lambda j, i: (i, 0)),
                  pl.BlockSpec((None, k, tf), lambda j, i: (layer, 0, j)),
                  pl.BlockSpec((None, k, tf), lambda j, i: (layer, 0, j + nf)),
                  pl.BlockSpec((None, CONV_WIDTH, tf), lambda j, i: (layer, 0, j)),
                  pl.BlockSpec((None, CONV_WIDTH, tf), lambda j, i: (layer, 0, j + nf)),
                  pl.BlockSpec((None, 1, tf), lambda j, i: (layer, 0, j)),
                  pl.BlockSpec((None, 1, tf), lambda j, i: (layer, 0, j + nf))],
        out_specs=pl.BlockSpec((tm, tf), lambda j, i: (i, j)),
        out_shape=jax.ShapeDtypeStruct((m, D_FF), BF16),
        scratch_shapes=[pltpu.VMEM((k, 2 * tf), BF16), pltpu.VMEM((tm + 2 * SUBLANES, 2 * tf), F32)],
        compiler_params=_params("arbitrary", "arbitrary"),
        name="ffn_up",
    )(xb, w_up, w_up, conv_w, conv_w, cb2, cb2)


FFN_DOWN_TM = 256
FFN_DOWN_WCHUNK = 512


def _ffn_down_ln_kernel(alpha, n_load, emit_bf16, a_ref, xres_ref, w_ref, g_ref, b_ref, *rest):
    if emit_bf16:
        xo_ref, xob_ref, wb_ref = rest
    else:
        (xo_ref, wb_ref), xob_ref = rest, None
    step = pl.program_id(0)
    rows = w_ref.shape[0]

    @pl.when(step < n_load)
    def _():
        off = pl.multiple_of(step * rows, rows)
        wb_ref[pl.ds(off, rows), :] = w_ref[...].astype(BF16)

    @pl.when(step >= n_load)
    def _():
        y = jnp.dot(a_ref[...], wb_ref[...], preferred_element_type=F32)
        out = _deepnorm(xres_ref[...], y, g_ref[...], b_ref[...], alpha)
        xo_ref[...] = out
        if emit_bf16:
            xob_ref[...] = out.astype(BF16)


def _ffn_down_ln(a, xres, w, layer, g, b, alpha, emit_bf16, name):
    m, k = a.shape
    n = w.shape[2]
    tm, rows = FFN_DOWN_TM, FFN_DOWN_WCHUNK
    n_load = k // rows
    tile = lambda s: (jnp.maximum(s - n_load, 0), 0)
    out_specs = [pl.BlockSpec((tm, n), tile)]
    out_shape = [jax.ShapeDtypeStruct((m, n), F32)]
    if emit_bf16:
        out_specs.append(pl.BlockSpec((tm, n), tile))
        out_shape.append(jax.ShapeDtypeStruct((m, n), BF16))
    return pl.pallas_call(
        functools.partial(_ffn_down_ln_kernel, alpha, n_load, emit_bf16),
        grid=(n_load + m // tm,),
        in_specs=[pl.BlockSpec((tm, k), tile),
                  pl.BlockSpec((tm, n), tile),
                  pl.BlockSpec((None, rows, n), lambda s: (layer, jnp.minimum(s, n_load - 1), 0)),
                  pl.BlockSpec((1, n), lambda s: (0, 0)),
                  pl.BlockSpec((1, n), lambda s: (0, 0))],
        out_specs=out_specs,
        out_shape=out_shape,
        scratch_shapes=[pltpu.VMEM((k, n), BF16)],
        compiler_params=_params("arbitrary"),
        name=name,
    )(a, xres, w, g, b)


def kernel(x, a_w_qkv, a_w_o, a_sinks, rel_bias, b_w_qkvf, b_f_bias, b_w_o,
           ffn_w_up, ffn_conv_w, ffn_conv_b, ffn_w_down, ln_g, ln_b):
    batch, seq, d = x.shape
    depth = ffn_w_up.shape[0]
    alpha = (2 * depth) ** 0.25
    m = batch * seq
    xf = x.reshape(m, d)
    xb = None
    biasmask = _swa_bias(rel_bias)
    for layer in range(depth):
        j = layer // 2
        g = ln_g[layer].reshape(2, 1, d)
        bt = ln_b[layer].reshape(2, 1, d)
        if layer % 2 == 0:
            n_qkv = D_MODEL + 2 * A_KV_DIM
            src = xf if xb is None else xb
            qkv = _ws_matmul(src, a_w_qkv, j, n_qkv, n_qkv // SWA_QKV_COL_TILES, SWA_QKV_TM, "swa_qkv",
                             slab_batch=(batch, seq))
            o = _swa_attention(qkv, a_sinks[j], biasmask)
            w_o = a_w_o
        else:
            w_t = jnp.swapaxes(b_w_qkvf, 1, 2)
            qkv = _ws_matmul(xb, w_t, j, 3 * D_MODEL, FOX_QKV_TN, FOX_QKV_TM, "fox_qkv",
                             w_is_transposed=True)
            fb3 = jnp.tile(b_f_bias[j].reshape(N_HEADS, 1), (FOX_PIECES, 1))
            cumt, kext = _fox_gate(xb, w_t, j, fb3, batch, seq)
            o = _fox_attention(qkv.reshape(batch, seq, 3 * D_MODEL), cumt, kext)
            w_o = b_w_o
        xf, xb = _oproj_ln(o, xf, w_o, j, g[0], bt[0], alpha, OPROJ_TM, "oproj_ln")
        a = _ffn_up(xb, ffn_w_up, ffn_conv_w, ffn_conv_b, layer, seq)
        last = layer == depth - 1
        res = _ffn_down_ln(a, xf, ffn_w_down, layer, g[1], bt[1], alpha, not last, "ffn_down_ln")
        xf, xb = (res[0], None) if last else res
    return xf.reshape(batch, seq, d)
```

```python
import functools
import math

import jax
import jax.numpy as jnp
import numpy as np
from jax import lax
from jax.experimental import pallas as pl
from jax.experimental.pallas import tpu as pltpu

D_MODEL = 2048
HEAD_DIM = 64
BLOCK = 128
N_HEADS = D_MODEL // HEAD_DIM
A_KV_HEADS = 4
A_GROUP = N_HEADS // A_KV_HEADS
A_KV_DIM = A_KV_HEADS * HEAD_DIM
REL_BUCKETS = 32
REL_MAX_DIST = 128
D_FF = 5632
LN_EPS = 1e-5
SCALE = HEAD_DIM ** -0.5

SUBLANES = 8
LANES = 128

BF16 = jnp.bfloat16
F32 = jnp.float32
NEG_INF = float("-inf")

_NT_DIMS = (((1,), (1,)), ((), ()))


def _params(*sem):
    return pltpu.CompilerParams(dimension_semantics=sem)


LOG2E = math.log2(math.e)
Q_SCALE = SCALE * LOG2E
SWA_QKV_COL_TILES = 2
SWA_QKV_TM = 1024
FOX_QKV_TN = 1024
FOX_QKV_TM = 2048
OPROJ_TM = 512


def _ws_matmul_kernel(w_is_transposed, x_ref, w_ref, o_ref, wb_ref):
    @pl.when(pl.program_id(1) == 0)
    def _():
        wb_ref[...] = w_ref[...].astype(BF16)

    x = x_ref[...].astype(BF16)
    if w_is_transposed:
        y = lax.dot_general(x, wb_ref[...], _NT_DIMS, preferred_element_type=F32)
    else:
        y = jnp.dot(x, wb_ref[...], preferred_element_type=F32)
    tn = y.shape[1]
    col = pl.program_id(0) * tn + lax.broadcasted_iota(jnp.int32, (1, tn), 1)
    y = (y * jnp.where(col < D_MODEL, Q_SCALE, 1.0)).astype(o_ref.dtype)
    if len(o_ref.shape) == 2:
        o_ref[...] = y
    else:
        for c in range(o_ref.shape[0]):
            o_ref[c] = y[:, c * LANES:(c + 1) * LANES]


def _ws_matmul(x, w, layer, n_out, tn, tm, name, w_is_transposed=False, slab_batch=None):
    m, k = x.shape
    if w_is_transposed:
        w_spec = pl.BlockSpec((None, tn, k), lambda j, i: (layer, j, 0))
        wb_shape = (tn, k)
    else:
        w_spec = pl.BlockSpec((None, k, tn), lambda j, i: (layer, 0, j))
        wb_shape = (k, tn)
    if slab_batch is None:
        out_spec = pl.BlockSpec((tm, tn), lambda j, i: (i, j))
        out_shape = jax.ShapeDtypeStruct((m, n_out), BF16)
    else:
        batch, seq = slab_batch
        per_seq = seq // tm
        out_spec = pl.BlockSpec((None, tn // LANES, tm, LANES),
                                lambda j, i: (i // per_seq, j, i % per_seq, 0))
        out_shape = jax.ShapeDtypeStruct((batch, n_out // LANES, seq, LANES), BF16)
    return pl.pallas_call(
        functools.partial(_ws_matmul_kernel, w_is_transposed),
        grid=(n_out // tn, m // tm),
        in_specs=[pl.BlockSpec((tm, k), lambda j, i: (i, 0)), w_spec],
        out_specs=out_spec,
        out_shape=out_shape,
        scratch_shapes=[pltpu.VMEM(wb_shape, BF16)],
        compiler_params=_params("arbitrary", "arbitrary"),
        name=name,
    )(x, w)


def _proj_t_kernel(x_ref, w_ref, o_ref, wb_ref):
    @pl.when(pl.program_id(1) == 0)
    def _():
        wb_ref[...] = w_ref[...].astype(BF16)

    o_ref[...] = lax.dot_general(wb_ref[...], x_ref[...], _NT_DIMS,
                                 preferred_element_type=F32).astype(o_ref.dtype)


def _proj_t(x, w_t, layer, row0, n_rows, tn, batch, seq, name):
    m, k = x.shape
    return pl.pallas_call(
        _proj_t_kernel,
        grid=(n_rows // tn, batch),
        in_specs=[pl.BlockSpec((seq, k), lambda j, i: (i, 0)),
                  pl.BlockSpec((None, tn, k), lambda j, i: (layer, row0 // tn + j, 0))],
        out_specs=pl.BlockSpec((None, tn, seq), lambda j, i: (i, j, 0)),
        out_shape=jax.ShapeDtypeStruct((batch, n_rows, seq), BF16),
        scratch_shapes=[pltpu.VMEM((tn, k), BF16)],
        compiler_params=_params("arbitrary", "arbitrary"),
        name=name,
    )(x, w_t)


def _t5_bucket_map():
    qi = np.arange(BLOCK)[None, :]
    kj = np.arange(2 * BLOCK)[:, None]
    dist = BLOCK + qi - kj
    n = np.maximum(dist, 0)
    max_exact = REL_BUCKETS // 2
    nf = np.maximum(n, 1).astype(np.float32)
    large = max_exact + (np.log(nf / max_exact) / math.log(REL_MAX_DIST / max_exact)
                         * (REL_BUCKETS - max_exact)).astype(np.int32)
    large = np.minimum(large, REL_BUCKETS - 1)
    bucket = np.where(n < max_exact, n, large)
    visible = (dist >= 0) & (dist < BLOCK)
    return np.where(visible, bucket, -1).astype(np.int32)


def _swa_bias_kernel(rb_ref, bk_ref, o_ref):
    bk = bk_ref[...]
    key = lax.broadcasted_iota(jnp.int32, bk.shape, 0)

    def head(h, _):
        acc = jnp.full(bk.shape, NEG_INF, F32)
        for b in range(REL_BUCKETS):
            acc = jnp.where(bk == b, rb_ref[b, h] * LOG2E, acc)
        o_ref[0, h] = acc
        o_ref[1, h] = jnp.where(key < BLOCK, NEG_INF, acc)
        return 0

    lax.fori_loop(0, N_HEADS, head, 0)


def _swa_bias(rel_bias):
    bk = jnp.asarray(_t5_bucket_map())
    return pl.pallas_call(
        _swa_bias_kernel,
        in_specs=[pl.BlockSpec(memory_space=pltpu.SMEM),
                  pl.BlockSpec(memory_space=pltpu.VMEM)],
        out_specs=pl.BlockSpec(memory_space=pltpu.VMEM),
        out_shape=jax.ShapeDtypeStruct((2, N_HEADS, 2 * BLOCK, BLOCK), F32),
        name="swa_bias",
    )(rel_bias, bk)


_TN_DIMS = (((0,), (0,)), ((), ()))
SWA_SCORES_AHEAD = 2
SWA_BLOCKS_PER_STEP = 2


def _swa_kernel(sink_ref, q_ref, kp_ref, kc_ref, vp_ref, vc_ref, bm_ref, o_ref,
                qs_ref, s_ref, p_ref, linv_ref):
    units = [(qb, kh) for qb in range(SWA_BLOCKS_PER_STEP) for kh in range(A_KV_HEADS)]
    first_table = jnp.where(pl.program_id(1) == 0, 1, 0)

    def head_at(head):
        slab, half = divmod(head, LANES // HEAD_DIM)
        return slab, slice(half * HEAD_DIM, (half + 1) * HEAD_DIM)

    def keys_values(prev_ref, cur_ref, qb, kh):
        slab, lanes = head_at(kh)
        if qb == 0:
            return jnp.concatenate([prev_ref[slab, :, lanes], cur_ref[slab, 0:BLOCK, lanes]], axis=0)
        return cur_ref[slab, (qb - 1) * BLOCK:(qb + 1) * BLOCK, lanes]

    def scores(u):
        qb, kh = units[u]
        for g in range(A_GROUP):
            slab, lanes = head_at(kh * A_GROUP + g)
            qs_ref[u, g * BLOCK:(g + 1) * BLOCK, :] = q_ref[slab, qb * BLOCK:(qb + 1) * BLOCK, lanes]
        k2 = keys_values(kp_ref, kc_ref, qb, kh)
        s = lax.dot_general(k2, qs_ref[u], _NT_DIMS, preferred_element_type=F32)
        for g in range(A_GROUP):
            s_ref[u, g] = s[:, g * BLOCK:(g + 1) * BLOCK]

    def softmax(u):
        qb, kh = units[u]
        table = first_table if qb == 0 else 0
        for g in range(A_GROUP):
            h = kh * A_GROUP + g
            cols = slice(g * BLOCK, (g + 1) * BLOCK)
            s = s_ref[u, g] + bm_ref[table, h]
            sink = sink_ref[h] * LOG2E
            m = jnp.maximum(jnp.max(s, axis=0, keepdims=True), sink)
            p = jnp.exp2(s - m)
            denom = jnp.sum(p, axis=0, keepdims=True) + jnp.exp2(sink - m)
            p_ref[u, g // 2, :, (g % 2) * BLOCK:(g % 2 + 1) * BLOCK] = p.astype(BF16)
            linv_ref[u, :, cols] = 1.0 / denom

    def weighted_values(u):
        qb, kh = units[u]
        v2 = keys_values(vp_ref, vc_ref, qb, kh)
        for c in range(A_GROUP // 2):
            cols = slice(2 * c * BLOCK, 2 * (c + 1) * BLOCK)
            ot = (lax.dot_general(v2, p_ref[u, c], _TN_DIMS, preferred_element_type=F32)
                  * linv_ref[u, :, cols])
            for gg in range(2):
                h = kh * A_GROUP + 2 * c + gg
                o_ref[h * HEAD_DIM:(h + 1) * HEAD_DIM, qb * BLOCK:(qb + 1) * BLOCK] = (
                    ot[:, gg * BLOCK:(gg + 1) * BLOCK].astype(o_ref.dtype))

    for u in range(min(SWA_SCORES_AHEAD, len(units))):
        scores(u)
    for u in range(len(units)):
        if u + SWA_SCORES_AHEAD < len(units):
            scores(u + SWA_SCORES_AHEAD)
        softmax(u)
        weighted_values(u)


def _swa_attention(qkv, sinks, biasmask):
    b, _, s, _ = qkv.shape
    rows = SWA_BLOCKS_PER_STEP * BLOCK
    gq = A_GROUP * BLOCK
    n_units = SWA_BLOCKS_PER_STEP * A_KV_HEADS
    q_slabs = D_MODEL // LANES
    kv_slabs = A_KV_DIM // LANES
    kcol = q_slabs // kv_slabs
    prev = lambda n: jnp.maximum(SWA_BLOCKS_PER_STEP * n - 1, 0)
    return pl.pallas_call(
        _swa_kernel,
        grid=(b, s // rows),
        in_specs=[pl.BlockSpec(memory_space=pltpu.SMEM),
                  pl.BlockSpec((None, q_slabs, rows, LANES), lambda bi, n: (bi, 0, n, 0)),
                  pl.BlockSpec((None, kv_slabs, BLOCK, LANES), lambda bi, n: (bi, kcol, prev(n), 0)),
                  pl.BlockSpec((None, kv_slabs, rows, LANES), lambda bi, n: (bi, kcol, n, 0)),
                  pl.BlockSpec((None, kv_slabs, BLOCK, LANES), lambda bi, n: (bi, kcol + 1, prev(n), 0)),
                  pl.BlockSpec((None, kv_slabs, rows, LANES), lambda bi, n: (bi, kcol + 1, n, 0)),
                  pl.BlockSpec((2, N_HEADS, 2 * BLOCK, BLOCK), lambda bi, n: (0, 0, 0, 0),
                               pipeline_mode=pl.Buffered(1))],
        out_specs=pl.BlockSpec((None, D_MODEL, rows), lambda bi, n: (bi, 0, n)),
        out_shape=jax.ShapeDtypeStruct((b, D_MODEL, s), BF16),
        scratch_shapes=[pltpu.VMEM((n_units, gq, HEAD_DIM), BF16),
                        pltpu.VMEM((n_units, A_GROUP, 2 * BLOCK, BLOCK), F32),
                        pltpu.VMEM((n_units, A_GROUP // 2, 2 * BLOCK, 2 * BLOCK), BF16),
                        pltpu.VMEM((n_units, 1, gq), F32)],
        compiler_params=_params("arbitrary", "arbitrary"),
        name="swa_attn",
    )(sinks, qkv, qkv, qkv, qkv, qkv, biasmask)


FOX_CUM_BLOCK = 256
FOX_TQ = 256
FOX_HEADS_PER_PAIR = LANES // HEAD_DIM
FOX_PAIRS_PER_STEP = 2
FOX_PIECES = 3
FOX_SUM_ROWS = 16
FOX_SCORE_SLOTS = 4
FOX_PV_LAG = 1
FOX_PROB_SLOTS = FOX_PV_LAG + 1
BF16_ONE_PAIR = 0x3F803F80


def _fox_ext_lane(head, piece):
    pair, odd = divmod(head, FOX_HEADS_PER_PAIR)
    return pair * LANES + (1 - odd) * HEAD_DIM + piece


def _fox_placement():
    pm = np.zeros((FOX_PIECES * N_HEADS, D_MODEL), np.float32)
    for t in range(FOX_PIECES):
        for h in range(N_HEADS):
            pm[t * N_HEADS + h, _fox_ext_lane(h, t)] = -1.0
    return pm


def _fox_lane_words():
    w = np.zeros((SUBLANES, LANES), np.uint32)
    for hh in range(FOX_HEADS_PER_PAIR):
        own = np.zeros(LANES, bool)
        own[hh * HEAD_DIM:(hh + 1) * HEAD_DIM] = True
        ext = np.zeros(LANES, bool)
        first = _fox_ext_lane(hh, 0)
        ext[first:first + FOX_PIECES] = True
        w[3 * hh + 0] = np.where(own, 0xFFFFFFFF, 0)
        w[3 * hh + 1] = np.where(ext, 0xFFFFFFFF, 0)
        w[3 * hh + 2] = np.where(ext, BF16_ONE_PAIR, 0)
    return w


def _log_sigmoid(x):
    return jnp.minimum(x, 0.0) - jnp.log1p(jnp.exp(-jnp.abs(x)))


def _round_bf16(x):
    return x.astype(BF16).astype(F32)


def _fox_gate_kernel(x_ref, wf_ref, fb_ref, pm_ref, cumt_ref, kext_ref, w3_ref):
    @pl.when(pl.program_id(0) == 0)
    def _():
        w = wf_ref[...].astype(BF16)
        for t in range(FOX_PIECES):
            w3_ref[t * N_HEADS:(t + 1) * N_HEADS, :] = w

    fz = lax.dot_general(w3_ref[...], x_ref[...], _NT_DIMS, preferred_element_type=F32) + fb_ref[...]
    lf = _log_sigmoid(fz)
    cb = FOX_CUM_BLOCK
    row = lax.broadcasted_iota(jnp.int32, (cb, cb), 0)
    col = lax.broadcasted_iota(jnp.int32, (cb, cb), 1)
    tri = jnp.where(row <= col, 1.0, 0.0).astype(BF16)
    tri3 = jnp.concatenate([tri] * FOX_PIECES, axis=0)
    group = lax.broadcasted_iota(jnp.int32, (lf.shape[0], cb), 0) // N_HEADS
    pm = pm_ref[...]
    n_blocks = lf.shape[1] // cb
    local = []
    for blk in range(n_blocks):
        x = lf[:, blk * cb:(blk + 1) * cb]
        hi = _round_bf16(x)
        r1 = x - hi
        mid = _round_bf16(r1)
        pieces = jnp.concatenate([hi, mid, r1 - mid], axis=1).astype(BF16)
        local.append(jnp.dot(pieces, tri3, preferred_element_type=F32))
    carry = jnp.zeros((lf.shape[0], 1), F32)
    for blk in range(n_blocks):
        cols = slice(blk * cb, (blk + 1) * cb)
        c = local[blk] + carry
        carry = carry + local[blk][:, cb - 1:cb]
        c2 = c * LOG2E
        cumt_ref[:, cols] = c2[:N_HEADS]
        chi = _round_bf16(c2)
        cr1 = c2 - chi
        cmid = _round_bf16(cr1)
        piece = jnp.where(group == 0, chi, jnp.where(group == 1, cmid, cr1 - cmid)).astype(BF16)
        kext_ref[cols, :] = lax.dot_general(piece, pm, _TN_DIMS, preferred_element_type=F32).astype(BF16)


def _fox_gate(xb, w_qkvf_t, layer, fb3, batch, seq):
    pm = jnp.asarray(_fox_placement(), BF16)
    n3 = FOX_PIECES * N_HEADS
    gate_row = 3 * D_MODEL // N_HEADS
    return pl.pallas_call(
        _fox_gate_kernel,
        grid=(batch,),
        in_specs=[pl.BlockSpec((seq, D_MODEL), lambda bi: (bi, 0)),
                  pl.BlockSpec((None, N_HEADS, D_MODEL), lambda bi: (layer, gate_row, 0)),
                  pl.BlockSpec((n3, 1), lambda bi: (0, 0)),
                  pl.BlockSpec((n3, D_MODEL), lambda bi: (0, 0))],
        out_specs=[pl.BlockSpec((None, N_HEADS, seq), lambda bi: (bi, 0, 0)),
                   pl.BlockSpec((None, seq, D_MODEL), lambda bi: (bi, 0, 0))],
        out_shape=[jax.ShapeDtypeStruct((batch, N_HEADS, seq), F32),
                   jax.ShapeDtypeStruct((batch, seq, D_MODEL), BF16)],
        scratch_shapes=[pltpu.VMEM((n3, D_MODEL), BF16)],
        compiler_params=_params("arbitrary"),
        name="fox_gate",
    )(xb, w_qkvf_t, fb3, pm)


def _fox_kernel(q_ref, k_ref, vt_in_ref, kext_ref, cbt_ref, lw_ref, o_ref, qa_ref, ka_ref, vt_ref,
                s_ref, p_ref):
    seq = q_ref.shape[0]
    tq = FOX_TQ
    key = lax.broadcasted_iota(jnp.int32, (tq, tq), 0)
    qry = lax.broadcasted_iota(jnp.int32, (tq, tq), 1)
    causal = key <= qry
    words = lambda x: pltpu.bitcast(x, jnp.uint32)
    for pair in range(FOX_PAIRS_PER_STEP):
        lanes = slice(pair * LANES, (pair + 1) * LANES)
        q = words(q_ref[:, lanes])
        k = words(k_ref[:, lanes])
        kext = words(kext_ref[:, lanes])
        for hh in range(FOX_HEADS_PER_PAIR):
            h = pair * FOX_HEADS_PER_PAIR + hh
            own = lw_ref[3 * hh + 0:3 * hh + 1, :]
            ext = lw_ref[3 * hh + 1:3 * hh + 2, :]
            ones = lw_ref[3 * hh + 2:3 * hh + 3, :]
            qa_ref[h] = pltpu.bitcast((q & own) | ones, BF16)
            ka_ref[h] = pltpu.bitcast((k & ~ext) | (kext & ext), BF16)
            vt_ref[h, 0:HEAD_DIM, :] = vt_in_ref[h * HEAD_DIM:(h + 1) * HEAD_DIM, :]
            vt_ref[h, HEAD_DIM:, :] = jnp.ones((FOX_SUM_ROWS, seq), BF16)

    heads = FOX_PAIRS_PER_STEP * FOX_HEADS_PER_PAIR
    units = [(i, h) for i in range(seq // tq) for h in range(heads)]

    def scores(u):
        i, h = units[u]
        slot = u % FOX_SCORE_SLOTS
        r0, r1 = i * tq, (i + 1) * tq
        qa = qa_ref[h, r0:r1, :]
        s_d = lax.dot_general(ka_ref[h, r0:r1, :], qa, _NT_DIMS, preferred_element_type=F32)
        s_ref[slot, r0:r1, :] = jnp.where(causal, s_d, NEG_INF)
        if i > 0:
            s_ref[slot, 0:r0, :] = lax.dot_general(ka_ref[h, 0:r0, :], qa, _NT_DIMS,
                                                   preferred_element_type=F32)

    def softmax(u):
        i, h = units[u]
        slot = u % FOX_SCORE_SLOTS
        r0, r1 = i * tq, (i + 1) * tq
        head = pl.program_id(1) * heads + h
        cbq = cbt_ref[pl.ds(head, 1), r0:r1]
        m = jnp.max(s_ref[slot, 0:r1, :], axis=0, keepdims=True)
        shift = cbq - (m + cbq)
        p_ref[u % FOX_PROB_SLOTS, 0:r1, :] = jnp.exp2(s_ref[slot, 0:r1, :] + shift).astype(BF16)

    def weighted_values(u):
        i, h = units[u]
        r0, r1 = i * tq, (i + 1) * tq
        ot = jnp.dot(vt_ref[h, :, 0:r1], p_ref[u % FOX_PROB_SLOTS, 0:r1, :], preferred_element_type=F32)
        out = ot[0:HEAD_DIM] * (1.0 / ot[HEAD_DIM:HEAD_DIM + 1])
        o_ref[h * HEAD_DIM:(h + 1) * HEAD_DIM, r0:r1] = out.astype(o_ref.dtype)

    ahead = FOX_SCORE_SLOTS - 1
    for u in range(min(ahead, len(units))):
        scores(u)
    for u in range(len(units)):
        if u + ahead < len(units):
            scores(u + ahead)
        softmax(u)
        if u >= FOX_PV_LAG:
            weighted_values(u - FOX_PV_LAG)
    for u in range(len(units) - FOX_PV_LAG, len(units)):
        weighted_values(u)


def _fox_attention(qk, vt, cumt, kext):
    b, s, _ = qk.shape
    heads = FOX_PAIRS_PER_STEP * FOX_HEADS_PER_PAIR
    width = FOX_PAIRS_PER_STEP * LANES
    lw = jnp.asarray(_fox_lane_words())
    ncol = D_MODEL // width
    return pl.pallas_call(
        _fox_kernel,
        grid=(b, N_HEADS // heads),
        in_specs=[pl.BlockSpec((None, s, width), lambda bi, hp: (bi, 0, hp)),
                  pl.BlockSpec((None, s, width), lambda bi, hp: (bi, 0, ncol + hp)),
                  pl.BlockSpec((None, heads * HEAD_DIM, s), lambda bi, hp: (bi, hp, 0)),
                  pl.BlockSpec((None, s, width), lambda bi, hp: (bi, 0, hp)),
                  pl.BlockSpec((None, N_HEADS, s), lambda bi, hp: (bi, 0, 0)),
                  pl.BlockSpec((SUBLANES, LANES), lambda bi, hp: (0, 0))],
        out_specs=pl.BlockSpec((None, heads * HEAD_DIM, s), lambda bi, hp: (bi, hp, 0)),
        out_shape=jax.ShapeDtypeStruct((b, D_MODEL, s), BF16),
        scratch_shapes=[pltpu.VMEM((heads, s, LANES), BF16), pltpu.VMEM((heads, s, LANES), BF16),
                        pltpu.VMEM((heads, HEAD_DIM + FOX_SUM_ROWS, s), BF16),
                        pltpu.VMEM((FOX_SCORE_SLOTS, s, FOX_TQ), F32),
                        pltpu.VMEM((FOX_PROB_SLOTS, s, FOX_TQ), BF16)],
        compiler_params=_params("arbitrary", "arbitrary"),
        name="fox_attn",
    )(qk, qk, vt, kext, cumt, lw)


def _deepnorm(xres, y, g, b, alpha):
    z = alpha * xres + y
    mu = jnp.mean(z, axis=-1, keepdims=True)
    zc = z - mu
    var = jnp.mean(zc * zc, axis=-1, keepdims=True)
    return zc * lax.rsqrt(var + LN_EPS) * g + b


LN_SUBTILES = 2


def _oproj_ln_kernel(alpha, ot_ref, xres_ref, w_ref, g_ref, b_ref, xo_ref, xob_ref, wb_ref):
    @pl.when(pl.program_id(0) == 0)
    def _():
        wb_ref[...] = w_ref[...].astype(BF16)

    ts = xres_ref.shape[0] // LN_SUBTILES
    ys = [lax.dot_general(ot_ref[:, r * ts:(r + 1) * ts], wb_ref[...], _TN_DIMS,
                          preferred_element_type=F32) for r in range(LN_SUBTILES)]
    for r, y in enumerate(ys):
        rows = slice(r * ts, (r + 1) * ts)
        out = _deepnorm(xres_ref[rows, :], y, g_ref[...], b_ref[...], alpha)
        xo_ref[rows, :] = out
        xob_ref[rows, :] = out.astype(BF16)


def _oproj_ln(ot, xres, w, layer, g, b, alpha, tm, name):
    m, n = xres.shape
    k = w.shape[1]
    per_seq = ot.shape[2] // tm
    return pl.pallas_call(
        functools.partial(_oproj_ln_kernel, alpha),
        grid=(m // tm,),
        in_specs=[pl.BlockSpec((None, k, tm), lambda i: (i // per_seq, 0, i % per_seq)),
                  pl.BlockSpec((tm, n), lambda i: (i, 0)),
                  pl.BlockSpec((None, k, n), lambda i: (layer, 0, 0), pipeline_mode=pl.Buffered(1)),
                  pl.BlockSpec((1, n), lambda i: (0, 0)),
                  pl.BlockSpec((1, n), lambda i: (0, 0))],
        out_specs=[pl.BlockSpec((tm, n), lambda i: (i, 0)),
                   pl.BlockSpec((tm, n), lambda i: (i, 0))],
        out_shape=[jax.ShapeDtypeStruct((m, n), F32), jax.ShapeDtypeStruct((m, n), BF16)],
        scratch_shapes=[pltpu.VMEM((k, n), BF16)],
        compiler_params=_params("arbitrary"),
        name=name,
    )(ot, xres, w, g, b)


CONV_WIDTH = 3
FFN_TF = 512
FFN_UP_TM = 1024


def _ffn_up_kernel(tiles_per_seq, x_ref, wg_ref, wu_ref, cwg_ref, cwu_ref, cbg_ref, cbu_ref,
                   a_ref, wb_ref, h_ref):
    i = pl.program_id(1)
    tm = x_ref.shape[0]
    tf = a_ref.shape[1]
    halo = SUBLANES

    @pl.when(i == 0)
    def _():
        wb_ref[:, 0:tf] = wg_ref[...].astype(BF16)
        wb_ref[:, tf:2 * tf] = wu_ref[...].astype(BF16)

    @pl.when(i % tiles_per_seq == 0)
    def _():
        h_ref[0:halo, :] = jnp.zeros((halo, h_ref.shape[1]), F32)

    h_ref[halo:halo + tm, :] = jnp.dot(x_ref[...], wb_ref[...], preferred_element_type=F32)

    def conv(cols, cw_ref, cb_ref):
        out = cb_ref[...]
        for tap in range(CONV_WIDTH):
            start = halo - (CONV_WIDTH - 1) + tap
            out = out + cw_ref[tap:tap + 1, :] * h_ref[start:start + tm, cols]
        return out

    g = conv(slice(0, tf), cwg_ref, cbg_ref)
    u = conv(slice(tf, 2 * tf), cwu_ref, cbu_ref)
    a_ref[...] = (g * jax.nn.sigmoid(g) * u).astype(a_ref.dtype)
    h_ref[0:halo, :] = h_ref[tm:tm + halo, :]


def _ffn_up(xb, w_up, conv_w, conv_b, layer, seq):
    m, k = xb.shape
    tm, tf = FFN_UP_TM, FFN_TF
    nf = D_FF // tf
    cb2 = conv_b.reshape(conv_b.shape[0], 1, 2 * D_FF)
    return pl.pallas_call(
        functools.partial(_ffn_up_kernel, seq // tm),
        grid=(nf, m // tm),
        in_specs=[pl.BlockSpec((tm, k), lambda j, i: (i, 0)),
                  pl.BlockSpec((None, k, tf), lambda j, i: (layer, 0, j)),
                  pl.BlockSpec((None, k, tf), lambda j, i: (layer, 0, j + nf)),
                  pl.BlockSpec((None, CONV_WIDTH, tf), lambda j, i: (layer, 0, j)),
                  pl.BlockSpec((None, CONV_WIDTH, tf), lambda j, i: (layer, 0, j + nf)),
                  pl.BlockSpec((None, 1, tf), lambda j, i: (layer, 0, j)),
                  pl.BlockSpec((None, 1, tf), lambda j, i: (layer, 0, j + nf))],
        out_specs=pl.BlockSpec((tm, tf), lambda j, i: (i, j)),
        out_shape=jax.ShapeDtypeStruct((m, D_FF), BF16),
        scratch_shapes=[pltpu.VMEM((k, 2 * tf), BF16), pltpu.VMEM((tm + 2 * SUBLANES, 2 * tf), F32)],
        compiler_params=_params("arbitrary", "arbitrary"),
        name="ffn_up",
    )(xb, w_up, w_up, conv_w, conv_w, cb2, cb2)


FFN_DOWN_TM = 256
FFN_DOWN_WCHUNK = 512


def _ffn_down_ln_kernel(alpha, n_load, emit_bf16, a_ref, xres_ref, w_ref, g_ref, b_ref, *rest):
    if emit_bf16:
        xo_ref, xob_ref, wb_ref = rest
    else:
        (xo_ref, wb_ref), xob_ref = rest, None
    step = pl.program_id(0)
    rows = w_ref.shape[0]

    @pl.when(step < n_load)
    def _():
        off = pl.multiple_of(step * rows, rows)
        wb_ref[pl.ds(off, rows), :] = w_ref[...].astype(BF16)

    @pl.when(step >= n_load)
    def _():
        y = jnp.dot(a_ref[...], wb_ref[...], preferred_element_type=F32)
        out = _deepnorm(xres_ref[...], y, g_ref[...], b_ref[...], alpha)
        xo_ref[...] = out
        if emit_bf16:
            xob_ref[...] = out.astype(BF16)


def _ffn_down_ln(a, xres, w, layer, g, b, alpha, emit_bf16, name):
    m, k = a.shape
    n = w.shape[2]
    tm, rows = FFN_DOWN_TM, FFN_DOWN_WCHUNK
    n_load = k // rows
    tile = lambda s: (jnp.maximum(s - n_load, 0), 0)
    out_specs = [pl.BlockSpec((tm, n), tile)]
    out_shape = [jax.ShapeDtypeStruct((m, n), F32)]
    if emit_bf16:
        out_specs.append(pl.BlockSpec((tm, n), tile))
        out_shape.append(jax.ShapeDtypeStruct((m, n), BF16))
    return pl.pallas_call(
        functools.partial(_ffn_down_ln_kernel, alpha, n_load, emit_bf16),
        grid=(n_load + m // tm,),
        in_specs=[pl.BlockSpec((tm, k), tile),
                  pl.BlockSpec((tm, n), tile),
                  pl.BlockSpec((None, rows, n), lambda s: (layer, jnp.minimum(s, n_load - 1), 0)),
                  pl.BlockSpec((1, n), lambda s: (0, 0)),
                  pl.BlockSpec((1, n), lambda s: (0, 0))],
        out_specs=out_specs,
        out_shape=out_shape,
        scratch_shapes=[pltpu.VMEM((k, n), BF16)],
        compiler_params=_params("arbitrary"),
        name=name,
    )(a, xres, w, g, b)


def kernel(x, a_w_qkv, a_w_o, a_sinks, rel_bias, b_w_qkvf, b_f_bias, b_w_o,
           ffn_w_up, ffn_conv_w, ffn_conv_b, ffn_w_down, ln_g, ln_b):
    batch, seq, d = x.shape
    depth = ffn_w_up.shape[0]
    alpha = (2 * depth) ** 0.25
    m = batch * seq
    xf = x.reshape(m, d)
    xb = None
    biasmask = _swa_bias(rel_bias)
    for layer in range(depth):
        j = layer // 2
        g = ln_g[layer].reshape(2, 1, d)
        bt = ln_b[layer].reshape(2, 1, d)
        if layer % 2 == 0:
            n_qkv = D_MODEL + 2 * A_KV_DIM
            src = xf if xb is None else xb
            qkv = _ws_matmul(src, a_w_qkv, j, n_qkv, n_qkv // SWA_QKV_COL_TILES, SWA_QKV_TM, "swa_qkv",
                             slab_batch=(batch, seq))
            o = _swa_attention(qkv, a_sinks[j], biasmask)
            w_o = a_w_o
        else:
            w_t = jnp.swapaxes(b_w_qkvf, 1, 2)
            qk = _ws_matmul(xb, w_t, j, 2 * D_MODEL, FOX_QKV_TN, FOX_QKV_TM, "fox_qk",
                            w_is_transposed=True)
            vt = _proj_t(xb, w_t, j, 2 * D_MODEL, D_MODEL, FOX_QKV_TN, batch, seq, "fox_vt")
            fb3 = jnp.tile(b_f_bias[j].reshape(N_HEADS, 1), (FOX_PIECES, 1))
            cumt, kext = _fox_gate(xb, w_t, j, fb3, batch, seq)
            o = _fox_attention(qk.reshape(batch, seq, 2 * D_MODEL), vt, cumt, kext)
            w_o = b_w_o
        xf, xb = _oproj_ln(o, xf, w_o, j, g[0], bt[0], alpha, OPROJ_TM, "oproj_ln")
        a = _ffn_up(xb, ffn_w_up, ffn_conv_w, ffn_conv_b, layer, seq)
        last = layer == depth - 1
        res = _ffn_down_ln(a, xf, ffn_w_down, layer, g[1], bt[1], alpha, not last, "ffn_down_ln")
        xf, xb = (res[0], None) if last else res
    return xf.reshape(batch, seq, d)
```

```python
import functools
import math

import jax
import jax.numpy as jnp
import numpy as np
from jax import lax
from jax.experimental import pallas as pl
from jax.experimental.pallas import tpu as pltpu

D_MODEL = 2048
HEAD_DIM = 64
BLOCK = 128
N_HEADS = D_MODEL // HEAD_DIM
A_KV_HEADS = 4
A_GROUP = N_HEADS // A_KV_HEADS
A_KV_DIM = A_KV_HEADS * HEAD_DIM
REL_BUCKETS = 32
REL_MAX_DIST = 128
D_FF = 5632
LN_EPS = 1e-5
SCALE = HEAD_DIM ** -0.5

SUBLANES = 8
LANES = 128

BF16 = jnp.bfloat16
F32 = jnp.float32
NEG_INF = float("-inf")

_NT_DIMS = (((1,), (1,)), ((), ()))


def _params(*sem):
    return pltpu.CompilerParams(dimension_semantics=sem)


LOG2E = math.log2(math.e)
Q_SCALE = SCALE * LOG2E
SWA_QKV_COL_TILES = 2
SWA_QKV_TM = 1024
FOX_QKV_TN = 1024
FOX_QKV_TM = 2048
OPROJ_TM = 512


def _ws_matmul_kernel(w_is_transposed, x_ref, w_ref, o_ref, wb_ref):
    @pl.when(pl.program_id(1) == 0)
    def _():
        wb_ref[...] = w_ref[...].astype(BF16)

    x = x_ref[...].astype(BF16)
    if w_is_transposed:
        y = lax.dot_general(x, wb_ref[...], _NT_DIMS, preferred_element_type=F32)
    else:
        y = jnp.dot(x, wb_ref[...], preferred_element_type=F32)
    tn = y.shape[1]
    col = pl.program_id(0) * tn + lax.broadcasted_iota(jnp.int32, (1, tn), 1)
    y = (y * jnp.where(col < D_MODEL, Q_SCALE, 1.0)).astype(o_ref.dtype)
    if len(o_ref.shape) == 2:
        o_ref[...] = y
    else:
        for c in range(o_ref.shape[0]):
            o_ref[c] = y[:, c * LANES:(c + 1) * LANES]


def _ws_matmul(x, w, layer, n_out, tn, tm, name, w_is_transposed=False, slab_batch=None):
    m, k = x.shape
    if w_is_transposed:
        w_spec = pl.BlockSpec((None, tn, k), lambda j, i: (layer, j, 0))
        wb_shape = (tn, k)
    else:
        w_spec = pl.BlockSpec((None, k, tn), lambda j, i: (layer, 0, j))
        wb_shape = (k, tn)
    if slab_batch is None:
        out_spec = pl.BlockSpec((tm, tn), lambda j, i: (i, j))
        out_shape = jax.ShapeDtypeStruct((m, n_out), BF16)
    else:
        batch, seq = slab_batch
        per_seq = seq // tm
        out_spec = pl.BlockSpec((None, tn // LANES, tm, LANES),
                                lambda j, i: (i // per_seq, j, i % per_seq, 0))
        out_shape = jax.ShapeDtypeStruct((batch, n_out // LANES, seq, LANES), BF16)
    return pl.pallas_call(
        functools.partial(_ws_matmul_kernel, w_is_transposed),
        grid=(n_out // tn, m // tm),
        in_specs=[pl.BlockSpec((tm, k), lambda j, i: (i, 0)), w_spec],
        out_specs=out_spec,
        out_shape=out_shape,
        scratch_shapes=[pltpu.VMEM(wb_shape, BF16)],
        compiler_params=_params("arbitrary", "arbitrary"),
        name=name,
    )(x, w)


def _t5_bucket_map():
    qi = np.arange(BLOCK)[None, :]
    kj = np.arange(2 * BLOCK)[:, None]
    dist = BLOCK + qi - kj
    n = np.maximum(dist, 0)
    max_exact = REL_BUCKETS // 2
    nf = np.maximum(n, 1).astype(np.float32)
    large = max_exact + (np.log(nf / max_exact) / math.log(REL_MAX_DIST / max_exact)
                         * (REL_BUCKETS - max_exact)).astype(np.int32)
    large = np.minimum(large, REL_BUCKETS - 1)
    bucket = np.where(n < max_exact, n, large)
    visible = (dist >= 0) & (dist < BLOCK)
    return np.where(visible, bucket, -1).astype(np.int32)


def _swa_bias_kernel(rb_ref, bk_ref, o_ref):
    bk = bk_ref[...]
    key = lax.broadcasted_iota(jnp.int32, bk.shape, 0)

    def head(h, _):
        acc = jnp.full(bk.shape, NEG_INF, F32)
        for b in range(REL_BUCKETS):
            acc = jnp.where(bk == b, rb_ref[b, h] * LOG2E, acc)
        o_ref[0, h] = acc
        o_ref[1, h] = jnp.where(key < BLOCK, NEG_INF, acc)
        return 0

    lax.fori_loop(0, N_HEADS, head, 0)


def _swa_bias(rel_bias):
    bk = jnp.asarray(_t5_bucket_map())
    return pl.pallas_call(
        _swa_bias_kernel,
        in_specs=[pl.BlockSpec(memory_space=pltpu.SMEM),
                  pl.BlockSpec(memory_space=pltpu.VMEM)],
        out_specs=pl.BlockSpec(memory_space=pltpu.VMEM),
        out_shape=jax.ShapeDtypeStruct((2, N_HEADS, 2 * BLOCK, BLOCK), F32),
        name="swa_bias",
    )(rel_bias, bk)


_TN_DIMS = (((0,), (0,)), ((), ()))
SWA_SCORES_AHEAD = 2
SWA_BLOCKS_PER_STEP = 2


def _swa_kernel(sink_ref, q_ref, kp_ref, kc_ref, vp_ref, vc_ref, bm_ref, o_ref,
                qs_ref, s_ref, p_ref, linv_ref):
    units = [(qb, kh) for qb in range(SWA_BLOCKS_PER_STEP) for kh in range(A_KV_HEADS)]
    first_table = jnp.where(pl.program_id(1) == 0, 1, 0)

    halves = LANES // HEAD_DIM
    lane = lax.broadcasted_iota(jnp.uint32, (1, LANES), 1)
    half_mask = [jnp.where((lane // HEAD_DIM) == hf, jnp.uint32(0xFFFFFFFF), jnp.uint32(0))
                 for hf in range(halves)]
    words = lambda x: pltpu.bitcast(x, jnp.uint32)

    def keys_values(prev_ref, cur_ref, qb, kh):
        slab = kh // halves
        if qb == 0:
            return jnp.concatenate([prev_ref[slab], cur_ref[slab, 0:BLOCK, :]], axis=0)
        return cur_ref[slab, (qb - 1) * BLOCK:(qb + 1) * BLOCK, :]

    def scores(u):
        qb, kh = units[u]
        k_same = keys_values(kp_ref, kc_ref, qb, kh)
        k_swapped = pltpu.bitcast(pltpu.roll(words(k_same), HEAD_DIM, axis=1), BF16)
        per_half = A_GROUP // halves
        for hf in range(halves):
            for j in range(per_half):
                g = j * halves + hf
                q = words(q_ref[(kh * A_GROUP + g) // halves, qb * BLOCK:(qb + 1) * BLOCK, :])
                qs_ref[u, hf, j * BLOCK:(j + 1) * BLOCK, :] = pltpu.bitcast(q & half_mask[hf], BF16)
            k2 = k_same if hf == kh % halves else k_swapped
            s = lax.dot_general(k2, qs_ref[u, hf], _NT_DIMS, preferred_element_type=F32)
            for j in range(per_half):
                s_ref[u, j * halves + hf] = s[:, j * BLOCK:(j + 1) * BLOCK]

    def softmax(u):
        qb, kh = units[u]
        table = first_table if qb == 0 else 0
        for g in range(A_GROUP):
            h = kh * A_GROUP + g
            cols = slice(g * BLOCK, (g + 1) * BLOCK)
            s = s_ref[u, g] + bm_ref[table, h]
            sink = sink_ref[h] * LOG2E
            m = jnp.maximum(jnp.max(s, axis=0, keepdims=True), sink)
            p = jnp.exp2(s - m)
            denom = jnp.sum(p, axis=0, keepdims=True) + jnp.exp2(sink - m)
            p_ref[u, g // 2, :, (g % 2) * BLOCK:(g % 2 + 1) * BLOCK] = p.astype(BF16)
            linv_ref[u, :, cols] = 1.0 / denom

    def weighted_values(u):
        qb, kh = units[u]
        v2 = keys_values(vp_ref, vc_ref, qb, kh)
        own = slice((kh % halves) * HEAD_DIM, (kh % halves + 1) * HEAD_DIM)
        for c in range(A_GROUP // 2):
            cols = slice(2 * c * BLOCK, 2 * (c + 1) * BLOCK)
            ot = (lax.dot_general(v2, p_ref[u, c], _TN_DIMS, preferred_element_type=F32)[own]
                  * linv_ref[u, :, cols])
            for gg in range(2):
                h = kh * A_GROUP + 2 * c + gg
                o_ref[h * HEAD_DIM:(h + 1) * HEAD_DIM, qb * BLOCK:(qb + 1) * BLOCK] = (
                    ot[:, gg * BLOCK:(gg + 1) * BLOCK].astype(o_ref.dtype))

    for u in range(min(SWA_SCORES_AHEAD, len(units))):
        scores(u)
    for u in range(len(units)):
        if u + SWA_SCORES_AHEAD < len(units):
            scores(u + SWA_SCORES_AHEAD)
        softmax(u)
        weighted_values(u)


def _swa_attention(qkv, sinks, biasmask):
    b, _, s, _ = qkv.shape
    rows = SWA_BLOCKS_PER_STEP * BLOCK
    gq = A_GROUP * BLOCK
    n_units = SWA_BLOCKS_PER_STEP * A_KV_HEADS
    q_slabs = D_MODEL // LANES
    kv_slabs = A_KV_DIM // LANES
    kcol = q_slabs // kv_slabs
    prev = lambda n: jnp.maximum(SWA_BLOCKS_PER_STEP * n - 1, 0)
    return pl.pallas_call(
        _swa_kernel,
        grid=(b, s // rows),
        in_specs=[pl.BlockSpec(memory_space=pltpu.SMEM),
                  pl.BlockSpec((None, q_slabs, rows, LANES), lambda bi, n: (bi, 0, n, 0)),
                  pl.BlockSpec((None, kv_slabs, BLOCK, LANES), lambda bi, n: (bi, kcol, prev(n), 0)),
                  pl.BlockSpec((None, kv_slabs, rows, LANES), lambda bi, n: (bi, kcol, n, 0)),
                  pl.BlockSpec((None, kv_slabs, BLOCK, LANES), lambda bi, n: (bi, kcol + 1, prev(n), 0)),
                  pl.BlockSpec((None, kv_slabs, rows, LANES), lambda bi, n: (bi, kcol + 1, n, 0)),
                  pl.BlockSpec((2, N_HEADS, 2 * BLOCK, BLOCK), lambda bi, n: (0, 0, 0, 0),
                               pipeline_mode=pl.Buffered(1))],
        out_specs=pl.BlockSpec((None, D_MODEL, rows), lambda bi, n: (bi, 0, n)),
        out_shape=jax.ShapeDtypeStruct((b, D_MODEL, s), BF16),
        scratch_shapes=[pltpu.VMEM((n_units, LANES // HEAD_DIM, gq // (LANES // HEAD_DIM), LANES), BF16),
                        pltpu.VMEM((n_units, A_GROUP, 2 * BLOCK, BLOCK), F32),
                        pltpu.VMEM((n_units, A_GROUP // 2, 2 * BLOCK, 2 * BLOCK), BF16),
                        pltpu.VMEM((n_units, 1, gq), F32)],
        compiler_params=_params("arbitrary", "arbitrary"),
        name="swa_attn",
    )(sinks, qkv, qkv, qkv, qkv, qkv, biasmask)


FOX_CUM_BLOCK = 256
FOX_TQ = 256
FOX_HEADS_PER_PAIR = LANES // HEAD_DIM
FOX_PAIRS_PER_STEP = 2
FOX_PIECES = 3
FOX_SUM_ROWS = 16
FOX_SCORE_SLOTS = 4
FOX_PV_LAG = 1
FOX_PROB_SLOTS = FOX_PV_LAG + 1
BF16_ONE_PAIR = 0x3F803F80


def _fox_ext_lane(head, piece):
    pair, odd = divmod(head, FOX_HEADS_PER_PAIR)
    return pair * LANES + (1 - odd) * HEAD_DIM + piece


def _fox_placement():
    pm = np.zeros((FOX_PIECES * N_HEADS, D_MODEL), np.float32)
    for t in range(FOX_PIECES):
        for h in range(N_HEADS):
            pm[t * N_HEADS + h, _fox_ext_lane(h, t)] = -1.0
    return pm


def _fox_lane_words():
    w = np.zeros((SUBLANES, LANES), np.uint32)
    for hh in range(FOX_HEADS_PER_PAIR):
        own = np.zeros(LANES, bool)
        own[hh * HEAD_DIM:(hh + 1) * HEAD_DIM] = True
        ext = np.zeros(LANES, bool)
        first = _fox_ext_lane(hh, 0)
        ext[first:first + FOX_PIECES] = True
        w[3 * hh + 0] = np.where(own, 0xFFFFFFFF, 0)
        w[3 * hh + 1] = np.where(ext, 0xFFFFFFFF, 0)
        w[3 * hh + 2] = np.where(ext, BF16_ONE_PAIR, 0)
    return w


def _log_sigmoid(x):
    return jnp.minimum(x, 0.0) - jnp.log1p(jnp.exp(-jnp.abs(x)))


def _round_bf16(x):
    return x.astype(BF16).astype(F32)


def _fox_gate_kernel(x_ref, wf_ref, fb_ref, pm_ref, cumt_ref, kext_ref, w3_ref):
    @pl.when(pl.program_id(0) == 0)
    def _():
        w = wf_ref[...].astype(BF16)
        for t in range(FOX_PIECES):
            w3_ref[t * N_HEADS:(t + 1) * N_HEADS, :] = w

    fz = lax.dot_general(w3_ref[...], x_ref[...], _NT_DIMS, preferred_element_type=F32) + fb_ref[...]
    lf = _log_sigmoid(fz)
    cb = FOX_CUM_BLOCK
    row = lax.broadcasted_iota(jnp.int32, (cb, cb), 0)
    col = lax.broadcasted_iota(jnp.int32, (cb, cb), 1)
    tri = jnp.where(row <= col, 1.0, 0.0).astype(BF16)
    tri3 = jnp.concatenate([tri] * FOX_PIECES, axis=0)
    group = lax.broadcasted_iota(jnp.int32, (lf.shape[0], cb), 0) // N_HEADS
    pm = pm_ref[...]
    n_blocks = lf.shape[1] // cb
    local = []
    for blk in range(n_blocks):
        x = lf[:, blk * cb:(blk + 1) * cb]
        hi = _round_bf16(x)
        r1 = x - hi
        mid = _round_bf16(r1)
        pieces = jnp.concatenate([hi, mid, r1 - mid], axis=1).astype(BF16)
        local.append(jnp.dot(pieces, tri3, preferred_element_type=F32))
    carry = jnp.zeros((lf.shape[0], 1), F32)
    for blk in range(n_blocks):
        cols = slice(blk * cb, (blk + 1) * cb)
        c = local[blk] + carry
        carry = carry + local[blk][:, cb - 1:cb]
        c2 = c * LOG2E
        cumt_ref[:, cols] = c2[:N_HEADS]
        chi = _round_bf16(c2)
        cr1 = c2 - chi
        cmid = _round_bf16(cr1)
        piece = jnp.where(group == 0, chi, jnp.where(group == 1, cmid, cr1 - cmid)).astype(BF16)
        kext_ref[cols, :] = lax.dot_general(piece, pm, _TN_DIMS, preferred_element_type=F32).astype(BF16)


def _fox_gate(xb, w_qkvf_t, layer, fb3, batch, seq):
    pm = jnp.asarray(_fox_placement(), BF16)
    n3 = FOX_PIECES * N_HEADS
    gate_row = 3 * D_MODEL // N_HEADS
    return pl.pallas_call(
        _fox_gate_kernel,
        grid=(batch,),
        in_specs=[pl.BlockSpec((seq, D_MODEL), lambda bi: (bi, 0)),
                  pl.BlockSpec((None, N_HEADS, D_MODEL), lambda bi: (layer, gate_row, 0)),
                  pl.BlockSpec((n3, 1), lambda bi: (0, 0)),
                  pl.BlockSpec((n3, D_MODEL), lambda bi: (0, 0))],
        out_specs=[pl.BlockSpec((None, N_HEADS, seq), lambda bi: (bi, 0, 0)),
                   pl.BlockSpec((None, seq, D_MODEL), lambda bi: (bi, 0, 0))],
        out_shape=[jax.ShapeDtypeStruct((batch, N_HEADS, seq), F32),
                   jax.ShapeDtypeStruct((batch, seq, D_MODEL), BF16)],
        scratch_shapes=[pltpu.VMEM((n3, D_MODEL), BF16)],
        compiler_params=_params("arbitrary"),
        name="fox_gate",
    )(xb, w_qkvf_t, fb3, pm)


def _fox_kernel(q_ref, k_ref, v_ref, kext_ref, cbt_ref, lw_ref, o_ref, qa_ref, ka_ref, vt_ref,
                s_ref, p_ref):
    seq = q_ref.shape[0]
    tq = FOX_TQ
    key = lax.broadcasted_iota(jnp.int32, (tq, tq), 0)
    qry = lax.broadcasted_iota(jnp.int32, (tq, tq), 1)
    causal = key <= qry
    words = lambda x: pltpu.bitcast(x, jnp.uint32)
    for pair in range(FOX_PAIRS_PER_STEP):
        lanes = slice(pair * LANES, (pair + 1) * LANES)
        q = words(q_ref[:, lanes])
        k = words(k_ref[:, lanes])
        kext = words(kext_ref[:, lanes])
        vt = v_ref[:, lanes].T
        for hh in range(FOX_HEADS_PER_PAIR):
            h = pair * FOX_HEADS_PER_PAIR + hh
            own = lw_ref[3 * hh + 0:3 * hh + 1, :]
            ext = lw_ref[3 * hh + 1:3 * hh + 2, :]
            ones = lw_ref[3 * hh + 2:3 * hh + 3, :]
            qa_ref[h] = pltpu.bitcast((q & own) | ones, BF16)
            ka_ref[h] = pltpu.bitcast((k & ~ext) | (kext & ext), BF16)
            vt_ref[h, 0:HEAD_DIM, :] = vt[hh * HEAD_DIM:(hh + 1) * HEAD_DIM]
            vt_ref[h, HEAD_DIM:, :] = jnp.ones((FOX_SUM_ROWS, seq), BF16)

    heads = FOX_PAIRS_PER_STEP * FOX_HEADS_PER_PAIR
    units = [(i, h) for i in range(seq // tq) for h in range(heads)]

    def scores(u):
        i, h = units[u]
        slot = u % FOX_SCORE_SLOTS
        r0, r1 = i * tq, (i + 1) * tq
        qa = qa_ref[h, r0:r1, :]
        s_d = lax.dot_general(ka_ref[h, r0:r1, :], qa, _NT_DIMS, preferred_element_type=F32)
        s_ref[slot, r0:r1, :] = jnp.where(causal, s_d, NEG_INF)
        if i > 0:
            s_ref[slot, 0:r0, :] = lax.dot_general(ka_ref[h, 0:r0, :], qa, _NT_DIMS,
                                                   preferred_element_type=F32)

    def softmax(u):
        i, h = units[u]
        slot = u % FOX_SCORE_SLOTS
        r0, r1 = i * tq, (i + 1) * tq
        head = pl.program_id(1) * heads + h
        cbq = cbt_ref[pl.ds(head, 1), r0:r1]
        m = jnp.max(s_ref[slot, 0:r1, :], axis=0, keepdims=True)
        shift = cbq - (m + cbq)
        p_ref[u % FOX_PROB_SLOTS, 0:r1, :] = jnp.exp2(s_ref[slot, 0:r1, :] + shift).astype(BF16)

    def weighted_values(u):
        i, h = units[u]
        r0, r1 = i * tq, (i + 1) * tq
        ot = jnp.dot(vt_ref[h, :, 0:r1], p_ref[u % FOX_PROB_SLOTS, 0:r1, :], preferred_element_type=F32)
        out = ot[0:HEAD_DIM] * (1.0 / ot[HEAD_DIM:HEAD_DIM + 1])
        o_ref[h * HEAD_DIM:(h + 1) * HEAD_DIM, r0:r1] = out.astype(o_ref.dtype)

    ahead = FOX_SCORE_SLOTS - 1
    for u in range(min(ahead, len(units))):
        scores(u)
    for u in range(len(units)):
        if u + ahead < len(units):
            scores(u + ahead)
        softmax(u)
        if u >= FOX_PV_LAG:
            weighted_values(u - FOX_PV_LAG)
    for u in range(len(units) - FOX_PV_LAG, len(units)):
        weighted_values(u)


def _fox_attention(qkv, cumt, kext):
    b, s, _ = qkv.shape
    heads = FOX_PAIRS_PER_STEP * FOX_HEADS_PER_PAIR
    width = FOX_PAIRS_PER_STEP * LANES
    lw = jnp.asarray(_fox_lane_words())
    ncol = D_MODEL // width
    return pl.pallas_call(
        _fox_kernel,
        grid=(b, N_HEADS // heads),
        in_specs=[pl.BlockSpec((None, s, width), lambda bi, hp: (bi, 0, hp)),
                  pl.BlockSpec((None, s, width), lambda bi, hp: (bi, 0, ncol + hp)),
                  pl.BlockSpec((None, s, width), lambda bi, hp: (bi, 0, 2 * ncol + hp)),
                  pl.BlockSpec((None, s, width), lambda bi, hp: (bi, 0, hp)),
                  pl.BlockSpec((None, N_HEADS, s), lambda bi, hp: (bi, 0, 0)),
                  pl.BlockSpec((SUBLANES, LANES), lambda bi, hp: (0, 0))],
        out_specs=pl.BlockSpec((None, heads * HEAD_DIM, s), lambda bi, hp: (bi, hp, 0)),
        out_shape=jax.ShapeDtypeStruct((b, D_MODEL, s), BF16),
        scratch_shapes=[pltpu.VMEM((heads, s, LANES), BF16), pltpu.VMEM((heads, s, LANES), BF16),
                        pltpu.VMEM((heads, HEAD_DIM + FOX_SUM_ROWS, s), BF16),
                        pltpu.VMEM((FOX_SCORE_SLOTS, s, FOX_TQ), F32),
                        pltpu.VMEM((FOX_PROB_SLOTS, s, FOX_TQ), BF16)],
        compiler_params=_params("arbitrary", "arbitrary"),
        name="fox_attn",
    )(qkv, qkv, qkv, kext, cumt, lw)


def _deepnorm(xres, y, g, b, alpha):
    z = alpha * xres + y
    mu = jnp.mean(z, axis=-1, keepdims=True)
    zc = z - mu
    var = jnp.mean(zc * zc, axis=-1, keepdims=True)
    return zc * lax.rsqrt(var + LN_EPS) * g + b


LN_SUBTILES = 2


def _oproj_ln_kernel(alpha, ot_ref, xres_ref, w_ref, g_ref, b_ref, xo_ref, xob_ref, wb_ref):
    @pl.when(pl.program_id(0) == 0)
    def _():
        wb_ref[...] = w_ref[...].astype(BF16)

    ts = xres_ref.shape[0] // LN_SUBTILES
    ys = [lax.dot_general(ot_ref[:, r * ts:(r + 1) * ts], wb_ref[...], _TN_DIMS,
                          preferred_element_type=F32) for r in range(LN_SUBTILES)]
    for r, y in enumerate(ys):
        rows = slice(r * ts, (r + 1) * ts)
        out = _deepnorm(xres_ref[rows, :], y, g_ref[...], b_ref[...], alpha)
        xo_ref[rows, :] = out
        xob_ref[rows, :] = out.astype(BF16)


def _oproj_ln(ot, xres, w, layer, g, b, alpha, tm, name):
    m, n = xres.shape
    k = w.shape[1]
    per_seq = ot.shape[2] // tm
    return pl.pallas_call(
        functools.partial(_oproj_ln_kernel, alpha),
        grid=(m // tm,),
        in_specs=[pl.BlockSpec((None, k, tm), lambda i: (i // per_seq, 0, i % per_seq)),
                  pl.BlockSpec((tm, n), lambda i: (i, 0)),
                  pl.BlockSpec((None, k, n), lambda i: (layer, 0, 0), pipeline_mode=pl.Buffered(1)),
                  pl.BlockSpec((1, n), lambda i: (0, 0)),
                  pl.BlockSpec((1, n), lambda i: (0, 0))],
        out_specs=[pl.BlockSpec((tm, n), lambda i: (i, 0)),
                   pl.BlockSpec((tm, n), lambda i: (i, 0))],
        out_shape=[jax.ShapeDtypeStruct((m, n), F32), jax.ShapeDtypeStruct((m, n), BF16)],
        scratch_shapes=[pltpu.VMEM((k, n), BF16)],
        compiler_params=_params("arbitrary"),
        name=name,
    )(ot, xres, w, g, b)


CONV_WIDTH = 3
FFN_TF = 512
FFN_UP_TM = 1024


def _ffn_up_kernel(tiles_per_seq, x_ref, wg_ref, wu_ref, cwg_ref, cwu_ref, cbg_ref, cbu_ref,
                   a_ref, wb_ref, h_ref):
    i = pl.program_id(1)
    tm = x_ref.shape[0]
    tf = a_ref.shape[1]
    halo = SUBLANES

    @pl.when(i == 0)
    def _():
        wb_ref[:, 0:tf] = wg_ref[...].astype(BF16)
        wb_ref[:, tf:2 * tf] = wu_ref[...].astype(BF16)

    @pl.when(i % tiles_per_seq == 0)
    def _():
        h_ref[0:halo, :] = jnp.zeros((halo, h_ref.shape[1]), F32)

    h_ref[halo:halo + tm, :] = jnp.dot(x_ref[...], wb_ref[...], preferred_element_type=F32)

    def conv(cols, cw_ref, cb_ref):
        out = cb_ref[...]
        for tap in range(CONV_WIDTH):
            start = halo - (CONV_WIDTH - 1) + tap
            out = out + cw_ref[tap:tap + 1, :] * h_ref[start:start + tm, cols]
        return out

    g = conv(slice(0, tf), cwg_ref, cbg_ref)
    u = conv(slice(tf, 2 * tf), cwu_ref, cbu_ref)
    a_ref[...] = (g * jax.nn.sigmoid(g) * u).astype(a_ref.dtype)
    h_ref[0:halo, :] = h_ref[tm:tm + halo, :]


def _ffn_up(xb, w_up, conv_w, conv_b, layer, seq):
    m, k = xb.shape
    tm, tf = FFN_UP_TM, FFN_TF
    nf = D_FF // tf
    cb2 = conv_b.reshape(conv_b.shape[0], 1, 2 * D_FF)
    return pl.pallas_call(
        functools.partial(_ffn_up_kernel, seq // tm),
        grid=(nf, m // tm),
        in_specs=[pl.BlockSpec((tm, k), lambda j, i: (i, 0)),
                  pl.BlockSpec((None, k, tf), lambda j, i: (layer, 0, j)),
                  pl.BlockSpec((None, k, tf), lambda j, i: (layer, 0, j + nf)),
                  pl.BlockSpec((None, CONV_WIDTH, tf), lambda j, i: (layer, 0, j)),
                  pl.BlockSpec((None, CONV_WIDTH, tf), lambda j, i: (layer, 0, j + nf)),
                  pl.BlockSpec((None, 1, tf), lambda j, i: (layer, 0, j)),
                  pl.BlockSpec((None, 1, tf), lambda j, i: (layer, 0, j + nf))],
        out_specs=pl.BlockSpec((tm, tf), lambda j, i: (i, j)),
        out_shape=jax.ShapeDtypeStruct((m, D_FF), BF16),
        scratch_shapes=[pltpu.VMEM((k, 2 * tf), BF16), pltpu.VMEM((tm + 2 * SUBLANES, 2 * tf), F32)],
        compiler_params=_params("arbitrary", "arbitrary"),
        name="ffn_up",
    )(xb, w_up, w_up, conv_w, conv_w, cb2, cb2)


FFN_DOWN_TM = 256
FFN_DOWN_WCHUNK = 512


def _ffn_down_ln_kernel(alpha, n_load, emit_bf16, a_ref, xres_ref, w_ref, g_ref, b_ref, *rest):
    if emit_bf16:
        xo_ref, xob_ref, wb_ref = rest
    else:
        (xo_ref, wb_ref), xob_ref = rest, None
    step = pl.program_id(0)
    rows = w_ref.shape[0]

    @pl.when(step < n_load)
    def _():
        off = pl.multiple_of(step * rows, rows)
        wb_ref[pl.ds(off, rows), :] = w_ref[...].astype(BF16)

    @pl.when(step >= n_load)
    def _():
        y = jnp.dot(a_ref[...], wb_ref[...], preferred_element_type=F32)
        out = _deepnorm(xres_ref[...], y, g_ref[...], b_ref[...], alpha)
        xo_ref[...] = out
        if emit_bf16:
            xob_ref[...] = out.astype(BF16)


def _ffn_down_ln(a, xres, w, layer, g, b, alpha, emit_bf16, name):
    m, k = a.shape
    n = w.shape[2]
    tm, rows = FFN_DOWN_TM, FFN_DOWN_WCHUNK
    n_load = k // rows
    tile = lambda s: (jnp.maximum(s - n_load, 0), 0)
    out_specs = [pl.BlockSpec((tm, n), tile)]
    out_shape = [jax.ShapeDtypeStruct((m, n), F32)]
    if emit_bf16:
        out_specs.append(pl.BlockSpec((tm, n), tile))
        out_shape.append(jax.ShapeDtypeStruct((m, n), BF16))
    return pl.pallas_call(
        functools.partial(_ffn_down_ln_kernel, alpha, n_load, emit_bf16),
        grid=(n_load + m // tm,),
        in_specs=[pl.BlockSpec((tm, k), tile),
                  pl.BlockSpec((tm, n), tile),
                  pl.BlockSpec((None, rows, n), lambda s: (layer, jnp.minimum(s, n_load - 1), 0)),
                  pl.BlockSpec((1, n), lambda s: (0, 0)),
                  pl.BlockSpec((1, n), lambda s: (0, 0))],
        out_specs=out_specs,
        out_shape=out_shape,
        scratch_shapes=[pltpu.VMEM((k, n), BF16)],
        compiler_params=_params("arbitrary"),
        name=name,
    )(a, xres, w, g, b)


def kernel(x, a_w_qkv, a_w_o, a_sinks, rel_bias, b_w_qkvf, b_f_bias, b_w_o,
           ffn_w_up, ffn_conv_w, ffn_conv_b, ffn_w_down, ln_g, ln_b):
    batch, seq, d = x.shape
    depth = ffn_w_up.shape[0]
    alpha = (2 * depth) ** 0.25
    m = batch * seq
    xf = x.reshape(m, d)
    xb = None
    biasmask = _swa_bias(rel_bias)
    for layer in range(depth):
        j = layer // 2
        g = ln_g[layer].reshape(2, 1, d)
        bt = ln_b[layer].reshape(2, 1, d)
        if layer % 2 == 0:
            n_qkv = D_MODEL + 2 * A_KV_DIM
            src = xf if xb is None else xb
            qkv = _ws_matmul(src, a_w_qkv, j, n_qkv, n_qkv // SWA_QKV_COL_TILES, SWA_QKV_TM, "swa_qkv",
                             slab_batch=(batch, seq))
            o = _swa_attention(qkv, a_sinks[j], biasmask)
            w_o = a_w_o
        else:
            w_t = jnp.swapaxes(b_w_qkvf, 1, 2)
            qkv = _ws_matmul(xb, w_t, j, 3 * D_MODEL, FOX_QKV_TN, FOX_QKV_TM, "fox_qkv",
                             w_is_transposed=True)
            fb3 = jnp.tile(b_f_bias[j].reshape(N_HEADS, 1), (FOX_PIECES, 1))
            cumt, kext = _fox_gate(xb, w_t, j, fb3, batch, seq)
            o = _fox_attention(qkv.reshape(batch, seq, 3 * D_MODEL), cumt, kext)
            w_o = b_w_o
        xf, xb = _oproj_ln(o, xf, w_o, j, g[0], bt[0], alpha, OPROJ_TM, "oproj_ln")
        a = _ffn_up(xb, ffn_w_up, ffn_conv_w, ffn_conv_b, layer, seq)
        last = layer == depth - 1
        res = _ffn_down_ln(a, xf, ffn_w_down, layer, g[1], bt[1], alpha, not last, "ffn_down_ln")
        xf, xb = (res[0], None) if last else res
    return xf.reshape(batch, seq, d)
```

```python
import functools
import math

import jax
import jax.numpy as jnp
import numpy as np
from jax import lax
from jax.experimental import pallas as pl
from jax.experimental.pallas import tpu as pltpu

D_MODEL = 2048
HEAD_DIM = 64
BLOCK = 128
N_HEADS = D_MODEL // HEAD_DIM
A_KV_HEADS = 4
A_GROUP = N_HEADS // A_KV_HEADS
A_KV_DIM = A_KV_HEADS * HEAD_DIM
REL_BUCKETS = 32
REL_MAX_DIST = 128
D_FF = 5632
LN_EPS = 1e-5
SCALE = HEAD_DIM ** -0.5

SUBLANES = 8
LANES = 128

BF16 = jnp.bfloat16
F32 = jnp.float32
NEG_INF = float("-inf")

_NT_DIMS = (((1,), (1,)), ((), ()))


def _params(*sem):
    return pltpu.CompilerParams(dimension_semantics=sem)


LOG2E = math.log2(math.e)
Q_SCALE = SCALE * LOG2E
SWA_QKV_COL_TILES = 2
SWA_QKV_TM = 1024
FOX_QKV_TN = 1024
FOX_QKV_TM = 2048
OPROJ_TM = 512


def _ws_matmul_kernel(w_is_transposed, x_ref, w_ref, o_ref, wb_ref):
    @pl.when(pl.program_id(1) == 0)
    def _():
        wb_ref[...] = w_ref[...].astype(BF16)

    x = x_ref[...].astype(BF16)
    if w_is_transposed:
        y = lax.dot_general(x, wb_ref[...], _NT_DIMS, preferred_element_type=F32)
    else:
        y = jnp.dot(x, wb_ref[...], preferred_element_type=F32)
    tn = y.shape[1]
    col = pl.program_id(0) * tn + lax.broadcasted_iota(jnp.int32, (1, tn), 1)
    y = (y * jnp.where(col < D_MODEL, Q_SCALE, 1.0)).astype(o_ref.dtype)
    if len(o_ref.shape) == 2:
        o_ref[...] = y
    else:
        for c in range(o_ref.shape[0]):
            o_ref[c] = y[:, c * LANES:(c + 1) * LANES]


def _ws_matmul(x, w, layer, n_out, tn, tm, name, w_is_transposed=False, slab_batch=None):
    m, k = x.shape
    if w_is_transposed:
        w_spec = pl.BlockSpec((None, tn, k), lambda j, i: (layer, j, 0))
        wb_shape = (tn, k)
    else:
        w_spec = pl.BlockSpec((None, k, tn), lambda j, i: (layer, 0, j))
        wb_shape = (k, tn)
    if slab_batch is None:
        out_spec = pl.BlockSpec((tm, tn), lambda j, i: (i, j))
        out_shape = jax.ShapeDtypeStruct((m, n_out), BF16)
    else:
        batch, seq = slab_batch
        per_seq = seq // tm
        out_spec = pl.BlockSpec((None, tn // LANES, tm, LANES),
                                lambda j, i: (i // per_seq, j, i % per_seq, 0))
        out_shape = jax.ShapeDtypeStruct((batch, n_out // LANES, seq, LANES), BF16)
    return pl.pallas_call(
        functools.partial(_ws_matmul_kernel, w_is_transposed),
        grid=(n_out // tn, m // tm),
        in_specs=[pl.BlockSpec((tm, k), lambda j, i: (i, 0)), w_spec],
        out_specs=out_spec,
        out_shape=out_shape,
        scratch_shapes=[pltpu.VMEM(wb_shape, BF16)],
        compiler_params=_params("arbitrary", "arbitrary"),
        name=name,
    )(x, w)


def _t5_bucket_map():
    qi = np.arange(BLOCK)[None, :]
    kj = np.arange(2 * BLOCK)[:, None]
    dist = BLOCK + qi - kj
    n = np.maximum(dist, 0)
    max_exact = REL_BUCKETS // 2
    nf = np.maximum(n, 1).astype(np.float32)
    large = max_exact + (np.log(nf / max_exact) / math.log(REL_MAX_DIST / max_exact)
                         * (REL_BUCKETS - max_exact)).astype(np.int32)
    large = np.minimum(large, REL_BUCKETS - 1)
    bucket = np.where(n < max_exact, n, large)
    visible = (dist >= 0) & (dist < BLOCK)
    return np.where(visible, bucket, -1).astype(np.int32)


def _swa_bias_kernel(rb_ref, bk_ref, o_ref):
    bk = bk_ref[...]
    key = lax.broadcasted_iota(jnp.int32, bk.shape, 0)

    def head(h, _):
        acc = jnp.full(bk.shape, NEG_INF, F32)
        for b in range(REL_BUCKETS):
            acc = jnp.where(bk == b, rb_ref[b, h] * LOG2E, acc)
        o_ref[0, h] = acc
        o_ref[1, h] = jnp.where(key < BLOCK, NEG_INF, acc)
        return 0

    lax.fori_loop(0, N_HEADS, head, 0)


def _swa_bias(rel_bias):
    bk = jnp.asarray(_t5_bucket_map())
    return pl.pallas_call(
        _swa_bias_kernel,
        in_specs=[pl.BlockSpec(memory_space=pltpu.SMEM),
                  pl.BlockSpec(memory_space=pltpu.VMEM)],
        out_specs=pl.BlockSpec(memory_space=pltpu.VMEM),
        out_shape=jax.ShapeDtypeStruct((2, N_HEADS, 2 * BLOCK, BLOCK), F32),
        name="swa_bias",
    )(rel_bias, bk)


_TN_DIMS = (((0,), (0,)), ((), ()))
SWA_SCORES_AHEAD = 2
SWA_BLOCKS_PER_STEP = 4


def _swa_kernel(sink_ref, q_ref, kp_ref, kc_ref, vp_ref, vc_ref, bm_ref, o_ref,
                qs_ref, s_ref, p_ref, linv_ref):
    units = [(qb, kh) for qb in range(SWA_BLOCKS_PER_STEP) for kh in range(A_KV_HEADS)]
    first_table = jnp.where(pl.program_id(1) == 0, 1, 0)

    def head_at(head):
        slab, half = divmod(head, LANES // HEAD_DIM)
        return slab, slice(half * HEAD_DIM, (half + 1) * HEAD_DIM)

    def keys_values(prev_ref, cur_ref, qb, kh):
        slab, lanes = head_at(kh)
        if qb == 0:
            return jnp.concatenate([prev_ref[slab, :, lanes], cur_ref[slab, 0:BLOCK, lanes]], axis=0)
        return cur_ref[slab, (qb - 1) * BLOCK:(qb + 1) * BLOCK, lanes]

    def scores(u):
        qb, kh = units[u]
        for g in range(A_GROUP):
            slab, lanes = head_at(kh * A_GROUP + g)
            qs_ref[u, g * BLOCK:(g + 1) * BLOCK, :] = q_ref[slab, qb * BLOCK:(qb + 1) * BLOCK, lanes]
        k2 = keys_values(kp_ref, kc_ref, qb, kh)
        s = lax.dot_general(k2, qs_ref[u], _NT_DIMS, preferred_element_type=F32)
        for g in range(A_GROUP):
            s_ref[u, g] = s[:, g * BLOCK:(g + 1) * BLOCK]

    def softmax(u):
        qb, kh = units[u]
        table = first_table if qb == 0 else 0
        for g in range(A_GROUP):
            h = kh * A_GROUP + g
            cols = slice(g * BLOCK, (g + 1) * BLOCK)
            s = s_ref[u, g] + bm_ref[table, h]
            sink = sink_ref[h] * LOG2E
            m = jnp.maximum(jnp.max(s, axis=0, keepdims=True), sink)
            p = jnp.exp2(s - m)
            denom = jnp.sum(p, axis=0, keepdims=True) + jnp.exp2(sink - m)
            p_ref[u, g // 2, :, (g % 2) * BLOCK:(g % 2 + 1) * BLOCK] = p.astype(BF16)
            linv_ref[u, :, cols] = 1.0 / denom

    def weighted_values(u):
        qb, kh = units[u]
        v2 = keys_values(vp_ref, vc_ref, qb, kh)
        for c in range(A_GROUP // 2):
            cols = slice(2 * c * BLOCK, 2 * (c + 1) * BLOCK)
            ot = (lax.dot_general(v2, p_ref[u, c], _TN_DIMS, preferred_element_type=F32)
                  * linv_ref[u, :, cols])
            for gg in range(2):
                h = kh * A_GROUP + 2 * c + gg
                o_ref[h * HEAD_DIM:(h + 1) * HEAD_DIM, qb * BLOCK:(qb + 1) * BLOCK] = (
                    ot[:, gg * BLOCK:(gg + 1) * BLOCK].astype(o_ref.dtype))

    for u in range(min(SWA_SCORES_AHEAD, len(units))):
        scores(u)
    for u in range(len(units)):
        if u + SWA_SCORES_AHEAD < len(units):
            scores(u + SWA_SCORES_AHEAD)
        softmax(u)
        weighted_values(u)


def _swa_attention(qkv, sinks, biasmask):
    b, _, s, _ = qkv.shape
    rows = SWA_BLOCKS_PER_STEP * BLOCK
    gq = A_GROUP * BLOCK
    n_units = SWA_BLOCKS_PER_STEP * A_KV_HEADS
    q_slabs = D_MODEL // LANES
    kv_slabs = A_KV_DIM // LANES
    kcol = q_slabs // kv_slabs
    prev = lambda n: jnp.maximum(SWA_BLOCKS_PER_STEP * n - 1, 0)
    return pl.pallas_call(
        _swa_kernel,
        grid=(b, s // rows),
        in_specs=[pl.BlockSpec(memory_space=pltpu.SMEM),
                  pl.BlockSpec((None, q_slabs, rows, LANES), lambda bi, n: (bi, 0, n, 0)),
                  pl.BlockSpec((None, kv_slabs, BLOCK, LANES), lambda bi, n: (bi, kcol, prev(n), 0)),
                  pl.BlockSpec((None, kv_slabs, rows, LANES), lambda bi, n: (bi, kcol, n, 0)),
                  pl.BlockSpec((None, kv_slabs, BLOCK, LANES), lambda bi, n: (bi, kcol + 1, prev(n), 0)),
                  pl.BlockSpec((None, kv_slabs, rows, LANES), lambda bi, n: (bi, kcol + 1, n, 0)),
                  pl.BlockSpec((2, N_HEADS, 2 * BLOCK, BLOCK), lambda bi, n: (0, 0, 0, 0),
                               pipeline_mode=pl.Buffered(1))],
        out_specs=pl.BlockSpec((None, D_MODEL, rows), lambda bi, n: (bi, 0, n)),
        out_shape=jax.ShapeDtypeStruct((b, D_MODEL, s), BF16),
        scratch_shapes=[pltpu.VMEM((n_units, gq, HEAD_DIM), BF16),
                        pltpu.VMEM((n_units, A_GROUP, 2 * BLOCK, BLOCK), F32),
                        pltpu.VMEM((n_units, A_GROUP // 2, 2 * BLOCK, 2 * BLOCK), BF16),
                        pltpu.VMEM((n_units, 1, gq), F32)],
        compiler_params=_params("arbitrary", "arbitrary"),
        name="swa_attn",
    )(sinks, qkv, qkv, qkv, qkv, qkv, biasmask)


FOX_CUM_BLOCK = 256
FOX_TQ = 256
FOX_HEADS_PER_PAIR = LANES // HEAD_DIM
FOX_PAIRS_PER_STEP = 2
FOX_PIECES = 3
FOX_SUM_ROWS = 16
FOX_SCORE_SLOTS = 4
FOX_PV_LAG = 1
FOX_PROB_SLOTS = FOX_PV_LAG + 1
BF16_ONE_PAIR = 0x3F803F80


def _fox_ext_lane(head, piece):
    pair, odd = divmod(head, FOX_HEADS_PER_PAIR)
    return pair * LANES + (1 - odd) * HEAD_DIM + piece


def _fox_placement():
    pm = np.zeros((FOX_PIECES * N_HEADS, D_MODEL), np.float32)
    for t in range(FOX_PIECES):
        for h in range(N_HEADS):
            pm[t * N_HEADS + h, _fox_ext_lane(h, t)] = -1.0
    return pm


def _fox_lane_words():
    w = np.zeros((SUBLANES, LANES), np.uint32)
    for hh in range(FOX_HEADS_PER_PAIR):
        own = np.zeros(LANES, bool)
        own[hh * HEAD_DIM:(hh + 1) * HEAD_DIM] = True
        ext = np.zeros(LANES, bool)
        first = _fox_ext_lane(hh, 0)
        ext[first:first + FOX_PIECES] = True
        w[3 * hh + 0] = np.where(own, 0xFFFFFFFF, 0)
        w[3 * hh + 1] = np.where(ext, 0xFFFFFFFF, 0)
        w[3 * hh + 2] = np.where(ext, BF16_ONE_PAIR, 0)
    return w


def _log_sigmoid(x):
    return jnp.minimum(x, 0.0) - jnp.log1p(jnp.exp(-jnp.abs(x)))


def _round_bf16(x):
    return x.astype(BF16).astype(F32)


def _fox_gate_kernel(x_ref, wf_ref, fb_ref, pm_ref, cumt_ref, kext_ref, w3_ref):
    @pl.when(pl.program_id(0) == 0)
    def _():
        w = wf_ref[...].astype(BF16)
        for t in range(FOX_PIECES):
            w3_ref[t * N_HEADS:(t + 1) * N_HEADS, :] = w

    fz = lax.dot_general(w3_ref[...], x_ref[...], _NT_DIMS, preferred_element_type=F32) + fb_ref[...]
    lf = _log_sigmoid(fz)
    cb = FOX_CUM_BLOCK
    row = lax.broadcasted_iota(jnp.int32, (cb, cb), 0)
    col = lax.broadcasted_iota(jnp.int32, (cb, cb), 1)
    tri = jnp.where(row <= col, 1.0, 0.0).astype(BF16)
    tri3 = jnp.concatenate([tri] * FOX_PIECES, axis=0)
    group = lax.broadcasted_iota(jnp.int32, (lf.shape[0], cb), 0) // N_HEADS
    pm = pm_ref[...]
    n_blocks = lf.shape[1] // cb
    local = []
    for blk in range(n_blocks):
        x = lf[:, blk * cb:(blk + 1) * cb]
        hi = _round_bf16(x)
        r1 = x - hi
        mid = _round_bf16(r1)
        pieces = jnp.concatenate([hi, mid, r1 - mid], axis=1).astype(BF16)
        local.append(jnp.dot(pieces, tri3, preferred_element_type=F32))
    carry = jnp.zeros((lf.shape[0], 1), F32)
    for blk in range(n_blocks):
        cols = slice(blk * cb, (blk + 1) * cb)
        c = local[blk] + carry
        carry = carry + local[blk][:, cb - 1:cb]
        c2 = c * LOG2E
        cumt_ref[:, cols] = c2[:N_HEADS]
        chi = _round_bf16(c2)
        cr1 = c2 - chi
        cmid = _round_bf16(cr1)
        piece = jnp.where(group == 0, chi, jnp.where(group == 1, cmid, cr1 - cmid)).astype(BF16)
        kext_ref[cols, :] = lax.dot_general(piece, pm, _TN_DIMS, preferred_element_type=F32).astype(BF16)


def _fox_gate(xb, w_qkvf_t, layer, fb3, batch, seq):
    pm = jnp.asarray(_fox_placement(), BF16)
    n3 = FOX_PIECES * N_HEADS
    gate_row = 3 * D_MODEL // N_HEADS
    return pl.pallas_call(
        _fox_gate_kernel,
        grid=(batch,),
        in_specs=[pl.BlockSpec((seq, D_MODEL), lambda bi: (bi, 0)),
                  pl.BlockSpec((None, N_HEADS, D_MODEL), lambda bi: (layer, gate_row, 0)),
                  pl.BlockSpec((n3, 1), lambda bi: (0, 0)),
                  pl.BlockSpec((n3, D_MODEL), lambda bi: (0, 0))],
        out_specs=[pl.BlockSpec((None, N_HEADS, seq), lambda bi: (bi, 0, 0)),
                   pl.BlockSpec((None, seq, D_MODEL), lambda bi: (bi, 0, 0))],
        out_shape=[jax.ShapeDtypeStruct((batch, N_HEADS, seq), F32),
                   jax.ShapeDtypeStruct((batch, seq, D_MODEL), BF16)],
        scratch_shapes=[pltpu.VMEM((n3, D_MODEL), BF16)],
        compiler_params=_params("arbitrary"),
        name="fox_gate",
    )(xb, w_qkvf_t, fb3, pm)


def _fox_kernel(q_ref, k_ref, v_ref, kext_ref, cbt_ref, lw_ref, o_ref, qa_ref, ka_ref, vt_ref,
                s_ref, p_ref):
    seq = q_ref.shape[0]
    tq = FOX_TQ
    key = lax.broadcasted_iota(jnp.int32, (tq, tq), 0)
    qry = lax.broadcasted_iota(jnp.int32, (tq, tq), 1)
    causal = key <= qry
    words = lambda x: pltpu.bitcast(x, jnp.uint32)
    for pair in range(FOX_PAIRS_PER_STEP):
        lanes = slice(pair * LANES, (pair + 1) * LANES)
        q = words(q_ref[:, lanes])
        k = words(k_ref[:, lanes])
        kext = words(kext_ref[:, lanes])
        vt = v_ref[:, lanes].T
        for hh in range(FOX_HEADS_PER_PAIR):
            h = pair * FOX_HEADS_PER_PAIR + hh
            own = lw_ref[3 * hh + 0:3 * hh + 1, :]
            ext = lw_ref[3 * hh + 1:3 * hh + 2, :]
            ones = lw_ref[3 * hh + 2:3 * hh + 3, :]
            qa_ref[h] = pltpu.bitcast((q & own) | ones, BF16)
            ka_ref[h] = pltpu.bitcast((k & ~ext) | (kext & ext), BF16)
            vt_ref[h, 0:HEAD_DIM, :] = vt[hh * HEAD_DIM:(hh + 1) * HEAD_DIM]
            vt_ref[h, HEAD_DIM:, :] = jnp.ones((FOX_SUM_ROWS, seq), BF16)

    heads = FOX_PAIRS_PER_STEP * FOX_HEADS_PER_PAIR
    units = [(i, h) for i in range(seq // tq) for h in range(heads)]

    def scores(u):
        i, h = units[u]
        slot = u % FOX_SCORE_SLOTS
        r0, r1 = i * tq, (i + 1) * tq
        qa = qa_ref[h, r0:r1, :]
        s_d = lax.dot_general(ka_ref[h, r0:r1, :], qa, _NT_DIMS, preferred_element_type=F32)
        s_ref[slot, r0:r1, :] = jnp.where(causal, s_d, NEG_INF)
        if i > 0:
            s_ref[slot, 0:r0, :] = lax.dot_general(ka_ref[h, 0:r0, :], qa, _NT_DIMS,
                                                   preferred_element_type=F32)

    def softmax(u):
        i, h = units[u]
        slot = u % FOX_SCORE_SLOTS
        r0, r1 = i * tq, (i + 1) * tq
        head = pl.program_id(1) * heads + h
        cbq = cbt_ref[pl.ds(head, 1), r0:r1]
        m = jnp.max(s_ref[slot, 0:r1, :], axis=0, keepdims=True)
        shift = cbq - (m + cbq)
        p_ref[u % FOX_PROB_SLOTS, 0:r1, :] = jnp.exp2(s_ref[slot, 0:r1, :] + shift).astype(BF16)

    def weighted_values(u):
        i, h = units[u]
        r0, r1 = i * tq, (i + 1) * tq
        ot = jnp.dot(vt_ref[h, :, 0:r1], p_ref[u % FOX_PROB_SLOTS, 0:r1, :], preferred_element_type=F32)
        out = ot[0:HEAD_DIM] * (1.0 / ot[HEAD_DIM:HEAD_DIM + 1])
        o_ref[h * HEAD_DIM:(h + 1) * HEAD_DIM, r0:r1] = out.astype(o_ref.dtype)

    ahead = FOX_SCORE_SLOTS - 1
    for u in range(min(ahead, len(units))):
        scores(u)
    for u in range(len(units)):
        if u + ahead < len(units):
            scores(u + ahead)
        softmax(u)
        if u >= FOX_PV_LAG:
            weighted_values(u - FOX_PV_LAG)
    for u in range(len(units) - FOX_PV_LAG, len(units)):
        weighted_values(u)


def _fox_attention(qkv, cumt, kext):
    b, s, _ = qkv.shape
    heads = FOX_PAIRS_PER_STEP * FOX_HEADS_PER_PAIR
    width = FOX_PAIRS_PER_STEP * LANES
    lw = jnp.asarray(_fox_lane_words())
    ncol = D_MODEL // width
    return pl.pallas_call(
        _fox_kernel,
        grid=(b, N_HEADS // heads),
        in_specs=[pl.BlockSpec((None, s, width), lambda bi, hp: (bi, 0, hp)),
                  pl.BlockSpec((None, s, width), lambda bi, hp: (bi, 0, ncol + hp)),
                  pl.BlockSpec((None, s, width), lambda bi, hp: (bi, 0, 2 * ncol + hp)),
                  pl.BlockSpec((None, s, width), lambda bi, hp: (bi, 0, hp)),
                  pl.BlockSpec((None, N_HEADS, s), lambda bi, hp: (bi, 0, 0)),
                  pl.BlockSpec((SUBLANES, LANES), lambda bi, hp: (0, 0))],
        out_specs=pl.BlockSpec((None, heads * HEAD_DIM, s), lambda bi, hp: (bi, hp, 0)),
        out_shape=jax.ShapeDtypeStruct((b, D_MODEL, s), BF16),
        scratch_shapes=[pltpu.VMEM((heads, s, LANES), BF16), pltpu.VMEM((heads, s, LANES), BF16),
                        pltpu.VMEM((heads, HEAD_DIM + FOX_SUM_ROWS, s), BF16),
                        pltpu.VMEM((FOX_SCORE_SLOTS, s, FOX_TQ), F32),
                        pltpu.VMEM((FOX_PROB_SLOTS, s, FOX_TQ), BF16)],
        compiler_params=_params("arbitrary", "arbitrary"),
        name="fox_attn",
    )(qkv, qkv, qkv, kext, cumt, lw)


def _deepnorm(xres, y, g, b, alpha):
    z = alpha * xres + y
    mu = jnp.mean(z, axis=-1, keepdims=True)
    zc = z - mu
    var = jnp.mean(zc * zc, axis=-1, keepdims=True)
    return zc * lax.rsqrt(var + LN_EPS) * g + b


LN_SUBTILES = 2


def _oproj_ln_kernel(alpha, ot_ref, xres_ref, w_ref, g_ref, b_ref, xo_ref, xob_ref, wb_ref):
    @pl.when(pl.program_id(0) == 0)
    def _():
        wb_ref[...] = w_ref[...].astype(BF16)

    ts = xres_ref.shape[0] // LN_SUBTILES
    ys = [lax.dot_general(ot_ref[:, r * ts:(r + 1) * ts], wb_ref[...], _TN_DIMS,
                          preferred_element_type=F32) for r in range(LN_SUBTILES)]
    for r, y in enumerate(ys):
        rows = slice(r * ts, (r + 1) * ts)
        out = _deepnorm(xres_ref[rows, :], y, g_ref[...], b_ref[...], alpha)
        xo_ref[rows, :] = out
        xob_ref[rows, :] = out.astype(BF16)


def _oproj_ln(ot, xres, w, layer, g, b, alpha, tm, name):
    m, n = xres.shape
    k = w.shape[1]
    per_seq = ot.shape[2] // tm
    return pl.pallas_call(
        functools.partial(_oproj_ln_kernel, alpha),
        grid=(m // tm,),
        in_specs=[pl.BlockSpec((None, k, tm), lambda i: (i // per_seq, 0, i % per_seq)),
                  pl.BlockSpec((tm, n), lambda i: (i, 0)),
                  pl.BlockSpec((None, k, n), lambda i: (layer, 0, 0), pipeline_mode=pl.Buffered(1)),
                  pl.BlockSpec((1, n), lambda i: (0, 0)),
                  pl.BlockSpec((1, n), lambda i: (0, 0))],
        out_specs=[pl.BlockSpec((tm, n), lambda i: (i, 0)),
                   pl.BlockSpec((tm, n), lambda i: (i, 0))],
        out_shape=[jax.ShapeDtypeStruct((m, n), F32), jax.ShapeDtypeStruct((m, n), BF16)],
        scratch_shapes=[pltpu.VMEM((k, n), BF16)],
        compiler_params=_params("arbitrary"),
        name=name,
    )(ot, xres, w, g, b)


CONV_WIDTH = 3
FFN_TF = 512
FFN_UP_TM = 1024


def _ffn_up_kernel(tiles_per_seq, x_ref, wg_ref, wu_ref, cwg_ref, cwu_ref, cbg_ref, cbu_ref,
                   a_ref, wb_ref, h_ref):
    i = pl.program_id(1)
    tm = x_ref.shape[0]
    tf = a_ref.shape[1]
    halo = SUBLANES

    @pl.when(i == 0)
    def _():
        wb_ref[:, 0:tf] = wg_ref[...].astype(BF16)
        wb_ref[:, tf:2 * tf] = wu_ref[...].astype(BF16)

    @pl.when(i % tiles_per_seq == 0)
    def _():
        h_ref[0:halo, :] = jnp.zeros((halo, h_ref.shape[1]), F32)

    h_ref[halo:halo + tm, :] = jnp.dot(x_ref[...], wb_ref[...], preferred_element_type=F32)

    def conv(cols, cw_ref, cb_ref):
        out = cb_ref[...]
        for tap in range(CONV_WIDTH):
            start = halo - (CONV_WIDTH - 1) + tap
            out = out + cw_ref[tap:tap + 1, :] * h_ref[start:start + tm, cols]
        return out

    g = conv(slice(0, tf), cwg_ref, cbg_ref)
    u = conv(slice(tf, 2 * tf), cwu_ref, cbu_ref)
    a_ref[...] = (g * jax.nn.sigmoid(g) * u).astype(a_ref.dtype)
    h_ref[0:halo, :] = h_ref[tm:tm + halo, :]


def _ffn_up(xb, w_up, conv_w, conv_b, layer, seq):
    m, k = xb.shape
    tm, tf = FFN_UP_TM, FFN_TF
    nf = D_FF // tf
    cb2 = conv_b.reshape(conv_b.shape[0], 1, 2 * D_FF)
    return pl.pallas_call(
        functools.partial(_ffn_up_kernel, seq // tm),
        grid=(nf, m // tm),
        in_specs=[pl.BlockSpec((tm, k), lambda j, i: (i, 0)),
                  pl.BlockSpec((None, k, tf), lambda j, i: (layer, 0, j)),
                  pl.BlockSpec((None, k, tf), lambda j, i: (layer, 0, j + nf)),
                  pl.BlockSpec((None, CONV_WIDTH, tf), lambda j, i: (layer, 0, j)),
                  pl.BlockSpec((None, CONV_WIDTH, tf), lambda j, i: (layer, 0, j + nf)),
                  pl.BlockSpec((None, 1, tf), lambda j, i: (layer, 0, j)),
                  pl.BlockSpec((None, 1, tf), lambda j, i: (layer, 0, j + nf))],
        out_specs=pl.BlockSpec((tm, tf), lambda j, i: (i, j)),
        out_shape=jax.ShapeDtypeStruct((m, D_FF), BF16),
        scratch_shapes=[pltpu.VMEM((k, 2 * tf), BF16), pltpu.VMEM((tm + 2 * SUBLANES, 2 * tf), F32)],
        compiler_params=_params("arbitrary", "arbitrary"),
        name="ffn_up",
    )(xb, w_up, w_up, conv_w, conv_w, cb2, cb2)


FFN_DOWN_TM = 256
FFN_DOWN_WCHUNK = 512


def _ffn_down_ln_kernel(alpha, n_load, emit_bf16, a_ref, xres_ref, w_ref, g_ref, b_ref, *rest):
    if emit_bf16:
        xo_ref, xob_ref, wb_ref = rest
    else:
        (xo_ref, wb_ref), xob_ref = rest, None
    step = pl.program_id(0)
    rows = w_ref.shape[0]

    @pl.when(step < n_load)
    def _():
        off = pl.multiple_of(step * rows, rows)
        wb_ref[pl.ds(off, rows), :] = w_ref[...].astype(BF16)

    @pl.when(step >= n_load)
    def _():
        y = jnp.dot(a_ref[...], wb_ref[...], preferred_element_type=F32)
        out = _deepnorm(xres_ref[...], y, g_ref[...], b_ref[...], alpha)
        xo_ref[...] = out
        if emit_bf16:
            xob_ref[...] = out.astype(BF16)


def _ffn_down_ln(a, xres, w, layer, g, b, alpha, emit_bf16, name):
    m, k = a.shape
    n = w.shape[2]
    tm, rows = FFN_DOWN_TM, FFN_DOWN_WCHUNK
    n_load = k // rows
    tile = lambda s: (jnp.maximum(s - n_load, 0), 0)
    out_specs = [pl.BlockSpec((tm, n), tile)]
    out_shape = [jax.ShapeDtypeStruct((m, n), F32)]
    if emit_bf16:
        out_specs.append(pl.BlockSpec((tm, n), tile))
        out_shape.append(jax.ShapeDtypeStruct((m, n), BF16))
    return pl.pallas_call(
        functools.partial(_ffn_down_ln_kernel, alpha, n_load, emit_bf16),
        grid=(n_load + m // tm,),
        in_specs=[pl.BlockSpec((tm, k), tile),
                  pl.BlockSpec((tm, n), tile),
                  pl.BlockSpec((None, rows, n), lambda s: (layer, jnp.minimum(s, n_load - 1), 0)),
                  pl.BlockSpec((1, n), lambda s: (0, 0)),
                  pl.BlockSpec((1, n), lambda s: (0, 0))],
        out_specs=out_specs,
        out_shape=out_shape,
        scratch_shapes=[pltpu.VMEM((k, n), BF16)],
        compiler_params=_params("arbitrary"),
        name=name,
    )(a, xres, w, g, b)


def kernel(x, a_w_qkv, a_w_o, a_sinks, rel_bias, b_w_qkvf, b_f_bias, b_w_o,
           ffn_w_up, ffn_conv_w, ffn_conv_b, ffn_w_down, ln_g, ln_b):
    batch, seq, d = x.shape
    depth = ffn_w_up.shape[0]
    alpha = (2 * depth) ** 0.25
    m = batch * seq
    xf = x.reshape(m, d)
    xb = None
    biasmask = _swa_bias(rel_bias)
    for layer in range(depth):
        j = layer // 2
        g = ln_g[layer].reshape(2, 1, d)
        bt = ln_b[layer].reshape(2, 1, d)
        if layer % 2 == 0:
            n_qkv = D_MODEL + 2 * A_KV_DIM
            src = xf if xb is None else xb
            qkv = _ws_matmul(src, a_w_qkv, j, n_qkv, n_qkv // SWA_QKV_COL_TILES, SWA_QKV_TM, "swa_qkv",
                             slab_batch=(batch, seq))
            o = _swa_attention(qkv, a_sinks[j], biasmask)
            w_o = a_w_o
        else:
            w_t = jnp.swapaxes(b_w_qkvf, 1, 2)
            qkv = _ws_matmul(xb, w_t, j, 3 * D_MODEL, FOX_QKV_TN, FOX_QKV_TM, "fox_qkv",
                             w_is_transposed=True)
            fb3 = jnp.tile(b_f_bias[j].reshape(N_HEADS, 1), (FOX_PIECES, 1))
            cumt, kext = _fox_gate(xb, w_t, j, fb3, batch, seq)
            o = _fox_attention(qkv.reshape(batch, seq, 3 * D_MODEL), cumt, kext)
            w_o = b_w_o
        xf, xb = _oproj_ln(o, xf, w_o, j, g[0], bt[0], alpha, OPROJ_TM, "oproj_ln")
        a = _ffn_up(xb, ffn_w_up, ffn_conv_w, ffn_conv_b, layer, seq)
        last = layer == depth - 1
        res = _ffn_down_ln(a, xf, ffn_w_down, layer, g[1], bt[1], alpha, not last, "ffn_down_ln")
        xf, xb = (res[0], None) if last else res
    return xf.reshape(batch, seq, d)
```

```python
import functools
import math

import jax
import jax.numpy as jnp
import numpy as np
from jax import lax
from jax.experimental import pallas as pl
from jax.experimental.pallas import tpu as pltpu

D_MODEL = 2048
HEAD_DIM = 64
BLOCK = 128
N_HEADS = D_MODEL // HEAD_DIM
A_KV_HEADS = 4
A_GROUP = N_HEADS // A_KV_HEADS
A_KV_DIM = A_KV_HEADS * HEAD_DIM
REL_BUCKETS = 32
REL_MAX_DIST = 128
D_FF = 5632
LN_EPS = 1e-5
SCALE = HEAD_DIM ** -0.5

SUBLANES = 8
LANES = 128

BF16 = jnp.bfloat16
F32 = jnp.float32
NEG_INF = float("-inf")

_NT_DIMS = (((1,), (1,)), ((), ()))


def _params(*sem):
    return pltpu.CompilerParams(dimension_semantics=sem)


LOG2E = math.log2(math.e)
Q_SCALE = SCALE * LOG2E
SWA_QKV_COL_TILES = 2
SWA_QKV_TM = 1024
FOX_QKV_TN = 1024
FOX_QKV_TM = 2048
OPROJ_TM = 512


def _ws_matmul_kernel(w_is_transposed, x_ref, w_ref, o_ref, wb_ref):
    @pl.when(pl.program_id(1) == 0)
    def _():
        wb_ref[...] = w_ref[...].astype(BF16)

    x = x_ref[...].astype(BF16)
    if w_is_transposed:
        y = lax.dot_general(x, wb_ref[...], _NT_DIMS, preferred_element_type=F32)
    else:
        y = jnp.dot(x, wb_ref[...], preferred_element_type=F32)
    tn = y.shape[1]
    col = pl.program_id(0) * tn + lax.broadcasted_iota(jnp.int32, (1, tn), 1)
    y = (y * jnp.where(col < D_MODEL, Q_SCALE, 1.0)).astype(o_ref.dtype)
    if len(o_ref.shape) == 2:
        o_ref[...] = y
    else:
        for c in range(o_ref.shape[0]):
            o_ref[c] = y[:, c * LANES:(c + 1) * LANES]


def _ws_matmul(x, w, layer, n_out, tn, tm, name, w_is_transposed=False, slab_batch=None):
    m, k = x.shape
    if w_is_transposed:
        w_spec = pl.BlockSpec((None, tn, k), lambda j, i: (layer, j, 0))
        wb_shape = (tn, k)
    else:
        w_spec = pl.BlockSpec((None, k, tn), lambda j, i: (layer, 0, j))
        wb_shape = (k, tn)
    if slab_batch is None:
        out_spec = pl.BlockSpec((tm, tn), lambda j, i: (i, j))
        out_shape = jax.ShapeDtypeStruct((m, n_out), BF16)
    else:
        batch, seq = slab_batch
        per_seq = seq // tm
        out_spec = pl.BlockSpec((None, tn // LANES, tm, LANES),
                                lambda j, i: (i // per_seq, j, i % per_seq, 0))
        out_shape = jax.ShapeDtypeStruct((batch, n_out // LANES, seq, LANES), BF16)
    return pl.pallas_call(
        functools.partial(_ws_matmul_kernel, w_is_transposed),
        grid=(n_out // tn, m // tm),
        in_specs=[pl.BlockSpec((tm, k), lambda j, i: (i, 0)), w_spec],
        out_specs=out_spec,
        out_shape=out_shape,
        scratch_shapes=[pltpu.VMEM(wb_shape, BF16)],
        compiler_params=_params("arbitrary", "arbitrary"),
        name=name,
    )(x, w)


def _t5_bucket_map():
    qi = np.arange(BLOCK)[None, :]
    kj = np.arange(2 * BLOCK)[:, None]
    dist = BLOCK + qi - kj
    n = np.maximum(dist, 0)
    max_exact = REL_BUCKETS // 2
    nf = np.maximum(n, 1).astype(np.float32)
    large = max_exact + (np.log(nf / max_exact) / math.log(REL_MAX_DIST / max_exact)
                         * (REL_BUCKETS - max_exact)).astype(np.int32)
    large = np.minimum(large, REL_BUCKETS - 1)
    bucket = np.where(n < max_exact, n, large)
    visible = (dist >= 0) & (dist < BLOCK)
    return np.where(visible, bucket, -1).astype(np.int32)


def _swa_bias_kernel(rb_ref, bk_ref, o_ref):
    bk = bk_ref[...]
    key = lax.broadcasted_iota(jnp.int32, bk.shape, 0)

    def head(h, _):
        acc = jnp.full(bk.shape, NEG_INF, F32)
        for b in range(REL_BUCKETS):
            acc = jnp.where(bk == b, rb_ref[b, h] * LOG2E, acc)
        o_ref[0, h] = acc
        o_ref[1, h] = jnp.where(key < BLOCK, NEG_INF, acc)
        return 0

    lax.fori_loop(0, N_HEADS, head, 0)


def _swa_bias(rel_bias):
    bk = jnp.asarray(_t5_bucket_map())
    return pl.pallas_call(
        _swa_bias_kernel,
        in_specs=[pl.BlockSpec(memory_space=pltpu.SMEM),
                  pl.BlockSpec(memory_space=pltpu.VMEM)],
        out_specs=pl.BlockSpec(memory_space=pltpu.VMEM),
        out_shape=jax.ShapeDtypeStruct((2, N_HEADS, 2 * BLOCK, BLOCK), F32),
        name="swa_bias",
    )(rel_bias, bk)


_TN_DIMS = (((0,), (0,)), ((), ()))
SWA_SCORES_AHEAD = 2
SWA_BLOCKS_PER_STEP = 4


def _swa_kernel(sink_ref, q_ref, kp_ref, kc_ref, vp_ref, vc_ref, bm_ref, o_ref,
                qs_ref, s_ref, p_ref, linv_ref):
    units = [(qb, kh) for qb in range(SWA_BLOCKS_PER_STEP) for kh in range(A_KV_HEADS)]
    first_table = jnp.where(pl.program_id(1) == 0, 1, 0)

    def head_at(head):
        slab, half = divmod(head, LANES // HEAD_DIM)
        return slab, slice(half * HEAD_DIM, (half + 1) * HEAD_DIM)

    def keys_values(prev_ref, cur_ref, qb, kh):
        slab, lanes = head_at(kh)
        if qb == 0:
            return jnp.concatenate([prev_ref[slab, :, lanes], cur_ref[slab, 0:BLOCK, lanes]], axis=0)
        return cur_ref[slab, (qb - 1) * BLOCK:(qb + 1) * BLOCK, lanes]

    def scores(u):
        qb, kh = units[u]
        for g in range(A_GROUP):
            slab, lanes = head_at(kh * A_GROUP + g)
            qs_ref[u, g * BLOCK:(g + 1) * BLOCK, :] = q_ref[slab, qb * BLOCK:(qb + 1) * BLOCK, lanes]
        k2 = keys_values(kp_ref, kc_ref, qb, kh)
        s = lax.dot_general(k2, qs_ref[u], _NT_DIMS, preferred_element_type=F32)
        for g in range(A_GROUP):
            s_ref[u, g] = s[:, g * BLOCK:(g + 1) * BLOCK]

    def softmax(u):
        qb, kh = units[u]
        table = first_table if qb == 0 else 0
        for g in range(A_GROUP):
            h = kh * A_GROUP + g
            cols = slice(g * BLOCK, (g + 1) * BLOCK)
            s = s_ref[u, g] + bm_ref[table, h]
            sink = sink_ref[h] * LOG2E
            m = jnp.maximum(jnp.max(s, axis=0, keepdims=True), sink)
            p = jnp.exp2(s - m)
            denom = jnp.sum(p, axis=0, keepdims=True) + jnp.exp2(sink - m)
            p_ref[u, g // 2, :, (g % 2) * BLOCK:(g % 2 + 1) * BLOCK] = p.astype(BF16)
            linv_ref[u, :, cols] = 1.0 / denom

    def weighted_values(u):
        qb, kh = units[u]
        v2 = keys_values(vp_ref, vc_ref, qb, kh)
        for c in range(A_GROUP // 2):
            cols = slice(2 * c * BLOCK, 2 * (c + 1) * BLOCK)
            ot = (lax.dot_general(v2, p_ref[u, c], _TN_DIMS, preferred_element_type=F32)
                  * linv_ref[u, :, cols])
            for gg in range(2):
                h = kh * A_GROUP + 2 * c + gg
                o_ref[h * HEAD_DIM:(h + 1) * HEAD_DIM, qb * BLOCK:(qb + 1) * BLOCK] = (
                    ot[:, gg * BLOCK:(gg + 1) * BLOCK].astype(o_ref.dtype))

    for u in range(min(SWA_SCORES_AHEAD, len(units))):
        scores(u)
    for u in range(len(units)):
        if u + SWA_SCORES_AHEAD < len(units):
            scores(u + SWA_SCORES_AHEAD)
        softmax(u)
        weighted_values(u)


def _swa_attention(qkv, sinks, biasmask):
    b, _, s, _ = qkv.shape
    rows = SWA_BLOCKS_PER_STEP * BLOCK
    gq = A_GROUP * BLOCK
    n_units = SWA_BLOCKS_PER_STEP * A_KV_HEADS
    q_slabs = D_MODEL // LANES
    kv_slabs = A_KV_DIM // LANES
    kcol = q_slabs // kv_slabs
    prev = lambda n: jnp.maximum(SWA_BLOCKS_PER_STEP * n - 1, 0)
    return pl.pallas_call(
        _swa_kernel,
        grid=(b, s // rows),
        in_specs=[pl.BlockSpec(memory_space=pltpu.SMEM),
                  pl.BlockSpec((None, q_slabs, rows, LANES), lambda bi, n: (bi, 0, n, 0)),
                  pl.BlockSpec((None, kv_slabs, BLOCK, LANES), lambda bi, n: (bi, kcol, prev(n), 0)),
                  pl.BlockSpec((None, kv_slabs, rows, LANES), lambda bi, n: (bi, kcol, n, 0)),
                  pl.BlockSpec((None, kv_slabs, BLOCK, LANES), lambda bi, n: (bi, kcol + 1, prev(n), 0)),
                  pl.BlockSpec((None, kv_slabs, rows, LANES), lambda bi, n: (bi, kcol + 1, n, 0)),
                  pl.BlockSpec((2, N_HEADS, 2 * BLOCK, BLOCK), lambda bi, n: (0, 0, 0, 0),
                               pipeline_mode=pl.Buffered(1))],
        out_specs=pl.BlockSpec((None, D_MODEL, rows), lambda bi, n: (bi, 0, n)),
        out_shape=jax.ShapeDtypeStruct((b, D_MODEL, s), BF16),
        scratch_shapes=[pltpu.VMEM((n_units, gq, HEAD_DIM), BF16),
                        pltpu.VMEM((n_units, A_GROUP, 2 * BLOCK, BLOCK), F32),
                        pltpu.VMEM((n_units, A_GROUP // 2, 2 * BLOCK, 2 * BLOCK), BF16),
                        pltpu.VMEM((n_units, 1, gq), F32)],
        compiler_params=_params("arbitrary", "arbitrary"),
        name="swa_attn",
    )(sinks, qkv, qkv, qkv, qkv, qkv, biasmask)


FOX_CUM_BLOCK = 256
FOX_TQ = 256
FOX_HEADS_PER_PAIR = LANES // HEAD_DIM
FOX_PAIRS_PER_STEP = 2
FOX_PIECES = 3
FOX_SUM_ROWS = 16
FOX_SCORE_SLOTS = 4
FOX_PV_LAG = 1
FOX_PROB_SLOTS = FOX_PV_LAG + 1
BF16_ONE_PAIR = 0x3F803F80


def _fox_ext_lane(head, piece):
    pair, odd = divmod(head, FOX_HEADS_PER_PAIR)
    return pair * LANES + (1 - odd) * HEAD_DIM + piece


def _fox_placement():
    pm = np.zeros((FOX_PIECES * N_HEADS, D_MODEL), np.float32)
    for t in range(FOX_PIECES):
        for h in range(N_HEADS):
            pm[t * N_HEADS + h, _fox_ext_lane(h, t)] = -1.0
    return pm


def _fox_lane_words():
    w = np.zeros((SUBLANES, LANES), np.uint32)
    for hh in range(FOX_HEADS_PER_PAIR):
        own = np.zeros(LANES, bool)
        own[hh * HEAD_DIM:(hh + 1) * HEAD_DIM] = True
        ext = np.zeros(LANES, bool)
        first = _fox_ext_lane(hh, 0)
        ext[first:first + FOX_PIECES] = True
        w[3 * hh + 0] = np.where(own, 0xFFFFFFFF, 0)
        w[3 * hh + 1] = np.where(ext, 0xFFFFFFFF, 0)
        w[3 * hh + 2] = np.where(ext, BF16_ONE_PAIR, 0)
    return w


def _log_sigmoid(x):
    return jnp.minimum(x, 0.0) - jnp.log1p(jnp.exp(-jnp.abs(x)))


def _round_bf16(x):
    return x.astype(BF16).astype(F32)


def _fox_gate_kernel(x_ref, wf_ref, fb_ref, pm_ref, cumt_ref, kext_ref, w3_ref):
    @pl.when(pl.program_id(0) == 0)
    def _():
        w = wf_ref[...].astype(BF16)
        for t in range(FOX_PIECES):
            w3_ref[t * N_HEADS:(t + 1) * N_HEADS, :] = w

    fz = lax.dot_general(w3_ref[...], x_ref[...], _NT_DIMS, preferred_element_type=F32) + fb_ref[...]
    lf = _log_sigmoid(fz)
    cb = FOX_CUM_BLOCK
    row = lax.broadcasted_iota(jnp.int32, (cb, cb), 0)
    col = lax.broadcasted_iota(jnp.int32, (cb, cb), 1)
    tri = jnp.where(row <= col, 1.0, 0.0).astype(BF16)
    tri3 = jnp.concatenate([tri] * FOX_PIECES, axis=0)
    group = lax.broadcasted_iota(jnp.int32, (lf.shape[0], cb), 0) // N_HEADS
    pm = pm_ref[...]
    n_blocks = lf.shape[1] // cb
    local = []
    for blk in range(n_blocks):
        x = lf[:, blk * cb:(blk + 1) * cb]
        hi = _round_bf16(x)
        r1 = x - hi
        mid = _round_bf16(r1)
        pieces = jnp.concatenate([hi, mid, r1 - mid], axis=1).astype(BF16)
        local.append(jnp.dot(pieces, tri3, preferred_element_type=F32))
    carry = jnp.zeros((lf.shape[0], 1), F32)
    for blk in range(n_blocks):
        cols = slice(blk * cb, (blk + 1) * cb)
        c = local[blk] + carry
        carry = carry + local[blk][:, cb - 1:cb]
        c2 = c * LOG2E
        cumt_ref[:, cols] = c2[:N_HEADS]
        chi = _round_bf16(c2)
        cr1 = c2 - chi
        cmid = _round_bf16(cr1)
        piece = jnp.where(group == 0, chi, jnp.where(group == 1, cmid, cr1 - cmid)).astype(BF16)
        kext_ref[cols, :] = lax.dot_general(piece, pm, _TN_DIMS, preferred_element_type=F32).astype(BF16)


def _fox_gate(xb, w_qkvf_t, layer, fb3, batch, seq):
    pm = jnp.asarray(_fox_placement(), BF16)
    n3 = FOX_PIECES * N_HEADS
    gate_row = 3 * D_MODEL // N_HEADS
    return pl.pallas_call(
        _fox_gate_kernel,
        grid=(batch,),
        in_specs=[pl.BlockSpec((seq, D_MODEL), lambda bi: (bi, 0)),
                  pl.BlockSpec((None, N_HEADS, D_MODEL), lambda bi: (layer, gate_row, 0)),
                  pl.BlockSpec((n3, 1), lambda bi: (0, 0)),
                  pl.BlockSpec((n3, D_MODEL), lambda bi: (0, 0))],
        out_specs=[pl.BlockSpec((None, N_HEADS, seq), lambda bi: (bi, 0, 0)),
                   pl.BlockSpec((None, seq, D_MODEL), lambda bi: (bi, 0, 0))],
        out_shape=[jax.ShapeDtypeStruct((batch, N_HEADS, seq), F32),
                   jax.ShapeDtypeStruct((batch, seq, D_MODEL), BF16)],
        scratch_shapes=[pltpu.VMEM((n3, D_MODEL), BF16)],
        compiler_params=_params("arbitrary"),
        name="fox_gate",
    )(xb, w_qkvf_t, fb3, pm)


def _fox_kernel(q_ref, k_ref, v_ref, kext_ref, cbt_ref, lw_ref, o_ref, qa_ref, ka_ref, vt_ref,
                s_ref, p_ref):
    seq = q_ref.shape[0]
    tq = FOX_TQ
    key = lax.broadcasted_iota(jnp.int32, (tq, tq), 0)
    qry = lax.broadcasted_iota(jnp.int32, (tq, tq), 1)
    causal = key <= qry
    words = lambda x: pltpu.bitcast(x, jnp.uint32)
    for pair in range(FOX_PAIRS_PER_STEP):
        lanes = slice(pair * LANES, (pair + 1) * LANES)
        q = words(q_ref[:, lanes])
        k = words(k_ref[:, lanes])
        kext = words(kext_ref[:, lanes])
        vt = v_ref[:, lanes].T
        for hh in range(FOX_HEADS_PER_PAIR):
            h = pair * FOX_HEADS_PER_PAIR + hh
            own = lw_ref[3 * hh + 0:3 * hh + 1, :]
            ext = lw_ref[3 * hh + 1:3 * hh + 2, :]
            ones = lw_ref[3 * hh + 2:3 * hh + 3, :]
            qa_ref[h] = pltpu.bitcast((q & own) | ones, BF16)
            ka_ref[h] = pltpu.bitcast((k & ~ext) | (kext & ext), BF16)
            vt_ref[h, 0:HEAD_DIM, :] = vt[hh * HEAD_DIM:(hh + 1) * HEAD_DIM]
            vt_ref[h, HEAD_DIM:, :] = jnp.ones((FOX_SUM_ROWS, seq), BF16)

    heads = FOX_PAIRS_PER_STEP * FOX_HEADS_PER_PAIR
    units = [(i, h) for i in range(seq // tq) for h in range(heads)]

    def scores(u):
        i, h = units[u]
        slot = u % FOX_SCORE_SLOTS
        r0, r1 = i * tq, (i + 1) * tq
        qa = qa_ref[h, r0:r1, :]
        s_d = lax.dot_general(ka_ref[h, r0:r1, :], qa, _NT_DIMS, preferred_element_type=F32)
        s_ref[slot, r0:r1, :] = jnp.where(causal, s_d, NEG_INF)
        if i > 0:
            s_ref[slot, 0:r0, :] = lax.dot_general(ka_ref[h, 0:r0, :], qa, _NT_DIMS,
                                                   preferred_element_type=F32)

    def softmax(u):
        i, h = units[u]
        slot = u % FOX_SCORE_SLOTS
        r0, r1 = i * tq, (i + 1) * tq
        head = pl.program_id(1) * heads + h
        cbq = cbt_ref[pl.ds(head, 1), r0:r1]
        m = jnp.max(s_ref[slot, 0:r1, :], axis=0, keepdims=True)
        shift = cbq - (m + cbq)
        p_ref[u % FOX_PROB_SLOTS, 0:r1, :] = jnp.exp2(s_ref[slot, 0:r1, :] + shift).astype(BF16)

    def weighted_values(u):
        i, h = units[u]
        r0, r1 = i * tq, (i + 1) * tq
        ot = jnp.dot(vt_ref[h, :, 0:r1], p_ref[u % FOX_PROB_SLOTS, 0:r1, :], preferred_element_type=F32)
        out = ot[0:HEAD_DIM] * (1.0 / ot[HEAD_DIM:HEAD_DIM + 1])
        o_ref[h * HEAD_DIM:(h + 1) * HEAD_DIM, r0:r1] = out.astype(o_ref.dtype)

    ahead = FOX_SCORE_SLOTS - 1
    for u in range(min(ahead, len(units))):
        scores(u)
    for u in range(len(units)):
        if u + ahead < len(units):
            scores(u + ahead)
        softmax(u)
        if u >= FOX_PV_LAG:
            weighted_values(u - FOX_PV_LAG)
    for u in range(len(units) - FOX_PV_LAG, len(units)):
        weighted_values(u)


def _fox_attention(qkv, cumt, kext):
    b, s, _ = qkv.shape
    heads = FOX_PAIRS_PER_STEP * FOX_HEADS_PER_PAIR
    width = FOX_PAIRS_PER_STEP * LANES
    lw = jnp.asarray(_fox_lane_words())
    ncol = D_MODEL // width
    return pl.pallas_call(
        _fox_kernel,
        grid=(b, N_HEADS // heads),
        in_specs=[pl.BlockSpec((None, s, width), lambda bi, hp: (bi, 0, hp)),
                  pl.BlockSpec((None, s, width), lambda bi, hp: (bi, 0, ncol + hp)),
                  pl.BlockSpec((None, s, width), lambda bi, hp: (bi, 0, 2 * ncol + hp)),
                  pl.BlockSpec((None, s, width), lambda bi, hp: (bi, 0, hp)),
                  pl.BlockSpec((None, N_HEADS, s), lambda bi, hp: (bi, 0, 0)),
                  pl.BlockSpec((SUBLANES, LANES), lambda bi, hp: (0, 0))],
        out_specs=pl.BlockSpec((None, heads * HEAD_DIM, s), lambda bi, hp: (bi, hp, 0)),
        out_shape=jax.ShapeDtypeStruct((b, D_MODEL, s), BF16),
        scratch_shapes=[pltpu.VMEM((heads, s, LANES), BF16), pltpu.VMEM((heads, s, LANES), BF16),
                        pltpu.VMEM((heads, HEAD_DIM + FOX_SUM_ROWS, s), BF16),
                        pltpu.VMEM((FOX_SCORE_SLOTS, s, FOX_TQ), F32),
                        pltpu.VMEM((FOX_PROB_SLOTS, s, FOX_TQ), BF16)],
        compiler_params=_params("arbitrary", "arbitrary"),
        name="fox_attn",
    )(qkv, qkv, qkv, kext, cumt, lw)


def _deepnorm(xres, y, g, b, alpha):
    z = alpha * xres + y
    mu = jnp.mean(z, axis=-1, keepdims=True)
    zc = z - mu
    var = jnp.mean(zc * zc, axis=-1, keepdims=True)
    return zc * lax.rsqrt(var + LN_EPS) * g + b


LN_SUBTILES = 2


def _oproj_ln_kernel(alpha, ot_ref, xres_ref, w_ref, g_ref, b_ref, xo_ref, xob_ref, wb_ref):
    @pl.when(pl.program_id(0) == 0)
    def _():
        wb_ref[...] = w_ref[...].astype(BF16)

    ts = xres_ref.shape[0] // LN_SUBTILES
    ys = [lax.dot_general(ot_ref[:, r * ts:(r + 1) * ts], wb_ref[...], _TN_DIMS,
                          preferred_element_type=F32) for r in range(LN_SUBTILES)]
    for r, y in enumerate(ys):
        rows = slice(r * ts, (r + 1) * ts)
        out = _deepnorm(xres_ref[rows, :], y, g_ref[...], b_ref[...], alpha)
        xo_ref[rows, :] = out
        xob_ref[rows, :] = out.astype(BF16)


def _oproj_ln(ot, xres, w, layer, g, b, alpha, tm, name):
    m, n = xres.shape
    k = w.shape[1]
    per_seq = ot.shape[2] // tm
    return pl.pallas_call(
        functools.partial(_oproj_ln_kernel, alpha),
        grid=(m // tm,),
        in_specs=[pl.BlockSpec((None, k, tm), lambda i: (i // per_seq, 0, i % per_seq)),
                  pl.BlockSpec((tm, n), lambda i: (i, 0)),
                  pl.BlockSpec((None, k, n), lambda i: (layer, 0, 0), pipeline_mode=pl.Buffered(1)),
                  pl.BlockSpec((1, n), lambda i: (0, 0)),
                  pl.BlockSpec((1, n), lambda i: (0, 0))],
        out_specs=[pl.BlockSpec((tm, n), lambda i: (i, 0)),
                   pl.BlockSpec((tm, n), lambda i: (i, 0))],
        out_shape=[jax.ShapeDtypeStruct((m, n), F32), jax.ShapeDtypeStruct((m, n), BF16)],
        scratch_shapes=[pltpu.VMEM((k, n), BF16)],
        compiler_params=_params("arbitrary"),
        name=name,
    )(ot, xres, w, g, b)


CONV_WIDTH = 3
FFN_TF = 512
FFN_UP_TM = 1024


def _ffn_up_kernel(tiles_per_seq, x_ref, wg_ref, wu_ref, cwg_ref, cwu_ref, cbg_ref, cbu_ref, wd_ref,
                   a_ref, wdb_ref, wb_ref, h_ref):
    wdb_ref[...] = wd_ref[...].astype(BF16)
    i = pl.program_id(1)
    tm = x_ref.shape[0]
    tf = a_ref.shape[1]
    halo = SUBLANES

    @pl.when(i == 0)
    def _():
        wb_ref[:, 0:tf] = wg_ref[...].astype(BF16)
        wb_ref[:, tf:2 * tf] = wu_ref[...].astype(BF16)

    @pl.when(i % tiles_per_seq == 0)
    def _():
        h_ref[0:halo, :] = jnp.zeros((halo, h_ref.shape[1]), F32)

    h_ref[halo:halo + tm, :] = jnp.dot(x_ref[...], wb_ref[...], preferred_element_type=F32)

    def conv(cols, cw_ref, cb_ref):
        out = cb_ref[...]
        for tap in range(CONV_WIDTH):
            start = halo - (CONV_WIDTH - 1) + tap
            out = out + cw_ref[tap:tap + 1, :] * h_ref[start:start + tm, cols]
        return out

    g = conv(slice(0, tf), cwg_ref, cbg_ref)
    u = conv(slice(tf, 2 * tf), cwu_ref, cbu_ref)
    a_ref[...] = (g * jax.nn.sigmoid(g) * u).astype(a_ref.dtype)
    h_ref[0:halo, :] = h_ref[tm:tm + halo, :]


def _ffn_up(xb, w_up, conv_w, conv_b, w_down, layer, seq):
    m, k = xb.shape
    tm, tf = FFN_UP_TM, FFN_TF
    nf = D_FF // tf
    n_tiles = m // tm
    cb2 = conv_b.reshape(conv_b.shape[0], 1, 2 * D_FF)
    wd_rows = w_down.shape[1] // (nf * n_tiles)
    assert wd_rows * nf * n_tiles == w_down.shape[1] and wd_rows % (2 * SUBLANES) == 0
    return pl.pallas_call(
        functools.partial(_ffn_up_kernel, seq // tm),
        grid=(nf, n_tiles),
        in_specs=[pl.BlockSpec((tm, k), lambda j, i: (i, 0)),
                  pl.BlockSpec((None, k, tf), lambda j, i: (layer, 0, j)),
                  pl.BlockSpec((None, k, tf), lambda j, i: (layer, 0, j + nf)),
                  pl.BlockSpec((None, CONV_WIDTH, tf), lambda j, i: (layer, 0, j)),
                  pl.BlockSpec((None, CONV_WIDTH, tf), lambda j, i: (layer, 0, j + nf)),
                  pl.BlockSpec((None, 1, tf), lambda j, i: (layer, 0, j)),
                  pl.BlockSpec((None, 1, tf), lambda j, i: (layer, 0, j + nf)),
                  pl.BlockSpec((None, wd_rows, w_down.shape[2]), lambda j, i: (layer, j * n_tiles + i, 0))],
        out_specs=[pl.BlockSpec((tm, tf), lambda j, i: (i, j)),
                   pl.BlockSpec((wd_rows, w_down.shape[2]), lambda j, i: (j * n_tiles + i, 0))],
        out_shape=[jax.ShapeDtypeStruct((m, D_FF), BF16),
                   jax.ShapeDtypeStruct(w_down.shape[1:], BF16)],
        scratch_shapes=[pltpu.VMEM((k, 2 * tf), BF16), pltpu.VMEM((tm + 2 * SUBLANES, 2 * tf), F32)],
        compiler_params=_params("arbitrary", "arbitrary"),
        name="ffn_up",
    )(xb, w_up, w_up, conv_w, conv_w, cb2, cb2, w_down)


FFN_DOWN_TM = 256


def _ffn_down_ln_kernel(alpha, emit_bf16, a_ref, xres_ref, w_ref, g_ref, b_ref, xo_ref, *rest):
    y = jnp.dot(a_ref[...], w_ref[...], preferred_element_type=F32)
    out = _deepnorm(xres_ref[...], y, g_ref[...], b_ref[...], alpha)
    xo_ref[...] = out
    if emit_bf16:
        rest[0][...] = out.astype(BF16)


def _ffn_down_ln(a, xres, w_bf16, g, b, alpha, emit_bf16, name):
    m, k = a.shape
    n = w_bf16.shape[1]
    tm = FFN_DOWN_TM
    out_specs = [pl.BlockSpec((tm, n), lambda i: (i, 0))]
    out_shape = [jax.ShapeDtypeStruct((m, n), F32)]
    if emit_bf16:
        out_specs.append(pl.BlockSpec((tm, n), lambda i: (i, 0)))
        out_shape.append(jax.ShapeDtypeStruct((m, n), BF16))
    return pl.pallas_call(
        functools.partial(_ffn_down_ln_kernel, alpha, emit_bf16),
        grid=(m // tm,),
        in_specs=[pl.BlockSpec((tm, k), lambda i: (i, 0)),
                  pl.BlockSpec((tm, n), lambda i: (i, 0)),
                  pl.BlockSpec((k, n), lambda i: (0, 0), pipeline_mode=pl.Buffered(1)),
                  pl.BlockSpec((1, n), lambda i: (0, 0)),
                  pl.BlockSpec((1, n), lambda i: (0, 0))],
        out_specs=out_specs,
        out_shape=out_shape,
        compiler_params=_params("arbitrary"),
        name=name,
    )(a, xres, w_bf16, g, b)


def kernel(x, a_w_qkv, a_w_o, a_sinks, rel_bias, b_w_qkvf, b_f_bias, b_w_o,
           ffn_w_up, ffn_conv_w, ffn_conv_b, ffn_w_down, ln_g, ln_b):
    batch, seq, d = x.shape
    depth = ffn_w_up.shape[0]
    alpha = (2 * depth) ** 0.25
    m = batch * seq
    xf = x.reshape(m, d)
    xb = None
    biasmask = _swa_bias(rel_bias)
    for layer in range(depth):
        j = layer // 2
        g = ln_g[layer].reshape(2, 1, d)
        bt = ln_b[layer].reshape(2, 1, d)
        if layer % 2 == 0:
            n_qkv = D_MODEL + 2 * A_KV_DIM
            src = xf if xb is None else xb
            qkv = _ws_matmul(src, a_w_qkv, j, n_qkv, n_qkv // SWA_QKV_COL_TILES, SWA_QKV_TM, "swa_qkv",
                             slab_batch=(batch, seq))
            o = _swa_attention(qkv, a_sinks[j], biasmask)
            w_o = a_w_o
        else:
            w_t = jnp.swapaxes(b_w_qkvf, 1, 2)
            qkv = _ws_matmul(xb, w_t, j, 3 * D_MODEL, FOX_QKV_TN, FOX_QKV_TM, "fox_qkv",
                             w_is_transposed=True)
            fb3 = jnp.tile(b_f_bias[j].reshape(N_HEADS, 1), (FOX_PIECES, 1))
            cumt, kext = _fox_gate(xb, w_t, j, fb3, batch, seq)
            o = _fox_attention(qkv.reshape(batch, seq, 3 * D_MODEL), cumt, kext)
            w_o = b_w_o
        xf, xb = _oproj_ln(o, xf, w_o, j, g[0], bt[0], alpha, OPROJ_TM, "oproj_ln")
        a, w_down = _ffn_up(xb, ffn_w_up, ffn_conv_w, ffn_conv_b, ffn_w_down, layer, seq)
        last = layer == depth - 1
        res = _ffn_down_ln(a, xf, w_down, g[1], bt[1], alpha, not last, "ffn_down_ln")
        xf, xb = (res[0], None) if last else res
    return xf.reshape(batch, seq, d)
```

```python
import functools
import math

import jax
import jax.numpy as jnp
import numpy as np
from jax import lax
from jax.experimental import pallas as pl
from jax.experimental.pallas import tpu as pltpu

D_MODEL = 2048
HEAD_DIM = 64
BLOCK = 128
N_HEADS = D_MODEL // HEAD_DIM
A_KV_HEADS = 4
A_GROUP = N_HEADS // A_KV_HEADS
A_KV_DIM = A_KV_HEADS * HEAD_DIM
REL_BUCKETS = 32
REL_MAX_DIST = 128
D_FF = 5632
LN_EPS = 1e-5
SCALE = HEAD_DIM ** -0.5

SUBLANES = 8
LANES = 128

BF16 = jnp.bfloat16
F32 = jnp.float32
NEG_INF = float("-inf")

_NT_DIMS = (((1,), (1,)), ((), ()))


def _params(*sem):
    return pltpu.CompilerParams(dimension_semantics=sem)


LOG2E = math.log2(math.e)
Q_SCALE = SCALE * LOG2E
SWA_QKV_COL_TILES = 2
SWA_QKV_TM = 1024
FOX_QKV_TN = 1024
FOX_QKV_TM = 2048
OPROJ_TM = 512


def _ws_matmul_kernel(w_is_transposed, x_ref, w_ref, o_ref, wb_ref):
    @pl.when(pl.program_id(1) == 0)
    def _():
        wb_ref[...] = w_ref[...].astype(BF16)

    x = x_ref[...].astype(BF16)
    if w_is_transposed:
        y = lax.dot_general(x, wb_ref[...], _NT_DIMS, preferred_element_type=F32)
    else:
        y = jnp.dot(x, wb_ref[...], preferred_element_type=F32)
    tn = y.shape[1]
    col = pl.program_id(0) * tn + lax.broadcasted_iota(jnp.int32, (1, tn), 1)
    y = (y * jnp.where(col < D_MODEL, Q_SCALE, 1.0)).astype(o_ref.dtype)
    if len(o_ref.shape) == 2:
        o_ref[...] = y
    else:
        for c in range(o_ref.shape[0]):
            o_ref[c] = y[:, c * LANES:(c + 1) * LANES]


def _ws_matmul(x, w, layer, n_out, tn, tm, name, w_is_transposed=False, slab_batch=None):
    m, k = x.shape
    if w_is_transposed:
        w_spec = pl.BlockSpec((None, tn, k), lambda j, i: (layer, j, 0))
        wb_shape = (tn, k)
    else:
        w_spec = pl.BlockSpec((None, k, tn), lambda j, i: (layer, 0, j))
        wb_shape = (k, tn)
    if slab_batch is None:
        out_spec = pl.BlockSpec((tm, tn), lambda j, i: (i, j))
        out_shape = jax.ShapeDtypeStruct((m, n_out), BF16)
    else:
        batch, seq = slab_batch
        per_seq = seq // tm
        out_spec = pl.BlockSpec((None, tn // LANES, tm, LANES),
                                lambda j, i: (i // per_seq, j, i % per_seq, 0))
        out_shape = jax.ShapeDtypeStruct((batch, n_out // LANES, seq, LANES), BF16)
    return pl.pallas_call(
        functools.partial(_ws_matmul_kernel, w_is_transposed),
        grid=(n_out // tn, m // tm),
        in_specs=[pl.BlockSpec((tm, k), lambda j, i: (i, 0)), w_spec],
        out_specs=out_spec,
        out_shape=out_shape,
        scratch_shapes=[pltpu.VMEM(wb_shape, BF16)],
        compiler_params=_params("arbitrary", "arbitrary"),
        name=name,
    )(x, w)


def _t5_bucket_map():
    qi = np.arange(BLOCK)[None, :]
    kj = np.arange(2 * BLOCK)[:, None]
    dist = BLOCK + qi - kj
    n = np.maximum(dist, 0)
    max_exact = REL_BUCKETS // 2
    nf = np.maximum(n, 1).astype(np.float32)
    large = max_exact + (np.log(nf / max_exact) / math.log(REL_MAX_DIST / max_exact)
                         * (REL_BUCKETS - max_exact)).astype(np.int32)
    large = np.minimum(large, REL_BUCKETS - 1)
    bucket = np.where(n < max_exact, n, large)
    visible = (dist >= 0) & (dist < BLOCK)
    return np.where(visible, bucket, -1).astype(np.int32)


def _swa_bias_kernel(rb_ref, bk_ref, o_ref):
    bk = bk_ref[...]
    key = lax.broadcasted_iota(jnp.int32, bk.shape, 0)

    def head(h, _):
        acc = jnp.full(bk.shape, NEG_INF, F32)
        for b in range(REL_BUCKETS):
            acc = jnp.where(bk == b, rb_ref[b, h] * LOG2E, acc)
        o_ref[0, h] = acc
        o_ref[1, h] = jnp.where(key < BLOCK, NEG_INF, acc)
        return 0

    lax.fori_loop(0, N_HEADS, head, 0)


def _swa_bias(rel_bias):
    bk = jnp.asarray(_t5_bucket_map())
    return pl.pallas_call(
        _swa_bias_kernel,
        in_specs=[pl.BlockSpec(memory_space=pltpu.SMEM),
                  pl.BlockSpec(memory_space=pltpu.VMEM)],
        out_specs=pl.BlockSpec(memory_space=pltpu.VMEM),
        out_shape=jax.ShapeDtypeStruct((2, N_HEADS, 2 * BLOCK, BLOCK), F32),
        name="swa_bias",
    )(rel_bias, bk)


_TN_DIMS = (((0,), (0,)), ((), ()))
SWA_SCORES_AHEAD = 2
SWA_BLOCKS_PER_STEP = 4


def _swa_kernel(sink_ref, q_ref, kp_ref, kc_ref, vp_ref, vc_ref, bm_ref, o_ref,
                qs_ref, s_ref, p_ref, linv_ref):
    units = [(qb, kh) for qb in range(SWA_BLOCKS_PER_STEP) for kh in range(A_KV_HEADS)]
    first_table = jnp.where(pl.program_id(1) == 0, 1, 0)

    def head_at(head):
        slab, half = divmod(head, LANES // HEAD_DIM)
        return slab, slice(half * HEAD_DIM, (half + 1) * HEAD_DIM)

    def keys_values(prev_ref, cur_ref, qb, kh):
        slab, lanes = head_at(kh)
        if qb == 0:
            return jnp.concatenate([prev_ref[slab, :, lanes], cur_ref[slab, 0:BLOCK, lanes]], axis=0)
        return cur_ref[slab, (qb - 1) * BLOCK:(qb + 1) * BLOCK, lanes]

    def scores(u):
        qb, kh = units[u]
        for g in range(A_GROUP):
            slab, lanes = head_at(kh * A_GROUP + g)
            qs_ref[u, g * BLOCK:(g + 1) * BLOCK, :] = q_ref[slab, qb * BLOCK:(qb + 1) * BLOCK, lanes]
        k2 = keys_values(kp_ref, kc_ref, qb, kh)
        s = lax.dot_general(k2, qs_ref[u], _NT_DIMS, preferred_element_type=F32)
        for g in range(A_GROUP):
            s_ref[u, g] = s[:, g * BLOCK:(g + 1) * BLOCK]

    def softmax(u):
        qb, kh = units[u]
        table = first_table if qb == 0 else 0
        for g in range(A_GROUP):
            h = kh * A_GROUP + g
            cols = slice(g * BLOCK, (g + 1) * BLOCK)
            s = s_ref[u, g] + bm_ref[table, h]
            sink = sink_ref[h] * LOG2E
            m = jnp.maximum(jnp.max(s, axis=0, keepdims=True), sink)
            p = jnp.exp2(s - m)
            denom = jnp.sum(p, axis=0, keepdims=True) + jnp.exp2(sink - m)
            p_ref[u, g // 2, :, (g % 2) * BLOCK:(g % 2 + 1) * BLOCK] = p.astype(BF16)
            linv_ref[u, :, cols] = 1.0 / denom

    def weighted_values(u):
        qb, kh = units[u]
        v2 = keys_values(vp_ref, vc_ref, qb, kh)
        for c in range(A_GROUP // 2):
            cols = slice(2 * c * BLOCK, 2 * (c + 1) * BLOCK)
            ot = (lax.dot_general(v2, p_ref[u, c], _TN_DIMS, preferred_element_type=F32)
                  * linv_ref[u, :, cols])
            for gg in range(2):
                h = kh * A_GROUP + 2 * c + gg
                o_ref[h * HEAD_DIM:(h + 1) * HEAD_DIM, qb * BLOCK:(qb + 1) * BLOCK] = (
                    ot[:, gg * BLOCK:(gg + 1) * BLOCK].astype(o_ref.dtype))

    for u in range(min(SWA_SCORES_AHEAD, len(units))):
        scores(u)
    for u in range(len(units)):
        if u + SWA_SCORES_AHEAD < len(units):
            scores(u + SWA_SCORES_AHEAD)
        softmax(u)
        weighted_values(u)


def _swa_attention(qkv, sinks, biasmask):
    b, _, s, _ = qkv.shape
    rows = SWA_BLOCKS_PER_STEP * BLOCK
    gq = A_GROUP * BLOCK
    n_units = SWA_BLOCKS_PER_STEP * A_KV_HEADS
    q_slabs = D_MODEL // LANES
    kv_slabs = A_KV_DIM // LANES
    kcol = q_slabs // kv_slabs
    prev = lambda n: jnp.maximum(SWA_BLOCKS_PER_STEP * n - 1, 0)
    return pl.pallas_call(
        _swa_kernel,
        grid=(b, s // rows),
        in_specs=[pl.BlockSpec(memory_space=pltpu.SMEM),
                  pl.BlockSpec((None, q_slabs, rows, LANES), lambda bi, n: (bi, 0, n, 0)),
                  pl.BlockSpec((None, kv_slabs, BLOCK, LANES), lambda bi, n: (bi, kcol, prev(n), 0)),
                  pl.BlockSpec((None, kv_slabs, rows, LANES), lambda bi, n: (bi, kcol, n, 0)),
                  pl.BlockSpec((None, kv_slabs, BLOCK, LANES), lambda bi, n: (bi, kcol + 1, prev(n), 0)),
                  pl.BlockSpec((None, kv_slabs, rows, LANES), lambda bi, n: (bi, kcol + 1, n, 0)),
                  pl.BlockSpec((2, N_HEADS, 2 * BLOCK, BLOCK), lambda bi, n: (0, 0, 0, 0),
                               pipeline_mode=pl.Buffered(1))],
        out_specs=pl.BlockSpec((None, D_MODEL, rows), lambda bi, n: (bi, 0, n)),
        out_shape=jax.ShapeDtypeStruct((b, D_MODEL, s), BF16),
        scratch_shapes=[pltpu.VMEM((n_units, gq, HEAD_DIM), BF16),
                        pltpu.VMEM((n_units, A_GROUP, 2 * BLOCK, BLOCK), F32),
                        pltpu.VMEM((n_units, A_GROUP // 2, 2 * BLOCK, 2 * BLOCK), BF16),
                        pltpu.VMEM((n_units, 1, gq), F32)],
        compiler_params=_params("arbitrary", "arbitrary"),
        name="swa_attn",
    )(sinks, qkv, qkv, qkv, qkv, qkv, biasmask)


FOX_CUM_BLOCK = 256
FOX_TQ = 256
FOX_HEADS_PER_PAIR = LANES // HEAD_DIM
FOX_PAIRS_PER_STEP = 2
FOX_PIECES = 3
FOX_SUM_ROWS = 16
FOX_SCORE_SLOTS = 4
FOX_PV_LAG = 1
FOX_PROB_SLOTS = FOX_PV_LAG + 1
BF16_ONE_PAIR = 0x3F803F80


def _fox_ext_lane(head, piece):
    pair, odd = divmod(head, FOX_HEADS_PER_PAIR)
    return pair * LANES + (1 - odd) * HEAD_DIM + piece


def _fox_placement():
    pm = np.zeros((FOX_PIECES * N_HEADS, D_MODEL), np.float32)
    for t in range(FOX_PIECES):
        for h in range(N_HEADS):
            pm[t * N_HEADS + h, _fox_ext_lane(h, t)] = -1.0
    return pm


def _fox_lane_words():
    w = np.zeros((SUBLANES, LANES), np.uint32)
    for hh in range(FOX_HEADS_PER_PAIR):
        own = np.zeros(LANES, bool)
        own[hh * HEAD_DIM:(hh + 1) * HEAD_DIM] = True
        ext = np.zeros(LANES, bool)
        first = _fox_ext_lane(hh, 0)
        ext[first:first + FOX_PIECES] = True
        w[3 * hh + 0] = np.where(own, 0xFFFFFFFF, 0)
        w[3 * hh + 1] = np.where(ext, 0xFFFFFFFF, 0)
        w[3 * hh + 2] = np.where(ext, BF16_ONE_PAIR, 0)
    return w


def _log_sigmoid(x):
    return jnp.minimum(x, 0.0) - jnp.log1p(jnp.exp(-jnp.abs(x)))


def _round_bf16(x):
    return x.astype(BF16).astype(F32)


def _fox_gate_kernel(x_ref, wf_ref, fb_ref, pm_ref, cumt_ref, kext_ref, w3_ref):
    @pl.when(pl.program_id(0) == 0)
    def _():
        w = wf_ref[...].astype(BF16)
        for t in range(FOX_PIECES):
            w3_ref[t * N_HEADS:(t + 1) * N_HEADS, :] = w

    fz = lax.dot_general(w3_ref[...], x_ref[...], _NT_DIMS, preferred_element_type=F32) + fb_ref[...]
    lf = _log_sigmoid(fz)
    cb = FOX_CUM_BLOCK
    row = lax.broadcasted_iota(jnp.int32, (cb, cb), 0)
    col = lax.broadcasted_iota(jnp.int32, (cb, cb), 1)
    tri = jnp.where(row <= col, 1.0, 0.0).astype(BF16)
    tri3 = jnp.concatenate([tri] * FOX_PIECES, axis=0)
    group = lax.broadcasted_iota(jnp.int32, (lf.shape[0], cb), 0) // N_HEADS
    pm = pm_ref[...]
    n_blocks = lf.shape[1] // cb
    local = []
    for blk in range(n_blocks):
        x = lf[:, blk * cb:(blk + 1) * cb]
        hi = _round_bf16(x)
        r1 = x - hi
        mid = _round_bf16(r1)
        pieces = jnp.concatenate([hi, mid, r1 - mid], axis=1).astype(BF16)
        local.append(jnp.dot(pieces, tri3, preferred_element_type=F32))
    carry = jnp.zeros((lf.shape[0], 1), F32)
    for blk in range(n_blocks):
        cols = slice(blk * cb, (blk + 1) * cb)
        c = local[blk] + carry
        carry = carry + local[blk][:, cb - 1:cb]
        c2 = c * LOG2E
        cumt_ref[:, cols] = c2[:N_HEADS]
        chi = _round_bf16(c2)
        cr1 = c2 - chi
        cmid = _round_bf16(cr1)
        piece = jnp.where(group == 0, chi, jnp.where(group == 1, cmid, cr1 - cmid)).astype(BF16)
        kext_ref[cols, :] = lax.dot_general(piece, pm, _TN_DIMS, preferred_element_type=F32).astype(BF16)


def _fox_gate(xb, w_qkvf_t, layer, fb3, batch, seq):
    pm = jnp.asarray(_fox_placement(), BF16)
    n3 = FOX_PIECES * N_HEADS
    gate_row = 3 * D_MODEL // N_HEADS
    return pl.pallas_call(
        _fox_gate_kernel,
        grid=(batch,),
        in_specs=[pl.BlockSpec((seq, D_MODEL), lambda bi: (bi, 0)),
                  pl.BlockSpec((None, N_HEADS, D_MODEL), lambda bi: (layer, gate_row, 0)),
                  pl.BlockSpec((n3, 1), lambda bi: (0, 0)),
                  pl.BlockSpec((n3, D_MODEL), lambda bi: (0, 0))],
        out_specs=[pl.BlockSpec((None, N_HEADS, seq), lambda bi: (bi, 0, 0)),
                   pl.BlockSpec((None, seq, D_MODEL), lambda bi: (bi, 0, 0))],
        out_shape=[jax.ShapeDtypeStruct((batch, N_HEADS, seq), F32),
                   jax.ShapeDtypeStruct((batch, seq, D_MODEL), BF16)],
        scratch_shapes=[pltpu.VMEM((n3, D_MODEL), BF16)],
        compiler_params=_params("arbitrary"),
        name="fox_gate",
    )(xb, w_qkvf_t, fb3, pm)


def _fox_kernel(q_ref, k_ref, v_ref, kext_ref, cbt_ref, lw_ref, o_ref, qa_ref, ka_ref, vt_ref,
                s_ref, p_ref):
    seq = q_ref.shape[0]
    tq = FOX_TQ
    key = lax.broadcasted_iota(jnp.int32, (tq, tq), 0)
    qry = lax.broadcasted_iota(jnp.int32, (tq, tq), 1)
    causal = key <= qry
    words = lambda x: pltpu.bitcast(x, jnp.uint32)
    for pair in range(FOX_PAIRS_PER_STEP):
        lanes = slice(pair * LANES, (pair + 1) * LANES)
        q = words(q_ref[:, lanes])
        k = words(k_ref[:, lanes])
        kext = words(kext_ref[:, lanes])
        vt = v_ref[:, lanes].T
        for hh in range(FOX_HEADS_PER_PAIR):
            h = pair * FOX_HEADS_PER_PAIR + hh
            own = lw_ref[3 * hh + 0:3 * hh + 1, :]
            ext = lw_ref[3 * hh + 1:3 * hh + 2, :]
            ones = lw_ref[3 * hh + 2:3 * hh + 3, :]
            qa_ref[h] = pltpu.bitcast((q & own) | ones, BF16)
            ka_ref[h] = pltpu.bitcast((k & ~ext) | (kext & ext), BF16)
            vt_ref[h, 0:HEAD_DIM, :] = vt[hh * HEAD_DIM:(hh + 1) * HEAD_DIM]
            vt_ref[h, HEAD_DIM:, :] = jnp.ones((FOX_SUM_ROWS, seq), BF16)

    heads = FOX_PAIRS_PER_STEP * FOX_HEADS_PER_PAIR
    units = [(i, h) for i in range(seq // tq) for h in range(heads)]

    def scores(u):
        i, h = units[u]
        slot = u % FOX_SCORE_SLOTS
        r0, r1 = i * tq, (i + 1) * tq
        qa = qa_ref[h, r0:r1, :]
        s_d = lax.dot_general(ka_ref[h, r0:r1, :], qa, _NT_DIMS, preferred_element_type=F32)
        s_ref[slot, r0:r1, :] = jnp.where(causal, s_d, NEG_INF)
        if i > 0:
            s_ref[slot, 0:r0, :] = lax.dot_general(ka_ref[h, 0:r0, :], qa, _NT_DIMS,
                                                   preferred_element_type=F32)

    def softmax(u):
        i, h = units[u]
        slot = u % FOX_SCORE_SLOTS
        r0, r1 = i * tq, (i + 1) * tq
        head = pl.program_id(1) * heads + h
        cbq = cbt_ref[pl.ds(head, 1), r0:r1]
        m = jnp.max(s_ref[slot, 0:r1, :], axis=0, keepdims=True)
        shift = cbq - (m + cbq)
        p_ref[u % FOX_PROB_SLOTS, 0:r1, :] = jnp.exp2(s_ref[slot, 0:r1, :] + shift).astype(BF16)

    def weighted_values(u):
        i, h = units[u]
        r0, r1 = i * tq, (i + 1) * tq
        ot = jnp.dot(vt_ref[h, :, 0:r1], p_ref[u % FOX_PROB_SLOTS, 0:r1, :], preferred_element_type=F32)
        out = ot[0:HEAD_DIM] * (1.0 / ot[HEAD_DIM:HEAD_DIM + 1])
        o_ref[h * HEAD_DIM:(h + 1) * HEAD_DIM, r0:r1] = out.astype(o_ref.dtype)

    ahead = FOX_SCORE_SLOTS - 1
    for u in range(min(ahead, len(units))):
        scores(u)
    for u in range(len(units)):
        if u + ahead < len(units):
            scores(u + ahead)
        softmax(u)
        if u >= FOX_PV_LAG:
            weighted_values(u - FOX_PV_LAG)
    for u in range(len(units) - FOX_PV_LAG, len(units)):
        weighted_values(u)


def _fox_attention(qkv, cumt, kext):
    b, s, _ = qkv.shape
    heads = FOX_PAIRS_PER_STEP * FOX_HEADS_PER_PAIR
    width = FOX_PAIRS_PER_STEP * LANES
    lw = jnp.asarray(_fox_lane_words())
    ncol = D_MODEL // width
    return pl.pallas_call(
        _fox_kernel,
        grid=(b, N_HEADS // heads),
        in_specs=[pl.BlockSpec((None, s, width), lambda bi, hp: (bi, 0, hp)),
                  pl.BlockSpec((None, s, width), lambda bi, hp: (bi, 0, ncol + hp)),
                  pl.BlockSpec((None, s, width), lambda bi, hp: (bi, 0, 2 * ncol + hp)),
                  pl.BlockSpec((None, s, width), lambda bi, hp: (bi, 0, hp)),
                  pl.BlockSpec((None, N_HEADS, s), lambda bi, hp: (bi, 0, 0)),
                  pl.BlockSpec((SUBLANES, LANES), lambda bi, hp: (0, 0))],
        out_specs=pl.BlockSpec((None, heads * HEAD_DIM, s), lambda bi, hp: (bi, hp, 0)),
        out_shape=jax.ShapeDtypeStruct((b, D_MODEL, s), BF16),
        scratch_shapes=[pltpu.VMEM((heads, s, LANES), BF16), pltpu.VMEM((heads, s, LANES), BF16),
                        pltpu.VMEM((heads, HEAD_DIM + FOX_SUM_ROWS, s), BF16),
                        pltpu.VMEM((FOX_SCORE_SLOTS, s, FOX_TQ), F32),
                        pltpu.VMEM((FOX_PROB_SLOTS, s, FOX_TQ), BF16)],
        compiler_params=_params("arbitrary", "arbitrary"),
        name="fox_attn",
    )(qkv, qkv, qkv, kext, cumt, lw)


def _deepnorm(xres, y, g, b, alpha):
    z = alpha * xres + y
    mu = jnp.mean(z, axis=-1, keepdims=True)
    zc = z - mu
    var = jnp.mean(zc * zc, axis=-1, keepdims=True)
    return zc * lax.rsqrt(var + LN_EPS) * g + b


LN_SUBTILES = 2


def _oproj_ln_kernel(alpha, ot_ref, xres_ref, w_ref, g_ref, b_ref, xo_ref, xob_ref, wb_ref):
    @pl.when(pl.program_id(0) == 0)
    def _():
        wb_ref[...] = w_ref[...].astype(BF16)

    ts = xres_ref.shape[0] // LN_SUBTILES
    ys = [lax.dot_general(ot_ref[:, r * ts:(r + 1) * ts], wb_ref[...], _TN_DIMS,
                          preferred_element_type=F32) for r in range(LN_SUBTILES)]
    for r, y in enumerate(ys):
        rows = slice(r * ts, (r + 1) * ts)
        out = _deepnorm(xres_ref[rows, :], y, g_ref[...], b_ref[...], alpha)
        xo_ref[rows, :] = out
        xob_ref[rows, :] = out.astype(BF16)


def _oproj_ln(ot, xres, w, layer, g, b, alpha, tm, name):
    m, n = xres.shape
    k = w.shape[1]
    per_seq = ot.shape[2] // tm
    return pl.pallas_call(
        functools.partial(_oproj_ln_kernel, alpha),
        grid=(m // tm,),
        in_specs=[pl.BlockSpec((None, k, tm), lambda i: (i // per_seq, 0, i % per_seq)),
                  pl.BlockSpec((tm, n), lambda i: (i, 0)),
                  pl.BlockSpec((None, k, n), lambda i: (layer, 0, 0), pipeline_mode=pl.Buffered(1)),
                  pl.BlockSpec((1, n), lambda i: (0, 0)),
                  pl.BlockSpec((1, n), lambda i: (0, 0))],
        out_specs=[pl.BlockSpec((tm, n), lambda i: (i, 0)),
                   pl.BlockSpec((tm, n), lambda i: (i, 0))],
        out_shape=[jax.ShapeDtypeStruct((m, n), F32), jax.ShapeDtypeStruct((m, n), BF16)],
        scratch_shapes=[pltpu.VMEM((k, n), BF16)],
        compiler_params=_params("arbitrary"),
        name=name,
    )(ot, xres, w, g, b)


CONV_WIDTH = 3
FFN_TF = 512
FFN_UP_TM = 1024


def _ffn_up_kernel(tiles_per_seq, x_ref, wg_ref, wu_ref, cwg_ref, cwu_ref, cbg_ref, cbu_ref, wd_ref,
                   a_ref, wdb_ref, wb_ref, h_ref):
    wdb_ref[...] = wd_ref[...].astype(BF16)
    i = pl.program_id(1)
    tm = x_ref.shape[0]
    tf = a_ref.shape[1]
    halo = SUBLANES

    @pl.when(i == 0)
    def _():
        wb_ref[:, 0:tf] = wg_ref[...].astype(BF16)
        wb_ref[:, tf:2 * tf] = wu_ref[...].astype(BF16)

    @pl.when(i % tiles_per_seq == 0)
    def _():
        h_ref[0:halo, :] = jnp.zeros((halo, h_ref.shape[1]), F32)

    h_ref[halo:halo + tm, :] = jnp.dot(x_ref[...], wb_ref[...], preferred_element_type=F32)

    def conv(cols, cw_ref, cb_ref, scale):
        hh = h_ref[0:halo + tm, cols]
        cw = cw_ref[...] * scale
        acc = cw[0:1, :] * hh
        for tap in range(1, CONV_WIDTH):
            acc = cw[tap:tap + 1, :] * hh + pltpu.roll(acc, 1, 0)
        return acc[halo:, :] + cb_ref[...] * scale

    half = conv(slice(0, tf), cwg_ref, cbg_ref, 0.5)
    u = conv(slice(tf, 2 * tf), cwu_ref, cbu_ref, 1.0)
    a_ref[...] = ((half + half * jnp.tanh(half)) * u).astype(a_ref.dtype)
    h_ref[0:halo, :] = h_ref[tm:tm + halo, :]


def _ffn_up(xb, w_up, conv_w, conv_b, w_down, layer, seq):
    m, k = xb.shape
    tm, tf = FFN_UP_TM, FFN_TF
    nf = D_FF // tf
    n_tiles = m // tm
    cb2 = conv_b.reshape(conv_b.shape[0], 1, 2 * D_FF)
    wd_rows = w_down.shape[1] // (nf * n_tiles)
    assert wd_rows * nf * n_tiles == w_down.shape[1] and wd_rows % (2 * SUBLANES) == 0
    return pl.pallas_call(
        functools.partial(_ffn_up_kernel, seq // tm),
        grid=(nf, n_tiles),
        in_specs=[pl.BlockSpec((tm, k), lambda j, i: (i, 0)),
                  pl.BlockSpec((None, k, tf), lambda j, i: (layer, 0, j)),
                  pl.BlockSpec((None, k, tf), lambda j, i: (layer, 0, j + nf)),
                  pl.BlockSpec((None, CONV_WIDTH, tf), lambda j, i: (layer, 0, j)),
                  pl.BlockSpec((None, CONV_WIDTH, tf), lambda j, i: (layer, 0, j + nf)),
                  pl.BlockSpec((None, 1, tf), lambda j, i: (layer, 0, j)),
                  pl.BlockSpec((None, 1, tf), lambda j, i: (layer, 0, j + nf)),
                  pl.BlockSpec((None, wd_rows, w_down.shape[2]), lambda j, i: (layer, j * n_tiles + i, 0))],
        out_specs=[pl.BlockSpec((tm, tf), lambda j, i: (i, j)),
                   pl.BlockSpec((wd_rows, w_down.shape[2]), lambda j, i: (j * n_tiles + i, 0))],
        out_shape=[jax.ShapeDtypeStruct((m, D_FF), BF16),
                   jax.ShapeDtypeStruct(w_down.shape[1:], BF16)],
        scratch_shapes=[pltpu.VMEM((k, 2 * tf), BF16), pltpu.VMEM((tm + 2 * SUBLANES, 2 * tf), F32)],
        compiler_params=_params("arbitrary", "arbitrary"),
        name="ffn_up",
    )(xb, w_up, w_up, conv_w, conv_w, cb2, cb2, w_down)


FFN_DOWN_TM = 256


def _ffn_down_ln_kernel(alpha, emit_bf16, a_ref, xres_ref, w_ref, g_ref, b_ref, xo_ref, *rest):
    y = jnp.dot(a_ref[...], w_ref[...], preferred_element_type=F32)
    out = _deepnorm(xres_ref[...], y, g_ref[...], b_ref[...], alpha)
    xo_ref[...] = out
    if emit_bf16:
        rest[0][...] = out.astype(BF16)


def _ffn_down_ln(a, xres, w_bf16, g, b, alpha, emit_bf16, name):
    m, k = a.shape
    n = w_bf16.shape[1]
    tm = FFN_DOWN_TM
    out_specs = [pl.BlockSpec((tm, n), lambda i: (i, 0))]
    out_shape = [jax.ShapeDtypeStruct((m, n), F32)]
    if emit_bf16:
        out_specs.append(pl.BlockSpec((tm, n), lambda i: (i, 0)))
        out_shape.append(jax.ShapeDtypeStruct((m, n), BF16))
    return pl.pallas_call(
        functools.partial(_ffn_down_ln_kernel, alpha, emit_bf16),
        grid=(m // tm,),
        in_specs=[pl.BlockSpec((tm, k), lambda i: (i, 0)),
                  pl.BlockSpec((tm, n), lambda i: (i, 0)),
                  pl.BlockSpec((k, n), lambda i: (0, 0), pipeline_mode=pl.Buffered(1)),
                  pl.BlockSpec((1, n), lambda i: (0, 0)),
                  pl.BlockSpec((1, n), lambda i: (0, 0))],
        out_specs=out_specs,
        out_shape=out_shape,
        compiler_params=_params("arbitrary"),
        name=name,
    )(a, xres, w_bf16, g, b)


def kernel(x, a_w_qkv, a_w_o, a_sinks, rel_bias, b_w_qkvf, b_f_bias, b_w_o,
           ffn_w_up, ffn_conv_w, ffn_conv_b, ffn_w_down, ln_g, ln_b):
    batch, seq, d = x.shape
    depth = ffn_w_up.shape[0]
    alpha = (2 * depth) ** 0.25
    m = batch * seq
    xf = x.reshape(m, d)
    xb = None
    biasmask = _swa_bias(rel_bias)
    for layer in range(depth):
        j = layer // 2
        g = ln_g[layer].reshape(2, 1, d)
        bt = ln_b[layer].reshape(2, 1, d)
        if layer % 2 == 0:
            n_qkv = D_MODEL + 2 * A_KV_DIM
            src = xf if xb is None else xb
            qkv = _ws_matmul(src, a_w_qkv, j, n_qkv, n_qkv // SWA_QKV_COL_TILES, SWA_QKV_TM, "swa_qkv",
                             slab_batch=(batch, seq))
            o = _swa_attention(qkv, a_sinks[j], biasmask)
            w_o = a_w_o
        else:
            w_t = jnp.swapaxes(b_w_qkvf, 1, 2)
            qkv = _ws_matmul(xb, w_t, j, 3 * D_MODEL, FOX_QKV_TN, FOX_QKV_TM, "fox_qkv",
                             w_is_transposed=True)
            fb3 = jnp.tile(b_f_bias[j].reshape(N_HEADS, 1), (FOX_PIECES, 1))
            cumt, kext = _fox_gate(xb, w_t, j, fb3, batch, seq)
            o = _fox_attention(qkv.reshape(batch, seq, 3 * D_MODEL), cumt, kext)
            w_o = b_w_o
        xf, xb = _oproj_ln(o, xf, w_o, j, g[0], bt[0], alpha, OPROJ_TM, "oproj_ln")
        a, w_down = _ffn_up(xb, ffn_w_up, ffn_conv_w, ffn_conv_b, ffn_w_down, layer, seq)
        last = layer == depth - 1
        res = _ffn_down_ln(a, xf, w_down, g[1], bt[1], alpha, not last, "ffn_down_ln")
        xf, xb = (res[0], None) if last else res
    return xf.reshape(batch, seq, d)
```

```python
import functools
import math

import jax
import jax.numpy as jnp
import numpy as np
from jax import lax
from jax.experimental import pallas as pl
from jax.experimental.pallas import tpu as pltpu

D_MODEL = 2048
HEAD_DIM = 64
BLOCK = 128
N_HEADS = D_MODEL // HEAD_DIM
A_KV_HEADS = 4
A_GROUP = N_HEADS // A_KV_HEADS
A_KV_DIM = A_KV_HEADS * HEAD_DIM
REL_BUCKETS = 32
REL_MAX_DIST = 128
D_FF = 5632
LN_EPS = 1e-5
SCALE = HEAD_DIM ** -0.5

SUBLANES = 8
LANES = 128

BF16 = jnp.bfloat16
F32 = jnp.float32
NEG_INF = float("-inf")

_NT_DIMS = (((1,), (1,)), ((), ()))


def _params(*sem):
    return pltpu.CompilerParams(dimension_semantics=sem)


LOG2E = math.log2(math.e)
Q_SCALE = SCALE * LOG2E
SWA_QKV_COL_TILES = 2
SWA_QKV_TM = 1024
FOX_QKV_TN = 1024
FOX_QKV_TM = 2048
OPROJ_TM = 512


def _ws_matmul_kernel(w_is_transposed, x_ref, w_ref, o_ref, wb_ref):
    @pl.when(pl.program_id(1) == 0)
    def _():
        wb_ref[...] = w_ref[...].astype(BF16)

    x = x_ref[...].astype(BF16)
    if w_is_transposed:
        y = lax.dot_general(x, wb_ref[...], _NT_DIMS, preferred_element_type=F32)
    else:
        y = jnp.dot(x, wb_ref[...], preferred_element_type=F32)
    tn = y.shape[1]
    col = pl.program_id(0) * tn + lax.broadcasted_iota(jnp.int32, (1, tn), 1)
    y = (y * jnp.where(col < D_MODEL, Q_SCALE, 1.0)).astype(o_ref.dtype)
    if len(o_ref.shape) == 2:
        o_ref[...] = y
    else:
        for c in range(o_ref.shape[0]):
            o_ref[c] = y[:, c * LANES:(c + 1) * LANES]


def _ws_matmul(x, w, layer, n_out, tn, tm, name, w_is_transposed=False, slab_batch=None):
    m, k = x.shape
    if w_is_transposed:
        w_spec = pl.BlockSpec((None, tn, k), lambda j, i: (layer, j, 0))
        wb_shape = (tn, k)
    else:
        w_spec = pl.BlockSpec((None, k, tn), lambda j, i: (layer, 0, j))
        wb_shape = (k, tn)
    if slab_batch is None:
        out_spec = pl.BlockSpec((tm, tn), lambda j, i: (i, j))
        out_shape = jax.ShapeDtypeStruct((m, n_out), BF16)
    else:
        batch, seq = slab_batch
        per_seq = seq // tm
        out_spec = pl.BlockSpec((None, tn // LANES, tm, LANES),
                                lambda j, i: (i // per_seq, j, i % per_seq, 0))
        out_shape = jax.ShapeDtypeStruct((batch, n_out // LANES, seq, LANES), BF16)
    return pl.pallas_call(
        functools.partial(_ws_matmul_kernel, w_is_transposed),
        grid=(n_out // tn, m // tm),
        in_specs=[pl.BlockSpec((tm, k), lambda j, i: (i, 0)), w_spec],
        out_specs=out_spec,
        out_shape=out_shape,
        scratch_shapes=[pltpu.VMEM(wb_shape, BF16)],
        compiler_params=_params("arbitrary", "arbitrary"),
        name=name,
    )(x, w)


def _t5_bucket_map():
    qi = np.arange(BLOCK)[None, :]
    kj = np.arange(2 * BLOCK)[:, None]
    dist = BLOCK + qi - kj
    n = np.maximum(dist, 0)
    max_exact = REL_BUCKETS // 2
    nf = np.maximum(n, 1).astype(np.float32)
    large = max_exact + (np.log(nf / max_exact) / math.log(REL_MAX_DIST / max_exact)
                         * (REL_BUCKETS - max_exact)).astype(np.int32)
    large = np.minimum(large, REL_BUCKETS - 1)
    bucket = np.where(n < max_exact, n, large)
    visible = (dist >= 0) & (dist < BLOCK)
    return np.where(visible, bucket, -1).astype(np.int32)


def _swa_bias_kernel(rb_ref, bk_ref, o_ref):
    bk = bk_ref[...]
    key = lax.broadcasted_iota(jnp.int32, bk.shape, 0)

    def head(h, _):
        acc = jnp.full(bk.shape, NEG_INF, F32)
        for b in range(REL_BUCKETS):
            acc = jnp.where(bk == b, rb_ref[b, h] * LOG2E, acc)
        o_ref[0, h] = acc
        o_ref[1, h] = jnp.where(key < BLOCK, NEG_INF, acc)
        return 0

    lax.fori_loop(0, N_HEADS, head, 0)


def _swa_bias(rel_bias):
    bk = jnp.asarray(_t5_bucket_map())
    return pl.pallas_call(
        _swa_bias_kernel,
        in_specs=[pl.BlockSpec(memory_space=pltpu.SMEM),
                  pl.BlockSpec(memory_space=pltpu.VMEM)],
        out_specs=pl.BlockSpec(memory_space=pltpu.VMEM),
        out_shape=jax.ShapeDtypeStruct((2, N_HEADS, 2 * BLOCK, BLOCK), F32),
        name="swa_bias",
    )(rel_bias, bk)


_TN_DIMS = (((0,), (0,)), ((), ()))
SWA_SCORES_AHEAD = 2
SWA_BLOCKS_PER_STEP = 4


def _swa_kernel(sink_ref, q_ref, kp_ref, kc_ref, vp_ref, vc_ref, bm_ref, o_ref,
                qs_ref, s_ref, p_ref, linv_ref):
    units = [(qb, kh) for qb in range(SWA_BLOCKS_PER_STEP) for kh in range(A_KV_HEADS)]
    first_table = jnp.where(pl.program_id(1) == 0, 1, 0)

    def head_at(head):
        slab, half = divmod(head, LANES // HEAD_DIM)
        return slab, slice(half * HEAD_DIM, (half + 1) * HEAD_DIM)

    def keys_values(prev_ref, cur_ref, qb, kh):
        slab, lanes = head_at(kh)
        if qb == 0:
            return jnp.concatenate([prev_ref[slab, :, lanes], cur_ref[slab, 0:BLOCK, lanes]], axis=0)
        return cur_ref[slab, (qb - 1) * BLOCK:(qb + 1) * BLOCK, lanes]

    def scores(u):
        qb, kh = units[u]
        for g in range(A_GROUP):
            slab, lanes = head_at(kh * A_GROUP + g)
            qs_ref[u, g * BLOCK:(g + 1) * BLOCK, :] = q_ref[slab, qb * BLOCK:(qb + 1) * BLOCK, lanes]
        k2 = keys_values(kp_ref, kc_ref, qb, kh)
        s = lax.dot_general(k2, qs_ref[u], _NT_DIMS, preferred_element_type=F32)
        for g in range(A_GROUP):
            s_ref[u, g] = s[:, g * BLOCK:(g + 1) * BLOCK]

    def softmax(u):
        qb, kh = units[u]
        table = first_table if qb == 0 else 0
        for g in range(A_GROUP):
            h = kh * A_GROUP + g
            cols = slice(g * BLOCK, (g + 1) * BLOCK)
            s = s_ref[u, g] + bm_ref[table, h]
            sink = sink_ref[h] * LOG2E
            m = jnp.maximum(jnp.max(s, axis=0, keepdims=True), sink)
            p = jnp.exp2(s - m)
            denom = jnp.sum(p, axis=0, keepdims=True) + jnp.exp2(sink - m)
            p_ref[u, g // 2, :, (g % 2) * BLOCK:(g % 2 + 1) * BLOCK] = p.astype(BF16)
            linv_ref[u, :, cols] = 1.0 / denom

    def weighted_values(u):
        qb, kh = units[u]
        v2 = keys_values(vp_ref, vc_ref, qb, kh)
        for c in range(A_GROUP // 2):
            cols = slice(2 * c * BLOCK, 2 * (c + 1) * BLOCK)
            ot = (lax.dot_general(v2, p_ref[u, c], _TN_DIMS, preferred_element_type=F32)
                  * linv_ref[u, :, cols])
            for gg in range(2):
                h = kh * A_GROUP + 2 * c + gg
                o_ref[h * HEAD_DIM:(h + 1) * HEAD_DIM, qb * BLOCK:(qb + 1) * BLOCK] = (
                    ot[:, gg * BLOCK:(gg + 1) * BLOCK].astype(o_ref.dtype))

    for u in range(min(SWA_SCORES_AHEAD, len(units))):
        scores(u)
    for u in range(len(units)):
        if u + SWA_SCORES_AHEAD < len(units):
            scores(u + SWA_SCORES_AHEAD)
        softmax(u)
        weighted_values(u)


def _swa_attention(qkv, sinks, biasmask):
    b, _, s, _ = qkv.shape
    rows = SWA_BLOCKS_PER_STEP * BLOCK
    gq = A_GROUP * BLOCK
    n_units = SWA_BLOCKS_PER_STEP * A_KV_HEADS
    q_slabs = D_MODEL // LANES
    kv_slabs = A_KV_DIM // LANES
    kcol = q_slabs // kv_slabs
    prev = lambda n: jnp.maximum(SWA_BLOCKS_PER_STEP * n - 1, 0)
    return pl.pallas_call(
        _swa_kernel,
        grid=(b, s // rows),
        in_specs=[pl.BlockSpec(memory_space=pltpu.SMEM),
                  pl.BlockSpec((None, q_slabs, rows, LANES), lambda bi, n: (bi, 0, n, 0)),
                  pl.BlockSpec((None, kv_slabs, BLOCK, LANES), lambda bi, n: (bi, kcol, prev(n), 0)),
                  pl.BlockSpec((None, kv_slabs, rows, LANES), lambda bi, n: (bi, kcol, n, 0)),
                  pl.BlockSpec((None, kv_slabs, BLOCK, LANES), lambda bi, n: (bi, kcol + 1, prev(n), 0)),
                  pl.BlockSpec((None, kv_slabs, rows, LANES), lambda bi, n: (bi, kcol + 1, n, 0)),
                  pl.BlockSpec((2, N_HEADS, 2 * BLOCK, BLOCK), lambda bi, n: (0, 0, 0, 0),
                               pipeline_mode=pl.Buffered(1))],
        out_specs=pl.BlockSpec((None, D_MODEL, rows), lambda bi, n: (bi, 0, n)),
        out_shape=jax.ShapeDtypeStruct((b, D_MODEL, s), BF16),
        scratch_shapes=[pltpu.VMEM((n_units, gq, HEAD_DIM), BF16),
                        pltpu.VMEM((n_units, A_GROUP, 2 * BLOCK, BLOCK), F32),
                        pltpu.VMEM((n_units, A_GROUP // 2, 2 * BLOCK, 2 * BLOCK), BF16),
                        pltpu.VMEM((n_units, 1, gq), F32)],
        compiler_params=_params("arbitrary", "arbitrary"),
        name="swa_attn",
    )(sinks, qkv, qkv, qkv, qkv, qkv, biasmask)


FOX_CUM_BLOCK = 256
FOX_TQ = 256
FOX_HEADS_PER_PAIR = LANES // HEAD_DIM
FOX_PAIRS_PER_STEP = 4
FOX_PIECES = 3
FOX_SUM_ROWS = 16
FOX_SCORE_SLOTS = 4
FOX_PV_LAG = 1
FOX_PROB_SLOTS = FOX_PV_LAG + 1
BF16_ONE_PAIR = 0x3F803F80


def _fox_ext_lane(head, piece):
    pair, odd = divmod(head, FOX_HEADS_PER_PAIR)
    return pair * LANES + (1 - odd) * HEAD_DIM + piece


def _fox_placement():
    pm = np.zeros((FOX_PIECES * N_HEADS, D_MODEL), np.float32)
    for t in range(FOX_PIECES):
        for h in range(N_HEADS):
            pm[t * N_HEADS + h, _fox_ext_lane(h, t)] = -1.0
    return pm


def _fox_lane_words():
    w = np.zeros((SUBLANES, LANES), np.uint32)
    for hh in range(FOX_HEADS_PER_PAIR):
        own = np.zeros(LANES, bool)
        own[hh * HEAD_DIM:(hh + 1) * HEAD_DIM] = True
        ext = np.zeros(LANES, bool)
        first = _fox_ext_lane(hh, 0)
        ext[first:first + FOX_PIECES] = True
        w[3 * hh + 0] = np.where(own, 0xFFFFFFFF, 0)
        w[3 * hh + 1] = np.where(ext, 0xFFFFFFFF, 0)
        w[3 * hh + 2] = np.where(ext, BF16_ONE_PAIR, 0)
    return w


def _log_sigmoid(x):
    return jnp.minimum(x, 0.0) - jnp.log1p(jnp.exp(-jnp.abs(x)))


def _round_bf16(x):
    return x.astype(BF16).astype(F32)


def _fox_gate_kernel(x_ref, wf_ref, fb_ref, pm_ref, cumt_ref, kext_ref, w3_ref):
    @pl.when(pl.program_id(0) == 0)
    def _():
        w = wf_ref[...].astype(BF16)
        for t in range(FOX_PIECES):
            w3_ref[t * N_HEADS:(t + 1) * N_HEADS, :] = w

    fz = lax.dot_general(w3_ref[...], x_ref[...], _NT_DIMS, preferred_element_type=F32) + fb_ref[...]
    lf = _log_sigmoid(fz)
    cb = FOX_CUM_BLOCK
    row = lax.broadcasted_iota(jnp.int32, (cb, cb), 0)
    col = lax.broadcasted_iota(jnp.int32, (cb, cb), 1)
    tri = jnp.where(row <= col, 1.0, 0.0).astype(BF16)
    tri3 = jnp.concatenate([tri] * FOX_PIECES, axis=0)
    group = lax.broadcasted_iota(jnp.int32, (lf.shape[0], cb), 0) // N_HEADS
    pm = pm_ref[...]
    n_blocks = lf.shape[1] // cb
    local = []
    for blk in range(n_blocks):
        x = lf[:, blk * cb:(blk + 1) * cb]
        hi = _round_bf16(x)
        r1 = x - hi
        mid = _round_bf16(r1)
        pieces = jnp.concatenate([hi, mid, r1 - mid], axis=1).astype(BF16)
        local.append(jnp.dot(pieces, tri3, preferred_element_type=F32))
    carry = jnp.zeros((lf.shape[0], 1), F32)
    for blk in range(n_blocks):
        cols = slice(blk * cb, (blk + 1) * cb)
        c = local[blk] + carry
        carry = carry + local[blk][:, cb - 1:cb]
        c2 = c * LOG2E
        cumt_ref[:, cols] = c2[:N_HEADS]
        chi = _round_bf16(c2)
        cr1 = c2 - chi
        cmid = _round_bf16(cr1)
        piece = jnp.where(group == 0, chi, jnp.where(group == 1, cmid, cr1 - cmid)).astype(BF16)
        kext_ref[cols, :] = lax.dot_general(piece, pm, _TN_DIMS, preferred_element_type=F32).astype(BF16)


def _fox_gate(xb, w_qkvf_t, layer, fb3, batch, seq):
    pm = jnp.asarray(_fox_placement(), BF16)
    n3 = FOX_PIECES * N_HEADS
    gate_row = 3 * D_MODEL // N_HEADS
    return pl.pallas_call(
        _fox_gate_kernel,
        grid=(batch,),
        in_specs=[pl.BlockSpec((seq, D_MODEL), lambda bi: (bi, 0)),
                  pl.BlockSpec((None, N_HEADS, D_MODEL), lambda bi: (layer, gate_row, 0)),
                  pl.BlockSpec((n3, 1), lambda bi: (0, 0)),
                  pl.BlockSpec((n3, D_MODEL), lambda bi: (0, 0))],
        out_specs=[pl.BlockSpec((None, N_HEADS, seq), lambda bi: (bi, 0, 0)),
                   pl.BlockSpec((None, seq, D_MODEL), lambda bi: (bi, 0, 0))],
        out_shape=[jax.ShapeDtypeStruct((batch, N_HEADS, seq), F32),
                   jax.ShapeDtypeStruct((batch, seq, D_MODEL), BF16)],
        scratch_shapes=[pltpu.VMEM((n3, D_MODEL), BF16)],
        compiler_params=_params("arbitrary"),
        name="fox_gate",
    )(xb, w_qkvf_t, fb3, pm)


def _fox_kernel(q_ref, k_ref, v_ref, kext_ref, cbt_ref, lw_ref, o_ref, qa_ref, ka_ref, vt_ref,
                s_ref, p_ref):
    seq = q_ref.shape[0]
    tq = FOX_TQ
    key = lax.broadcasted_iota(jnp.int32, (tq, tq), 0)
    qry = lax.broadcasted_iota(jnp.int32, (tq, tq), 1)
    causal = key <= qry
    words = lambda x: pltpu.bitcast(x, jnp.uint32)
    for pair in range(FOX_PAIRS_PER_STEP):
        lanes = slice(pair * LANES, (pair + 1) * LANES)
        q = words(q_ref[:, lanes])
        k = words(k_ref[:, lanes])
        kext = words(kext_ref[:, lanes])
        vt = v_ref[:, lanes].T
        for hh in range(FOX_HEADS_PER_PAIR):
            h = pair * FOX_HEADS_PER_PAIR + hh
            own = lw_ref[3 * hh + 0:3 * hh + 1, :]
            ext = lw_ref[3 * hh + 1:3 * hh + 2, :]
            ones = lw_ref[3 * hh + 2:3 * hh + 3, :]
            qa_ref[h] = pltpu.bitcast((q & own) | ones, BF16)
            ka_ref[h] = pltpu.bitcast((k & ~ext) | (kext & ext), BF16)
            vt_ref[h, 0:HEAD_DIM, :] = vt[hh * HEAD_DIM:(hh + 1) * HEAD_DIM]
            vt_ref[h, HEAD_DIM:, :] = jnp.ones((FOX_SUM_ROWS, seq), BF16)

    heads = FOX_PAIRS_PER_STEP * FOX_HEADS_PER_PAIR
    units = [(i, h) for i in range(seq // tq) for h in range(heads)]

    def scores(u):
        i, h = units[u]
        slot = u % FOX_SCORE_SLOTS
        r0, r1 = i * tq, (i + 1) * tq
        qa = qa_ref[h, r0:r1, :]
        s_d = lax.dot_general(ka_ref[h, r0:r1, :], qa, _NT_DIMS, preferred_element_type=F32)
        s_ref[slot, r0:r1, :] = jnp.where(causal, s_d, NEG_INF)
        if i > 0:
            s_ref[slot, 0:r0, :] = lax.dot_general(ka_ref[h, 0:r0, :], qa, _NT_DIMS,
                                                   preferred_element_type=F32)

    def softmax(u):
        i, h = units[u]
        slot = u % FOX_SCORE_SLOTS
        r0, r1 = i * tq, (i + 1) * tq
        head = pl.program_id(1) * heads + h
        cbq = cbt_ref[pl.ds(head, 1), r0:r1]
        m = jnp.max(s_ref[slot, 0:r1, :], axis=0, keepdims=True)
        shift = cbq - (m + cbq)
        p_ref[u % FOX_PROB_SLOTS, 0:r1, :] = jnp.exp2(s_ref[slot, 0:r1, :] + shift).astype(BF16)

    def weighted_values(u):
        i, h = units[u]
        r0, r1 = i * tq, (i + 1) * tq
        ot = jnp.dot(vt_ref[h, :, 0:r1], p_ref[u % FOX_PROB_SLOTS, 0:r1, :], preferred_element_type=F32)
        out = ot[0:HEAD_DIM] * (1.0 / ot[HEAD_DIM:HEAD_DIM + 1])
        o_ref[h * HEAD_DIM:(h + 1) * HEAD_DIM, r0:r1] = out.astype(o_ref.dtype)

    ahead = FOX_SCORE_SLOTS - 1
    for u in range(min(ahead, len(units))):
        scores(u)
    for u in range(len(units)):
        if u + ahead < len(units):
            scores(u + ahead)
        softmax(u)
        if u >= FOX_PV_LAG:
            weighted_values(u - FOX_PV_LAG)
    for u in range(len(units) - FOX_PV_LAG, len(units)):
        weighted_values(u)


def _fox_attention(qkv, cumt, kext):
    b, s, _ = qkv.shape
    heads = FOX_PAIRS_PER_STEP * FOX_HEADS_PER_PAIR
    width = FOX_PAIRS_PER_STEP * LANES
    lw = jnp.asarray(_fox_lane_words())
    ncol = D_MODEL // width
    return pl.pallas_call(
        _fox_kernel,
        grid=(b, N_HEADS // heads),
        in_specs=[pl.BlockSpec((None, s, width), lambda bi, hp: (bi, 0, hp)),
                  pl.BlockSpec((None, s, width), lambda bi, hp: (bi, 0, ncol + hp)),
                  pl.BlockSpec((None, s, width), lambda bi, hp: (bi, 0, 2 * ncol + hp)),
                  pl.BlockSpec((None, s, width), lambda bi, hp: (bi, 0, hp)),
                  pl.BlockSpec((None, N_HEADS, s), lambda bi, hp: (bi, 0, 0)),
                  pl.BlockSpec((SUBLANES, LANES), lambda bi, hp: (0, 0))],
        out_specs=pl.BlockSpec((None, heads * HEAD_DIM, s), lambda bi, hp: (bi, hp, 0)),
        out_shape=jax.ShapeDtypeStruct((b, D_MODEL, s), BF16),
        scratch_shapes=[pltpu.VMEM((heads, s, LANES), BF16), pltpu.VMEM((heads, s, LANES), BF16),
                        pltpu.VMEM((heads, HEAD_DIM + FOX_SUM_ROWS, s), BF16),
                        pltpu.VMEM((FOX_SCORE_SLOTS, s, FOX_TQ), F32),
                        pltpu.VMEM((FOX_PROB_SLOTS, s, FOX_TQ), BF16)],
        compiler_params=_params("arbitrary", "arbitrary"),
        name="fox_attn",
    )(qkv, qkv, qkv, kext, cumt, lw)


def _deepnorm(xres, y, g, b, alpha):
    z = alpha * xres + y
    mu = jnp.mean(z, axis=-1, keepdims=True)
    zc = z - mu
    var = jnp.mean(zc * zc, axis=-1, keepdims=True)
    return zc * lax.rsqrt(var + LN_EPS) * g + b


LN_SUBTILES = 2


def _oproj_ln_kernel(alpha, ot_ref, xres_ref, w_ref, g_ref, b_ref, xo_ref, xob_ref, wb_ref):
    @pl.when(pl.program_id(0) == 0)
    def _():
        wb_ref[...] = w_ref[...].astype(BF16)

    ts = xres_ref.shape[0] // LN_SUBTILES
    ys = [lax.dot_general(ot_ref[:, r * ts:(r + 1) * ts], wb_ref[...], _TN_DIMS,
                          preferred_element_type=F32) for r in range(LN_SUBTILES)]
    for r, y in enumerate(ys):
        rows = slice(r * ts, (r + 1) * ts)
        out = _deepnorm(xres_ref[rows, :], y, g_ref[...], b_ref[...], alpha)
        xo_ref[rows, :] = out
        xob_ref[rows, :] = out.astype(BF16)


def _oproj_ln(ot, xres, w, layer, g, b, alpha, tm, name):
    m, n = xres.shape
    k = w.shape[1]
    per_seq = ot.shape[2] // tm
    return pl.pallas_call(
        functools.partial(_oproj_ln_kernel, alpha),
        grid=(m // tm,),
        in_specs=[pl.BlockSpec((None, k, tm), lambda i: (i // per_seq, 0, i % per_seq)),
                  pl.BlockSpec((tm, n), lambda i: (i, 0)),
                  pl.BlockSpec((None, k, n), lambda i: (layer, 0, 0), pipeline_mode=pl.Buffered(1)),
                  pl.BlockSpec((1, n), lambda i: (0, 0)),
                  pl.BlockSpec((1, n), lambda i: (0, 0))],
        out_specs=[pl.BlockSpec((tm, n), lambda i: (i, 0)),
                   pl.BlockSpec((tm, n), lambda i: (i, 0))],
        out_shape=[jax.ShapeDtypeStruct((m, n), F32), jax.ShapeDtypeStruct((m, n), BF16)],
        scratch_shapes=[pltpu.VMEM((k, n), BF16)],
        compiler_params=_params("arbitrary"),
        name=name,
    )(ot, xres, w, g, b)


CONV_WIDTH = 3
FFN_TF = 512
FFN_UP_TM = 1024


def _ffn_up_kernel(tiles_per_seq, x_ref, wg_ref, wu_ref, cwg_ref, cwu_ref, cbg_ref, cbu_ref, wd_ref,
                   a_ref, wdb_ref, wb_ref, h_ref):
    wdb_ref[...] = wd_ref[...].astype(BF16)
    i = pl.program_id(1)
    tm = x_ref.shape[0]
    tf = a_ref.shape[1]
    halo = SUBLANES

    @pl.when(i == 0)
    def _():
        wb_ref[:, 0:tf] = wg_ref[...].astype(BF16)
        wb_ref[:, tf:2 * tf] = wu_ref[...].astype(BF16)

    @pl.when(i % tiles_per_seq == 0)
    def _():
        h_ref[0:halo, :] = jnp.zeros((halo, h_ref.shape[1]), F32)

    h_ref[halo:halo + tm, :] = jnp.dot(x_ref[...], wb_ref[...], preferred_element_type=F32)

    def conv(cols, cw_ref, cb_ref, scale):
        hh = h_ref[0:halo + tm, cols]
        cw = cw_ref[...] * scale
        acc = cw[0:1, :] * hh
        for tap in range(1, CONV_WIDTH):
            acc = cw[tap:tap + 1, :] * hh + pltpu.roll(acc, 1, 0)
        return acc[halo:, :] + cb_ref[...] * scale

    half = conv(slice(0, tf), cwg_ref, cbg_ref, 0.5)
    u = conv(slice(tf, 2 * tf), cwu_ref, cbu_ref, 1.0)
    a_ref[...] = ((half + half * jnp.tanh(half)) * u).astype(a_ref.dtype)
    h_ref[0:halo, :] = h_ref[tm:tm + halo, :]


def _ffn_up(xb, w_up, conv_w, conv_b, w_down, layer, seq):
    m, k = xb.shape
    tm, tf = FFN_UP_TM, FFN_TF
    nf = D_FF // tf
    n_tiles = m // tm
    cb2 = conv_b.reshape(conv_b.shape[0], 1, 2 * D_FF)
    wd_rows = w_down.shape[1] // (nf * n_tiles)
    assert wd_rows * nf * n_tiles == w_down.shape[1] and wd_rows % (2 * SUBLANES) == 0
    return pl.pallas_call(
        functools.partial(_ffn_up_kernel, seq // tm),
        grid=(nf, n_tiles),
        in_specs=[pl.BlockSpec((tm, k), lambda j, i: (i, 0)),
                  pl.BlockSpec((None, k, tf), lambda j, i: (layer, 0, j)),
                  pl.BlockSpec((None, k, tf), lambda j, i: (layer, 0, j + nf)),
                  pl.BlockSpec((None, CONV_WIDTH, tf), lambda j, i: (layer, 0, j)),
                  pl.BlockSpec((None, CONV_WIDTH, tf), lambda j, i: (layer, 0, j + nf)),
                  pl.BlockSpec((None, 1, tf), lambda j, i: (layer, 0, j)),
                  pl.BlockSpec((None, 1, tf), lambda j, i: (layer, 0, j + nf)),
                  pl.BlockSpec((None, wd_rows, w_down.shape[2]), lambda j, i: (layer, j * n_tiles + i, 0))],
        out_specs=[pl.BlockSpec((tm, tf), lambda j, i: (i, j)),
                   pl.BlockSpec((wd_rows, w_down.shape[2]), lambda j, i: (j * n_tiles + i, 0))],
        out_shape=[jax.ShapeDtypeStruct((m, D_FF), BF16),
                   jax.ShapeDtypeStruct(w_down.shape[1:], BF16)],
        scratch_shapes=[pltpu.VMEM((k, 2 * tf), BF16), pltpu.VMEM((tm + 2 * SUBLANES, 2 * tf), F32)],
        compiler_params=_params("arbitrary", "arbitrary"),
        name="ffn_up",
    )(xb, w_up, w_up, conv_w, conv_w, cb2, cb2, w_down)


FFN_DOWN_TM = 256


def _ffn_down_ln_kernel(alpha, emit_bf16, a_ref, xres_ref, w_ref, g_ref, b_ref, xo_ref, *rest):
    y = jnp.dot(a_ref[...], w_ref[...], preferred_element_type=F32)
    out = _deepnorm(xres_ref[...], y, g_ref[...], b_ref[...], alpha)
    xo_ref[...] = out
    if emit_bf16:
        rest[0][...] = out.astype(BF16)


def _ffn_down_ln(a, xres, w_bf16, g, b, alpha, emit_bf16, name):
    m, k = a.shape
    n = w_bf16.shape[1]
    tm = FFN_DOWN_TM
    out_specs = [pl.BlockSpec((tm, n), lambda i: (i, 0))]
    out_shape = [jax.ShapeDtypeStruct((m, n), F32)]
    if emit_bf16:
        out_specs.append(pl.BlockSpec((tm, n), lambda i: (i, 0)))
        out_shape.append(jax.ShapeDtypeStruct((m, n), BF16))
    return pl.pallas_call(
        functools.partial(_ffn_down_ln_kernel, alpha, emit_bf16),
        grid=(m // tm,),
        in_specs=[pl.BlockSpec((tm, k), lambda i: (i, 0)),
                  pl.BlockSpec((tm, n), lambda i: (i, 0)),
                  pl.BlockSpec((k, n), lambda i: (0, 0), pipeline_mode=pl.Buffered(1)),
                  pl.BlockSpec((1, n), lambda i: (0, 0)),
                  pl.BlockSpec((1, n), lambda i: (0, 0))],
        out_specs=out_specs,
        out_shape=out_shape,
        compiler_params=_params("arbitrary"),
        name=name,
    )(a, xres, w_bf16, g, b)


def kernel(x, a_w_qkv, a_w_o, a_sinks, rel_bias, b_w_qkvf, b_f_bias, b_w_o,
           ffn_w_up, ffn_conv_w, ffn_conv_b, ffn_w_down, ln_g, ln_b):
    batch, seq, d = x.shape
    depth = ffn_w_up.shape[0]
    alpha = (2 * depth) ** 0.25
    m = batch * seq
    xf = x.reshape(m, d)
    xb = None
    biasmask = _swa_bias(rel_bias)
    for layer in range(depth):
        j = layer // 2
        g = ln_g[layer].reshape(2, 1, d)
        bt = ln_b[layer].reshape(2, 1, d)
        if layer % 2 == 0:
            n_qkv = D_MODEL + 2 * A_KV_DIM
            src = xf if xb is None else xb
            qkv = _ws_matmul(src, a_w_qkv, j, n_qkv, n_qkv // SWA_QKV_COL_TILES, SWA_QKV_TM, "swa_qkv",
                             slab_batch=(batch, seq))
            o = _swa_attention(qkv, a_sinks[j], biasmask)
            w_o = a_w_o
        else:
            w_t = jnp.swapaxes(b_w_qkvf, 1, 2)
            qkv = _ws_matmul(xb, w_t, j, 3 * D_MODEL, FOX_QKV_TN, FOX_QKV_TM, "fox_qkv",
                             w_is_transposed=True)
            fb3 = jnp.tile(b_f_bias[j].reshape(N_HEADS, 1), (FOX_PIECES, 1))
            cumt, kext = _fox_gate(xb, w_t, j, fb3, batch, seq)
            o = _fox_attention(qkv.reshape(batch, seq, 3 * D_MODEL), cumt, kext)
            w_o = b_w_o
        xf, xb = _oproj_ln(o, xf, w_o, j, g[0], bt[0], alpha, OPROJ_TM, "oproj_ln")
        a, w_down = _ffn_up(xb, ffn_w_up, ffn_conv_w, ffn_conv_b, ffn_w_down, layer, seq)
        last = layer == depth - 1
        res = _ffn_down_ln(a, xf, w_down, g[1], bt[1], alpha, not last, "ffn_down_ln")
        xf, xb = (res[0], None) if last else res
    return xf.reshape(batch, seq, d)
```

```python
import functools
import math

import jax
import jax.numpy as jnp
import numpy as np
from jax import lax
from jax.experimental import pallas as pl
from jax.experimental.pallas import tpu as pltpu

D_MODEL = 2048
HEAD_DIM = 64
BLOCK = 128
N_HEADS = D_MODEL // HEAD_DIM
A_KV_HEADS = 4
A_GROUP = N_HEADS // A_KV_HEADS
A_KV_DIM = A_KV_HEADS * HEAD_DIM
REL_BUCKETS = 32
REL_MAX_DIST = 128
D_FF = 5632
LN_EPS = 1e-5
SCALE = HEAD_DIM ** -0.5

SUBLANES = 8
LANES = 128

BF16 = jnp.bfloat16
F32 = jnp.float32
NEG_INF = float("-inf")

_NT_DIMS = (((1,), (1,)), ((), ()))


def _params(*sem):
    return pltpu.CompilerParams(dimension_semantics=sem)


LOG2E = math.log2(math.e)
Q_SCALE = SCALE * LOG2E
SWA_QKV_COL_TILES = 2
SWA_QKV_TM = 1024
FOX_QKV_TN = 1024
FOX_QKV_TM = 2048
OPROJ_TM = 512


def _ws_matmul_kernel(w_is_transposed, x_ref, w_ref, o_ref, wb_ref):
    @pl.when(pl.program_id(1) == 0)
    def _():
        wb_ref[...] = w_ref[...].astype(BF16)

    x = x_ref[...].astype(BF16)
    if w_is_transposed:
        y = lax.dot_general(x, wb_ref[...], _NT_DIMS, preferred_element_type=F32)
    else:
        y = jnp.dot(x, wb_ref[...], preferred_element_type=F32)
    tn = y.shape[1]
    col = pl.program_id(0) * tn + lax.broadcasted_iota(jnp.int32, (1, tn), 1)
    y = (y * jnp.where(col < D_MODEL, Q_SCALE, 1.0)).astype(o_ref.dtype)
    if len(o_ref.shape) == 2:
        o_ref[...] = y
    else:
        for c in range(o_ref.shape[0]):
            o_ref[c] = y[:, c * LANES:(c + 1) * LANES]


def _ws_matmul(x, w, layer, n_out, tn, tm, name, w_is_transposed=False, slab_batch=None):
    m, k = x.shape
    if w_is_transposed:
        w_spec = pl.BlockSpec((None, tn, k), lambda j, i: (layer, j, 0))
        wb_shape = (tn, k)
    else:
        w_spec = pl.BlockSpec((None, k, tn), lambda j, i: (layer, 0, j))
        wb_shape = (k, tn)
    if slab_batch is None:
        out_spec = pl.BlockSpec((tm, tn), lambda j, i: (i, j))
        out_shape = jax.ShapeDtypeStruct((m, n_out), BF16)
    else:
        batch, seq = slab_batch
        per_seq = seq // tm
        out_spec = pl.BlockSpec((None, tn // LANES, tm, LANES),
                                lambda j, i: (i // per_seq, j, i % per_seq, 0))
        out_shape = jax.ShapeDtypeStruct((batch, n_out // LANES, seq, LANES), BF16)
    return pl.pallas_call(
        functools.partial(_ws_matmul_kernel, w_is_transposed),
        grid=(n_out // tn, m // tm),
        in_specs=[pl.BlockSpec((tm, k), lambda j, i: (i, 0)), w_spec],
        out_specs=out_spec,
        out_shape=out_shape,
        scratch_shapes=[pltpu.VMEM(wb_shape, BF16)],
        compiler_params=_params("arbitrary", "arbitrary"),
        name=name,
    )(x, w)


def _t5_bucket_map():
    qi = np.arange(BLOCK)[None, :]
    kj = np.arange(2 * BLOCK)[:, None]
    dist = BLOCK + qi - kj
    n = np.maximum(dist, 0)
    max_exact = REL_BUCKETS // 2
    nf = np.maximum(n, 1).astype(np.float32)
    large = max_exact + (np.log(nf / max_exact) / math.log(REL_MAX_DIST / max_exact)
                         * (REL_BUCKETS - max_exact)).astype(np.int32)
    large = np.minimum(large, REL_BUCKETS - 1)
    bucket = np.where(n < max_exact, n, large)
    visible = (dist >= 0) & (dist < BLOCK)
    return np.where(visible, bucket, -1).astype(np.int32)


def _swa_bias_kernel(rb_ref, bk_ref, o_ref):
    bk = bk_ref[...]
    key = lax.broadcasted_iota(jnp.int32, (BLOCK, BLOCK), 0)
    qry = lax.broadcasted_iota(jnp.int32, (BLOCK, BLOCK), 1)
    current = qry >= key
    masked = jnp.full((BLOCK, BLOCK), NEG_INF, F32)

    def head(h, _):
        v = jnp.zeros(bk.shape, F32)
        for b in range(REL_BUCKETS):
            v = jnp.where(bk == b, rb_ref[b, h] * LOG2E, v)
        rot = pltpu.roll(jnp.broadcast_to(v, (BLOCK, BLOCK)), 0, 1, stride=1, stride_axis=0)
        upper = jnp.where(current, rot, NEG_INF)
        o_ref[0, h, 0:BLOCK, :] = jnp.where(current, NEG_INF, rot)
        o_ref[0, h, BLOCK:, :] = upper
        o_ref[1, h, 0:BLOCK, :] = masked
        o_ref[1, h, BLOCK:, :] = upper
        return 0

    lax.fori_loop(0, N_HEADS, head, 0)


def _swa_bias(rel_bias):
    bucket_map = _t5_bucket_map()
    by_dist = bucket_map[BLOCK:BLOCK + 1, :]
    kj, qi = np.arange(2 * BLOCK)[:, None], np.arange(BLOCK)[None, :]
    dist = BLOCK + qi - kj
    toeplitz = np.where((dist >= 0) & (dist < BLOCK), by_dist[0, dist % BLOCK], -1)
    assert (toeplitz == bucket_map).all()
    bk = jnp.asarray(by_dist)
    return pl.pallas_call(
        _swa_bias_kernel,
        in_specs=[pl.BlockSpec(memory_space=pltpu.SMEM),
                  pl.BlockSpec(memory_space=pltpu.VMEM)],
        out_specs=pl.BlockSpec(memory_space=pltpu.VMEM),
        out_shape=jax.ShapeDtypeStruct((2, N_HEADS, 2 * BLOCK, BLOCK), F32),
        name="swa_bias",
    )(rel_bias, bk)


_TN_DIMS = (((0,), (0,)), ((), ()))
SWA_SCORES_AHEAD = 2
SWA_BLOCKS_PER_STEP = 4


def _swa_kernel(sink_ref, q_ref, kp_ref, kc_ref, vp_ref, vc_ref, bm_ref, o_ref,
                qs_ref, s_ref, p_ref, linv_ref):
    units = [(qb, kh) for qb in range(SWA_BLOCKS_PER_STEP) for kh in range(A_KV_HEADS)]
    first_table = jnp.where(pl.program_id(1) == 0, 1, 0)

    def head_at(head):
        slab, half = divmod(head, LANES // HEAD_DIM)
        return slab, slice(half * HEAD_DIM, (half + 1) * HEAD_DIM)

    def keys_values(prev_ref, cur_ref, qb, kh):
        slab, lanes = head_at(kh)
        if qb == 0:
            return jnp.concatenate([prev_ref[slab, :, lanes], cur_ref[slab, 0:BLOCK, lanes]], axis=0)
        return cur_ref[slab, (qb - 1) * BLOCK:(qb + 1) * BLOCK, lanes]

    def scores(u):
        qb, kh = units[u]
        for g in range(A_GROUP):
            slab, lanes = head_at(kh * A_GROUP + g)
            qs_ref[u, g * BLOCK:(g + 1) * BLOCK, :] = q_ref[slab, qb * BLOCK:(qb + 1) * BLOCK, lanes]
        k2 = keys_values(kp_ref, kc_ref, qb, kh)
        s = lax.dot_general(k2, qs_ref[u], _NT_DIMS, preferred_element_type=F32)
        for g in range(A_GROUP):
            s_ref[u, g] = s[:, g * BLOCK:(g + 1) * BLOCK]

    def softmax(u):
        qb, kh = units[u]
        table = first_table if qb == 0 else 0
        for g in range(A_GROUP):
            h = kh * A_GROUP + g
            cols = slice(g * BLOCK, (g + 1) * BLOCK)
            s = s_ref[u, g] + bm_ref[table, h]
            sink = sink_ref[h] * LOG2E
            m = jnp.maximum(jnp.max(s, axis=0, keepdims=True), sink)
            p = jnp.exp2(s - m)
            denom = jnp.sum(p, axis=0, keepdims=True) + jnp.exp2(sink - m)
            p_ref[u, g // 2, :, (g % 2) * BLOCK:(g % 2 + 1) * BLOCK] = p.astype(BF16)
            linv_ref[u, :, cols] = 1.0 / denom

    def weighted_values(u):
        qb, kh = units[u]
        v2 = keys_values(vp_ref, vc_ref, qb, kh)
        for c in range(A_GROUP // 2):
            cols = slice(2 * c * BLOCK, 2 * (c + 1) * BLOCK)
            ot = (lax.dot_general(v2, p_ref[u, c], _TN_DIMS, preferred_element_type=F32)
                  * linv_ref[u, :, cols])
            for gg in range(2):
                h = kh * A_GROUP + 2 * c + gg
                o_ref[h * HEAD_DIM:(h + 1) * HEAD_DIM, qb * BLOCK:(qb + 1) * BLOCK] = (
                    ot[:, gg * BLOCK:(gg + 1) * BLOCK].astype(o_ref.dtype))

    for u in range(min(SWA_SCORES_AHEAD, len(units))):
        scores(u)
    for u in range(len(units)):
        if u + SWA_SCORES_AHEAD < len(units):
            scores(u + SWA_SCORES_AHEAD)
        softmax(u)
        weighted_values(u)


def _swa_attention(qkv, sinks, biasmask):
    b, _, s, _ = qkv.shape
    rows = SWA_BLOCKS_PER_STEP * BLOCK
    gq = A_GROUP * BLOCK
    n_units = SWA_BLOCKS_PER_STEP * A_KV_HEADS
    q_slabs = D_MODEL // LANES
    kv_slabs = A_KV_DIM // LANES
    kcol = q_slabs // kv_slabs
    prev = lambda n: jnp.maximum(SWA_BLOCKS_PER_STEP * n - 1, 0)
    return pl.pallas_call(
        _swa_kernel,
        grid=(b, s // rows),
        in_specs=[pl.BlockSpec(memory_space=pltpu.SMEM),
                  pl.BlockSpec((None, q_slabs, rows, LANES), lambda bi, n: (bi, 0, n, 0)),
                  pl.BlockSpec((None, kv_slabs, BLOCK, LANES), lambda bi, n: (bi, kcol, prev(n), 0)),
                  pl.BlockSpec((None, kv_slabs, rows, LANES), lambda bi, n: (bi, kcol, n, 0)),
                  pl.BlockSpec((None, kv_slabs, BLOCK, LANES), lambda bi, n: (bi, kcol + 1, prev(n), 0)),
                  pl.BlockSpec((None, kv_slabs, rows, LANES), lambda bi, n: (bi, kcol + 1, n, 0)),
                  pl.BlockSpec((2, N_HEADS, 2 * BLOCK, BLOCK), lambda bi, n: (0, 0, 0, 0),
                               pipeline_mode=pl.Buffered(1))],
        out_specs=pl.BlockSpec((None, D_MODEL, rows), lambda bi, n: (bi, 0, n)),
        out_shape=jax.ShapeDtypeStruct((b, D_MODEL, s), BF16),
        scratch_shapes=[pltpu.VMEM((n_units, gq, HEAD_DIM), BF16),
                        pltpu.VMEM((n_units, A_GROUP, 2 * BLOCK, BLOCK), F32),
                        pltpu.VMEM((n_units, A_GROUP // 2, 2 * BLOCK, 2 * BLOCK), BF16),
                        pltpu.VMEM((n_units, 1, gq), F32)],
        compiler_params=_params("arbitrary", "arbitrary"),
        name="swa_attn",
    )(sinks, qkv, qkv, qkv, qkv, qkv, biasmask)


FOX_CUM_BLOCK = 256
FOX_TQ = 256
FOX_HEADS_PER_PAIR = LANES // HEAD_DIM
FOX_PAIRS_PER_STEP = 2
FOX_PIECES = 3
FOX_SUM_ROWS = 16
FOX_SCORE_SLOTS = 4
FOX_PV_LAG = 1
FOX_PROB_SLOTS = FOX_PV_LAG + 1
BF16_ONE_PAIR = 0x3F803F80


def _fox_ext_lane(head, piece):
    pair, odd = divmod(head, FOX_HEADS_PER_PAIR)
    return pair * LANES + (1 - odd) * HEAD_DIM + piece


def _fox_placement():
    pm = np.zeros((FOX_PIECES * N_HEADS, D_MODEL), np.float32)
    for t in range(FOX_PIECES):
        for h in range(N_HEADS):
            pm[t * N_HEADS + h, _fox_ext_lane(h, t)] = -1.0
    return pm


def _fox_lane_words():
    w = np.zeros((SUBLANES, LANES), np.uint32)
    for hh in range(FOX_HEADS_PER_PAIR):
        own = np.zeros(LANES, bool)
        own[hh * HEAD_DIM:(hh + 1) * HEAD_DIM] = True
        ext = np.zeros(LANES, bool)
        first = _fox_ext_lane(hh, 0)
        ext[first:first + FOX_PIECES] = True
        w[3 * hh + 0] = np.where(own, 0xFFFFFFFF, 0)
        w[3 * hh + 1] = np.where(ext, 0xFFFFFFFF, 0)
        w[3 * hh + 2] = np.where(ext, BF16_ONE_PAIR, 0)
    return w


def _log_sigmoid(x):
    return jnp.minimum(x, 0.0) - jnp.log1p(jnp.exp(-jnp.abs(x)))


def _round_bf16(x):
    return x.astype(BF16).astype(F32)


def _fox_gate_kernel(x_ref, wf_ref, fb_ref, pm_ref, cumt_ref, kext_ref, w3_ref):
    @pl.when(pl.program_id(0) == 0)
    def _():
        w = wf_ref[...].astype(BF16)
        for t in range(FOX_PIECES):
            w3_ref[t * N_HEADS:(t + 1) * N_HEADS, :] = w

    fz = lax.dot_general(w3_ref[...], x_ref[...], _NT_DIMS, preferred_element_type=F32) + fb_ref[...]
    lf = _log_sigmoid(fz)
    cb = FOX_CUM_BLOCK
    row = lax.broadcasted_iota(jnp.int32, (cb, cb), 0)
    col = lax.broadcasted_iota(jnp.int32, (cb, cb), 1)
    tri = jnp.where(row <= col, 1.0, 0.0).astype(BF16)
    tri3 = jnp.concatenate([tri] * FOX_PIECES, axis=0)
    group = lax.broadcasted_iota(jnp.int32, (lf.shape[0], cb), 0) // N_HEADS
    pm = pm_ref[...]
    n_blocks = lf.shape[1] // cb
    local = []
    for blk in range(n_blocks):
        x = lf[:, blk * cb:(blk + 1) * cb]
        hi = _round_bf16(x)
        r1 = x - hi
        mid = _round_bf16(r1)
        pieces = jnp.concatenate([hi, mid, r1 - mid], axis=1).astype(BF16)
        local.append(jnp.dot(pieces, tri3, preferred_element_type=F32))
    carry = jnp.zeros((lf.shape[0], 1), F32)
    for blk in range(n_blocks):
        cols = slice(blk * cb, (blk + 1) * cb)
        c = local[blk] + carry
        carry = carry + local[blk][:, cb - 1:cb]
        c2 = c * LOG2E
        cumt_ref[:, cols] = c2[:N_HEADS]
        chi = _round_bf16(c2)
        cr1 = c2 - chi
        cmid = _round_bf16(cr1)
        piece = jnp.where(group == 0, chi, jnp.where(group == 1, cmid, cr1 - cmid)).astype(BF16)
        kext_ref[cols, :] = lax.dot_general(piece, pm, _TN_DIMS, preferred_element_type=F32).astype(BF16)


def _fox_gate(xb, w_qkvf_t, layer, fb3, batch, seq):
    pm = jnp.asarray(_fox_placement(), BF16)
    n3 = FOX_PIECES * N_HEADS
    gate_row = 3 * D_MODEL // N_HEADS
    return pl.pallas_call(
        _fox_gate_kernel,
        grid=(batch,),
        in_specs=[pl.BlockSpec((seq, D_MODEL), lambda bi: (bi, 0)),
                  pl.BlockSpec((None, N_HEADS, D_MODEL), lambda bi: (layer, gate_row, 0)),
                  pl.BlockSpec((n3, 1), lambda bi: (0, 0)),
                  pl.BlockSpec((n3, D_MODEL), lambda bi: (0, 0))],
        out_specs=[pl.BlockSpec((None, N_HEADS, seq), lambda bi: (bi, 0, 0)),
                   pl.BlockSpec((None, seq, D_MODEL), lambda bi: (bi, 0, 0))],
        out_shape=[jax.ShapeDtypeStruct((batch, N_HEADS, seq), F32),
                   jax.ShapeDtypeStruct((batch, seq, D_MODEL), BF16)],
        scratch_shapes=[pltpu.VMEM((n3, D_MODEL), BF16)],
        compiler_params=_params("arbitrary"),
        name="fox_gate",
    )(xb, w_qkvf_t, fb3, pm)


def _fox_kernel(q_ref, k_ref, v_ref, kext_ref, cbt_ref, lw_ref, o_ref, qa_ref, ka_ref, vt_ref,
                s_ref, p_ref):
    seq = q_ref.shape[0]
    tq = FOX_TQ
    key = lax.broadcasted_iota(jnp.int32, (tq, tq), 0)
    qry = lax.broadcasted_iota(jnp.int32, (tq, tq), 1)
    causal = key <= qry
    words = lambda x: pltpu.bitcast(x, jnp.uint32)
    for pair in range(FOX_PAIRS_PER_STEP):
        lanes = slice(pair * LANES, (pair + 1) * LANES)
        q = words(q_ref[:, lanes])
        k = words(k_ref[:, lanes])
        kext = words(kext_ref[:, lanes])
        vt = v_ref[:, lanes].T
        for hh in range(FOX_HEADS_PER_PAIR):
            h = pair * FOX_HEADS_PER_PAIR + hh
            own = lw_ref[3 * hh + 0:3 * hh + 1, :]
            ext = lw_ref[3 * hh + 1:3 * hh + 2, :]
            ones = lw_ref[3 * hh + 2:3 * hh + 3, :]
            qa_ref[h] = pltpu.bitcast((q & own) | ones, BF16)
            ka_ref[h] = pltpu.bitcast((k & ~ext) | (kext & ext), BF16)
            vt_ref[h, 0:HEAD_DIM, :] = vt[hh * HEAD_DIM:(hh + 1) * HEAD_DIM]
            vt_ref[h, HEAD_DIM:, :] = jnp.ones((FOX_SUM_ROWS, seq), BF16)

    heads = FOX_PAIRS_PER_STEP * FOX_HEADS_PER_PAIR
    units = [(i, h) for i in range(seq // tq) for h in range(heads)]

    def scores(u):
        i, h = units[u]
        slot = u % FOX_SCORE_SLOTS
        r0, r1 = i * tq, (i + 1) * tq
        qa = qa_ref[h, r0:r1, :]
        s_d = lax.dot_general(ka_ref[h, r0:r1, :], qa, _NT_DIMS, preferred_element_type=F32)
        s_ref[slot, r0:r1, :] = jnp.where(causal, s_d, NEG_INF)
        if i > 0:
            s_ref[slot, 0:r0, :] = lax.dot_general(ka_ref[h, 0:r0, :], qa, _NT_DIMS,
                                                   preferred_element_type=F32)

    def softmax(u):
        i, h = units[u]
        slot = u % FOX_SCORE_SLOTS
        r0, r1 = i * tq, (i + 1) * tq
        head = pl.program_id(1) * heads + h
        cbq = cbt_ref[pl.ds(head, 1), r0:r1]
        m = jnp.max(s_ref[slot, 0:r1, :], axis=0, keepdims=True)
        shift = cbq - (m + cbq)
        p_ref[u % FOX_PROB_SLOTS, 0:r1, :] = jnp.exp2(s_ref[slot, 0:r1, :] + shift).astype(BF16)

    def weighted_values(u):
        i, h = units[u]
        r0, r1 = i * tq, (i + 1) * tq
        ot = jnp.dot(vt_ref[h, :, 0:r1], p_ref[u % FOX_PROB_SLOTS, 0:r1, :], preferred_element_type=F32)
        out = ot[0:HEAD_DIM] * (1.0 / ot[HEAD_DIM:HEAD_DIM + 1])
        o_ref[h * HEAD_DIM:(h + 1) * HEAD_DIM, r0:r1] = out.astype(o_ref.dtype)

    ahead = FOX_SCORE_SLOTS - 1
    for u in range(min(ahead, len(units))):
        scores(u)
    for u in range(len(units)):
        if u + ahead < len(units):
            scores(u + ahead)
        softmax(u)
        if u >= FOX_PV_LAG:
            weighted_values(u - FOX_PV_LAG)
    for u in range(len(units) - FOX_PV_LAG, len(units)):
        weighted_values(u)


def _fox_attention(qkv, cumt, kext):
    b, s, _ = qkv.shape
    heads = FOX_PAIRS_PER_STEP * FOX_HEADS_PER_PAIR
    width = FOX_PAIRS_PER_STEP * LANES
    lw = jnp.asarray(_fox_lane_words())
    ncol = D_MODEL // width
    return pl.pallas_call(
        _fox_kernel,
        grid=(b, N_HEADS // heads),
        in_specs=[pl.BlockSpec((None, s, width), lambda bi, hp: (bi, 0, hp)),
                  pl.BlockSpec((None, s, width), lambda bi, hp: (bi, 0, ncol + hp)),
                  pl.BlockSpec((None, s, width), lambda bi, hp: (bi, 0, 2 * ncol + hp)),
                  pl.BlockSpec((None, s, width), lambda bi, hp: (bi, 0, hp)),
                  pl.BlockSpec((None, N_HEADS, s), lambda bi, hp: (bi, 0, 0)),
                  pl.BlockSpec((SUBLANES, LANES), lambda bi, hp: (0, 0))],
        out_specs=pl.BlockSpec((None, heads * HEAD_DIM, s), lambda bi, hp: (bi, hp, 0)),
        out_shape=jax.ShapeDtypeStruct((b, D_MODEL, s), BF16),
        scratch_shapes=[pltpu.VMEM((heads, s, LANES), BF16), pltpu.VMEM((heads, s, LANES), BF16),
                        pltpu.VMEM((heads, HEAD_DIM + FOX_SUM_ROWS, s), BF16),
                        pltpu.VMEM((FOX_SCORE_SLOTS, s, FOX_TQ), F32),
                        pltpu.VMEM((FOX_PROB_SLOTS, s, FOX_TQ), BF16)],
        compiler_params=_params("arbitrary", "arbitrary"),
        name="fox_attn",
    )(qkv, qkv, qkv, kext, cumt, lw)


def _deepnorm(xres, y, g, b, alpha):
    z = alpha * xres + y
    mu = jnp.mean(z, axis=-1, keepdims=True)
    zc = z - mu
    var = jnp.mean(zc * zc, axis=-1, keepdims=True)
    return zc * lax.rsqrt(var + LN_EPS) * g + b


LN_SUBTILES = 2


def _oproj_ln_kernel(alpha, ot_ref, xres_ref, w_ref, g_ref, b_ref, xo_ref, xob_ref, wb_ref):
    @pl.when(pl.program_id(0) == 0)
    def _():
        wb_ref[...] = w_ref[...].astype(BF16)

    ts = xres_ref.shape[0] // LN_SUBTILES
    ys = [lax.dot_general(ot_ref[:, r * ts:(r + 1) * ts], wb_ref[...], _TN_DIMS,
                          preferred_element_type=F32) for r in range(LN_SUBTILES)]
    for r, y in enumerate(ys):
        rows = slice(r * ts, (r + 1) * ts)
        out = _deepnorm(xres_ref[rows, :], y, g_ref[...], b_ref[...], alpha)
        xo_ref[rows, :] = out
        xob_ref[rows, :] = out.astype(BF16)


def _oproj_ln(ot, xres, w, layer, g, b, alpha, tm, name):
    m, n = xres.shape
    k = w.shape[1]
    per_seq = ot.shape[2] // tm
    return pl.pallas_call(
        functools.partial(_oproj_ln_kernel, alpha),
        grid=(m // tm,),
        in_specs=[pl.BlockSpec((None, k, tm), lambda i: (i // per_seq, 0, i % per_seq)),
                  pl.BlockSpec((tm, n), lambda i: (i, 0)),
                  pl.BlockSpec((None, k, n), lambda i: (layer, 0, 0), pipeline_mode=pl.Buffered(1)),
                  pl.BlockSpec((1, n), lambda i: (0, 0)),
                  pl.BlockSpec((1, n), lambda i: (0, 0))],
        out_specs=[pl.BlockSpec((tm, n), lambda i: (i, 0)),
                   pl.BlockSpec((tm, n), lambda i: (i, 0))],
        out_shape=[jax.ShapeDtypeStruct((m, n), F32), jax.ShapeDtypeStruct((m, n), BF16)],
        scratch_shapes=[pltpu.VMEM((k, n), BF16)],
        compiler_params=_params("arbitrary"),
        name=name,
    )(ot, xres, w, g, b)


CONV_WIDTH = 3
FFN_TF = 512
FFN_UP_TM = 1024


def _ffn_up_kernel(tiles_per_seq, x_ref, wg_ref, wu_ref, cwg_ref, cwu_ref, cbg_ref, cbu_ref, wd_ref,
                   a_ref, wdb_ref, wb_ref, h_ref):
    wdb_ref[...] = wd_ref[...].astype(BF16)
    i = pl.program_id(1)
    tm = x_ref.shape[0]
    tf = a_ref.shape[1]
    halo = SUBLANES

    @pl.when(i == 0)
    def _():
        wb_ref[:, 0:tf] = wg_ref[...].astype(BF16)
        wb_ref[:, tf:2 * tf] = wu_ref[...].astype(BF16)

    @pl.when(i % tiles_per_seq == 0)
    def _():
        h_ref[0:halo, :] = jnp.zeros((halo, h_ref.shape[1]), F32)

    h_ref[halo:halo + tm, :] = jnp.dot(x_ref[...], wb_ref[...], preferred_element_type=F32)

    def conv(cols, cw_ref, cb_ref, scale):
        hh = h_ref[0:halo + tm, cols]
        cw = cw_ref[...] * scale
        acc = cw[0:1, :] * hh
        for tap in range(1, CONV_WIDTH):
            acc = cw[tap:tap + 1, :] * hh + pltpu.roll(acc, 1, 0)
        return acc[halo:, :] + cb_ref[...] * scale

    half = conv(slice(0, tf), cwg_ref, cbg_ref, 0.5)
    u = conv(slice(tf, 2 * tf), cwu_ref, cbu_ref, 1.0)
    a_ref[...] = ((half + half * jnp.tanh(half)) * u).astype(a_ref.dtype)
    h_ref[0:halo, :] = h_ref[tm:tm + halo, :]


def _ffn_up(xb, w_up, conv_w, conv_b, w_down, layer, seq):
    m, k = xb.shape
    tm, tf = FFN_UP_TM, FFN_TF
    nf = D_FF // tf
    n_tiles = m // tm
    cb2 = conv_b.reshape(conv_b.shape[0], 1, 2 * D_FF)
    wd_rows = w_down.shape[1] // (nf * n_tiles)
    assert wd_rows * nf * n_tiles == w_down.shape[1] and wd_rows % (2 * SUBLANES) == 0
    return pl.pallas_call(
        functools.partial(_ffn_up_kernel, seq // tm),
        grid=(nf, n_tiles),
        in_specs=[pl.BlockSpec((tm, k), lambda j, i: (i, 0)),
                  pl.BlockSpec((None, k, tf), lambda j, i: (layer, 0, j)),
                  pl.BlockSpec((None, k, tf), lambda j, i: (layer, 0, j + nf)),
                  pl.BlockSpec((None, CONV_WIDTH, tf), lambda j, i: (layer, 0, j)),
                  pl.BlockSpec((None, CONV_WIDTH, tf), lambda j, i: (layer, 0, j + nf)),
                  pl.BlockSpec((None, 1, tf), lambda j, i: (layer, 0, j)),
                  pl.BlockSpec((None, 1, tf), lambda j, i: (layer, 0, j + nf)),
                  pl.BlockSpec((None, wd_rows, w_down.shape[2]), lambda j, i: (layer, j * n_tiles + i, 0))],
        out_specs=[pl.BlockSpec((tm, tf), lambda j, i: (i, j)),
                   pl.BlockSpec((wd_rows, w_down.shape[2]), lambda j, i: (j * n_tiles + i, 0))],
        out_shape=[jax.ShapeDtypeStruct((m, D_FF), BF16),
                   jax.ShapeDtypeStruct(w_down.shape[1:], BF16)],
        scratch_shapes=[pltpu.VMEM((k, 2 * tf), BF16), pltpu.VMEM((tm + 2 * SUBLANES, 2 * tf), F32)],
        compiler_params=_params("arbitrary", "arbitrary"),
        name="ffn_up",
    )(xb, w_up, w_up, conv_w, conv_w, cb2, cb2, w_down)


FFN_DOWN_TM = 256


def _ffn_down_ln_kernel(alpha, emit_bf16, a_ref, xres_ref, w_ref, g_ref, b_ref, xo_ref, *rest):
    y = jnp.dot(a_ref[...], w_ref[...], preferred_element_type=F32)
    out = _deepnorm(xres_ref[...], y, g_ref[...], b_ref[...], alpha)
    xo_ref[...] = out
    if emit_bf16:
        rest[0][...] = out.astype(BF16)


def _ffn_down_ln(a, xres, w_bf16, g, b, alpha, emit_bf16, name):
    m, k = a.shape
    n = w_bf16.shape[1]
    tm = FFN_DOWN_TM
    out_specs = [pl.BlockSpec((tm, n), lambda i: (i, 0))]
    out_shape = [jax.ShapeDtypeStruct((m, n), F32)]
    if emit_bf16:
        out_specs.append(pl.BlockSpec((tm, n), lambda i: (i, 0)))
        out_shape.append(jax.ShapeDtypeStruct((m, n), BF16))
    return pl.pallas_call(
        functools.partial(_ffn_down_ln_kernel, alpha, emit_bf16),
        grid=(m // tm,),
        in_specs=[pl.BlockSpec((tm, k), lambda i: (i, 0)),
                  pl.BlockSpec((tm, n), lambda i: (i, 0)),
                  pl.BlockSpec((k, n), lambda i: (0, 0), pipeline_mode=pl.Buffered(1)),
                  pl.BlockSpec((1, n), lambda i: (0, 0)),
                  pl.BlockSpec((1, n), lambda i: (0, 0))],
        out_specs=out_specs,
        out_shape=out_shape,
        compiler_params=_params("arbitrary"),
        name=name,
    )(a, xres, w_bf16, g, b)


def kernel(x, a_w_qkv, a_w_o, a_sinks, rel_bias, b_w_qkvf, b_f_bias, b_w_o,
           ffn_w_up, ffn_conv_w, ffn_conv_b, ffn_w_down, ln_g, ln_b):
    batch, seq, d = x.shape
    depth = ffn_w_up.shape[0]
    alpha = (2 * depth) ** 0.25
    m = batch * seq
    xf = x.reshape(m, d)
    xb = None
    biasmask = _swa_bias(rel_bias)
    for layer in range(depth):
        j = layer // 2
        g = ln_g[layer].reshape(2, 1, d)
        bt = ln_b[layer].reshape(2, 1, d)
        if layer % 2 == 0:
            n_qkv = D_MODEL + 2 * A_KV_DIM
            src = xf if xb is None else xb
            qkv = _ws_matmul(src, a_w_qkv, j, n_qkv, n_qkv // SWA_QKV_COL_TILES, SWA_QKV_TM, "swa_qkv",
                             slab_batch=(batch, seq))
            o = _swa_attention(qkv, a_sinks[j], biasmask)
            w_o = a_w_o
        else:
            w_t = jnp.swapaxes(b_w_qkvf, 1, 2)
            qkv = _ws_matmul(xb, w_t, j, 3 * D_MODEL, FOX_QKV_TN, FOX_QKV_TM, "fox_qkv",
                             w_is_transposed=True)
            fb3 = jnp.tile(b_f_bias[j].reshape(N_HEADS, 1), (FOX_PIECES, 1))
            cumt, kext = _fox_gate(xb, w_t, j, fb3, batch, seq)
            o = _fox_attention(qkv.reshape(batch, seq, 3 * D_MODEL), cumt, kext)
            w_o = b_w_o
        xf, xb = _oproj_ln(o, xf, w_o, j, g[0], bt[0], alpha, OPROJ_TM, "oproj_ln")
        a, w_down = _ffn_up(xb, ffn_w_up, ffn_conv_w, ffn_conv_b, ffn_w_down, layer, seq)
        last = layer == depth - 1
        res = _ffn_down_ln(a, xf, w_down, g[1], bt[1], alpha, not last, "ffn_down_ln")
        xf, xb = (res[0], None) if last else res
    return xf.reshape(batch, seq, d)
```

```python
import functools
import math

import jax
import jax.numpy as jnp
import numpy as np
from jax import lax
from jax.experimental import pallas as pl
from jax.experimental.pallas import tpu as pltpu

D_MODEL = 2048
HEAD_DIM = 64
BLOCK = 128
N_HEADS = D_MODEL // HEAD_DIM
A_KV_HEADS = 4
A_GROUP = N_HEADS // A_KV_HEADS
A_KV_DIM = A_KV_HEADS * HEAD_DIM
REL_BUCKETS = 32
REL_MAX_DIST = 128
D_FF = 5632
LN_EPS = 1e-5
SCALE = HEAD_DIM ** -0.5

SUBLANES = 8
LANES = 128

BF16 = jnp.bfloat16
F32 = jnp.float32
NEG_INF = float("-inf")

_NT_DIMS = (((1,), (1,)), ((), ()))


def _params(*sem):
    return pltpu.CompilerParams(dimension_semantics=sem)


LOG2E = math.log2(math.e)
Q_SCALE = SCALE * LOG2E
SWA_QKV_COL_TILES = 2
SWA_QKV_TM = 1024
FOX_QKV_TN = 1024
FOX_QKV_TM = 2048
OPROJ_TM = 512


def _ws_matmul_kernel(w_is_transposed, x_ref, w_ref, o_ref, wb_ref):
    @pl.when(pl.program_id(1) == 0)
    def _():
        wb_ref[...] = w_ref[...].astype(BF16)

    x = x_ref[...].astype(BF16)
    if w_is_transposed:
        y = lax.dot_general(x, wb_ref[...], _NT_DIMS, preferred_element_type=F32)
    else:
        y = jnp.dot(x, wb_ref[...], preferred_element_type=F32)
    tn = y.shape[1]
    col = pl.program_id(0) * tn + lax.broadcasted_iota(jnp.int32, (1, tn), 1)
    y = (y * jnp.where(col < D_MODEL, Q_SCALE, 1.0)).astype(o_ref.dtype)
    if len(o_ref.shape) == 2:
        o_ref[...] = y
    else:
        for c in range(o_ref.shape[0]):
            o_ref[c] = y[:, c * LANES:(c + 1) * LANES]


def _ws_matmul(x, w, layer, n_out, tn, tm, name, w_is_transposed=False, slab_batch=None):
    m, k = x.shape
    if w_is_transposed:
        w_spec = pl.BlockSpec((None, tn, k), lambda j, i: (layer, j, 0))
        wb_shape = (tn, k)
    else:
        w_spec = pl.BlockSpec((None, k, tn), lambda j, i: (layer, 0, j))
        wb_shape = (k, tn)
    if slab_batch is None:
        out_spec = pl.BlockSpec((tm, tn), lambda j, i: (i, j))
        out_shape = jax.ShapeDtypeStruct((m, n_out), BF16)
    else:
        batch, seq = slab_batch
        per_seq = seq // tm
        out_spec = pl.BlockSpec((None, tn // LANES, tm, LANES),
                                lambda j, i: (i // per_seq, j, i % per_seq, 0))
        out_shape = jax.ShapeDtypeStruct((batch, n_out // LANES, seq, LANES), BF16)
    return pl.pallas_call(
        functools.partial(_ws_matmul_kernel, w_is_transposed),
        grid=(n_out // tn, m // tm),
        in_specs=[pl.BlockSpec((tm, k), lambda j, i: (i, 0)), w_spec],
        out_specs=out_spec,
        out_shape=out_shape,
        scratch_shapes=[pltpu.VMEM(wb_shape, BF16)],
        compiler_params=_params("arbitrary", "arbitrary"),
        name=name,
    )(x, w)


def _t5_bucket_map():
    qi = np.arange(BLOCK)[None, :]
    kj = np.arange(2 * BLOCK)[:, None]
    dist = BLOCK + qi - kj
    n = np.maximum(dist, 0)
    max_exact = REL_BUCKETS // 2
    nf = np.maximum(n, 1).astype(np.float32)
    large = max_exact + (np.log(nf / max_exact) / math.log(REL_MAX_DIST / max_exact)
                         * (REL_BUCKETS - max_exact)).astype(np.int32)
    large = np.minimum(large, REL_BUCKETS - 1)
    bucket = np.where(n < max_exact, n, large)
    visible = (dist >= 0) & (dist < BLOCK)
    return np.where(visible, bucket, -1).astype(np.int32)


def _swa_bias_kernel(rb_ref, bk_ref, o_ref):
    bk = bk_ref[...]
    key = lax.broadcasted_iota(jnp.int32, (BLOCK, BLOCK), 0)
    qry = lax.broadcasted_iota(jnp.int32, (BLOCK, BLOCK), 1)
    current = qry >= key
    masked = jnp.full((BLOCK, BLOCK), NEG_INF, F32)

    def head(h, _):
        v = jnp.zeros(bk.shape, F32)
        for b in range(REL_BUCKETS):
            v = jnp.where(bk == b, rb_ref[b, h] * LOG2E, v)
        rot = pltpu.roll(jnp.broadcast_to(v, (BLOCK, BLOCK)), 0, 1, stride=1, stride_axis=0)
        upper = jnp.where(current, rot, NEG_INF)
        o_ref[0, h, 0:BLOCK, :] = jnp.where(current, NEG_INF, rot)
        o_ref[0, h, BLOCK:, :] = upper
        o_ref[1, h, 0:BLOCK, :] = masked
        o_ref[1, h, BLOCK:, :] = upper
        return 0

    lax.fori_loop(0, N_HEADS, head, 0)


def _swa_bias(rel_bias):
    bucket_map = _t5_bucket_map()
    by_dist = bucket_map[BLOCK:BLOCK + 1, :]
    kj, qi = np.arange(2 * BLOCK)[:, None], np.arange(BLOCK)[None, :]
    dist = BLOCK + qi - kj
    toeplitz = np.where((dist >= 0) & (dist < BLOCK), by_dist[0, dist % BLOCK], -1)
    assert (toeplitz == bucket_map).all()
    bk = jnp.asarray(by_dist)
    return pl.pallas_call(
        _swa_bias_kernel,
        in_specs=[pl.BlockSpec(memory_space=pltpu.SMEM),
                  pl.BlockSpec(memory_space=pltpu.VMEM)],
        out_specs=pl.BlockSpec(memory_space=pltpu.VMEM),
        out_shape=jax.ShapeDtypeStruct((2, N_HEADS, 2 * BLOCK, BLOCK), F32),
        name="swa_bias",
    )(rel_bias, bk)


_TN_DIMS = (((0,), (0,)), ((), ()))
SWA_SCORES_AHEAD = 2
SWA_BLOCKS_PER_STEP = 4


def _swa_kernel(sink_ref, q_ref, kp_ref, kc_ref, vp_ref, vc_ref, bm_ref, o_ref,
                qs_ref, s_ref, p_ref, linv_ref):
    units = [(qb, kh) for qb in range(SWA_BLOCKS_PER_STEP) for kh in range(A_KV_HEADS)]
    first_table = jnp.where(pl.program_id(1) == 0, 1, 0)

    def head_at(head):
        slab, half = divmod(head, LANES // HEAD_DIM)
        return slab, slice(half * HEAD_DIM, (half + 1) * HEAD_DIM)

    def keys_values(prev_ref, cur_ref, qb, kh):
        slab, lanes = head_at(kh)
        if qb == 0:
            return jnp.concatenate([prev_ref[slab, :, lanes], cur_ref[slab, 0:BLOCK, lanes]], axis=0)
        return cur_ref[slab, (qb - 1) * BLOCK:(qb + 1) * BLOCK, lanes]

    def scores(u):
        qb, kh = units[u]
        for g in range(A_GROUP):
            slab, lanes = head_at(kh * A_GROUP + g)
            qs_ref[u, g * BLOCK:(g + 1) * BLOCK, :] = q_ref[slab, qb * BLOCK:(qb + 1) * BLOCK, lanes]
        k2 = keys_values(kp_ref, kc_ref, qb, kh)
        s = lax.dot_general(k2, qs_ref[u], _NT_DIMS, preferred_element_type=F32)
        for g in range(A_GROUP):
            s_ref[u, g] = s[:, g * BLOCK:(g + 1) * BLOCK]

    def softmax(u):
        qb, kh = units[u]
        table = first_table if qb == 0 else 0
        for g in range(A_GROUP):
            h = kh * A_GROUP + g
            cols = slice(g * BLOCK, (g + 1) * BLOCK)
            s = s_ref[u, g] + bm_ref[table, h]
            sink = sink_ref[h] * LOG2E
            m = jnp.maximum(jnp.max(s, axis=0, keepdims=True), sink)
            p = jnp.exp2(s - m)
            denom = jnp.sum(p, axis=0, keepdims=True) + jnp.exp2(sink - m)
            p_ref[u, g // 2, :, (g % 2) * BLOCK:(g % 2 + 1) * BLOCK] = p.astype(BF16)
            linv_ref[u, :, cols] = 1.0 / denom

    def weighted_values(u):
        qb, kh = units[u]
        v2 = keys_values(vp_ref, vc_ref, qb, kh)
        for c in range(A_GROUP // 2):
            cols = slice(2 * c * BLOCK, 2 * (c + 1) * BLOCK)
            ot = (lax.dot_general(v2, p_ref[u, c], _TN_DIMS, preferred_element_type=F32)
                  * linv_ref[u, :, cols])
            for gg in range(2):
                h = kh * A_GROUP + 2 * c + gg
                o_ref[h * HEAD_DIM:(h + 1) * HEAD_DIM, qb * BLOCK:(qb + 1) * BLOCK] = (
                    ot[:, gg * BLOCK:(gg + 1) * BLOCK].astype(o_ref.dtype))

    for u in range(min(SWA_SCORES_AHEAD, len(units))):
        scores(u)
    for u in range(len(units)):
        if u + SWA_SCORES_AHEAD < len(units):
            scores(u + SWA_SCORES_AHEAD)
        softmax(u)
        weighted_values(u)


def _swa_attention(qkv, sinks, biasmask):
    b, _, s, _ = qkv.shape
    rows = SWA_BLOCKS_PER_STEP * BLOCK
    gq = A_GROUP * BLOCK
    n_units = SWA_BLOCKS_PER_STEP * A_KV_HEADS
    q_slabs = D_MODEL // LANES
    kv_slabs = A_KV_DIM // LANES
    kcol = q_slabs // kv_slabs
    prev = lambda n: jnp.maximum(SWA_BLOCKS_PER_STEP * n - 1, 0)
    return pl.pallas_call(
        _swa_kernel,
        grid=(b, s // rows),
        in_specs=[pl.BlockSpec(memory_space=pltpu.SMEM),
                  pl.BlockSpec((None, q_slabs, rows, LANES), lambda bi, n: (bi, 0, n, 0)),
                  pl.BlockSpec((None, kv_slabs, BLOCK, LANES), lambda bi, n: (bi, kcol, prev(n), 0)),
                  pl.BlockSpec((None, kv_slabs, rows, LANES), lambda bi, n: (bi, kcol, n, 0)),
                  pl.BlockSpec((None, kv_slabs, BLOCK, LANES), lambda bi, n: (bi, kcol + 1, prev(n), 0)),
                  pl.BlockSpec((None, kv_slabs, rows, LANES), lambda bi, n: (bi, kcol + 1, n, 0)),
                  pl.BlockSpec((2, N_HEADS, 2 * BLOCK, BLOCK), lambda bi, n: (0, 0, 0, 0),
                               pipeline_mode=pl.Buffered(1))],
        out_specs=pl.BlockSpec((None, D_MODEL, rows), lambda bi, n: (bi, 0, n)),
        out_shape=jax.ShapeDtypeStruct((b, D_MODEL, s), BF16),
        scratch_shapes=[pltpu.VMEM((n_units, gq, HEAD_DIM), BF16),
                        pltpu.VMEM((n_units, A_GROUP, 2 * BLOCK, BLOCK), F32),
                        pltpu.VMEM((n_units, A_GROUP // 2, 2 * BLOCK, 2 * BLOCK), BF16),
                        pltpu.VMEM((n_units, 1, gq), F32)],
        compiler_params=_params("arbitrary", "arbitrary"),
        name="swa_attn",
    )(sinks, qkv, qkv, qkv, qkv, qkv, biasmask)


FOX_CUM_BLOCK = 256
FOX_TQ = 256
FOX_HEADS_PER_PAIR = LANES // HEAD_DIM
FOX_PAIRS_PER_STEP = 2
FOX_PIECES = 3
FOX_SUM_ROWS = 16
FOX_SCORE_SLOTS = 4
FOX_PV_LAG = 1
FOX_PROB_SLOTS = FOX_PV_LAG + 1
BF16_ONE_PAIR = 0x3F803F80


def _fox_ext_lane(head, piece):
    pair, odd = divmod(head, FOX_HEADS_PER_PAIR)
    return pair * LANES + (1 - odd) * HEAD_DIM + piece


def _fox_placement():
    pm = np.zeros((FOX_PIECES * N_HEADS, D_MODEL), np.float32)
    for t in range(FOX_PIECES):
        for h in range(N_HEADS):
            pm[t * N_HEADS + h, _fox_ext_lane(h, t)] = -1.0
    return pm


def _fox_lane_words():
    w = np.zeros((SUBLANES, LANES), np.uint32)
    for hh in range(FOX_HEADS_PER_PAIR):
        own = np.zeros(LANES, bool)
        own[hh * HEAD_DIM:(hh + 1) * HEAD_DIM] = True
        ext = np.zeros(LANES, bool)
        first = _fox_ext_lane(hh, 0)
        ext[first:first + FOX_PIECES] = True
        w[3 * hh + 0] = np.where(own, 0xFFFFFFFF, 0)
        w[3 * hh + 1] = np.where(ext, 0xFFFFFFFF, 0)
        w[3 * hh + 2] = np.where(ext, BF16_ONE_PAIR, 0)
    return w


def _log_sigmoid(x):
    return jnp.minimum(x, 0.0) - jnp.log1p(jnp.exp(-jnp.abs(x)))


def _round_bf16(x):
    return x.astype(BF16).astype(F32)


def _fox_gate_kernel(x_ref, wf_ref, fb_ref, pm_ref, cumt_ref, kext_ref, w3_ref):
    @pl.when(pl.program_id(0) == 0)
    def _():
        w = wf_ref[...].astype(BF16)
        for t in range(FOX_PIECES):
            w3_ref[t * N_HEADS:(t + 1) * N_HEADS, :] = w

    fz = lax.dot_general(w3_ref[...], x_ref[...], _NT_DIMS, preferred_element_type=F32) + fb_ref[...]
    lf = _log_sigmoid(fz)
    cb = FOX_CUM_BLOCK
    row = lax.broadcasted_iota(jnp.int32, (cb, cb), 0)
    col = lax.broadcasted_iota(jnp.int32, (cb, cb), 1)
    tri = jnp.where(row <= col, 1.0, 0.0).astype(BF16)
    tri3 = jnp.concatenate([tri] * FOX_PIECES, axis=0)
    group = lax.broadcasted_iota(jnp.int32, (lf.shape[0], cb), 0) // N_HEADS
    pm = pm_ref[...]
    n_blocks = lf.shape[1] // cb
    local = []
    for blk in range(n_blocks):
        x = lf[:, blk * cb:(blk + 1) * cb]
        hi = _round_bf16(x)
        r1 = x - hi
        mid = _round_bf16(r1)
        pieces = jnp.concatenate([hi, mid, r1 - mid], axis=1).astype(BF16)
        local.append(jnp.dot(pieces, tri3, preferred_element_type=F32))
    carry = jnp.zeros((lf.shape[0], 1), F32)
    for blk in range(n_blocks):
        cols = slice(blk * cb, (blk + 1) * cb)
        c = local[blk] + carry
        carry = carry + local[blk][:, cb - 1:cb]
        c2 = c * LOG2E
        cumt_ref[:, cols] = c2[:N_HEADS]
        chi = _round_bf16(c2)
        cr1 = c2 - chi
        cmid = _round_bf16(cr1)
        piece = jnp.where(group == 0, chi, jnp.where(group == 1, cmid, cr1 - cmid)).astype(BF16)
        kext_ref[cols, :] = lax.dot_general(piece, pm, _TN_DIMS, preferred_element_type=F32).astype(BF16)


def _fox_gate(xb, w_qkvf_t, layer, fb3, batch, seq):
    pm = jnp.asarray(_fox_placement(), BF16)
    n3 = FOX_PIECES * N_HEADS
    gate_row = 3 * D_MODEL // N_HEADS
    return pl.pallas_call(
        _fox_gate_kernel,
        grid=(batch,),
        in_specs=[pl.BlockSpec((seq, D_MODEL), lambda bi: (bi, 0)),
                  pl.BlockSpec((None, N_HEADS, D_MODEL), lambda bi: (layer, gate_row, 0)),
                  pl.BlockSpec((n3, 1), lambda bi: (0, 0)),
                  pl.BlockSpec((n3, D_MODEL), lambda bi: (0, 0))],
        out_specs=[pl.BlockSpec((None, N_HEADS, seq), lambda bi: (bi, 0, 0)),
                   pl.BlockSpec((None, seq, D_MODEL), lambda bi: (bi, 0, 0))],
        out_shape=[jax.ShapeDtypeStruct((batch, N_HEADS, seq), F32),
                   jax.ShapeDtypeStruct((batch, seq, D_MODEL), BF16)],
        scratch_shapes=[pltpu.VMEM((n3, D_MODEL), BF16)],
        compiler_params=_params("arbitrary"),
        name="fox_gate",
    )(xb, w_qkvf_t, fb3, pm)


def _fox_kernel(q_ref, k_ref, v_ref, kext_ref, cbt_ref, lw_ref, o_ref, qa_ref, ka_ref, vt_ref,
                s_ref, p_ref):
    seq = q_ref.shape[0]
    tq = FOX_TQ
    key = lax.broadcasted_iota(jnp.int32, (tq, tq), 0)
    qry = lax.broadcasted_iota(jnp.int32, (tq, tq), 1)
    causal = key <= qry
    words = lambda x: pltpu.bitcast(x, jnp.uint32)
    for pair in range(FOX_PAIRS_PER_STEP):
        lanes = slice(pair * LANES, (pair + 1) * LANES)
        q = words(q_ref[:, lanes])
        k = words(k_ref[:, lanes])
        kext = words(kext_ref[:, lanes])
        vt = v_ref[:, lanes].T
        for hh in range(FOX_HEADS_PER_PAIR):
            h = pair * FOX_HEADS_PER_PAIR + hh
            own = lw_ref[3 * hh + 0:3 * hh + 1, :]
            ext = lw_ref[3 * hh + 1:3 * hh + 2, :]
            ones = lw_ref[3 * hh + 2:3 * hh + 3, :]
            qa_ref[h] = pltpu.bitcast((q & own) | ones, BF16)
            ka_ref[h] = pltpu.bitcast((k & ~ext) | (kext & ext), BF16)
            vt_ref[h, 0:HEAD_DIM, :] = vt[hh * HEAD_DIM:(hh + 1) * HEAD_DIM]
            vt_ref[h, HEAD_DIM:, :] = jnp.ones((FOX_SUM_ROWS, seq), BF16)

    heads = FOX_PAIRS_PER_STEP * FOX_HEADS_PER_PAIR
    units = [(i, h) for i in range(seq // tq) for h in range(heads)]

    def scores(u):
        i, h = units[u]
        slot = u % FOX_SCORE_SLOTS
        r0, r1 = i * tq, (i + 1) * tq
        qa = qa_ref[h, r0:r1, :]
        s_d = lax.dot_general(ka_ref[h, r0:r1, :], qa, _NT_DIMS, preferred_element_type=F32)
        s_ref[slot, r0:r1, :] = jnp.where(causal, s_d, NEG_INF)
        if i > 0:
            s_ref[slot, 0:r0, :] = lax.dot_general(ka_ref[h, 0:r0, :], qa, _NT_DIMS,
                                                   preferred_element_type=F32)

    def softmax(u):
        i, h = units[u]
        slot = u % FOX_SCORE_SLOTS
        r0, r1 = i * tq, (i + 1) * tq
        head = pl.program_id(1) * heads + h
        cbq = cbt_ref[pl.ds(head, 1), r0:r1]
        m = jnp.max(s_ref[slot, 0:r1, :], axis=0, keepdims=True)
        shift = cbq - (m + cbq)
        p_ref[u % FOX_PROB_SLOTS, 0:r1, :] = jnp.exp2(s_ref[slot, 0:r1, :] + shift).astype(BF16)

    def weighted_values(u):
        i, h = units[u]
        r0, r1 = i * tq, (i + 1) * tq
        ot = jnp.dot(vt_ref[h, :, 0:r1], p_ref[u % FOX_PROB_SLOTS, 0:r1, :], preferred_element_type=F32)
        out = ot[0:HEAD_DIM] * (1.0 / ot[HEAD_DIM:HEAD_DIM + 1])
        o_ref[h * HEAD_DIM:(h + 1) * HEAD_DIM, r0:r1] = out.astype(o_ref.dtype)

    ahead = FOX_SCORE_SLOTS - 1
    for u in range(min(ahead, len(units))):
        scores(u)
    for u in range(len(units)):
        if u + ahead < len(units):
            scores(u + ahead)
        softmax(u)
        if u >= FOX_PV_LAG:
            weighted_values(u - FOX_PV_LAG)
    for u in range(len(units) - FOX_PV_LAG, len(units)):
        weighted_values(u)


def _fox_attention(qkv, cumt, kext):
    b, s, _ = qkv.shape
    heads = FOX_PAIRS_PER_STEP * FOX_HEADS_PER_PAIR
    width = FOX_PAIRS_PER_STEP * LANES
    lw = jnp.asarray(_fox_lane_words())
    ncol = D_MODEL // width
    return pl.pallas_call(
        _fox_kernel,
        grid=(b, N_HEADS // heads),
        in_specs=[pl.BlockSpec((None, s, width), lambda bi, hp: (bi, 0, hp)),
                  pl.BlockSpec((None, s, width), lambda bi, hp: (bi, 0, ncol + hp)),
                  pl.BlockSpec((None, s, width), lambda bi, hp: (bi, 0, 2 * ncol + hp)),
                  pl.BlockSpec((None, s, width), lambda bi, hp: (bi, 0, hp)),
                  pl.BlockSpec((None, N_HEADS, s), lambda bi, hp: (bi, 0, 0)),
                  pl.BlockSpec((SUBLANES, LANES), lambda bi, hp: (0, 0))],
        out_specs=pl.BlockSpec((None, heads * HEAD_DIM, s), lambda bi, hp: (bi, hp, 0)),
        out_shape=jax.ShapeDtypeStruct((b, D_MODEL, s), BF16),
        scratch_shapes=[pltpu.VMEM((heads, s, LANES), BF16), pltpu.VMEM((heads, s, LANES), BF16),
                        pltpu.VMEM((heads, HEAD_DIM + FOX_SUM_ROWS, s), BF16),
                        pltpu.VMEM((FOX_SCORE_SLOTS, s, FOX_TQ), F32),
                        pltpu.VMEM((FOX_PROB_SLOTS, s, FOX_TQ), BF16)],
        compiler_params=_params("arbitrary", "arbitrary"),
        name="fox_attn",
    )(qkv, qkv, qkv, kext, cumt, lw)


def _deepnorm(xres, y, g, b, alpha):
    z = alpha * xres + y
    mu = jnp.mean(z, axis=-1, keepdims=True)
    zc = z - mu
    var = jnp.mean(zc * zc, axis=-1, keepdims=True)
    return zc * lax.rsqrt(var + LN_EPS) * g + b


LN_SUBTILES = 2


def _oproj_ln_kernel(alpha, ot_ref, xres_ref, w_ref, g_ref, b_ref, xo_ref, xob_ref, wb_ref):
    @pl.when(pl.program_id(0) == 0)
    def _():
        wb_ref[...] = w_ref[...].astype(BF16)

    ts = xres_ref.shape[0] // LN_SUBTILES
    ys = [lax.dot_general(ot_ref[:, r * ts:(r + 1) * ts], wb_ref[...], _TN_DIMS,
                          preferred_element_type=F32) for r in range(LN_SUBTILES)]
    for r, y in enumerate(ys):
        rows = slice(r * ts, (r + 1) * ts)
        out = _deepnorm(xres_ref[rows, :], y, g_ref[...], b_ref[...], alpha)
        xo_ref[rows, :] = out
        xob_ref[rows, :] = out.astype(BF16)


def _oproj_ln(ot, xres, w, layer, g, b, alpha, tm, name):
    m, n = xres.shape
    k = w.shape[1]
    per_seq = ot.shape[2] // tm
    return pl.pallas_call(
        functools.partial(_oproj_ln_kernel, alpha),
        grid=(m // tm,),
        in_specs=[pl.BlockSpec((None, k, tm), lambda i: (i // per_seq, 0, i % per_seq)),
                  pl.BlockSpec((tm, n), lambda i: (i, 0)),
                  pl.BlockSpec((None, k, n), lambda i: (layer, 0, 0), pipeline_mode=pl.Buffered(1)),
                  pl.BlockSpec((1, n), lambda i: (0, 0)),
                  pl.BlockSpec((1, n), lambda i: (0, 0))],
        out_specs=[pl.BlockSpec((tm, n), lambda i: (i, 0)),
                   pl.BlockSpec((tm, n), lambda i: (i, 0))],
        out_shape=[jax.ShapeDtypeStruct((m, n), F32), jax.ShapeDtypeStruct((m, n), BF16)],
        scratch_shapes=[pltpu.VMEM((k, n), BF16)],
        compiler_params=_params("arbitrary"),
        name=name,
    )(ot, xres, w, g, b)


CONV_WIDTH = 3
FFN_TF = 512
FFN_UP_TM = 1024


def _ffn_up_kernel(tiles_per_seq, x_ref, wg_ref, wu_ref, cwg_ref, cwu_ref, cbg_ref, cbu_ref, wd_ref,
                   a_ref, wdb_ref, wb_ref, h_ref):
    wdb_ref[...] = wd_ref[...].astype(BF16)
    i = pl.program_id(1)
    tm = x_ref.shape[0]
    tf = a_ref.shape[1]
    halo = SUBLANES

    @pl.when(i == 0)
    def _():
        wb_ref[:, 0:tf] = wg_ref[...].astype(BF16)
        wb_ref[:, tf:2 * tf] = wu_ref[...].astype(BF16)

    @pl.when(i % tiles_per_seq == 0)
    def _():
        h_ref[0:halo, :] = jnp.zeros((halo, h_ref.shape[1]), F32)

    h_ref[halo:halo + tm, :] = jnp.dot(x_ref[...], wb_ref[...], preferred_element_type=F32)

    def conv(cols, cw_ref, cb_ref, scale):
        hh = h_ref[0:halo + tm, cols]
        cw = cw_ref[...] * scale
        acc = cw[0:1, :] * hh
        for tap in range(1, CONV_WIDTH):
            acc = cw[tap:tap + 1, :] * hh + pltpu.roll(acc, 1, 0)
        return acc[halo:, :] + cb_ref[...] * scale

    half = conv(slice(0, tf), cwg_ref, cbg_ref, 0.5)
    u = conv(slice(tf, 2 * tf), cwu_ref, cbu_ref, 1.0)
    a_ref[...] = ((half + half * jnp.tanh(half)) * u).astype(a_ref.dtype)
    h_ref[0:halo, :] = h_ref[tm:tm + halo, :]


def _ffn_up(xb, w_up, conv_w, conv_b, w_down, layer, seq):
    m, k = xb.shape
    tm, tf = FFN_UP_TM, FFN_TF
    nf = D_FF // tf
    n_tiles = m // tm
    cb2 = conv_b.reshape(conv_b.shape[0], 1, 2 * D_FF)
    wd_rows = w_down.shape[1] // (nf * n_tiles)
    assert wd_rows * nf * n_tiles == w_down.shape[1] and wd_rows % (2 * SUBLANES) == 0
    return pl.pallas_call(
        functools.partial(_ffn_up_kernel, seq // tm),
        grid=(nf, n_tiles),
        in_specs=[pl.BlockSpec((tm, k), lambda j, i: (i, 0)),
                  pl.BlockSpec((None, k, tf), lambda j, i: (layer, 0, j)),
                  pl.BlockSpec((None, k, tf), lambda j, i: (layer, 0, j + nf)),
                  pl.BlockSpec((None, CONV_WIDTH, tf), lambda j, i: (layer, 0, j)),
                  pl.BlockSpec((None, CONV_WIDTH, tf), lambda j, i: (layer, 0, j + nf)),
                  pl.BlockSpec((None, 1, tf), lambda j, i: (layer, 0, j)),
                  pl.BlockSpec((None, 1, tf), lambda j, i: (layer, 0, j + nf)),
                  pl.BlockSpec((None, wd_rows, w_down.shape[2]), lambda j, i: (layer, j * n_tiles + i, 0))],
        out_specs=[pl.BlockSpec((tm, tf), lambda j, i: (i, j)),
                   pl.BlockSpec((wd_rows, w_down.shape[2]), lambda j, i: (j * n_tiles + i, 0))],
        out_shape=[jax.ShapeDtypeStruct((m, D_FF), BF16),
                   jax.ShapeDtypeStruct(w_down.shape[1:], BF16)],
        scratch_shapes=[pltpu.VMEM((k, 2 * tf), BF16), pltpu.VMEM((tm + 2 * SUBLANES, 2 * tf), F32)],
        compiler_params=_params("arbitrary", "arbitrary"),
        name="ffn_up",
    )(xb, w_up, w_up, conv_w, conv_w, cb2, cb2, w_down)


FFN_DOWN_TM = 256


FFN_DOWN_W_CHUNKS = 11


def _ffn_down_ln_kernel(alpha, emit_bf16, a_ref, xres_ref, w_hbm, g_ref, b_ref, xo_ref, *rest):
    *maybe_xob, w_ref, sem = rest
    kc = w_ref.shape[0] // FFN_DOWN_W_CHUNKS

    def finish(y):
        out = _deepnorm(xres_ref[...], y, g_ref[...], b_ref[...], alpha)
        xo_ref[...] = out
        if emit_bf16:
            maybe_xob[0][...] = out.astype(BF16)

    @pl.when(pl.program_id(0) == 0)
    def _():
        chunks = [slice(c * kc, (c + 1) * kc) for c in range(FFN_DOWN_W_CHUNKS)]
        copies = [pltpu.make_async_copy(w_hbm.at[rows], w_ref.at[rows], sem.at[c])
                  for c, rows in enumerate(chunks)]
        for copy in copies:
            copy.start()
        y = jnp.zeros(xo_ref.shape, F32)
        for copy, rows in zip(copies, chunks):
            copy.wait()
            y = y + jnp.dot(a_ref[:, rows], w_ref[rows, :], preferred_element_type=F32)
        finish(y)

    @pl.when(pl.program_id(0) > 0)
    def _():
        finish(jnp.dot(a_ref[...], w_ref[...], preferred_element_type=F32))


def _ffn_down_ln(a, xres, w_bf16, g, b, alpha, emit_bf16, name):
    m, k = a.shape
    n = w_bf16.shape[1]
    tm = FFN_DOWN_TM
    out_specs = [pl.BlockSpec((tm, n), lambda i: (i, 0))]
    out_shape = [jax.ShapeDtypeStruct((m, n), F32)]
    if emit_bf16:
        out_specs.append(pl.BlockSpec((tm, n), lambda i: (i, 0)))
        out_shape.append(jax.ShapeDtypeStruct((m, n), BF16))
    return pl.pallas_call(
        functools.partial(_ffn_down_ln_kernel, alpha, emit_bf16),
        grid=(m // tm,),
        in_specs=[pl.BlockSpec((tm, k), lambda i: (i, 0)),
                  pl.BlockSpec((tm, n), lambda i: (i, 0)),
                  pl.BlockSpec(memory_space=pl.ANY),
                  pl.BlockSpec((1, n), lambda i: (0, 0)),
                  pl.BlockSpec((1, n), lambda i: (0, 0))],
        out_specs=out_specs,
        out_shape=out_shape,
        scratch_shapes=[pltpu.VMEM((k, n), BF16), pltpu.SemaphoreType.DMA((FFN_DOWN_W_CHUNKS,))],
        compiler_params=_params("arbitrary"),
        name=name,
    )(a, xres, w_bf16, g, b)


def kernel(x, a_w_qkv, a_w_o, a_sinks, rel_bias, b_w_qkvf, b_f_bias, b_w_o,
           ffn_w_up, ffn_conv_w, ffn_conv_b, ffn_w_down, ln_g, ln_b):
    batch, seq, d = x.shape
    depth = ffn_w_up.shape[0]
    alpha = (2 * depth) ** 0.25
    m = batch * seq
    xf = x.reshape(m, d)
    xb = None
    biasmask = _swa_bias(rel_bias)
    for layer in range(depth):
        j = layer // 2
        g = ln_g[layer].reshape(2, 1, d)
        bt = ln_b[layer].reshape(2, 1, d)
        if layer % 2 == 0:
            n_qkv = D_MODEL + 2 * A_KV_DIM
            src = xf if xb is None else xb
            qkv = _ws_matmul(src, a_w_qkv, j, n_qkv, n_qkv // SWA_QKV_COL_TILES, SWA_QKV_TM, "swa_qkv",
                             slab_batch=(batch, seq))
            o = _swa_attention(qkv, a_sinks[j], biasmask)
            w_o = a_w_o
        else:
            w_t = jnp.swapaxes(b_w_qkvf, 1, 2)
            qkv = _ws_matmul(xb, w_t, j, 3 * D_MODEL, FOX_QKV_TN, FOX_QKV_TM, "fox_qkv",
                             w_is_transposed=True)
            fb3 = jnp.tile(b_f_bias[j].reshape(N_HEADS, 1), (FOX_PIECES, 1))
            cumt, kext = _fox_gate(xb, w_t, j, fb3, batch, seq)
            o = _fox_attention(qkv.reshape(batch, seq, 3 * D_MODEL), cumt, kext)
            w_o = b_w_o
        xf, xb = _oproj_ln(o, xf, w_o, j, g[0], bt[0], alpha, OPROJ_TM, "oproj_ln")
        a, w_down = _ffn_up(xb, ffn_w_up, ffn_conv_w, ffn_conv_b, ffn_w_down, layer, seq)
        last = layer == depth - 1
        res = _ffn_down_ln(a, xf, w_down, g[1], bt[1], alpha, not last, "ffn_down_ln")
        xf, xb = (res[0], None) if last else res
    return xf.reshape(batch, seq, d)
```

```python
import functools
import math

import jax
import jax.numpy as jnp
import numpy as np
from jax import lax
from jax.experimental import pallas as pl
from jax.experimental.pallas import tpu as pltpu

D_MODEL = 2048
HEAD_DIM = 64
BLOCK = 128
N_HEADS = D_MODEL // HEAD_DIM
A_KV_HEADS = 4
A_GROUP = N_HEADS // A_KV_HEADS
A_KV_DIM = A_KV_HEADS * HEAD_DIM
REL_BUCKETS = 32
REL_MAX_DIST = 128
D_FF = 5632
LN_EPS = 1e-5
SCALE = HEAD_DIM ** -0.5

SUBLANES = 8
LANES = 128

BF16 = jnp.bfloat16
F32 = jnp.float32
NEG_INF = float("-inf")

_NT_DIMS = (((1,), (1,)), ((), ()))


def _params(*sem):
    return pltpu.CompilerParams(dimension_semantics=sem)


LOG2E = math.log2(math.e)
Q_SCALE = SCALE * LOG2E
SWA_QKV_COL_TILES = 2
SWA_QKV_TM = 1024
FOX_QKV_TN = 1024
FOX_QKV_TM = 2048
OPROJ_TM = 512


def _ws_matmul_kernel(w_is_transposed, x_ref, w_ref, o_ref, wb_ref):
    @pl.when(pl.program_id(1) == 0)
    def _():
        wb_ref[...] = w_ref[...].astype(BF16)

    x = x_ref[...].astype(BF16)
    if w_is_transposed:
        y = lax.dot_general(x, wb_ref[...], _NT_DIMS, preferred_element_type=F32)
    else:
        y = jnp.dot(x, wb_ref[...], preferred_element_type=F32)
    tn = y.shape[1]
    col = pl.program_id(0) * tn + lax.broadcasted_iota(jnp.int32, (1, tn), 1)
    y = (y * jnp.where(col < D_MODEL, Q_SCALE, 1.0)).astype(o_ref.dtype)
    if len(o_ref.shape) == 2:
        o_ref[...] = y
    else:
        for c in range(o_ref.shape[0]):
            o_ref[c] = y[:, c * LANES:(c + 1) * LANES]


def _ws_matmul(x, w, layer, n_out, tn, tm, name, w_is_transposed=False, slab_batch=None):
    m, k = x.shape
    if w_is_transposed:
        w_spec = pl.BlockSpec((None, tn, k), lambda j, i: (layer, j, 0))
        wb_shape = (tn, k)
    else:
        w_spec = pl.BlockSpec((None, k, tn), lambda j, i: (layer, 0, j))
        wb_shape = (k, tn)
    if slab_batch is None:
        out_spec = pl.BlockSpec((tm, tn), lambda j, i: (i, j))
        out_shape = jax.ShapeDtypeStruct((m, n_out), BF16)
    else:
        batch, seq = slab_batch
        per_seq = seq // tm
        out_spec = pl.BlockSpec((None, tn // LANES, tm, LANES),
                                lambda j, i: (i // per_seq, j, i % per_seq, 0))
        out_shape = jax.ShapeDtypeStruct((batch, n_out // LANES, seq, LANES), BF16)
    return pl.pallas_call(
        functools.partial(_ws_matmul_kernel, w_is_transposed),
        grid=(n_out // tn, m // tm),
        in_specs=[pl.BlockSpec((tm, k), lambda j, i: (i, 0)), w_spec],
        out_specs=out_spec,
        out_shape=out_shape,
        scratch_shapes=[pltpu.VMEM(wb_shape, BF16)],
        compiler_params=_params("arbitrary", "arbitrary"),
        name=name,
    )(x, w)


def _t5_bucket_map():
    qi = np.arange(BLOCK)[None, :]
    kj = np.arange(2 * BLOCK)[:, None]
    dist = BLOCK + qi - kj
    n = np.maximum(dist, 0)
    max_exact = REL_BUCKETS // 2
    nf = np.maximum(n, 1).astype(np.float32)
    large = max_exact + (np.log(nf / max_exact) / math.log(REL_MAX_DIST / max_exact)
                         * (REL_BUCKETS - max_exact)).astype(np.int32)
    large = np.minimum(large, REL_BUCKETS - 1)
    bucket = np.where(n < max_exact, n, large)
    visible = (dist >= 0) & (dist < BLOCK)
    return np.where(visible, bucket, -1).astype(np.int32)


def _swa_bias_kernel(rb_ref, bk_ref, o_ref):
    bk = bk_ref[...]
    key = lax.broadcasted_iota(jnp.int32, (BLOCK, BLOCK), 0)
    qry = lax.broadcasted_iota(jnp.int32, (BLOCK, BLOCK), 1)
    current = qry >= key
    masked = jnp.full((BLOCK, BLOCK), NEG_INF, F32)

    def head(h, _):
        v = jnp.zeros(bk.shape, F32)
        for b in range(REL_BUCKETS):
            v = jnp.where(bk == b, rb_ref[b, h] * LOG2E, v)
        rot = pltpu.roll(jnp.broadcast_to(v, (BLOCK, BLOCK)), 0, 1, stride=1, stride_axis=0)
        upper = jnp.where(current, rot, NEG_INF)
        o_ref[0, h, 0:BLOCK, :] = jnp.where(current, NEG_INF, rot)
        o_ref[0, h, BLOCK:, :] = upper
        o_ref[1, h, 0:BLOCK, :] = masked
        o_ref[1, h, BLOCK:, :] = upper
        return 0

    lax.fori_loop(0, N_HEADS, head, 0)


def _swa_bias(rel_bias):
    bucket_map = _t5_bucket_map()
    by_dist = bucket_map[BLOCK:BLOCK + 1, :]
    kj, qi = np.arange(2 * BLOCK)[:, None], np.arange(BLOCK)[None, :]
    dist = BLOCK + qi - kj
    toeplitz = np.where((dist >= 0) & (dist < BLOCK), by_dist[0, dist % BLOCK], -1)
    assert (toeplitz == bucket_map).all()
    bk = jnp.asarray(by_dist)
    return pl.pallas_call(
        _swa_bias_kernel,
        in_specs=[pl.BlockSpec(memory_space=pltpu.SMEM),
                  pl.BlockSpec(memory_space=pltpu.VMEM)],
        out_specs=pl.BlockSpec(memory_space=pltpu.VMEM),
        out_shape=jax.ShapeDtypeStruct((2, N_HEADS, 2 * BLOCK, BLOCK), F32),
        name="swa_bias",
    )(rel_bias, bk)


_TN_DIMS = (((0,), (0,)), ((), ()))
SWA_SCORES_AHEAD = 2
SWA_BLOCKS_PER_STEP = 4


def _swa_kernel(sink_ref, q_ref, kp_ref, kc_ref, vp_ref, vc_ref, bm_ref, o_ref,
                qs_ref, s_ref, p_ref, linv_ref):
    units = [(qb, kh) for qb in range(SWA_BLOCKS_PER_STEP) for kh in range(A_KV_HEADS)]
    first_table = jnp.where(pl.program_id(1) == 0, 1, 0)

    def head_at(head):
        slab, half = divmod(head, LANES // HEAD_DIM)
        return slab, slice(half * HEAD_DIM, (half + 1) * HEAD_DIM)

    def keys_values(prev_ref, cur_ref, qb, kh):
        slab, lanes = head_at(kh)
        if qb == 0:
            return jnp.concatenate([prev_ref[slab, :, lanes], cur_ref[slab, 0:BLOCK, lanes]], axis=0)
        return cur_ref[slab, (qb - 1) * BLOCK:(qb + 1) * BLOCK, lanes]

    def scores(u):
        qb, kh = units[u]
        for g in range(A_GROUP):
            slab, lanes = head_at(kh * A_GROUP + g)
            qs_ref[u, g * BLOCK:(g + 1) * BLOCK, :] = q_ref[slab, qb * BLOCK:(qb + 1) * BLOCK, lanes]
        k2 = keys_values(kp_ref, kc_ref, qb, kh)
        s = lax.dot_general(k2, qs_ref[u], _NT_DIMS, preferred_element_type=F32)
        for g in range(A_GROUP):
            s_ref[u, g] = s[:, g * BLOCK:(g + 1) * BLOCK]

    def softmax(u):
        qb, kh = units[u]
        table = first_table if qb == 0 else 0
        for g in range(A_GROUP):
            h = kh * A_GROUP + g
            cols = slice(g * BLOCK, (g + 1) * BLOCK)
            s = s_ref[u, g] + bm_ref[table, h]
            sink = sink_ref[h] * LOG2E
            m = jnp.maximum(jnp.max(s, axis=0, keepdims=True), sink)
            p = jnp.exp2(s - m)
            denom = jnp.sum(p, axis=0, keepdims=True) + jnp.exp2(sink - m)
            p_ref[u, g // 2, :, (g % 2) * BLOCK:(g % 2 + 1) * BLOCK] = p.astype(BF16)
            linv_ref[u, :, cols] = 1.0 / denom

    def weighted_values(u):
        qb, kh = units[u]
        v2 = keys_values(vp_ref, vc_ref, qb, kh)
        for c in range(A_GROUP // 2):
            cols = slice(2 * c * BLOCK, 2 * (c + 1) * BLOCK)
            ot = (lax.dot_general(v2, p_ref[u, c], _TN_DIMS, preferred_element_type=F32)
                  * linv_ref[u, :, cols])
            for gg in range(2):
                h = kh * A_GROUP + 2 * c + gg
                o_ref[h * HEAD_DIM:(h + 1) * HEAD_DIM, qb * BLOCK:(qb + 1) * BLOCK] = (
                    ot[:, gg * BLOCK:(gg + 1) * BLOCK].astype(o_ref.dtype))

    for u in range(min(SWA_SCORES_AHEAD, len(units))):
        scores(u)
    for u in range(len(units)):
        if u + SWA_SCORES_AHEAD < len(units):
            scores(u + SWA_SCORES_AHEAD)
        softmax(u)
        weighted_values(u)


def _swa_attention(qkv, sinks, biasmask):
    b, _, s, _ = qkv.shape
    rows = SWA_BLOCKS_PER_STEP * BLOCK
    gq = A_GROUP * BLOCK
    n_units = SWA_BLOCKS_PER_STEP * A_KV_HEADS
    q_slabs = D_MODEL // LANES
    kv_slabs = A_KV_DIM // LANES
    kcol = q_slabs // kv_slabs
    prev = lambda n: jnp.maximum(SWA_BLOCKS_PER_STEP * n - 1, 0)
    return pl.pallas_call(
        _swa_kernel,
        grid=(b, s // rows),
        in_specs=[pl.BlockSpec(memory_space=pltpu.SMEM),
                  pl.BlockSpec((None, q_slabs, rows, LANES), lambda bi, n: (bi, 0, n, 0)),
                  pl.BlockSpec((None, kv_slabs, BLOCK, LANES), lambda bi, n: (bi, kcol, prev(n), 0)),
                  pl.BlockSpec((None, kv_slabs, rows, LANES), lambda bi, n: (bi, kcol, n, 0)),
                  pl.BlockSpec((None, kv_slabs, BLOCK, LANES), lambda bi, n: (bi, kcol + 1, prev(n), 0)),
                  pl.BlockSpec((None, kv_slabs, rows, LANES), lambda bi, n: (bi, kcol + 1, n, 0)),
                  pl.BlockSpec((2, N_HEADS, 2 * BLOCK, BLOCK), lambda bi, n: (0, 0, 0, 0),
                               pipeline_mode=pl.Buffered(1))],
        out_specs=pl.BlockSpec((None, D_MODEL, rows), lambda bi, n: (bi, 0, n)),
        out_shape=jax.ShapeDtypeStruct((b, D_MODEL, s), BF16),
        scratch_shapes=[pltpu.VMEM((n_units, gq, HEAD_DIM), BF16),
                        pltpu.VMEM((n_units, A_GROUP, 2 * BLOCK, BLOCK), F32),
                        pltpu.VMEM((n_units, A_GROUP // 2, 2 * BLOCK, 2 * BLOCK), BF16),
                        pltpu.VMEM((n_units, 1, gq), F32)],
        compiler_params=_params("arbitrary", "arbitrary"),
        name="swa_attn",
    )(sinks, qkv, qkv, qkv, qkv, qkv, biasmask)


FOX_CUM_BLOCK = 256
FOX_TQ = 256
FOX_HEADS_PER_PAIR = LANES // HEAD_DIM
FOX_PAIRS_PER_STEP = 2
FOX_PIECES = 3
FOX_SUM_ROWS = 16
FOX_SCORE_SLOTS = 4
FOX_PV_LAG = 1
FOX_PROB_SLOTS = FOX_PV_LAG + 1
BF16_ONE_PAIR = 0x3F803F80


def _fox_ext_lane(head, piece):
    pair, odd = divmod(head, FOX_HEADS_PER_PAIR)
    return pair * LANES + (1 - odd) * HEAD_DIM + piece


def _fox_placement():
    pm = np.zeros((FOX_PIECES * N_HEADS, D_MODEL), np.float32)
    for t in range(FOX_PIECES):
        for h in range(N_HEADS):
            pm[t * N_HEADS + h, _fox_ext_lane(h, t)] = -1.0
    return pm


def _fox_lane_words():
    w = np.zeros((SUBLANES, LANES), np.uint32)
    for hh in range(FOX_HEADS_PER_PAIR):
        own = np.zeros(LANES, bool)
        own[hh * HEAD_DIM:(hh + 1) * HEAD_DIM] = True
        ext = np.zeros(LANES, bool)
        first = _fox_ext_lane(hh, 0)
        ext[first:first + FOX_PIECES] = True
        w[3 * hh + 0] = np.where(own, 0xFFFFFFFF, 0)
        w[3 * hh + 1] = np.where(ext, 0xFFFFFFFF, 0)
        w[3 * hh + 2] = np.where(ext, BF16_ONE_PAIR, 0)
    return w


def _log_sigmoid(x):
    return jnp.minimum(x, 0.0) - jnp.log1p(jnp.exp(-jnp.abs(x)))


def _round_bf16(x):
    return x.astype(BF16).astype(F32)


def _fox_gate_kernel(x_ref, wf_ref, fb_ref, pm_ref, cumt_ref, kext_ref, w3_ref):
    @pl.when(pl.program_id(0) == 0)
    def _():
        w = wf_ref[...].astype(BF16)
        for t in range(FOX_PIECES):
            w3_ref[t * N_HEADS:(t + 1) * N_HEADS, :] = w

    fz = lax.dot_general(w3_ref[...], x_ref[...], _NT_DIMS, preferred_element_type=F32) + fb_ref[...]
    lf = _log_sigmoid(fz)
    cb = FOX_CUM_BLOCK
    row = lax.broadcasted_iota(jnp.int32, (cb, cb), 0)
    col = lax.broadcasted_iota(jnp.int32, (cb, cb), 1)
    tri = jnp.where(row <= col, 1.0, 0.0).astype(BF16)
    tri3 = jnp.concatenate([tri] * FOX_PIECES, axis=0)
    group = lax.broadcasted_iota(jnp.int32, (lf.shape[0], cb), 0) // N_HEADS
    pm = pm_ref[...]
    n_blocks = lf.shape[1] // cb
    local = []
    for blk in range(n_blocks):
        x = lf[:, blk * cb:(blk + 1) * cb]
        hi = _round_bf16(x)
        r1 = x - hi
        mid = _round_bf16(r1)
        pieces = jnp.concatenate([hi, mid, r1 - mid], axis=1).astype(BF16)
        local.append(jnp.dot(pieces, tri3, preferred_element_type=F32))
    carry = jnp.zeros((lf.shape[0], 1), F32)
    for blk in range(n_blocks):
        cols = slice(blk * cb, (blk + 1) * cb)
        c = local[blk] + carry
        carry = carry + local[blk][:, cb - 1:cb]
        c2 = c * LOG2E
        cumt_ref[:, cols] = c2[:N_HEADS]
        chi = _round_bf16(c2)
        cr1 = c2 - chi
        cmid = _round_bf16(cr1)
        piece = jnp.where(group == 0, chi, jnp.where(group == 1, cmid, cr1 - cmid)).astype(BF16)
        kext_ref[cols, :] = lax.dot_general(piece, pm, _TN_DIMS, preferred_element_type=F32).astype(BF16)


def _fox_gate(xb, w_qkvf_t, layer, fb3, batch, seq):
    pm = jnp.asarray(_fox_placement(), BF16)
    n3 = FOX_PIECES * N_HEADS
    gate_row = 3 * D_MODEL // N_HEADS
    return pl.pallas_call(
        _fox_gate_kernel,
        grid=(batch,),
        in_specs=[pl.BlockSpec((seq, D_MODEL), lambda bi: (bi, 0)),
                  pl.BlockSpec((None, N_HEADS, D_MODEL), lambda bi: (layer, gate_row, 0)),
                  pl.BlockSpec((n3, 1), lambda bi: (0, 0)),
                  pl.BlockSpec((n3, D_MODEL), lambda bi: (0, 0))],
        out_specs=[pl.BlockSpec((None, N_HEADS, seq), lambda bi: (bi, 0, 0)),
                   pl.BlockSpec((None, seq, D_MODEL), lambda bi: (bi, 0, 0))],
        out_shape=[jax.ShapeDtypeStruct((batch, N_HEADS, seq), F32),
                   jax.ShapeDtypeStruct((batch, seq, D_MODEL), BF16)],
        scratch_shapes=[pltpu.VMEM((n3, D_MODEL), BF16)],
        compiler_params=_params("arbitrary"),
        name="fox_gate",
    )(xb, w_qkvf_t, fb3, pm)


def _fox_kernel(q_ref, k_ref, v_ref, kext_ref, cbt_ref, lw_ref, o_ref, qa_ref, ka_ref, vt_ref,
                s_ref, p_ref):
    seq = q_ref.shape[0]
    tq = FOX_TQ
    key = lax.broadcasted_iota(jnp.int32, (tq, tq), 0)
    qry = lax.broadcasted_iota(jnp.int32, (tq, tq), 1)
    causal = key <= qry
    words = lambda x: pltpu.bitcast(x, jnp.uint32)
    for pair in range(FOX_PAIRS_PER_STEP):
        lanes = slice(pair * LANES, (pair + 1) * LANES)
        q = words(q_ref[:, lanes])
        k = words(k_ref[:, lanes])
        kext = words(kext_ref[:, lanes])
        vt = v_ref[:, lanes].T
        for hh in range(FOX_HEADS_PER_PAIR):
            h = pair * FOX_HEADS_PER_PAIR + hh
            own = lw_ref[3 * hh + 0:3 * hh + 1, :]
            ext = lw_ref[3 * hh + 1:3 * hh + 2, :]
            ones = lw_ref[3 * hh + 2:3 * hh + 3, :]
            qa_ref[h] = pltpu.bitcast((q & own) | ones, BF16)
            ka_ref[h] = pltpu.bitcast((k & ~ext) | (kext & ext), BF16)
            vt_ref[h, 0:HEAD_DIM, :] = vt[hh * HEAD_DIM:(hh + 1) * HEAD_DIM]
            vt_ref[h, HEAD_DIM:, :] = jnp.ones((FOX_SUM_ROWS, seq), BF16)

    heads = FOX_PAIRS_PER_STEP * FOX_HEADS_PER_PAIR
    units = [(i, h) for i in range(seq // tq) for h in range(heads)]

    def scores(u):
        i, h = units[u]
        slot = u % FOX_SCORE_SLOTS
        r0, r1 = i * tq, (i + 1) * tq
        qa = qa_ref[h, r0:r1, :]
        s_d = lax.dot_general(ka_ref[h, r0:r1, :], qa, _NT_DIMS, preferred_element_type=F32)
        s_ref[slot, r0:r1, :] = jnp.where(causal, s_d, NEG_INF)
        if i > 0:
            s_ref[slot, 0:r0, :] = lax.dot_general(ka_ref[h, 0:r0, :], qa, _NT_DIMS,
                                                   preferred_element_type=F32)

    def softmax(u):
        i, h = units[u]
        slot = u % FOX_SCORE_SLOTS
        r0, r1 = i * tq, (i + 1) * tq
        head = pl.program_id(1) * heads + h
        cbq = cbt_ref[pl.ds(head, 1), r0:r1]
        m = jnp.max(s_ref[slot, 0:r1, :], axis=0, keepdims=True)
        shift = cbq - (m + cbq)
        p_ref[u % FOX_PROB_SLOTS, 0:r1, :] = jnp.exp2(s_ref[slot, 0:r1, :] + shift).astype(BF16)

    def weighted_values(u):
        i, h = units[u]
        r0, r1 = i * tq, (i + 1) * tq
        ot = jnp.dot(vt_ref[h, :, 0:r1], p_ref[u % FOX_PROB_SLOTS, 0:r1, :], preferred_element_type=F32)
        out = ot[0:HEAD_DIM] * (1.0 / ot[HEAD_DIM:HEAD_DIM + 1])
        o_ref[h * HEAD_DIM:(h + 1) * HEAD_DIM, r0:r1] = out.astype(o_ref.dtype)

    ahead = FOX_SCORE_SLOTS - 1
    for u in range(min(ahead, len(units))):
        scores(u)
    for u in range(len(units)):
        if u + ahead < len(units):
            scores(u + ahead)
        softmax(u)
        if u >= FOX_PV_LAG:
            weighted_values(u - FOX_PV_LAG)
    for u in range(len(units) - FOX_PV_LAG, len(units)):
        weighted_values(u)


def _fox_attention(qkv, cumt, kext):
    b, s, _ = qkv.shape
    heads = FOX_PAIRS_PER_STEP * FOX_HEADS_PER_PAIR
    width = FOX_PAIRS_PER_STEP * LANES
    lw = jnp.asarray(_fox_lane_words())
    ncol = D_MODEL // width
    return pl.pallas_call(
        _fox_kernel,
        grid=(b, N_HEADS // heads),
        in_specs=[pl.BlockSpec((None, s, width), lambda bi, hp: (bi, 0, hp)),
                  pl.BlockSpec((None, s, width), lambda bi, hp: (bi, 0, ncol + hp)),
                  pl.BlockSpec((None, s, width), lambda bi, hp: (bi, 0, 2 * ncol + hp)),
                  pl.BlockSpec((None, s, width), lambda bi, hp: (bi, 0, hp)),
                  pl.BlockSpec((None, N_HEADS, s), lambda bi, hp: (bi, 0, 0)),
                  pl.BlockSpec((SUBLANES, LANES), lambda bi, hp: (0, 0))],
        out_specs=pl.BlockSpec((None, heads * HEAD_DIM, s), lambda bi, hp: (bi, hp, 0)),
        out_shape=jax.ShapeDtypeStruct((b, D_MODEL, s), BF16),
        scratch_shapes=[pltpu.VMEM((heads, s, LANES), BF16), pltpu.VMEM((heads, s, LANES), BF16),
                        pltpu.VMEM((heads, HEAD_DIM + FOX_SUM_ROWS, s), BF16),
                        pltpu.VMEM((FOX_SCORE_SLOTS, s, FOX_TQ), F32),
                        pltpu.VMEM((FOX_PROB_SLOTS, s, FOX_TQ), BF16)],
        compiler_params=_params("arbitrary", "arbitrary"),
        name="fox_attn",
    )(qkv, qkv, qkv, kext, cumt, lw)


def _deepnorm(xres, y, g, b, alpha):
    z = alpha * xres + y
    mu = jnp.mean(z, axis=-1, keepdims=True)
    zc = z - mu
    var = jnp.mean(zc * zc, axis=-1, keepdims=True)
    return zc * lax.rsqrt(var + LN_EPS) * g + b


LN_SUBTILES = 2


OPROJ_W_CHUNKS = 8
OPROJ_W_SLOTS = 4


def _oproj_ln_kernel(alpha, layer, ot_ref, xres_ref, w_hbm, g_ref, b_ref, xo_ref, xob_ref,
                     wf_ref, wb_ref, sem):
    def finish(rows, y):
        out = _deepnorm(xres_ref[rows, :], y, g_ref[...], b_ref[...], alpha)
        xo_ref[rows, :] = out
        xob_ref[rows, :] = out.astype(BF16)

    @pl.when(pl.program_id(0) == 0)
    def _():
        slots, kc = wf_ref.shape[0], wf_ref.shape[1]
        chunks = [slice(c * kc, (c + 1) * kc) for c in range(wb_ref.shape[0] // kc)]
        copies = [pltpu.make_async_copy(w_hbm.at[layer, rows], wf_ref.at[c % slots], sem.at[c % slots])
                  for c, rows in enumerate(chunks)]
        for copy in copies[:slots]:
            copy.start()
        y = jnp.zeros(xo_ref.shape, F32)
        for c, rows in enumerate(chunks):
            copies[c].wait()
            wb_ref[rows, :] = wf_ref[c % slots].astype(BF16)
            if c + slots < len(chunks):
                copies[c + slots].start()
            y = y + lax.dot_general(ot_ref[rows, :], wb_ref[rows, :], _TN_DIMS,
                                    preferred_element_type=F32)
        finish(slice(None), y)

    @pl.when(pl.program_id(0) > 0)
    def _():
        ts = xres_ref.shape[0] // LN_SUBTILES
        ys = [lax.dot_general(ot_ref[:, r * ts:(r + 1) * ts], wb_ref[...], _TN_DIMS,
                              preferred_element_type=F32) for r in range(LN_SUBTILES)]
        for r, y in enumerate(ys):
            finish(slice(r * ts, (r + 1) * ts), y)


def _oproj_ln(ot, xres, w, layer, g, b, alpha, tm, name):
    m, n = xres.shape
    k = w.shape[1]
    per_seq = ot.shape[2] // tm
    return pl.pallas_call(
        functools.partial(_oproj_ln_kernel, alpha, layer),
        grid=(m // tm,),
        in_specs=[pl.BlockSpec((None, k, tm), lambda i: (i // per_seq, 0, i % per_seq)),
                  pl.BlockSpec((tm, n), lambda i: (i, 0)),
                  pl.BlockSpec(memory_space=pl.ANY),
                  pl.BlockSpec((1, n), lambda i: (0, 0)),
                  pl.BlockSpec((1, n), lambda i: (0, 0))],
        out_specs=[pl.BlockSpec((tm, n), lambda i: (i, 0)),
                   pl.BlockSpec((tm, n), lambda i: (i, 0))],
        out_shape=[jax.ShapeDtypeStruct((m, n), F32), jax.ShapeDtypeStruct((m, n), BF16)],
        scratch_shapes=[pltpu.VMEM((OPROJ_W_SLOTS, k // OPROJ_W_CHUNKS, n), F32),
                        pltpu.VMEM((k, n), BF16), pltpu.SemaphoreType.DMA((OPROJ_W_SLOTS,))],
        compiler_params=_params("arbitrary"),
        name=name,
    )(ot, xres, w, g, b)


CONV_WIDTH = 3
FFN_TF = 512
FFN_UP_TM = 1024


def _ffn_up_kernel(tiles_per_seq, x_ref, wg_ref, wu_ref, cwg_ref, cwu_ref, cbg_ref, cbu_ref, wd_ref,
                   a_ref, wdb_ref, wb_ref, h_ref):
    wdb_ref[...] = wd_ref[...].astype(BF16)
    i = pl.program_id(1)
    tm = x_ref.shape[0]
    tf = a_ref.shape[1]
    halo = SUBLANES

    @pl.when(i == 0)
    def _():
        wb_ref[:, 0:tf] = wg_ref[...].astype(BF16)
        wb_ref[:, tf:2 * tf] = wu_ref[...].astype(BF16)

    @pl.when(i % tiles_per_seq == 0)
    def _():
        h_ref[0:halo, :] = jnp.zeros((halo, h_ref.shape[1]), F32)

    h_ref[halo:halo + tm, :] = jnp.dot(x_ref[...], wb_ref[...], preferred_element_type=F32)

    def conv(cols, cw_ref, cb_ref, scale):
        hh = h_ref[0:halo + tm, cols]
        cw = cw_ref[...] * scale
        acc = cw[0:1, :] * hh
        for tap in range(1, CONV_WIDTH):
            acc = cw[tap:tap + 1, :] * hh + pltpu.roll(acc, 1, 0)
        return acc[halo:, :] + cb_ref[...] * scale

    half = conv(slice(0, tf), cwg_ref, cbg_ref, 0.5)
    u = conv(slice(tf, 2 * tf), cwu_ref, cbu_ref, 1.0)
    a_ref[...] = ((half + half * jnp.tanh(half)) * u).astype(a_ref.dtype)
    h_ref[0:halo, :] = h_ref[tm:tm + halo, :]


def _ffn_up(xb, w_up, conv_w, conv_b, w_down, layer, seq):
    m, k = xb.shape
    tm, tf = FFN_UP_TM, FFN_TF
    nf = D_FF // tf
    n_tiles = m // tm
    cb2 = conv_b.reshape(conv_b.shape[0], 1, 2 * D_FF)
    wd_rows = w_down.shape[1] // (nf * n_tiles)
    assert wd_rows * nf * n_tiles == w_down.shape[1] and wd_rows % (2 * SUBLANES) == 0
    return pl.pallas_call(
        functools.partial(_ffn_up_kernel, seq // tm),
        grid=(nf, n_tiles),
        in_specs=[pl.BlockSpec((tm, k), lambda j, i: (i, 0)),
                  pl.BlockSpec((None, k, tf), lambda j, i: (layer, 0, j)),
                  pl.BlockSpec((None, k, tf), lambda j, i: (layer, 0, j + nf)),
                  pl.BlockSpec((None, CONV_WIDTH, tf), lambda j, i: (layer, 0, j)),
                  pl.BlockSpec((None, CONV_WIDTH, tf), lambda j, i: (layer, 0, j + nf)),
                  pl.BlockSpec((None, 1, tf), lambda j, i: (layer, 0, j)),
                  pl.BlockSpec((None, 1, tf), lambda j, i: (layer, 0, j + nf)),
                  pl.BlockSpec((None, wd_rows, w_down.shape[2]), lambda j, i: (layer, j * n_tiles + i, 0))],
        out_specs=[pl.BlockSpec((tm, tf), lambda j, i: (i, j)),
                   pl.BlockSpec((wd_rows, w_down.shape[2]), lambda j, i: (j * n_tiles + i, 0))],
        out_shape=[jax.ShapeDtypeStruct((m, D_FF), BF16),
                   jax.ShapeDtypeStruct(w_down.shape[1:], BF16)],
        scratch_shapes=[pltpu.VMEM((k, 2 * tf), BF16), pltpu.VMEM((tm + 2 * SUBLANES, 2 * tf), F32)],
        compiler_params=_params("arbitrary", "arbitrary"),
        name="ffn_up",
    )(xb, w_up, w_up, conv_w, conv_w, cb2, cb2, w_down)


FFN_DOWN_TM = 256


FFN_DOWN_W_CHUNKS = 11


def _ffn_down_ln_kernel(alpha, emit_bf16, a_ref, xres_ref, w_hbm, g_ref, b_ref, xo_ref, *rest):
    *maybe_xob, w_ref, sem = rest
    kc = w_ref.shape[0] // FFN_DOWN_W_CHUNKS

    def finish(y):
        out = _deepnorm(xres_ref[...], y, g_ref[...], b_ref[...], alpha)
        xo_ref[...] = out
        if emit_bf16:
            maybe_xob[0][...] = out.astype(BF16)

    @pl.when(pl.program_id(0) == 0)
    def _():
        chunks = [slice(c * kc, (c + 1) * kc) for c in range(FFN_DOWN_W_CHUNKS)]
        copies = [pltpu.make_async_copy(w_hbm.at[rows], w_ref.at[rows], sem.at[c])
                  for c, rows in enumerate(chunks)]
        for copy in copies:
            copy.start()
        y = jnp.zeros(xo_ref.shape, F32)
        for copy, rows in zip(copies, chunks):
            copy.wait()
            y = y + jnp.dot(a_ref[:, rows], w_ref[rows, :], preferred_element_type=F32)
        finish(y)

    @pl.when(pl.program_id(0) > 0)
    def _():
        finish(jnp.dot(a_ref[...], w_ref[...], preferred_element_type=F32))


def _ffn_down_ln(a, xres, w_bf16, g, b, alpha, emit_bf16, name):
    m, k = a.shape
    n = w_bf16.shape[1]
    tm = FFN_DOWN_TM
    out_specs = [pl.BlockSpec((tm, n), lambda i: (i, 0))]
    out_shape = [jax.ShapeDtypeStruct((m, n), F32)]
    if emit_bf16:
        out_specs.append(pl.BlockSpec((tm, n), lambda i: (i, 0)))
        out_shape.append(jax.ShapeDtypeStruct((m, n), BF16))
    return pl.pallas_call(
        functools.partial(_ffn_down_ln_kernel, alpha, emit_bf16),
        grid=(m // tm,),
        in_specs=[pl.BlockSpec((tm, k), lambda i: (i, 0)),
                  pl.BlockSpec((tm, n), lambda i: (i, 0)),
                  pl.BlockSpec(memory_space=pl.ANY),
                  pl.BlockSpec((1, n), lambda i: (0, 0)),
                  pl.BlockSpec((1, n), lambda i: (0, 0))],
        out_specs=out_specs,
        out_shape=out_shape,
        scratch_shapes=[pltpu.VMEM((k, n), BF16), pltpu.SemaphoreType.DMA((FFN_DOWN_W_CHUNKS,))],
        compiler_params=_params("arbitrary"),
        name=name,
    )(a, xres, w_bf16, g, b)


def kernel(x, a_w_qkv, a_w_o, a_sinks, rel_bias, b_w_qkvf, b_f_bias, b_w_o,
           ffn_w_up, ffn_conv_w, ffn_conv_b, ffn_w_down, ln_g, ln_b):
    batch, seq, d = x.shape
    depth = ffn_w_up.shape[0]
    alpha = (2 * depth) ** 0.25
    m = batch * seq
    xf = x.reshape(m, d)
    xb = None
    biasmask = _swa_bias(rel_bias)
    for layer in range(depth):
        j = layer // 2
        g = ln_g[layer].reshape(2, 1, d)
        bt = ln_b[layer].reshape(2, 1, d)
        if layer % 2 == 0:
            n_qkv = D_MODEL + 2 * A_KV_DIM
            src = xf if xb is None else xb
            qkv = _ws_matmul(src, a_w_qkv, j, n_qkv, n_qkv // SWA_QKV_COL_TILES, SWA_QKV_TM, "swa_qkv",
                             slab_batch=(batch, seq))
            o = _swa_attention(qkv, a_sinks[j], biasmask)
            w_o = a_w_o
        else:
            w_t = jnp.swapaxes(b_w_qkvf, 1, 2)
            qkv = _ws_matmul(xb, w_t, j, 3 * D_MODEL, FOX_QKV_TN, FOX_QKV_TM, "fox_qkv",
                             w_is_transposed=True)
            fb3 = jnp.tile(b_f_bias[j].reshape(N_HEADS, 1), (FOX_PIECES, 1))
            cumt, kext = _fox_gate(xb, w_t, j, fb3, batch, seq)
            o = _fox_attention(qkv.reshape(batch, seq, 3 * D_MODEL), cumt, kext)
            w_o = b_w_o
        xf, xb = _oproj_ln(o, xf, w_o, j, g[0], bt[0], alpha, OPROJ_TM, "oproj_ln")
        a, w_down = _ffn_up(xb, ffn_w_up, ffn_conv_w, ffn_conv_b, ffn_w_down, layer, seq)
        last = layer == depth - 1
        res = _ffn_down_ln(a, xf, w_down, g[1], bt[1], alpha, not last, "ffn_down_ln")
        xf, xb = (res[0], None) if last else res
    return xf.reshape(batch, seq, d)
```

```python
import functools
import math

import jax
import jax.numpy as jnp
import numpy as np
from jax import lax
from jax.experimental import pallas as pl
from jax.experimental.pallas import tpu as pltpu

D_MODEL = 2048
HEAD_DIM = 64
BLOCK = 128
N_HEADS = D_MODEL // HEAD_DIM
A_KV_HEADS = 4
A_GROUP = N_HEADS // A_KV_HEADS
A_KV_DIM = A_KV_HEADS * HEAD_DIM
REL_BUCKETS = 32
REL_MAX_DIST = 128
D_FF = 5632
LN_EPS = 1e-5
SCALE = HEAD_DIM ** -0.5

SUBLANES = 8
LANES = 128

BF16 = jnp.bfloat16
F32 = jnp.float32
NEG_INF = float("-inf")

_NT_DIMS = (((1,), (1,)), ((), ()))


def _params(*sem):
    return pltpu.CompilerParams(dimension_semantics=sem)


LOG2E = math.log2(math.e)
Q_SCALE = SCALE * LOG2E
SWA_QKV_COL_TILES = 2
SWA_QKV_TM = 1024
FOX_QKV_TN = 1024
FOX_QKV_TM = 2048
OPROJ_TM = 512


def _ws_matmul_kernel(w_is_transposed, x_ref, w_ref, o_ref, wb_ref):
    @pl.when(pl.program_id(1) == 0)
    def _():
        wb_ref[...] = w_ref[...].astype(BF16)

    x = x_ref[...].astype(BF16)
    if w_is_transposed:
        y = lax.dot_general(x, wb_ref[...], _NT_DIMS, preferred_element_type=F32)
    else:
        y = jnp.dot(x, wb_ref[...], preferred_element_type=F32)
    tn = y.shape[1]
    col = pl.program_id(0) * tn + lax.broadcasted_iota(jnp.int32, (1, tn), 1)
    y = (y * jnp.where(col < D_MODEL, Q_SCALE, 1.0)).astype(o_ref.dtype)
    if len(o_ref.shape) == 2:
        o_ref[...] = y
    else:
        for c in range(o_ref.shape[0]):
            o_ref[c] = y[:, c * LANES:(c + 1) * LANES]


def _ws_matmul(x, w, layer, n_out, tn, tm, name, w_is_transposed=False, slab_batch=None):
    m, k = x.shape
    if w_is_transposed:
        w_spec = pl.BlockSpec((None, tn, k), lambda j, i: (layer, j, 0))
        wb_shape = (tn, k)
    else:
        w_spec = pl.BlockSpec((None, k, tn), lambda j, i: (layer, 0, j))
        wb_shape = (k, tn)
    if slab_batch is None:
        out_spec = pl.BlockSpec((tm, tn), lambda j, i: (i, j))
        out_shape = jax.ShapeDtypeStruct((m, n_out), BF16)
    else:
        batch, seq = slab_batch
        per_seq = seq // tm
        out_spec = pl.BlockSpec((None, tn // LANES, tm, LANES),
                                lambda j, i: (i // per_seq, j, i % per_seq, 0))
        out_shape = jax.ShapeDtypeStruct((batch, n_out // LANES, seq, LANES), BF16)
    return pl.pallas_call(
        functools.partial(_ws_matmul_kernel, w_is_transposed),
        grid=(n_out // tn, m // tm),
        in_specs=[pl.BlockSpec((tm, k), lambda j, i: (i, 0)), w_spec],
        out_specs=out_spec,
        out_shape=out_shape,
        scratch_shapes=[pltpu.VMEM(wb_shape, BF16)],
        compiler_params=_params("arbitrary", "arbitrary"),
        name=name,
    )(x, w)


def _t5_bucket_map():
    qi = np.arange(BLOCK)[None, :]
    kj = np.arange(2 * BLOCK)[:, None]
    dist = BLOCK + qi - kj
    n = np.maximum(dist, 0)
    max_exact = REL_BUCKETS // 2
    nf = np.maximum(n, 1).astype(np.float32)
    large = max_exact + (np.log(nf / max_exact) / math.log(REL_MAX_DIST / max_exact)
                         * (REL_BUCKETS - max_exact)).astype(np.int32)
    large = np.minimum(large, REL_BUCKETS - 1)
    bucket = np.where(n < max_exact, n, large)
    visible = (dist >= 0) & (dist < BLOCK)
    return np.where(visible, bucket, -1).astype(np.int32)


def _swa_bias_kernel(rb_ref, bk_ref, o_ref):
    bk = bk_ref[...]
    key = lax.broadcasted_iota(jnp.int32, (BLOCK, BLOCK), 0)
    qry = lax.broadcasted_iota(jnp.int32, (BLOCK, BLOCK), 1)
    current = qry >= key
    masked = jnp.full((BLOCK, BLOCK), NEG_INF, F32)

    def head(h, _):
        v = jnp.zeros(bk.shape, F32)
        for b in range(REL_BUCKETS):
            v = jnp.where(bk == b, rb_ref[b, h] * LOG2E, v)
        rot = pltpu.roll(jnp.broadcast_to(v, (BLOCK, BLOCK)), 0, 1, stride=1, stride_axis=0)
        upper = jnp.where(current, rot, NEG_INF)
        o_ref[0, h, 0:BLOCK, :] = jnp.where(current, NEG_INF, rot)
        o_ref[0, h, BLOCK:, :] = upper
        o_ref[1, h, 0:BLOCK, :] = masked
        o_ref[1, h, BLOCK:, :] = upper
        return 0

    lax.fori_loop(0, N_HEADS, head, 0)


def _swa_bias(rel_bias):
    bucket_map = _t5_bucket_map()
    by_dist = bucket_map[BLOCK:BLOCK + 1, :]
    kj, qi = np.arange(2 * BLOCK)[:, None], np.arange(BLOCK)[None, :]
    dist = BLOCK + qi - kj
    toeplitz = np.where((dist >= 0) & (dist < BLOCK), by_dist[0, dist % BLOCK], -1)
    assert (toeplitz == bucket_map).all()
    bk = jnp.asarray(by_dist)
    return pl.pallas_call(
        _swa_bias_kernel,
        in_specs=[pl.BlockSpec(memory_space=pltpu.SMEM),
                  pl.BlockSpec(memory_space=pltpu.VMEM)],
        out_specs=pl.BlockSpec(memory_space=pltpu.VMEM),
        out_shape=jax.ShapeDtypeStruct((2, N_HEADS, 2 * BLOCK, BLOCK), F32),
        name="swa_bias",
    )(rel_bias, bk)


_TN_DIMS = (((0,), (0,)), ((), ()))
SWA_SCORES_AHEAD = 2
SWA_BLOCKS_PER_STEP = 4


def _swa_kernel(sink_ref, q_ref, kp_ref, kc_ref, vp_ref, vc_ref, bm_ref, o_ref,
                qs_ref, s_ref, p_ref, linv_ref):
    units = [(qb, kh) for qb in range(SWA_BLOCKS_PER_STEP) for kh in range(A_KV_HEADS)]
    first_table = jnp.where(pl.program_id(1) == 0, 1, 0)

    def head_at(head):
        slab, half = divmod(head, LANES // HEAD_DIM)
        return slab, slice(half * HEAD_DIM, (half + 1) * HEAD_DIM)

    def keys_values(prev_ref, cur_ref, qb, kh):
        slab, lanes = head_at(kh)
        if qb == 0:
            return jnp.concatenate([prev_ref[slab, :, lanes], cur_ref[slab, 0:BLOCK, lanes]], axis=0)
        return cur_ref[slab, (qb - 1) * BLOCK:(qb + 1) * BLOCK, lanes]

    def scores(u):
        qb, kh = units[u]
        for g in range(A_GROUP):
            slab, lanes = head_at(kh * A_GROUP + g)
            qs_ref[u, g * BLOCK:(g + 1) * BLOCK, :] = q_ref[slab, qb * BLOCK:(qb + 1) * BLOCK, lanes]
        k2 = keys_values(kp_ref, kc_ref, qb, kh)
        s = lax.dot_general(k2, qs_ref[u], _NT_DIMS, preferred_element_type=F32)
        for g in range(A_GROUP):
            s_ref[u, g] = s[:, g * BLOCK:(g + 1) * BLOCK]

    def softmax(u):
        qb, kh = units[u]
        table = first_table if qb == 0 else 0
        for g in range(A_GROUP):
            h = kh * A_GROUP + g
            cols = slice(g * BLOCK, (g + 1) * BLOCK)
            s = s_ref[u, g] + bm_ref[table, h]
            sink = sink_ref[h] * LOG2E
            m = jnp.maximum(jnp.max(s, axis=0, keepdims=True), sink)
            p = jnp.exp2(s - m)
            denom = jnp.sum(p, axis=0, keepdims=True) + jnp.exp2(sink - m)
            p_ref[u, g // 2, :, (g % 2) * BLOCK:(g % 2 + 1) * BLOCK] = p.astype(BF16)
            linv_ref[u, :, cols] = 1.0 / denom

    def weighted_values(u):
        qb, kh = units[u]
        v2 = keys_values(vp_ref, vc_ref, qb, kh)
        for c in range(A_GROUP // 2):
            cols = slice(2 * c * BLOCK, 2 * (c + 1) * BLOCK)
            ot = (lax.dot_general(v2, p_ref[u, c], _TN_DIMS, preferred_element_type=F32)
                  * linv_ref[u, :, cols])
            for gg in range(2):
                h = kh * A_GROUP + 2 * c + gg
                o_ref[h * HEAD_DIM:(h + 1) * HEAD_DIM, qb * BLOCK:(qb + 1) * BLOCK] = (
                    ot[:, gg * BLOCK:(gg + 1) * BLOCK].astype(o_ref.dtype))

    for u in range(min(SWA_SCORES_AHEAD, len(units))):
        scores(u)
    for u in range(len(units)):
        if u + SWA_SCORES_AHEAD < len(units):
            scores(u + SWA_SCORES_AHEAD)
        softmax(u)
        weighted_values(u)


def _swa_attention(qkv, sinks, biasmask):
    b, _, s, _ = qkv.shape
    rows = SWA_BLOCKS_PER_STEP * BLOCK
    gq = A_GROUP * BLOCK
    n_units = SWA_BLOCKS_PER_STEP * A_KV_HEADS
    q_slabs = D_MODEL // LANES
    kv_slabs = A_KV_DIM // LANES
    kcol = q_slabs // kv_slabs
    prev = lambda n: jnp.maximum(SWA_BLOCKS_PER_STEP * n - 1, 0)
    return pl.pallas_call(
        _swa_kernel,
        grid=(b, s // rows),
        in_specs=[pl.BlockSpec(memory_space=pltpu.SMEM),
                  pl.BlockSpec((None, q_slabs, rows, LANES), lambda bi, n: (bi, 0, n, 0)),
                  pl.BlockSpec((None, kv_slabs, BLOCK, LANES), lambda bi, n: (bi, kcol, prev(n), 0)),
                  pl.BlockSpec((None, kv_slabs, rows, LANES), lambda bi, n: (bi, kcol, n, 0)),
                  pl.BlockSpec((None, kv_slabs, BLOCK, LANES), lambda bi, n: (bi, kcol + 1, prev(n), 0)),
                  pl.BlockSpec((None, kv_slabs, rows, LANES), lambda bi, n: (bi, kcol + 1, n, 0)),
                  pl.BlockSpec((2, N_HEADS, 2 * BLOCK, BLOCK), lambda bi, n: (0, 0, 0, 0),
                               pipeline_mode=pl.Buffered(1))],
        out_specs=pl.BlockSpec((None, D_MODEL, rows), lambda bi, n: (bi, 0, n)),
        out_shape=jax.ShapeDtypeStruct((b, D_MODEL, s), BF16),
        scratch_shapes=[pltpu.VMEM((n_units, gq, HEAD_DIM), BF16),
                        pltpu.VMEM((n_units, A_GROUP, 2 * BLOCK, BLOCK), F32),
                        pltpu.VMEM((n_units, A_GROUP // 2, 2 * BLOCK, 2 * BLOCK), BF16),
                        pltpu.VMEM((n_units, 1, gq), F32)],
        compiler_params=_params("arbitrary", "arbitrary"),
        name="swa_attn",
    )(sinks, qkv, qkv, qkv, qkv, qkv, biasmask)


FOX_CUM_BLOCK = 256
FOX_TQ = 256
FOX_HEADS_PER_PAIR = LANES // HEAD_DIM
FOX_PAIRS_PER_STEP = 2
FOX_PIECES = 3
FOX_SUM_ROWS = 16
FOX_SCORE_SLOTS = 4
FOX_PV_LAG = 1
FOX_PROB_SLOTS = FOX_PV_LAG + 1
BF16_ONE_PAIR = 0x3F803F80


def _fox_ext_lane(head, piece):
    pair, odd = divmod(head, FOX_HEADS_PER_PAIR)
    return pair * LANES + (1 - odd) * HEAD_DIM + piece


def _fox_placement():
    pm = np.zeros((FOX_PIECES * N_HEADS, D_MODEL), np.float32)
    for t in range(FOX_PIECES):
        for h in range(N_HEADS):
            pm[t * N_HEADS + h, _fox_ext_lane(h, t)] = -1.0
    return pm


def _fox_lane_words():
    w = np.zeros((SUBLANES, LANES), np.uint32)
    for hh in range(FOX_HEADS_PER_PAIR):
        own = np.zeros(LANES, bool)
        own[hh * HEAD_DIM:(hh + 1) * HEAD_DIM] = True
        ext = np.zeros(LANES, bool)
        first = _fox_ext_lane(hh, 0)
        ext[first:first + FOX_PIECES] = True
        w[3 * hh + 0] = np.where(own, 0xFFFFFFFF, 0)
        w[3 * hh + 1] = np.where(ext, 0xFFFFFFFF, 0)
        w[3 * hh + 2] = np.where(ext, BF16_ONE_PAIR, 0)
    return w


def _log_sigmoid(x):
    return jnp.minimum(x, 0.0) - jnp.log1p(jnp.exp(-jnp.abs(x)))


def _round_bf16(x):
    return x.astype(BF16).astype(F32)


def _fox_gate_kernel(x_ref, wf_ref, fb_ref, pm_ref, cumt_ref, kext_ref, w3_ref):
    @pl.when(pl.program_id(0) == 0)
    def _():
        w = wf_ref[...].astype(BF16)
        for t in range(FOX_PIECES):
            w3_ref[t * N_HEADS:(t + 1) * N_HEADS, :] = w

    fz = lax.dot_general(w3_ref[...], x_ref[...], _NT_DIMS, preferred_element_type=F32) + fb_ref[...]
    lf = _log_sigmoid(fz)
    cb = FOX_CUM_BLOCK
    row = lax.broadcasted_iota(jnp.int32, (cb, cb), 0)
    col = lax.broadcasted_iota(jnp.int32, (cb, cb), 1)
    tri = jnp.where(row <= col, 1.0, 0.0).astype(BF16)
    tri3 = jnp.concatenate([tri] * FOX_PIECES, axis=0)
    group = lax.broadcasted_iota(jnp.int32, (lf.shape[0], cb), 0) // N_HEADS
    pm = pm_ref[...]
    n_blocks = lf.shape[1] // cb
    local = []
    for blk in range(n_blocks):
        x = lf[:, blk * cb:(blk + 1) * cb]
        hi = _round_bf16(x)
        r1 = x - hi
        mid = _round_bf16(r1)
        pieces = jnp.concatenate([hi, mid, r1 - mid], axis=1).astype(BF16)
        local.append(jnp.dot(pieces, tri3, preferred_element_type=F32))
    carry = jnp.zeros((lf.shape[0], 1), F32)
    for blk in range(n_blocks):
        cols = slice(blk * cb, (blk + 1) * cb)
        c = local[blk] + carry
        carry = carry + local[blk][:, cb - 1:cb]
        c2 = c * LOG2E
        cumt_ref[:, cols] = c2[:N_HEADS]
        chi = _round_bf16(c2)
        cr1 = c2 - chi
        cmid = _round_bf16(cr1)
        piece = jnp.where(group == 0, chi, jnp.where(group == 1, cmid, cr1 - cmid)).astype(BF16)
        kext_ref[cols, :] = lax.dot_general(piece, pm, _TN_DIMS, preferred_element_type=F32).astype(BF16)


def _fox_gate(xb, w_qkvf_t, layer, fb3, batch, seq):
    pm = jnp.asarray(_fox_placement(), BF16)
    n3 = FOX_PIECES * N_HEADS
    gate_row = 3 * D_MODEL // N_HEADS
    return pl.pallas_call(
        _fox_gate_kernel,
        grid=(batch,),
        in_specs=[pl.BlockSpec((seq, D_MODEL), lambda bi: (bi, 0)),
                  pl.BlockSpec((None, N_HEADS, D_MODEL), lambda bi: (layer, gate_row, 0)),
                  pl.BlockSpec((n3, 1), lambda bi: (0, 0)),
                  pl.BlockSpec((n3, D_MODEL), lambda bi: (0, 0))],
        out_specs=[pl.BlockSpec((None, N_HEADS, seq), lambda bi: (bi, 0, 0)),
                   pl.BlockSpec((None, seq, D_MODEL), lambda bi: (bi, 0, 0))],
        out_shape=[jax.ShapeDtypeStruct((batch, N_HEADS, seq), F32),
                   jax.ShapeDtypeStruct((batch, seq, D_MODEL), BF16)],
        scratch_shapes=[pltpu.VMEM((n3, D_MODEL), BF16)],
        compiler_params=_params("arbitrary"),
        name="fox_gate",
    )(xb, w_qkvf_t, fb3, pm)


def _fox_kernel(q_ref, k_ref, v_ref, kext_ref, cbt_ref, lw_ref, o_ref, qa_ref, ka_ref, vt_ref,
                s_ref, p_ref):
    seq = q_ref.shape[0]
    tq = FOX_TQ
    key = lax.broadcasted_iota(jnp.int32, (tq, tq), 0)
    qry = lax.broadcasted_iota(jnp.int32, (tq, tq), 1)
    causal = key <= qry
    words = lambda x: pltpu.bitcast(x, jnp.uint32)
    for pair in range(FOX_PAIRS_PER_STEP):
        lanes = slice(pair * LANES, (pair + 1) * LANES)
        q = words(q_ref[:, lanes])
        k = words(k_ref[:, lanes])
        kext = words(kext_ref[:, lanes])
        vt = v_ref[:, lanes].T
        for hh in range(FOX_HEADS_PER_PAIR):
            h = pair * FOX_HEADS_PER_PAIR + hh
            own = lw_ref[3 * hh + 0:3 * hh + 1, :]
            ext = lw_ref[3 * hh + 1:3 * hh + 2, :]
            ones = lw_ref[3 * hh + 2:3 * hh + 3, :]
            qa_ref[h] = pltpu.bitcast((q & own) | ones, BF16)
            ka_ref[h] = pltpu.bitcast((k & ~ext) | (kext & ext), BF16)
            vt_ref[h, 0:HEAD_DIM, :] = vt[hh * HEAD_DIM:(hh + 1) * HEAD_DIM]
            vt_ref[h, HEAD_DIM:, :] = jnp.ones((FOX_SUM_ROWS, seq), BF16)

    heads = FOX_PAIRS_PER_STEP * FOX_HEADS_PER_PAIR
    units = [(i, h) for i in range(seq // tq) for h in range(heads)]

    def scores(u):
        i, h = units[u]
        slot = u % FOX_SCORE_SLOTS
        r0, r1 = i * tq, (i + 1) * tq
        qa = qa_ref[h, r0:r1, :]
        s_d = lax.dot_general(ka_ref[h, r0:r1, :], qa, _NT_DIMS, preferred_element_type=F32)
        s_ref[slot, r0:r1, :] = jnp.where(causal, s_d, NEG_INF)
        if i > 0:
            s_ref[slot, 0:r0, :] = lax.dot_general(ka_ref[h, 0:r0, :], qa, _NT_DIMS,
                                                   preferred_element_type=F32)

    def softmax(u):
        i, h = units[u]
        slot = u % FOX_SCORE_SLOTS
        r0, r1 = i * tq, (i + 1) * tq
        head = pl.program_id(1) * heads + h
        cbq = cbt_ref[pl.ds(head, 1), r0:r1]
        m = jnp.max(s_ref[slot, 0:r1, :], axis=0, keepdims=True)
        shift = cbq - (m + cbq)
        p_ref[u % FOX_PROB_SLOTS, 0:r1, :] = jnp.exp2(s_ref[slot, 0:r1, :] + shift).astype(BF16)

    def weighted_values(u):
        i, h = units[u]
        r0, r1 = i * tq, (i + 1) * tq
        ot = jnp.dot(vt_ref[h, :, 0:r1], p_ref[u % FOX_PROB_SLOTS, 0:r1, :], preferred_element_type=F32)
        out = ot[0:HEAD_DIM] * (1.0 / ot[HEAD_DIM:HEAD_DIM + 1])
        o_ref[h * HEAD_DIM:(h + 1) * HEAD_DIM, r0:r1] = out.astype(o_ref.dtype)

    ahead = FOX_SCORE_SLOTS - 1
    for u in range(min(ahead, len(units))):
        scores(u)
    for u in range(len(units)):
        if u + ahead < len(units):
            scores(u + ahead)
        softmax(u)
        if u >= FOX_PV_LAG:
            weighted_values(u - FOX_PV_LAG)
    for u in range(len(units) - FOX_PV_LAG, len(units)):
        weighted_values(u)


def _fox_attention(qkv, cumt, kext):
    b, s, _ = qkv.shape
    heads = FOX_PAIRS_PER_STEP * FOX_HEADS_PER_PAIR
    width = FOX_PAIRS_PER_STEP * LANES
    lw = jnp.asarray(_fox_lane_words())
    ncol = D_MODEL // width
    return pl.pallas_call(
        _fox_kernel,
        grid=(b, N_HEADS // heads),
        in_specs=[pl.BlockSpec((None, s, width), lambda bi, hp: (bi, 0, hp)),
                  pl.BlockSpec((None, s, width), lambda bi, hp: (bi, 0, ncol + hp)),
                  pl.BlockSpec((None, s, width), lambda bi, hp: (bi, 0, 2 * ncol + hp)),
                  pl.BlockSpec((None, s, width), lambda bi, hp: (bi, 0, hp)),
                  pl.BlockSpec((None, N_HEADS, s), lambda bi, hp: (bi, 0, 0)),
                  pl.BlockSpec((SUBLANES, LANES), lambda bi, hp: (0, 0))],
        out_specs=pl.BlockSpec((None, heads * HEAD_DIM, s), lambda bi, hp: (bi, hp, 0)),
        out_shape=jax.ShapeDtypeStruct((b, D_MODEL, s), BF16),
        scratch_shapes=[pltpu.VMEM((heads, s, LANES), BF16), pltpu.VMEM((heads, s, LANES), BF16),
                        pltpu.VMEM((heads, HEAD_DIM + FOX_SUM_ROWS, s), BF16),
                        pltpu.VMEM((FOX_SCORE_SLOTS, s, FOX_TQ), F32),
                        pltpu.VMEM((FOX_PROB_SLOTS, s, FOX_TQ), BF16)],
        compiler_params=_params("arbitrary", "arbitrary"),
        name="fox_attn",
    )(qkv, qkv, qkv, kext, cumt, lw)


def _deepnorm(xres, y, g, b, alpha):
    z = alpha * xres + y
    mu = jnp.mean(z, axis=-1, keepdims=True)
    zc = z - mu
    var = jnp.mean(zc * zc, axis=-1, keepdims=True)
    return zc * lax.rsqrt(var + LN_EPS) * g + b


LN_SUBTILES = 2


def _oproj_ln_kernel(alpha, ot_ref, xres_ref, w_ref, g_ref, b_ref, xo_ref, xob_ref, wb_ref):
    @pl.when(pl.program_id(0) == 0)
    def _():
        wb_ref[...] = w_ref[...].astype(BF16)

    ts = xres_ref.shape[0] // LN_SUBTILES
    ys = [lax.dot_general(ot_ref[:, r * ts:(r + 1) * ts], wb_ref[...], _TN_DIMS,
                          preferred_element_type=F32) for r in range(LN_SUBTILES)]
    for r, y in enumerate(ys):
        rows = slice(r * ts, (r + 1) * ts)
        out = _deepnorm(xres_ref[rows, :], y, g_ref[...], b_ref[...], alpha)
        xo_ref[rows, :] = out
        xob_ref[rows, :] = out.astype(BF16)


def _oproj_ln(ot, xres, w, layer, g, b, alpha, tm, name):
    m, n = xres.shape
    k = w.shape[1]
    per_seq = ot.shape[2] // tm
    return pl.pallas_call(
        functools.partial(_oproj_ln_kernel, alpha),
        grid=(m // tm,),
        in_specs=[pl.BlockSpec((None, k, tm), lambda i: (i // per_seq, 0, i % per_seq)),
                  pl.BlockSpec((tm, n), lambda i: (i, 0)),
                  pl.BlockSpec((None, k, n), lambda i: (layer, 0, 0), pipeline_mode=pl.Buffered(1)),
                  pl.BlockSpec((1, n), lambda i: (0, 0)),
                  pl.BlockSpec((1, n), lambda i: (0, 0))],
        out_specs=[pl.BlockSpec((tm, n), lambda i: (i, 0)),
                   pl.BlockSpec((tm, n), lambda i: (i, 0))],
        out_shape=[jax.ShapeDtypeStruct((m, n), F32), jax.ShapeDtypeStruct((m, n), BF16)],
        scratch_shapes=[pltpu.VMEM((k, n), BF16)],
        compiler_params=_params("arbitrary"),
        name=name,
    )(ot, xres, w, g, b)


CONV_WIDTH = 3
FFN_TF = 512
FFN_UP_TM = 1024


def _ffn_up_kernel(tiles_per_seq, x_ref, wg_ref, wu_ref, cwg_ref, cwu_ref, cbg_ref, cbu_ref, wd_ref,
                   a_ref, wdb_ref, wb_ref, h_ref, t_ref):
    wdb_ref[...] = wd_ref[...].astype(BF16)
    i = pl.program_id(1)
    tm = x_ref.shape[0]
    tf = a_ref.shape[1]
    halo = SUBLANES
    slabs = tf // LANES

    @pl.when(i == 0)
    def _():
        wb_ref[:, 0:tf] = wg_ref[...].astype(BF16)
        wb_ref[:, tf:2 * tf] = wu_ref[...].astype(BF16)

    @pl.when(i % tiles_per_seq == 0)
    def _():
        h_ref[:, 0:halo, :] = jnp.zeros((2 * slabs, halo, LANES), F32)

    h = jnp.dot(x_ref[...], wb_ref[...], preferred_element_type=F32)
    for c in range(2 * slabs):
        h_ref[c, halo:halo + tm, :] = h[:, c * LANES:(c + 1) * LANES]

    def conv(slab, parity, cw, cb):
        out = cb
        for tap in range(CONV_WIDTH):
            start = halo + parity - (CONV_WIDTH - 1) + tap
            out = out + cw[tap:tap + 1, :] * h_ref[slab, pl.ds(start, tm // 2, stride=2), :]
        return out

    for c in range(slabs):
        lanes = slice(c * LANES, (c + 1) * LANES)
        cwg, cbg = cwg_ref[:, lanes] * 0.5, cbg_ref[:, lanes] * 0.5
        cwu, cbu = cwu_ref[:, lanes], cbu_ref[:, lanes]
        for parity in range(2):
            half = conv(c, parity, cwg, cbg)
            u = conv(slabs + c, parity, cwu, cbu)
            t_ref[c, pl.ds(parity, tm // 2, stride=2), :] = (half + half * jnp.tanh(half)) * u
    for c in range(slabs):
        a_ref[:, c * LANES:(c + 1) * LANES] = t_ref[c].astype(a_ref.dtype)
    h_ref[:, 0:halo, :] = h_ref[:, tm:tm + halo, :]


def _ffn_up(xb, w_up, conv_w, conv_b, w_down, layer, seq):
    m, k = xb.shape
    tm, tf = FFN_UP_TM, FFN_TF
    nf = D_FF // tf
    n_tiles = m // tm
    cb2 = conv_b.reshape(conv_b.shape[0], 1, 2 * D_FF)
    wd_rows = w_down.shape[1] // (nf * n_tiles)
    assert wd_rows * nf * n_tiles == w_down.shape[1] and wd_rows % (2 * SUBLANES) == 0
    return pl.pallas_call(
        functools.partial(_ffn_up_kernel, seq // tm),
        grid=(nf, n_tiles),
        in_specs=[pl.BlockSpec((tm, k), lambda j, i: (i, 0)),
                  pl.BlockSpec((None, k, tf), lambda j, i: (layer, 0, j)),
                  pl.BlockSpec((None, k, tf), lambda j, i: (layer, 0, j + nf)),
                  pl.BlockSpec((None, CONV_WIDTH, tf), lambda j, i: (layer, 0, j)),
                  pl.BlockSpec((None, CONV_WIDTH, tf), lambda j, i: (layer, 0, j + nf)),
                  pl.BlockSpec((None, 1, tf), lambda j, i: (layer, 0, j)),
                  pl.BlockSpec((None, 1, tf), lambda j, i: (layer, 0, j + nf)),
                  pl.BlockSpec((None, wd_rows, w_down.shape[2]), lambda j, i: (layer, j * n_tiles + i, 0))],
        out_specs=[pl.BlockSpec((tm, tf), lambda j, i: (i, j)),
                   pl.BlockSpec((wd_rows, w_down.shape[2]), lambda j, i: (j * n_tiles + i, 0))],
        out_shape=[jax.ShapeDtypeStruct((m, D_FF), BF16),
                   jax.ShapeDtypeStruct(w_down.shape[1:], BF16)],
        scratch_shapes=[pltpu.VMEM((k, 2 * tf), BF16),
                        pltpu.VMEM((2 * tf // LANES, tm + 2 * SUBLANES, LANES), F32),
                        pltpu.VMEM((tf // LANES, tm, LANES), F32)],
        compiler_params=_params("arbitrary", "arbitrary"),
        name="ffn_up",
    )(xb, w_up, w_up, conv_w, conv_w, cb2, cb2, w_down)


FFN_DOWN_TM = 256


FFN_DOWN_W_CHUNKS = 11


def _ffn_down_ln_kernel(alpha, emit_bf16, a_ref, xres_ref, w_hbm, g_ref, b_ref, xo_ref, *rest):
    *maybe_xob, w_ref, sem = rest
    kc = w_ref.shape[0] // FFN_DOWN_W_CHUNKS

    def finish(y):
        out = _deepnorm(xres_ref[...], y, g_ref[...], b_ref[...], alpha)
        xo_ref[...] = out
        if emit_bf16:
            maybe_xob[0][...] = out.astype(BF16)

    @pl.when(pl.program_id(0) == 0)
    def _():
        chunks = [slice(c * kc, (c + 1) * kc) for c in range(FFN_DOWN_W_CHUNKS)]
        copies = [pltpu.make_async_copy(w_hbm.at[rows], w_ref.at[rows], sem.at[c])
                  for c, rows in enumerate(chunks)]
        for copy in copies:
            copy.start()
        y = jnp.zeros(xo_ref.shape, F32)
        for copy, rows in zip(copies, chunks):
            copy.wait()
            y = y + jnp.dot(a_ref[:, rows], w_ref[rows, :], preferred_element_type=F32)
        finish(y)

    @pl.when(pl.program_id(0) > 0)
    def _():
        finish(jnp.dot(a_ref[...], w_ref[...], preferred_element_type=F32))


def _ffn_down_ln(a, xres, w_bf16, g, b, alpha, emit_bf16, name):
    m, k = a.shape
    n = w_bf16.shape[1]
    tm = FFN_DOWN_TM
    out_specs = [pl.BlockSpec((tm, n), lambda i: (i, 0))]
    out_shape = [jax.ShapeDtypeStruct((m, n), F32)]
    if emit_bf16:
        out_specs.append(pl.BlockSpec((tm, n), lambda i: (i, 0)))
        out_shape.append(jax.ShapeDtypeStruct((m, n), BF16))
    return pl.pallas_call(
        functools.partial(_ffn_down_ln_kernel, alpha, emit_bf16),
        grid=(m // tm,),
        in_specs=[pl.BlockSpec((tm, k), lambda i: (i, 0)),
                  pl.BlockSpec((tm, n), lambda i: (i, 0)),
                  pl.BlockSpec(memory_space=pl.ANY),
                  pl.BlockSpec((1, n), lambda i: (0, 0)),
                  pl.BlockSpec((1, n), lambda i: (0, 0))],
        out_specs=out_specs,
        out_shape=out_shape,
        scratch_shapes=[pltpu.VMEM((k, n), BF16), pltpu.SemaphoreType.DMA((FFN_DOWN_W_CHUNKS,))],
        compiler_params=_params("arbitrary"),
        name=name,
    )(a, xres, w_bf16, g, b)


def kernel(x, a_w_qkv, a_w_o, a_sinks, rel_bias, b_w_qkvf, b_f_bias, b_w_o,
           ffn_w_up, ffn_conv_w, ffn_conv_b, ffn_w_down, ln_g, ln_b):
    batch, seq, d = x.shape
    depth = ffn_w_up.shape[0]
    alpha = (2 * depth) ** 0.25
    m = batch * seq
    xf = x.reshape(m, d)
    xb = None
    biasmask = _swa_bias(rel_bias)
    for layer in range(depth):
        j = layer // 2
        g = ln_g[layer].reshape(2, 1, d)
        bt = ln_b[layer].reshape(2, 1, d)
        if layer % 2 == 0:
            n_qkv = D_MODEL + 2 * A_KV_DIM
            src = xf if xb is None else xb
            qkv = _ws_matmul(src, a_w_qkv, j, n_qkv, n_qkv // SWA_QKV_COL_TILES, SWA_QKV_TM, "swa_qkv",
                             slab_batch=(batch, seq))
            o = _swa_attention(qkv, a_sinks[j], biasmask)
            w_o = a_w_o
        else:
            w_t = jnp.swapaxes(b_w_qkvf, 1, 2)
            qkv = _ws_matmul(xb, w_t, j, 3 * D_MODEL, FOX_QKV_TN, FOX_QKV_TM, "fox_qkv",
                             w_is_transposed=True)
            fb3 = jnp.tile(b_f_bias[j].reshape(N_HEADS, 1), (FOX_PIECES, 1))
            cumt, kext = _fox_gate(xb, w_t, j, fb3, batch, seq)
            o = _fox_attention(qkv.reshape(batch, seq, 3 * D_MODEL), cumt, kext)
            w_o = b_w_o
        xf, xb = _oproj_ln(o, xf, w_o, j, g[0], bt[0], alpha, OPROJ_TM, "oproj_ln")
        a, w_down = _ffn_up(xb, ffn_w_up, ffn_conv_w, ffn_conv_b, ffn_w_down, layer, seq)
        last = layer == depth - 1
        res = _ffn_down_ln(a, xf, w_down, g[1], bt[1], alpha, not last, "ffn_down_ln")
        xf, xb = (res[0], None) if last else res
    return xf.reshape(batch, seq, d)
```
